```python
import math
import jax, jax.numpy as jnp
from jax import lax
import numpy as np

D_MODEL = 1024
BATCH = 16
SEQ = 256
DEPTH = 2
DEC_BATCH = 8
DEC_SEQ = 1024
PAST_LEN = 512

GRID_W = 64
N_EVEN = (DEPTH + 1) // 2
N_ODD = DEPTH // 2
EPS = 1e-6
FOURIER_GROUPS = 4
FOURIER_GROUP_W = D_MODEL // 16
FOURIER_W = FOURIER_GROUPS * FOURIER_GROUP_W
HEAD_DIM = 64
N_Q_HEADS = (D_MODEL - FOURIER_W) // HEAD_DIM
N_KV_HEADS = N_Q_HEADS // 3
GQA_GROUP = N_Q_HEADS // N_KV_HEADS
Q_W = N_Q_HEADS * HEAD_DIM
KV_W = N_KV_HEADS * HEAD_DIM
EVEN_IN = FOURIER_W + Q_W + 2 * KV_W
EVEN_MIX = FOURIER_W + Q_W
Q_BLOCK = 128
ROPE_THETA = 10000.0
AX_PAIRS = HEAD_DIM // 4
CONV_W = 4
LRU_W = D_MODEL // 2
LRU_BLOCKS = 8
LRU_BLOCK_W = LRU_W // LRU_BLOCKS
LRU_C = 8.0
SSD_INNER = D_MODEL
SSD_HEAD_P = 64
SSD_HEADS = SSD_INNER // SSD_HEAD_P
SSD_GROUPS = 2
SSD_STATE = 64
SSD_CHUNK = 128
SSD_CONV_CH = SSD_INNER + 2 * SSD_GROUPS * SSD_STATE
ODD_IN = 2 * LRU_W + SSD_INNER + SSD_CONV_CH + 2 * SSD_HEADS
ODD_MIX = LRU_W + SSD_INNER
D_FF = -(-(8 * D_MODEL) // (3 * 256)) * 256

kernel_name = "hybrid_dit_fourier_gqa_rglru_ssd_step"


def rmsnorm(x, g):
    xf = x.astype(jnp.float32)
    y = xf * lax.rsqrt(jnp.mean(xf * xf, axis=-1, keepdims=True) + EPS)
    return (y * g.astype(jnp.float32)).astype(x.dtype)


def modulation(cond, w, b):
    m = jax.nn.silu(cond) @ w + b
    return [t[:, None, :] for t in jnp.split(m, 6, axis=-1)]


def centred_dwconv(x, w, b):
    y = lax.conv_general_dilated(
        x, w.astype(x.dtype)[:, None, :], window_strides=(1,),
        padding=[((CONV_W - 1) // 2, CONV_W // 2)],
        dimension_numbers=("NWC", "WIO", "NWC"), feature_group_count=x.shape[-1])
    return y + b.astype(x.dtype)


def swiglu(h, w1, w3, w2):
    return (jax.nn.silu(h @ w1) * (h @ w3)) @ w2


def fourier_mix(f):
    B, S, _ = f.shape
    fg = f.astype(jnp.float32).reshape(B, S, FOURIER_GROUPS, FOURIER_GROUP_W)
    out = jnp.fft.fft2(fg, axes=(1, 3), norm="ortho").real
    return out.reshape(B, S, FOURIER_W).astype(f.dtype)


def axial_rope_tables(S):
    rows = S // GRID_W
    row = jnp.repeat(jnp.arange(rows, dtype=jnp.float32), GRID_W)
    col = jnp.tile(jnp.arange(GRID_W, dtype=jnp.float32), rows)
    freqs = ROPE_THETA ** (-jnp.arange(AX_PAIRS, dtype=jnp.float32) / AX_PAIRS)
    ang = jnp.stack([row[:, None] * freqs, col[:, None] * freqs], axis=1)
    return jnp.cos(ang), jnp.sin(ang)


def apply_axial_rope(x, cos, sin):
    B, S, H, _ = x.shape
    xa = x.astype(jnp.float32).reshape(B, S, H, 2, 2, AX_PAIRS)
    x1, x2 = xa[..., 0, :], xa[..., 1, :]
    c = cos[None, :, None]
    s = sin[None, :, None]
    out = jnp.stack([x1 * c - x2 * s, x2 * c + x1 * s], axis=-2)
    return out.reshape(x.shape).astype(x.dtype)


def block_attention(q, k, v):
    B, S, _, _ = q.shape
    nb = S // Q_BLOCK
    qb = q.reshape(B, nb, Q_BLOCK, N_KV_HEADS, GQA_GROUP, HEAD_DIM).transpose(1, 0, 2, 3, 4, 5)
    kf = k.astype(jnp.float32)
    vf = v.astype(jnp.float32)
    scale = HEAD_DIM ** -0.5

    def one_block(qblk):
        s = jnp.einsum("bqhgd,bkhd->bhgqk", qblk.astype(jnp.float32), kf) * scale
        p = jax.nn.softmax(s, axis=-1)
        return jnp.einsum("bhgqk,bkhd->bqhgd", p, vf)

    o = lax.map(one_block, qb)
    return o.transpose(1, 0, 2, 3, 4, 5).reshape(B, S, Q_W).astype(q.dtype)


def even_mixer(h, w_in, q_norm, k_norm, w_out, ctx_kv):
    B, S, _ = h.shape
    p = h @ w_in
    f, q, k, v = jnp.split(p, [FOURIER_W, FOURIER_W + Q_W, FOURIER_W + Q_W + KV_W], axis=-1)
    fo = fourier_mix(f)
    q = rmsnorm(q.reshape(B, S, N_Q_HEADS, HEAD_DIM), q_norm)
    k = rmsnorm(k.reshape(B, S, N_KV_HEADS, HEAD_DIM), k_norm)
    v = v.reshape(B, S, N_KV_HEADS, HEAD_DIM)
    if ctx_kv is None:
        ao = block_attention(q, k, v)
        kv_out = (k, v)
    else:
        cos, sin = axial_rope_tables(S)
        q = apply_axial_rope(q, cos, sin)
        kl = apply_axial_rope(k, cos, sin)
        ck, cv = ctx_kv
        keys = jnp.concatenate([kl, ck.astype(kl.dtype)], axis=1)
        vals = jnp.concatenate([v, cv.astype(v.dtype)], axis=1)
        ao = block_attention(q, keys, vals)
        kv_out = None
    return jnp.concatenate([fo, ao], axis=-1) @ w_out, kv_out


def linear_scan(a, b, h0, reverse):
    if reverse:
        a = jnp.flip(a, axis=1)
        b = jnp.flip(b, axis=1)
    b = b.at[:, 0].add(a[:, 0] * h0)

    def combine(l, r):
        return (l[0] * r[0], r[0] * l[1] + r[1])

    _, h = lax.associative_scan(combine, (a, b), axis=1)
    final = h[:, -1]
    if reverse:
        h = jnp.flip(h, axis=1)
    return h, final


def rglru_direction(xc, wa, ba, wx, bx, lam, h0, reverse):
    B, S, _ = xc.shape
    xf = xc.astype(jnp.float32)
    xb = xf.reshape(B, S, LRU_BLOCKS, LRU_BLOCK_W)
    r = jax.nn.sigmoid(jnp.einsum("bshi,hij->bshj", xb, wa.astype(jnp.float32)).reshape(B, S, LRU_W)
                       + ba.astype(jnp.float32))
    i = jax.nn.sigmoid(jnp.einsum("bshi,hij->bshj", xb, wx.astype(jnp.float32)).reshape(B, S, LRU_W)
                       + bx.astype(jnp.float32))
    log_a = -LRU_C * r * jax.nn.softplus(-lam.astype(jnp.float32))
    a = jnp.exp(log_a)
    b = jnp.sqrt(-jnp.expm1(2.0 * log_a)) * (i * xf)
    return linear_scan(a, b, h0.astype(jnp.float32), reverse)


def ssd_direction(x, dt, A, Bm, Cm, h0, reverse):
    if reverse:
        x, dt, Bm, Cm = (jnp.flip(t, axis=1) for t in (x, dt, Bm, Cm))
    Bsz, L = x.shape[:2]
    nc = L // SSD_CHUNK
    Q = SSD_CHUNK
    G = SSD_GROUPS
    Hg = SSD_HEADS // G
    xc = x.reshape(Bsz, nc, Q, G, Hg, SSD_HEAD_P)
    dtc = dt.reshape(Bsz, nc, Q, G, Hg)
    Bc = Bm.reshape(Bsz, nc, Q, G, SSD_STATE)
    Cc = Cm.reshape(Bsz, nc, Q, G, SSD_STATE)
    acs = jnp.cumsum(dtc * A.reshape(G, Hg), axis=2)
    diff = acs[:, :, :, None] - acs[:, :, None, :]
    causal = jnp.tril(jnp.ones((Q, Q), dtype=bool))[:, :, None, None]
    decay = jnp.exp(jnp.where(causal, diff, -jnp.inf))
    cb = jnp.einsum("bcqgn,bckgn->bcqkg", Cc, Bc)
    y_diag = jnp.einsum("bcqkg,bcqkgh,bckgh,bckghp->bcqghp", cb, decay, dtc, xc)
    decay_end = jnp.exp(acs[:, :, -1:] - acs)
    states = jnp.einsum("bckgn,bckgh,bckghp->bcghpn", Bc, decay_end * dtc, xc)
    chunk_decay = jnp.exp(acs[:, :, -1])

    def step(h_prev, inp):
        st, dec = inp
        return dec[..., None, None] * h_prev + st, h_prev

    h0g = h0.reshape(Bsz, G, Hg, SSD_HEAD_P, SSD_STATE)
    h_final, h_in = lax.scan(step, h0g, (jnp.moveaxis(states, 1, 0), jnp.moveaxis(chunk_decay, 1, 0)))
    h_in = jnp.moveaxis(h_in, 0, 1)
    y_off = jnp.einsum("bcqgn,bcghpn,bcqgh->bcqghp", Cc, h_in, jnp.exp(acs))
    y = (y_diag + y_off).reshape(Bsz, L, SSD_HEADS, SSD_HEAD_P)
    if reverse:
        y = jnp.flip(y, axis=1)
    return y, h_final.reshape(Bsz, SSD_HEADS, SSD_HEAD_P, SSD_STATE)


def odd_mixer(h, w_in, conv_lru_w, conv_lru_b, lru_wa, lru_ba, lru_wx, lru_bx, lru_lambda,
              conv_ssd_w, conv_ssd_b, ssd_dt_bias, ssd_a_log, ssd_d, ssd_norm, w_out,
              lru_h0, ssd_h0):
    B, S, _ = h.shape
    f32 = jnp.float32
    p = h @ w_in
    g, xl, z, xbc, dt = jnp.split(
        p, [LRU_W, 2 * LRU_W, 2 * LRU_W + SSD_INNER, 2 * LRU_W + SSD_INNER + SSD_CONV_CH], axis=-1)
    xc = centred_dwconv(xl, conv_lru_w, conv_lru_b)
    y_f, hl_f = rglru_direction(xc, lru_wa[0], lru_ba[0], lru_wx[0], lru_bx[0], lru_lambda[0],
                                lru_h0[:, 0], False)
    y_b, hl_b = rglru_direction(xc, lru_wa[1], lru_ba[1], lru_wx[1], lru_bx[1], lru_lambda[1],
                                lru_h0[:, 1], True)
    y_lru = ((y_f + y_b) * jax.nn.gelu(g.astype(f32))).astype(h.dtype)
    xbc = jax.nn.silu(centred_dwconv(xbc, conv_ssd_w, conv_ssd_b)).astype(f32)
    xs, Bm, Cm = jnp.split(xbc, [SSD_INNER, SSD_INNER + SSD_GROUPS * SSD_STATE], axis=-1)
    xs = xs.reshape(B, S, SSD_HEADS, SSD_HEAD_P)
    Bm = Bm.reshape(B, S, SSD_GROUPS, SSD_STATE)
    Cm = Cm.reshape(B, S, SSD_GROUPS, SSD_STATE)
    dt = dt.astype(f32)
    dt_f = jax.nn.softplus(dt[..., :SSD_HEADS] + ssd_dt_bias[0].astype(f32))
    dt_b = jax.nn.softplus(dt[..., SSD_HEADS:] + ssd_dt_bias[1].astype(f32))
    A_f = -jnp.exp(ssd_a_log[0].astype(f32))
    A_b = -jnp.exp(ssd_a_log[1].astype(f32))
    ys_f, hs_f = ssd_direction(xs, dt_f, A_f, Bm, Cm, ssd_h0[:, 0].astype(f32), False)
    ys_b, hs_b = ssd_direction(xs, dt_b, A_b, Bm, Cm, ssd_h0[:, 1].astype(f32), True)
    y = ys_f + ys_b + ssd_d.astype(f32)[:, None] * xs
    y = (y.reshape(B, S, SSD_INNER) * jax.nn.silu(z.astype(f32)))
    y_ssd = rmsnorm(y, ssd_norm).astype(h.dtype)
    out = jnp.concatenate([y_lru, y_ssd], axis=-1) @ w_out
    return out, jnp.stack([hl_f, hl_b], axis=1), jnp.stack([hs_f, hs_b], axis=1)


def run_trunk(x, cond, p, ctx_k, ctx_v, ctx_lru, ctx_ssd, is_ctx):
    B = x.shape[0]
    new_k, new_v, new_lru, new_ssd = [], [], [], []
    for l in range(DEPTH):
        sm, cm, gm, sf, cf, gf = modulation(cond, p["w_ada"][l], p["b_ada"][l])
        hmix = rmsnorm(x, p["norm_mix"][l]) * (1 + cm) + sm
        j = l // 2
        if l % 2 == 0:
            kv = None if is_ctx else (ctx_k[:, j], ctx_v[:, j])
            out, kv_new = even_mixer(hmix, p["w_in_even"][j], p["q_norm"][j], p["k_norm"][j],
                                     p["w_out_even"][j], kv)
            if is_ctx:
                new_k.append(kv_new[0])
                new_v.append(kv_new[1])
        else:
            if is_ctx:
                h0l = jnp.zeros((B, 2, LRU_W), jnp.float32)
                h0s = jnp.zeros((B, 2, SSD_HEADS, SSD_HEAD_P, SSD_STATE), jnp.float32)
            else:
                h0l = ctx_lru[:, j]
                h0s = ctx_ssd[:, j]
            out, sl, ss = odd_mixer(
                hmix, p["w_in_odd"][j], p["conv_lru_w"][j], p["conv_lru_b"][j],
                p["lru_wa"][j], p["lru_ba"][j], p["lru_wx"][j], p["lru_bx"][j], p["lru_lambda"][j],
                p["conv_ssd_w"][j], p["conv_ssd_b"][j], p["ssd_dt_bias"][j], p["ssd_a_log"][j],
                p["ssd_d"][j], p["ssd_norm"][j], p["w_out_odd"][j], h0l, h0s)
            if is_ctx:
                new_lru.append(sl.astype(x.dtype))
                new_ssd.append(ss.astype(x.dtype))
        x = x + gm * out
        hff = rmsnorm(x, p["norm_ffn"][l]) * (1 + cf) + sf
        x = x + gf * swiglu(hff, p["ffn_w1"][l], p["ffn_w3"][l], p["ffn_w2"][l])
    return x, new_k, new_v, new_lru, new_ssd


def setup_inputs(seed: int = 0) -> dict:
    key = jax.random.key(seed)
    ks = iter(jax.random.split(key, 48))
    f32 = jnp.float32

    def nrm(shape, scale=1.0):
        return jax.random.normal(next(ks), shape, f32) * scale

    def gain(shape):
        return 1.0 + 0.02 * jax.random.normal(next(ks), shape, f32)

    u = jax.random.uniform(next(ks), (N_ODD, 2, LRU_W), f32,
                           minval=0.9 ** (1.0 / LRU_C), maxval=0.999 ** (1.0 / LRU_C))
    lru_lambda = jnp.log(u) - jnp.log1p(-u)
    dt0 = jnp.exp(jax.random.uniform(next(ks), (N_ODD, 2, SSD_HEADS), f32,
                                     minval=math.log(1e-3), maxval=math.log(1e-1)))
    ssd_dt_bias = dt0 + jnp.log(-jnp.expm1(-dt0))
    ssd_a_log = jnp.log(jax.random.uniform(next(ks), (N_ODD, 2, SSD_HEADS), f32, minval=1.0, maxval=16.0))

    return {
        "x_prompt": nrm((BATCH, SEQ, D_MODEL)),
        "x_sample": nrm((DEC_BATCH, DEC_SEQ, D_MODEL)),
        "c": nrm((DEC_BATCH, D_MODEL)),
        "cache_k": nrm((DEC_BATCH, N_EVEN, PAST_LEN, N_KV_HEADS, HEAD_DIM)),
        "cache_v": nrm((DEC_BATCH, N_EVEN, PAST_LEN, N_KV_HEADS, HEAD_DIM)),
        "state_lru": nrm((DEC_BATCH, N_ODD, 2, LRU_W), 0.5),
        "state_ssd": nrm((DEC_BATCH, N_ODD, 2, SSD_HEADS, SSD_HEAD_P, SSD_STATE), 0.5),
        "c_ctx": nrm((D_MODEL,)),
        "w_ada": nrm((DEPTH, D_MODEL, 6 * D_MODEL), 0.5 * D_MODEL ** -0.5),
        "b_ada": nrm((DEPTH, 6 * D_MODEL), 0.02),
        "norm_mix": gain((DEPTH, D_MODEL)),
        "norm_ffn": gain((DEPTH, D_MODEL)),
        "w_in_even": nrm((N_EVEN, D_MODEL, EVEN_IN), D_MODEL ** -0.5),
        "q_norm": gain((N_EVEN, HEAD_DIM)),
        "k_norm": gain((N_EVEN, HEAD_DIM)),
        "w_out_even": nrm((N_EVEN, EVEN_MIX, D_MODEL), EVEN_MIX ** -0.5),
        "w_in_odd": nrm((N_ODD, D_MODEL, ODD_IN), D_MODEL ** -0.5),
        "conv_lru_w": nrm((N_ODD, CONV_W, LRU_W), CONV_W ** -0.5),
        "conv_lru_b": nrm((N_ODD, LRU_W), 0.02),
        "lru_wa": nrm((N_ODD, 2, LRU_BLOCKS, LRU_BLOCK_W, LRU_BLOCK_W), LRU_BLOCK_W ** -0.5),
        "lru_ba": nrm((N_ODD, 2, LRU_W), 0.02),
        "lru_wx": nrm((N_ODD, 2, LRU_BLOCKS, LRU_BLOCK_W, LRU_BLOCK_W), LRU_BLOCK_W ** -0.5),
        "lru_bx": nrm((N_ODD, 2, LRU_W), 0.02),
        "lru_lambda": lru_lambda,
        "conv_ssd_w": nrm((N_ODD, CONV_W, SSD_CONV_CH), CONV_W ** -0.5),
        "conv_ssd_b": nrm((N_ODD, SSD_CONV_CH), 0.02),
        "ssd_dt_bias": ssd_dt_bias,
        "ssd_a_log": ssd_a_log,
        "ssd_d": gain((N_ODD, SSD_HEADS)),
        "ssd_norm": gain((N_ODD, SSD_INNER)),
        "w_out_odd": nrm((N_ODD, ODD_MIX, D_MODEL), ODD_MIX ** -0.5),
        "ffn_w1": nrm((DEPTH, D_MODEL, D_FF), D_MODEL ** -0.5),
        "ffn_w3": nrm((DEPTH, D_MODEL, D_FF), D_MODEL ** -0.5),
        "ffn_w2": nrm((DEPTH, D_FF, D_MODEL), D_FF ** -0.5),
    }


def reference(x_prompt, x_sample, c, cache_k, cache_v, state_lru, state_ssd, c_ctx,
              w_ada, b_ada, norm_mix, norm_ffn, w_in_even, q_norm, k_norm, w_out_even,
              w_in_odd, conv_lru_w, conv_lru_b, lru_wa, lru_ba, lru_wx, lru_bx, lru_lambda,
              conv_ssd_w, conv_ssd_b, ssd_dt_bias, ssd_a_log, ssd_d, ssd_norm, w_out_odd,
              ffn_w1, ffn_w3, ffn_w2):
    p = dict(w_ada=w_ada, b_ada=b_ada, norm_mix=norm_mix, norm_ffn=norm_ffn,
             w_in_even=w_in_even, q_norm=q_norm, k_norm=k_norm, w_out_even=w_out_even,
             w_in_odd=w_in_odd, conv_lru_w=conv_lru_w, conv_lru_b=conv_lru_b,
             lru_wa=lru_wa, lru_ba=lru_ba, lru_wx=lru_wx, lru_bx=lru_bx, lru_lambda=lru_lambda,
             conv_ssd_w=conv_ssd_w, conv_ssd_b=conv_ssd_b, ssd_dt_bias=ssd_dt_bias,
             ssd_a_log=ssd_a_log, ssd_d=ssd_d, ssd_norm=ssd_norm, w_out_odd=w_out_odd,
             ffn_w1=ffn_w1, ffn_w3=ffn_w3, ffn_w2=ffn_w2)
    y_prompt, ks_, vs_, lrus_, ssds_ = run_trunk(
        x_prompt, c_ctx[None, :], p, None, None, None, None, True)
    new_k = jnp.stack(ks_, axis=1)
    new_v = jnp.stack(vs_, axis=1)
    new_lru = jnp.stack(lrus_, axis=1)
    new_ssd = jnp.stack(ssds_, axis=1)
    y_sample, _, _, _, _ = run_trunk(
        x_sample, c, p, cache_k, cache_v, state_lru, state_ssd, False)
    return (y_prompt, y_sample, new_k, new_v, new_lru, new_ssd)
```

```python
import functools
import math

import numpy as np
import jax
import jax.numpy as jnp
from jax import lax
from jax.experimental import pallas as pl
from jax.experimental.pallas import tpu as pltpu

F32 = jnp.float32
BF16 = jnp.bfloat16

D_MODEL = 1024
EPS = 1e-6
GRID_W = 64
HEAD_DIM = 64
N_Q_HEADS = 12
N_KV_HEADS = 4
FOURIER_W = 256
FOURIER_GROUP_W = 64
Q_W = N_Q_HEADS * HEAD_DIM
KV_W = N_KV_HEADS * HEAD_DIM
EVEN_IN = FOURIER_W + Q_W + 2 * KV_W
ROPE_THETA = 10000.0
AX_PAIRS = HEAD_DIM // 4
LRU_W = 512
LRU_C = 8.0
SSD_INNER = 1024
SSD_HEADS = 16
SSD_STATE = 64
SSD_CONV_CH = SSD_INNER + 4 * SSD_STATE
ODD_IN = 2 * LRU_W + SSD_INNER + SSD_CONV_CH + 2 * SSD_HEADS
ODD_IN_PAD = 3456
D_FF = 2816
LANES = 128
SUBLANES = 8
TOKEN_BLOCK = 256
SSD_Q = 128
VMEM_LIMIT = 56 * 1024 * 1024


def _params(sem, vmem=VMEM_LIMIT):
    return pltpu.CompilerParams(dimension_semantics=sem, vmem_limit_bytes=vmem)


def _const_spec(shape):
    nd = len(shape)
    return pl.BlockSpec(shape, lambda *_: (0,) * nd, pipeline_mode=pl.Buffered(1))


def _dot(a, b):
    return jnp.dot(a, b, preferred_element_type=F32)


def _dot_nt(a, b):
    return lax.dot_general(a, b, (((1,), (1,)), ((), ())), preferred_element_type=F32)


def _dot3(m, a):
    hi = a.astype(BF16)
    r1 = a - hi.astype(F32)
    mid = r1.astype(BF16)
    lo = (r1 - mid.astype(F32)).astype(BF16)
    return _dot(m, hi) + _dot(m, mid) + _dot(m, lo)


def _dot2r(a, m):
    hi = a.astype(BF16)
    lo = (a - hi.astype(F32)).astype(BF16)
    return _dot(hi, m) + _dot(lo, m)


def _silu(x):
    return x * jax.nn.sigmoid(x)


def _softplus(x):
    return jnp.maximum(x, 0.0) + jnp.log1p(jnp.exp(-jnp.abs(x)))


def _gelu_tanh(x):
    c = math.sqrt(2.0 / math.pi)
    return 0.5 * x * (1.0 + jnp.tanh(c * (x + 0.044715 * (x * x * x))))


def _modnorm(x, g, shift, scale):
    ms = jnp.mean(x * x, axis=-1, keepdims=True)
    y = x * lax.rsqrt(ms + EPS) * g
    return y * (1.0 + scale) + shift


def _mod_kernel(cond_ref, w_ref, b_ref, o_ref):
    s = _silu(cond_ref[...])
    o_ref[0] = _dot(s.astype(BF16), w_ref[0].astype(BF16)) + b_ref[0]


def _modulation(cond, w_ada, b_ada):
    depth, d, n = w_ada.shape
    rows = cond.shape[0]
    tn = 1536
    return pl.pallas_call(
        _mod_kernel,
        grid=(depth, n // tn),
        in_specs=[pl.BlockSpec((rows, d), lambda l, j: (0, 0)),
                  pl.BlockSpec((1, d, tn), lambda l, j: (l, 0, j)),
                  pl.BlockSpec((1, 1, tn), lambda l, j: (l, 0, j))],
        out_specs=pl.BlockSpec((1, rows, tn), lambda l, j: (l, 0, j)),
        out_shape=jax.ShapeDtypeStruct((depth, rows, n), F32),
        compiler_params=_params(("arbitrary", "arbitrary")),
        name="modulation",
    )(cond, w_ada, b_ada.reshape(depth, 1, n))


def _mod_spec(mod):
    if mod.shape[0] == 1:
        return pl.BlockSpec((1, 6, D_MODEL), lambda b, i: (0, 0, 0))
    return pl.BlockSpec((1, 6, D_MODEL), lambda b, i: (b, 0, 0))


def _head_rmsnorm(x, gain):
    rows, width = x.shape
    low = lax.broadcasted_iota(jnp.int32, (rows, LANES), 1) < HEAD_DIM
    out = []
    for b in range(width // LANES):
        blk = x[:, b * LANES:(b + 1) * LANES]
        sq = blk * blk
        lo = jnp.sum(jnp.where(low, sq, 0.0), axis=-1, keepdims=True)
        hi = jnp.sum(jnp.where(low, 0.0, sq), axis=-1, keepdims=True)
        ms = jnp.where(low, lo, hi) * (1.0 / HEAD_DIM)
        out.append(blk * lax.rsqrt(ms + EPS) * gain[:, b * LANES:(b + 1) * LANES])
    return jnp.concatenate(out, axis=-1)


def _rope(x, cos, sin_signed):
    rows, width = x.shape
    lane = lax.broadcasted_iota(jnp.int32, (rows, LANES), 1)
    second = (lane % (2 * AX_PAIRS)) >= AX_PAIRS
    out = []
    for b in range(width // LANES):
        blk = x[:, b * LANES:(b + 1) * LANES]
        partner = jnp.where(second, pltpu.roll(blk, AX_PAIRS, 1), pltpu.roll(blk, LANES - AX_PAIRS, 1))
        out.append(blk * cos + partner * sin_signed)
    return jnp.concatenate(out, axis=-1)


def _even_pre_kernel(*refs, rope):
    if rope:
        x_ref, mod_ref, g_ref, w_ref, qg_ref, kg_ref, cos_ref, sin_ref, f_ref, q_ref, k_ref, v_ref = refs
    else:
        x_ref, mod_ref, g_ref, w_ref, qg_ref, kg_ref, f_ref, q_ref, k_ref, v_ref = refs
    h = _modnorm(x_ref[0], g_ref[...], mod_ref[0, 0:1, :], mod_ref[0, 1:2, :])
    p = _dot(h.astype(BF16), w_ref[...])
    f_ref[0] = p[:, :FOURIER_W]
    q = _head_rmsnorm(p[:, FOURIER_W:FOURIER_W + Q_W], qg_ref[...])
    k = _head_rmsnorm(p[:, FOURIER_W + Q_W:FOURIER_W + Q_W + KV_W], kg_ref[...])
    if rope:
        q = _rope(q, cos_ref[...], sin_ref[...])
        k = _rope(k, cos_ref[...], sin_ref[...])
    q_ref[0] = (q * (HEAD_DIM ** -0.5)).astype(BF16)
    k_ref[0] = k
    v_ref[0] = p[:, FOURIER_W + Q_W + KV_W:]


def _rope_tables(seq):
    pos = np.arange(seq)
    freqs = ROPE_THETA ** (-np.arange(AX_PAIRS, dtype=np.float64) / AX_PAIRS)
    ang = np.zeros((seq, HEAD_DIM))
    sign = np.zeros((HEAD_DIM,))
    for a, p_a in enumerate((pos // GRID_W, pos % GRID_W)):
        for t in range(2):
            lo = a * 2 * AX_PAIRS + t * AX_PAIRS
            ang[:, lo:lo + AX_PAIRS] = p_a[:, None] * freqs[None, :]
            sign[lo:lo + AX_PAIRS] = -1.0 if t == 0 else 1.0
    cos = np.tile(np.cos(ang), (1, 2)).astype(np.float32)
    sin = np.tile(np.sin(ang) * sign[None, :], (1, 2)).astype(np.float32)
    return jnp.asarray(cos), jnp.asarray(sin)


def _even_pre(x, mod, g, w_in, qg, kg, rope):
    bsz, seq, d = x.shape
    tm = TOKEN_BLOCK
    tok = lambda w: pl.BlockSpec((1, tm, w), lambda b, i: (b, i, 0))
    in_specs = [tok(d), _mod_spec(mod), _const_spec((1, d)), _const_spec((d, EVEN_IN)),
                _const_spec((1, Q_W)), _const_spec((1, KV_W))]
    args = [x, mod, g, w_in, qg, kg]
    if rope:
        cos, sin = _rope_tables(seq)
        in_specs += [pl.BlockSpec((tm, LANES), lambda b, i: (i, 0))] * 2
        args += [cos, sin]
    return pl.pallas_call(
        functools.partial(_even_pre_kernel, rope=rope),
        grid=(bsz, seq // tm),
        in_specs=in_specs,
        out_specs=[tok(FOURIER_W), tok(Q_W), tok(KV_W), tok(KV_W)],
        out_shape=[jax.ShapeDtypeStruct((bsz, seq, FOURIER_W), F32),
                   jax.ShapeDtypeStruct((bsz, seq, Q_W), BF16),
                   jax.ShapeDtypeStruct((bsz, seq, KV_W), F32),
                   jax.ShapeDtypeStruct((bsz, seq, KV_W), F32)],
        compiler_params=_params(("arbitrary", "arbitrary")),
        name="even_pre",
    )(*args)


def _attn_kernel(*refs, seq, has_ctx):
    if has_ctx:
        q_ref, k_ref, v_ref, ck_ref, cv_ref, f_ref, cs_ref, cbd_ref, o_ref, kop, vop, g_scr, cs_scr = refs
    else:
        q_ref, k_ref, v_ref, f_ref, cs_ref, cbd_ref, o_ref, kop, vop, g_scr, cs_scr = refs
    tq = q_ref.shape[1]

    @pl.when((pl.program_id(0) == 0) & (pl.program_id(1) == 0))
    def _round_tables():
        for r0 in range(0, seq, tq):
            cs_scr[r0:r0 + tq, :] = cs_ref[r0:r0 + tq, :].astype(BF16)

    @pl.when(pl.program_id(1) == 0)
    def _prepare():
        def put(kx, vx, r0):
            n = kx.shape[0]
            low = lax.broadcasted_iota(jnp.int32, (n, LANES), 1) < HEAD_DIM
            for jp in range(N_KV_HEADS // 2):
                kp = kx[:, jp * LANES:(jp + 1) * LANES]
                vp = vx[:, jp * LANES:(jp + 1) * LANES]
                k_lo = jnp.where(low, kp, 0.0)
                k_hi = jnp.where(low, 0.0, kp)
                kop[4 * jp + 0, r0:r0 + n, :] = k_lo.astype(BF16)
                kop[4 * jp + 1, r0:r0 + n, :] = pltpu.roll(k_lo, HEAD_DIM, 1).astype(BF16)
                kop[4 * jp + 2, r0:r0 + n, :] = pltpu.roll(k_hi, HEAD_DIM, 1).astype(BF16)
                kop[4 * jp + 3, r0:r0 + n, :] = k_hi.astype(BF16)
                vop[2 * jp + 0, r0:r0 + n, :] = vp.astype(BF16)
                vop[2 * jp + 1, r0:r0 + n, :] = pltpu.roll(vp, HEAD_DIM, 1).astype(BF16)

        put(k_ref[0], v_ref[0], 0)
        if has_ctx:
            put(ck_ref[0], cv_ref[0], seq)
        f = f_ref[0].astype(BF16)
        g_scr[0:seq, :] = _dot(f, cbd_ref[0].astype(BF16)).astype(BF16)
        g_scr[seq:2 * seq, :] = _dot(f, cbd_ref[1].astype(BF16)).astype(BF16)

    r0 = pl.multiple_of(pl.program_id(1) * tq, tq)
    o_ref[0, :, 0:FOURIER_W] = _dot(cs_scr[pl.ds(r0, tq), :], g_scr[...]).astype(BF16)
    q = q_ref[0]
    low = lax.broadcasted_iota(jnp.int32, (tq, LANES), 1) < HEAD_DIM
    for qp in range(N_Q_HEADS // 2):
        qpair = q[:, qp * LANES:(qp + 1) * LANES]
        outs = []
        for half in range(2):
            j = (2 * qp + half) // (N_Q_HEADS // N_KV_HEADS)
            s = _dot_nt(qpair, kop[2 * j + half])
            m = jnp.max(s, axis=-1, keepdims=True)
            p = jnp.exp(s - m)
            l = jnp.sum(p, axis=-1, keepdims=True)
            vmat = vop[2 * (j // 2) + (0 if j % 2 == half else 1)]
            outs.append(_dot(p.astype(BF16), vmat) / l)
        o_ref[0, :, FOURIER_W + qp * LANES:FOURIER_W + (qp + 1) * LANES] = (
            jnp.where(low, outs[0], outs[1]).astype(BF16))


def _dft_tables(seq):
    s = np.arange(seq)
    ang = 2.0 * np.pi * ((s[:, None] * s[None, :]) % seq) / seq
    pos = np.concatenate([np.cos(ang), -np.sin(ang)], axis=1) / math.sqrt(seq)
    c = np.arange(FOURIER_GROUP_W)
    ang_c = 2.0 * np.pi * ((c[:, None] * c[None, :]) % FOURIER_GROUP_W) / FOURIER_GROUP_W
    eye = np.eye(FOURIER_W // FOURIER_GROUP_W)
    chan = np.stack([np.kron(eye, np.cos(ang_c)), np.kron(eye, np.sin(ang_c))]) / math.sqrt(FOURIER_GROUP_W)
    return jnp.asarray(pos, dtype=F32), jnp.asarray(chan, dtype=F32)


def _even_mixer(q, k, v, f, ctx):
    bsz, seq, _ = q.shape
    tq = TOKEN_BLOCK
    has_ctx = ctx is not None
    past = ctx[0].shape[1] if has_ctx else 0
    kt = seq + past
    pos_dft, chan_dft = _dft_tables(seq)
    full = lambda n, w: pl.BlockSpec((1, n, w), lambda b, i: (b, 0, 0))
    in_specs = [pl.BlockSpec((1, tq, Q_W), lambda b, i: (b, i, 0)), full(seq, KV_W), full(seq, KV_W)]
    args = [q, k, v]
    if has_ctx:
        in_specs += [full(past, KV_W), full(past, KV_W)]
        args += list(ctx)
    in_specs += [full(seq, FOURIER_W), _const_spec((seq, 2 * seq)),
                 _const_spec((2, FOURIER_W, FOURIER_W))]
    args += [f, pos_dft, chan_dft]
    return pl.pallas_call(
        functools.partial(_attn_kernel, seq=seq, has_ctx=has_ctx),
        grid=(bsz, seq // tq),
        in_specs=in_specs,
        out_specs=pl.BlockSpec((1, tq, D_MODEL), lambda b, i: (b, i, 0)),
        out_shape=jax.ShapeDtypeStruct((bsz, seq, D_MODEL), BF16),
        scratch_shapes=[pltpu.VMEM((2 * N_KV_HEADS, kt, LANES), BF16),
                        pltpu.VMEM((N_KV_HEADS, kt, LANES), BF16),
                        pltpu.VMEM((2 * seq, FOURIER_W), BF16),
                        pltpu.VMEM((seq, 2 * seq), BF16)],
        compiler_params=_params(("arbitrary", "arbitrary")),
        name="even_mixer",
    )(*args)


def _post_ffn_kernel(*refs, odd):
    if odd:
        (x_ref, ylru_ref, g_ref, yssd_ref, z_ref, sn_ref, mod_ref, wout_ref, gf_ref,
         w1_ref, w3_ref, w2_ref, o_ref) = refs
        y_lru = ylru_ref[0] * _gelu_tanh(g_ref[0])
        y = yssd_ref[0] * _silu(z_ref[0])
        ms = jnp.mean(y * y, axis=-1, keepdims=True)
        y_ssd = y * lax.rsqrt(ms + EPS) * sn_ref[...]
        mix = jnp.concatenate([y_lru, y_ssd], axis=-1).astype(BF16)
    else:
        x_ref, mix_ref, mod_ref, wout_ref, gf_ref, w1_ref, w3_ref, w2_ref, o_ref = refs
        mix = mix_ref[0]
    x = x_ref[0] + mod_ref[0, 2:3, :] * _dot(mix, wout_ref[...])
    h = _modnorm(x, gf_ref[...], mod_ref[0, 3:4, :], mod_ref[0, 4:5, :]).astype(BF16)
    a = (_silu(_dot(h, w1_ref[...])) * _dot(h, w3_ref[...])).astype(BF16)
    o_ref[0] = x + mod_ref[0, 5:6, :] * _dot(a, w2_ref[...])


def _post_ffn(x, mix_args, mod, w_out, g_ffn, w1, w3, w2, odd, ssd_norm=None):
    bsz, seq, d = x.shape
    tm = TOKEN_BLOCK
    tok = lambda w: pl.BlockSpec((1, tm, w), lambda b, i: (b, i, 0))
    if odd:
        ylru, g, yssd, z = mix_args
        in_specs = [tok(d), tok(LRU_W), tok(LRU_W), tok(SSD_INNER), tok(SSD_INNER), _const_spec((1, SSD_INNER))]
        args = [x, ylru, g, yssd, z, ssd_norm]
    else:
        in_specs = [tok(d), tok(d)]
        args = [x, mix_args]
    in_specs += [_mod_spec(mod), _const_spec(w_out.shape), _const_spec((1, d)),
                 _const_spec(w1.shape), _const_spec(w3.shape), _const_spec(w2.shape)]
    args += [mod, w_out, g_ffn, w1, w3, w2]
    return pl.pallas_call(
        functools.partial(_post_ffn_kernel, odd=odd),
        grid=(bsz, seq // tm),
        in_specs=in_specs,
        out_specs=tok(d),
        out_shape=jax.ShapeDtypeStruct((bsz, seq, d), F32),
        compiler_params=_params(("arbitrary", "arbitrary")),
        name="post_ffn_odd" if odd else "post_ffn_even",
    )(*args)


def _odd_pre_kernel(x_ref, mod_ref, g_ref, w_ref, gate_ref, xl_ref, z_ref, xbc_ref, dt_ref):
    h = _modnorm(x_ref[0], g_ref[...], mod_ref[0, 0:1, :], mod_ref[0, 1:2, :])
    p = _dot(h.astype(BF16), w_ref[...])
    gate_ref[0] = p[:, 0:LRU_W]
    xl_ref[0] = p[:, LRU_W:2 * LRU_W]
    z_ref[0] = p[:, 2 * LRU_W:2 * LRU_W + SSD_INNER]
    xbc_ref[0] = p[:, 2 * LRU_W + SSD_INNER:2 * LRU_W + SSD_INNER + SSD_CONV_CH]
    dt_ref[0] = p[:, 2 * LRU_W + SSD_INNER + SSD_CONV_CH:]


def _odd_pre(x, mod, g, w_in):
    bsz, seq, d = x.shape
    tm = TOKEN_BLOCK
    tok = lambda w: pl.BlockSpec((1, tm, w), lambda b, i: (b, i, 0))
    widths = [LRU_W, LRU_W, SSD_INNER, SSD_CONV_CH, LANES]
    return pl.pallas_call(
        _odd_pre_kernel,
        grid=(bsz, seq // tm),
        in_specs=[tok(d), _mod_spec(mod), _const_spec((1, d)), _const_spec((d, ODD_IN_PAD))],
        out_specs=[tok(w) for w in widths],
        out_shape=[jax.ShapeDtypeStruct((bsz, seq, w), F32) for w in widths],
        compiler_params=_params(("arbitrary", "arbitrary")),
        name="odd_pre",
    )(x, mod, g, w_in)


def _conv_rows(x_ref, r0, rows, seq, w_ref, b_ref):
    main = x_ref[0, pl.ds(r0, rows), :]
    prev8 = x_ref[0, pl.ds(pl.multiple_of(jnp.maximum(r0 - SUBLANES, 0), SUBLANES), SUBLANES), :]
    next8 = x_ref[0, pl.ds(pl.multiple_of(jnp.minimum(r0 + rows, seq - SUBLANES), SUBLANES), SUBLANES), :]
    has_prev = r0 > 0
    has_next = r0 + rows < seq
    before = jnp.where(has_prev, prev8[SUBLANES - 1:SUBLANES, :], 0.0)
    after0 = jnp.where(has_next, next8[0:1, :], 0.0)
    after1 = jnp.where(has_next, next8[1:2, :], 0.0)
    row = lax.broadcasted_iota(jnp.int32, (rows, 1), 0)
    xm1 = jnp.where(row == 0, before, pltpu.roll(main, 1, 0))
    xp1 = jnp.where(row == rows - 1, after0, pltpu.roll(main, rows - 1, 0))
    xp2 = jnp.where(row == rows - 2, after0, jnp.where(row == rows - 1, after1, pltpu.roll(main, rows - 2, 0)))
    return (xm1 * w_ref[0:1, :] + main * w_ref[1:2, :] + xp1 * w_ref[2:3, :] + xp2 * w_ref[3:4, :]
            + b_ref[...])


def _lru_kernel(xl_ref, cw_ref, cb_ref, wg_ref, bg_ref, lam_ref, h0_ref, y_ref, hout_ref, a_scr, b_scr, *, seq):
    rows = TOKEN_BLOCK
    sp = _softplus(-lam_ref[...])

    def gates(c, carry):
        r0 = pl.multiple_of(c * rows, rows)
        xc = _conv_rows(xl_ref, r0, rows, seq, cw_ref, cb_ref)
        for lb in range(LRU_W // LANES):
            sl = slice(lb * LANES, (lb + 1) * LANES)
            xb = xc[:, sl]
            gt = _dot(xb.astype(BF16), wg_ref[lb]) + bg_ref[lb]
            for d in range(2):
                r = jax.nn.sigmoid(gt[:, 2 * d * LANES:(2 * d + 1) * LANES])
                i = jax.nn.sigmoid(gt[:, (2 * d + 1) * LANES:(2 * d + 2) * LANES])
                log_a = -LRU_C * r * sp[d:d + 1, sl]
                a = jnp.exp(log_a)
                b = jnp.sqrt(jnp.tanh(-log_a) * (a * a + 1.0)) * (i * xb)
                a_scr[d, pl.ds(r0, rows), sl] = a
                b_scr[d, pl.ds(r0, rows), sl] = b
        y_ref[0, pl.ds(r0, rows), :] = jnp.zeros((rows, LRU_W), F32)
        return carry

    lax.fori_loop(0, seq // rows, gates, 0)

    nb = seq // SUBLANES
    row = lax.broadcasted_iota(jnp.int32, (SUBLANES, LRU_W), 0)

    def scan(i, carry):
        hf, hb = carry
        rf = pl.multiple_of(i * SUBLANES, SUBLANES)
        rb = pl.multiple_of((nb - 1 - i) * SUBLANES, SUBLANES)
        a = a_scr[0, pl.ds(rf, SUBLANES), :]
        b = b_scr[0, pl.ds(rf, SUBLANES), :]
        for k in (1, 2, 4):
            keep = row >= k
            b = a * jnp.where(keep, pltpu.roll(b, k, 0), 0.0) + b
            a = a * jnp.where(keep, pltpu.roll(a, k, 0), 1.0)
        h = a * hf + b
        y_ref[0, pl.ds(rf, SUBLANES), :] += h
        hf = h[SUBLANES - 1:SUBLANES, :]
        a = a_scr[1, pl.ds(rb, SUBLANES), :]
        b = b_scr[1, pl.ds(rb, SUBLANES), :]
        for k in (1, 2, 4):
            keep = row < SUBLANES - k
            b = a * jnp.where(keep, pltpu.roll(b, SUBLANES - k, 0), 0.0) + b
            a = a * jnp.where(keep, pltpu.roll(a, SUBLANES - k, 0), 1.0)
        h = a * hb + b
        y_ref[0, pl.ds(rb, SUBLANES), :] += h
        hb = h[0:1, :]
        return hf, hb

    hf, hb = lax.fori_loop(0, nb, scan, (h0_ref[0, 0:1, :], h0_ref[0, 1:2, :]))
    hout_ref[0, 0:1, :] = hf
    hout_ref[0, 1:2, :] = hb


def _lru(xl, cw, cb, wg, bg, lam, h0):
    bsz, seq, _ = xl.shape
    per_seq = lambda n, w: pl.BlockSpec((1, n, w), lambda b: (b, 0, 0))
    return pl.pallas_call(
        functools.partial(_lru_kernel, seq=seq),
        grid=(bsz,),
        in_specs=[per_seq(seq, LRU_W), _const_spec(cw.shape), _const_spec(cb.shape), _const_spec(wg.shape),
                  _const_spec(bg.shape), _const_spec(lam.shape), per_seq(2, LRU_W)],
        out_specs=[per_seq(seq, LRU_W), per_seq(2, LRU_W)],
        out_shape=[jax.ShapeDtypeStruct((bsz, seq, LRU_W), F32), jax.ShapeDtypeStruct((bsz, 2, LRU_W), F32)],
        scratch_shapes=[pltpu.VMEM((2, seq, LRU_W), F32), pltpu.VMEM((2, seq, LRU_W), F32)],
        compiler_params=_params(("arbitrary",)),
        name="rglru",
    )(xl, cw, cb, wg, bg, lam, h0)


def _ssd_kernel(xbc_ref, dt_ref, cw_ref, cb_ref, dtb_ref, alog_ref, dexp_ref, ef_ref, eb_ref, h0_ref,
                y_ref, hout_ref,
                xs_scr, b_scr, c_scr, cs_scr, dts_scr, cst_scr, dtt_scr, bt_scr, st_scr, *, seq):
    q = SSD_Q
    nc = seq // q
    lane = lax.broadcasted_iota(jnp.int32, (q, LANES), 1)
    qi = lax.broadcasted_iota(jnp.int32, (q, q), 0)
    ki = lax.broadcasted_iota(jnp.int32, (q, q), 1)
    tri_lower = jnp.where(ki <= qi, 1.0, 0.0).astype(BF16)
    tri_upper = jnp.where(ki >= qi, 1.0, 0.0).astype(BF16)
    a_neg = jnp.where(lane[0:1, :] < 2 * SSD_HEADS, -jnp.exp(alog_ref[...]), 0.0)

    def prepare(c, carry):
        r0 = pl.multiple_of(c * q, q)
        rows = pl.ds(r0, q)
        xc = _silu(_conv_rows(xbc_ref, r0, q, seq, cw_ref, cb_ref))
        xs = xc[:, :SSD_INNER]
        bm = xc[:, SSD_INNER:SSD_INNER + LANES]
        xs_scr[rows, :] = xs
        b_scr[rows, :] = bm
        c_scr[rows, :] = xc[:, SSD_INNER + LANES:]
        dts = _softplus(dt_ref[0, rows, :] + dtb_ref[...])
        da = dts * a_neg
        cs = jnp.where(lane < SSD_HEADS, _dot3(tri_lower, da), _dot3(tri_upper, da))
        cs_scr[rows, :] = cs
        dts_scr[rows, :] = dts
        cst_scr[c] = cs.T
        dtt_scr[c] = dts.T
        bt_scr[c] = bm.T
        y_ref[0, rows, :] = dexp_ref[...] * xs
        return carry

    lax.fori_loop(0, nc, prepare, 0)
    st_scr[...] = h0_ref[0]

    low = lane < SSD_STATE
    col = lax.broadcasted_iota(jnp.int32, (SSD_STATE, SSD_INNER), 1)
    first_group = col < SSD_INNER // 2

    def chunk(c, fwd):
        d = 0 if fwd else 1
        r0 = pl.multiple_of(c * q, q)
        rows = pl.ds(r0, q)
        cs = cs_scr[rows, :]
        dts = dts_scr[rows, :]
        cst = cst_scr[c]
        dtt = dtt_scr[c]
        xs = xs_scr[rows, :]
        xsb = xs.astype(BF16)
        bb = b_scr[rows, :].astype(BF16)
        cb_ = c_scr[rows, :].astype(BF16)
        zero = jnp.zeros_like(cb_)
        cbg = [_dot_nt(jnp.where(low, cb_, zero), bb), _dot_nt(jnp.where(low, zero, cb_), bb)]
        mask = (ki <= qi) if fwd else (ki >= qi)
        expand = ef_ref[...] if fwd else eb_ref[...]
        st = st_scr[d]
        hh = jnp.concatenate([jnp.where(first_group, st, 0.0), jnp.where(first_group, 0.0, st)], axis=0)
        y = _dot(cb_, hh.astype(BF16)) * _dot2r(jnp.exp(cs), expand)
        pairs = []
        for pp in range(SSD_HEADS // 2):
            ms = []
            for h in (2 * pp, 2 * pp + 1):
                l = d * SSD_HEADS + h
                decay = jnp.where(mask, jnp.exp(cs[:, l:l + 1] - cst[l:l + 1, :]), 0.0)
                ms.append((cbg[pp // (SSD_HEADS // 4)] * decay * dtt[l:l + 1, :]).astype(BF16))
            xp = xsb[:, pp * LANES:(pp + 1) * LANES]
            zx = jnp.zeros_like(xp)
            x2 = jnp.concatenate([jnp.where(low, xp, zx), jnp.where(low, zx, xp)], axis=0)
            pairs.append(_dot(jnp.concatenate(ms, axis=1), x2))
        y_ref[0, rows, :] += y + jnp.concatenate(pairs, axis=1)
        tot = cs[q - 1:q, :] if fwd else cs[0:1, :]
        mine = (lane >= d * SSD_HEADS) & (lane < (d + 1) * SSD_HEADS)
        w = jnp.exp(jnp.where(mine, tot - cs, 0.0)) * dts
        xw = (xs * _dot2r(w, expand)).astype(BF16)
        st2 = _dot(bt_scr[c].astype(BF16), xw)
        st_new = jnp.where(first_group, st2[:SSD_STATE], st2[SSD_STATE:])
        dec = _dot2r(jnp.broadcast_to(jnp.exp(tot), (SUBLANES, LANES)), expand)[0:1, :]
        st_scr[d] = dec * st + st_new

    def body(i, carry):
        chunk(i, True)
        chunk(nc - 1 - i, False)
        return carry

    lax.fori_loop(0, nc, body, 0)
    hout_ref[0] = st_scr[...]


def _ssd(xbc, dt, cw, cb, dtb, alog, dexp, h0):
    bsz, seq, _ = xbc.shape
    nc = seq // SSD_Q
    expand = np.zeros((2, LANES, SSD_INNER), np.float32)
    for d in range(2):
        for h in range(SSD_HEADS):
            expand[d, d * SSD_HEADS + h, h * 64:(h + 1) * 64] = 1.0
    expand = jnp.asarray(expand, dtype=BF16)
    per_seq = lambda n, w: pl.BlockSpec((1, n, w), lambda b: (b, 0, 0))
    state_spec = pl.BlockSpec((1, 2, SSD_STATE, SSD_INNER), lambda b: (b, 0, 0, 0))
    return pl.pallas_call(
        functools.partial(_ssd_kernel, seq=seq),
        grid=(bsz,),
        in_specs=[per_seq(seq, SSD_CONV_CH), per_seq(seq, LANES), _const_spec(cw.shape), _const_spec(cb.shape),
                  _const_spec(dtb.shape), _const_spec(alog.shape), _const_spec(dexp.shape),
                  _const_spec((LANES, SSD_INNER)), _const_spec((LANES, SSD_INNER)), state_spec],
        out_specs=[per_seq(seq, SSD_INNER), state_spec],
        out_shape=[jax.ShapeDtypeStruct((bsz, seq, SSD_INNER), F32),
                   jax.ShapeDtypeStruct((bsz, 2, SSD_STATE, SSD_INNER), F32)],
        scratch_shapes=[pltpu.VMEM((seq, SSD_INNER), F32), pltpu.VMEM((seq, LANES), F32),
                        pltpu.VMEM((seq, LANES), F32), pltpu.VMEM((seq, LANES), F32),
                        pltpu.VMEM((seq, LANES), F32), pltpu.VMEM((nc, LANES, SSD_Q), F32),
                        pltpu.VMEM((nc, LANES, SSD_Q), F32), pltpu.VMEM((nc, LANES, SSD_Q), F32),
                        pltpu.VMEM((2, SSD_STATE, SSD_INNER), F32)],
        compiler_params=_params(("arbitrary",)),
        name="ssd",
    )(xbc, dt, cw, cb, dtb, alog, dexp, expand[0], expand[1], h0)


def _lru_gate_weights(wa, ba, wx, bx):
    eye = jnp.eye(2, dtype=F32)

    def pair_blocks(w):
        w4 = w.reshape(4, 2, 64, 64)
        return jnp.einsum("laij,ab->laibj", w4, eye).reshape(4, LANES, LANES)

    mats = [pair_blocks(w) for w in (wa[0], wx[0], wa[1], wx[1])]
    wg = jnp.concatenate(mats, axis=-1).astype(BF16)
    bias = [b.reshape(4, 1, LANES) for b in (ba[0], bx[0], ba[1], bx[1])]
    return wg, jnp.concatenate(bias, axis=-1)


def _trunk(x, mods, p, ctx):
    is_ctx = ctx is None
    bsz, seq, _ = x.shape
    f, q, k, v = _even_pre(x, mods[0], p["norm_mix"][0], p["w_in_even"], p["q_gain"], p["k_gain"], rope=not is_ctx)
    mix = _even_mixer(q, k, v, f, None if is_ctx else (ctx["k"], ctx["v"]))
    x = _post_ffn(x, mix, mods[0], p["w_out_even"], p["norm_ffn"][0], p["w1"][0], p["w3"][0], p["w2"][0], odd=False)
    gate, xl, z, xbc, dt = _odd_pre(x, mods[1], p["norm_mix"][1], p["w_in_odd"])
    if is_ctx:
        h0_lru = jnp.zeros((bsz, 2, LRU_W), F32)
        h0_ssd = jnp.zeros((bsz, 2, SSD_STATE, SSD_INNER), F32)
    else:
        h0_lru, h0_ssd = ctx["lru"], ctx["ssd"]
    ylru, lru_state = _lru(xl, p["conv_lru_w"], p["conv_lru_b"], p["lru_wg"], p["lru_bg"], p["lru_lambda"], h0_lru)
    yssd, ssd_state = _ssd(xbc, dt, p["conv_ssd_w"], p["conv_ssd_b"], p["ssd_dtb"], p["ssd_alog"], p["ssd_dexp"],
                           h0_ssd)
    x = _post_ffn(x, (ylru, gate, yssd, z), mods[1], p["w_out_odd"], p["norm_ffn"][1], p["w1"][1], p["w3"][1],
                  p["w2"][1], odd=True, ssd_norm=p["ssd_norm"])
    return x, k, v, lru_state, ssd_state


def kernel(x_prompt, x_sample, c, cache_k, cache_v, state_lru, state_ssd, c_ctx, w_ada, b_ada, norm_mix, norm_ffn,
           w_in_even, q_norm, k_norm, w_out_even, w_in_odd, conv_lru_w, conv_lru_b, lru_wa, lru_ba, lru_wx, lru_bx,
           lru_lambda, conv_ssd_w, conv_ssd_b, ssd_dt_bias, ssd_a_log, ssd_d, ssd_norm, w_out_odd, ffn_w1, ffn_w3,
           ffn_w2):
    bsz, seq, d = x_prompt.shape
    dec_b = x_sample.shape[0]
    past = cache_k.shape[2]

    cond_rows = 16
    cond = jnp.concatenate([c, c_ctx[None, :], jnp.zeros((cond_rows - dec_b - 1, d), F32)], axis=0)
    mod = _modulation(cond, w_ada, b_ada).reshape(2, cond_rows, 6, d)
    mods_sample = [mod[l, :dec_b] for l in range(2)]
    mods_prompt = [mod[l, dec_b:dec_b + 1] for l in range(2)]

    lru_wg, lru_bg = _lru_gate_weights(lru_wa[0], lru_ba[0], lru_wx[0], lru_bx[0])
    pad32 = lambda a: jnp.pad(a.reshape(1, 2 * SSD_HEADS), ((0, 0), (0, LANES - 2 * SSD_HEADS)))
    p = dict(
        norm_mix=norm_mix.reshape(2, 1, d), norm_ffn=norm_ffn.reshape(2, 1, d),
        w_in_even=w_in_even[0].astype(BF16),
        q_gain=jnp.tile(q_norm[0], N_Q_HEADS)[None, :], k_gain=jnp.tile(k_norm[0], N_KV_HEADS)[None, :],
        w_out_even=w_out_even[0].astype(BF16),
        w_in_odd=jnp.pad(w_in_odd[0].astype(BF16), ((0, 0), (0, ODD_IN_PAD - ODD_IN))),
        conv_lru_w=conv_lru_w[0], conv_lru_b=conv_lru_b[0][None, :],
        lru_wg=lru_wg, lru_bg=lru_bg, lru_lambda=lru_lambda[0],
        conv_ssd_w=conv_ssd_w[0], conv_ssd_b=conv_ssd_b[0][None, :],
        ssd_dtb=pad32(ssd_dt_bias[0]), ssd_alog=pad32(ssd_a_log[0]),
        ssd_dexp=jnp.repeat(ssd_d[0], SSD_INNER // SSD_HEADS)[None, :],
        ssd_norm=ssd_norm[0][None, :],
        w_out_odd=w_out_odd[0].astype(BF16),
        w1=ffn_w1.astype(BF16), w3=ffn_w3.astype(BF16), w2=ffn_w2.astype(BF16),
    )

    y_prompt, k_new, v_new, lru_new, ssd_new = _trunk(x_prompt, mods_prompt, p, None)
    ctx = dict(
        k=cache_k[:, 0].reshape(dec_b, past, KV_W), v=cache_v[:, 0].reshape(dec_b, past, KV_W),
        lru=state_lru[:, 0],
        ssd=state_ssd[:, 0].transpose(0, 1, 4, 2, 3).reshape(dec_b, 2, SSD_STATE, SSD_INNER),
    )
    y_sample, _, _, _, _ = _trunk(x_sample, mods_sample, p, ctx)

    new_k = k_new.reshape(bsz, 1, seq, N_KV_HEADS, HEAD_DIM)
    new_v = v_new.reshape(bsz, 1, seq, N_KV_HEADS, HEAD_DIM)
    new_lru = lru_new.reshape(bsz, 1, 2, LRU_W)
    new_ssd = (ssd_new.reshape(bsz, 2, SSD_STATE, SSD_HEADS, SSD_INNER // SSD_HEADS)
               .transpose(0, 1, 3, 4, 2).reshape(bsz, 1, 2, SSD_HEADS, SSD_INNER // SSD_HEADS, SSD_STATE))
    return (y_prompt, y_sample, new_k, new_v, new_lru, new_ssd)
```

```python
import functools
import math

import numpy as np
import jax
import jax.numpy as jnp
from jax import lax
from jax.experimental import pallas as pl
from jax.experimental.pallas import tpu as pltpu

F32 = jnp.float32
BF16 = jnp.bfloat16

D_MODEL = 1024
EPS = 1e-6
GRID_W = 64
HEAD_DIM = 64
N_Q_HEADS = 12
N_KV_HEADS = 4
FOURIER_W = 256
FOURIER_GROUP_W = 64
Q_W = N_Q_HEADS * HEAD_DIM
KV_W = N_KV_HEADS * HEAD_DIM
EVEN_IN = FOURIER_W + Q_W + 2 * KV_W
ROPE_THETA = 10000.0
AX_PAIRS = HEAD_DIM // 4
LRU_W = 512
LRU_C = 8.0
SSD_INNER = 1024
SSD_HEADS = 16
SSD_STATE = 64
SSD_CONV_CH = SSD_INNER + 4 * SSD_STATE
ODD_IN = 2 * LRU_W + SSD_INNER + SSD_CONV_CH + 2 * SSD_HEADS
ODD_IN_PAD = 3456
D_FF = 2816
LANES = 128
SUBLANES = 8
TOKEN_BLOCK = 256
SSD_Q = 128
V_ROWS = HEAD_DIM + 16
VMEM_LIMIT = 56 * 1024 * 1024


def _params(sem, vmem=VMEM_LIMIT):
    return pltpu.CompilerParams(dimension_semantics=sem, vmem_limit_bytes=vmem)


def _const_spec(shape):
    nd = len(shape)
    return pl.BlockSpec(shape, lambda *_: (0,) * nd, pipeline_mode=pl.Buffered(1))


def _dot(a, b):
    return jnp.dot(a, b, preferred_element_type=F32)


def _dot_nt(a, b):
    return lax.dot_general(a, b, (((1,), (1,)), ((), ())), preferred_element_type=F32)


def _dot3(m, a):
    hi = a.astype(BF16)
    r1 = a - hi.astype(F32)
    mid = r1.astype(BF16)
    lo = (r1 - mid.astype(F32)).astype(BF16)
    return _dot(m, hi) + _dot(m, mid) + _dot(m, lo)


def _silu(x):
    return x * jax.nn.sigmoid(x)


def _sigmoid_tanh(x):
    return 0.5 * jnp.tanh(0.5 * x) + 0.5


def _softplus(x):
    return jnp.maximum(x, 0.0) + jnp.log1p(jnp.exp(-jnp.abs(x)))


def _gelu_tanh(x):
    c = math.sqrt(2.0 / math.pi)
    return 0.5 * x * (1.0 + jnp.tanh(c * (x + 0.044715 * (x * x * x))))


def _modnorm(x, g, shift, scale):
    ms = jnp.mean(x * x, axis=-1, keepdims=True)
    y = x * lax.rsqrt(ms + EPS) * g
    return y * (1.0 + scale) + shift


def _mod_kernel(cond_ref, w_ref, b_ref, o_ref):
    s = _silu(cond_ref[...])
    o_ref[0] = _dot(s.astype(BF16), w_ref[0].astype(BF16)) + b_ref[0]


def _modulation(cond, w_ada, b_ada):
    depth, d, n = w_ada.shape
    rows = cond.shape[0]
    tn = 1536
    return pl.pallas_call(
        _mod_kernel,
        grid=(depth, n // tn),
        in_specs=[pl.BlockSpec((rows, d), lambda l, j: (0, 0)),
                  pl.BlockSpec((1, d, tn), lambda l, j: (l, 0, j)),
                  pl.BlockSpec((1, 1, tn), lambda l, j: (l, 0, j))],
        out_specs=pl.BlockSpec((1, rows, tn), lambda l, j: (l, 0, j)),
        out_shape=jax.ShapeDtypeStruct((depth, rows, n), F32),
        compiler_params=_params(("arbitrary", "arbitrary")),
        name="modulation",
    )(cond, w_ada, b_ada.reshape(depth, 1, n))


def _mod_spec(mod):
    if mod.shape[0] == 1:
        return pl.BlockSpec((1, 6, D_MODEL), lambda b, i: (0, 0, 0))
    return pl.BlockSpec((1, 6, D_MODEL), lambda b, i: (b, 0, 0))


def _head_rmsnorm(x, gain):
    rows, width = x.shape
    low = lax.broadcasted_iota(jnp.int32, (rows, LANES), 1) < HEAD_DIM
    out = []
    for b in range(width // LANES):
        blk = x[:, b * LANES:(b + 1) * LANES]
        sq = blk * blk
        lo = jnp.sum(jnp.where(low, sq, 0.0), axis=-1, keepdims=True)
        hi = jnp.sum(jnp.where(low, 0.0, sq), axis=-1, keepdims=True)
        ms = jnp.where(low, lo, hi) * (1.0 / HEAD_DIM)
        out.append(blk * lax.rsqrt(ms + EPS) * gain[:, b * LANES:(b + 1) * LANES])
    return jnp.concatenate(out, axis=-1)


def _rope(x, cos, sin_signed):
    rows, width = x.shape
    lane = lax.broadcasted_iota(jnp.int32, (rows, LANES), 1)
    second = (lane % (2 * AX_PAIRS)) >= AX_PAIRS
    out = []
    for b in range(width // LANES):
        blk = x[:, b * LANES:(b + 1) * LANES]
        partner = jnp.where(second, pltpu.roll(blk, AX_PAIRS, 1), pltpu.roll(blk, LANES - AX_PAIRS, 1))
        out.append(blk * cos + partner * sin_signed)
    return jnp.concatenate(out, axis=-1)


def _even_pre_kernel(*refs, rope):
    if rope:
        x_ref, mod_ref, g_ref, w_ref, qg_ref, kg_ref, cos_ref, sin_ref, f_ref, q_ref, k_ref, v_ref = refs
    else:
        x_ref, mod_ref, g_ref, w_ref, qg_ref, kg_ref, f_ref, q_ref, k_ref, v_ref = refs
    h = _modnorm(x_ref[0], g_ref[...], mod_ref[0, 0:1, :], mod_ref[0, 1:2, :])
    p = _dot(h.astype(BF16), w_ref[...])
    f_ref[0] = p[:, :FOURIER_W]
    q = _head_rmsnorm(p[:, FOURIER_W:FOURIER_W + Q_W], qg_ref[...])
    k = _head_rmsnorm(p[:, FOURIER_W + Q_W:FOURIER_W + Q_W + KV_W], kg_ref[...])
    if rope:
        q = _rope(q, cos_ref[...], sin_ref[...])
        k = _rope(k, cos_ref[...], sin_ref[...])
    q_ref[0] = (q * (HEAD_DIM ** -0.5)).astype(BF16)
    k_ref[0] = k
    v_ref[0] = p[:, FOURIER_W + Q_W + KV_W:]


def _rope_tables(seq):
    pos = np.arange(seq)
    freqs = ROPE_THETA ** (-np.arange(AX_PAIRS, dtype=np.float64) / AX_PAIRS)
    ang = np.zeros((seq, HEAD_DIM))
    sign = np.zeros((HEAD_DIM,))
    for a, p_a in enumerate((pos // GRID_W, pos % GRID_W)):
        for t in range(2):
            lo = a * 2 * AX_PAIRS + t * AX_PAIRS
            ang[:, lo:lo + AX_PAIRS] = p_a[:, None] * freqs[None, :]
            sign[lo:lo + AX_PAIRS] = -1.0 if t == 0 else 1.0
    cos = np.tile(np.cos(ang), (1, 2)).astype(np.float32)
    sin = np.tile(np.sin(ang) * sign[None, :], (1, 2)).astype(np.float32)
    return jnp.asarray(cos), jnp.asarray(sin)


def _even_pre(x, mod, g, w_in, qg, kg, rope):
    bsz, seq, d = x.shape
    tm = TOKEN_BLOCK
    tok = lambda w: pl.BlockSpec((1, tm, w), lambda b, i: (b, i, 0))
    in_specs = [tok(d), _mod_spec(mod), _const_spec((1, d)), _const_spec((d, EVEN_IN)),
                _const_spec((1, Q_W)), _const_spec((1, KV_W))]
    args = [x, mod, g, w_in, qg, kg]
    if rope:
        cos, sin = _rope_tables(seq)
        in_specs += [pl.BlockSpec((tm, LANES), lambda b, i: (i, 0))] * 2
        args += [cos, sin]
    return pl.pallas_call(
        functools.partial(_even_pre_kernel, rope=rope),
        grid=(bsz, seq // tm),
        in_specs=in_specs,
        out_specs=[tok(FOURIER_W), tok(Q_W), tok(KV_W), tok(KV_W)],
        out_shape=[jax.ShapeDtypeStruct((bsz, seq, FOURIER_W), F32),
                   jax.ShapeDtypeStruct((bsz, seq, Q_W), BF16),
                   jax.ShapeDtypeStruct((bsz, seq, KV_W), F32),
                   jax.ShapeDtypeStruct((bsz, seq, KV_W), F32)],
        compiler_params=_params(("arbitrary", "arbitrary")),
        name="even_pre",
    )(*args)


def _attn_kernel(*refs, seq, has_ctx):
    if has_ctx:
        q_ref, k_ref, v_ref, ck_ref, cv_ref, f_ref, cs_ref, cbd_ref, o_ref, kop, vopt, g_scr, cs_scr, s_scr, p_scr = refs
    else:
        q_ref, k_ref, v_ref, f_ref, cs_ref, cbd_ref, o_ref, kop, vopt, g_scr, cs_scr, s_scr, p_scr = refs
    tq = q_ref.shape[1]

    @pl.when((pl.program_id(0) == 0) & (pl.program_id(1) == 0))
    def _round_tables():
        for r0 in range(0, seq, tq):
            cs_scr[r0:r0 + tq, :] = cs_ref[r0:r0 + tq, :].astype(BF16)

    @pl.when(pl.program_id(1) == 0)
    def _prepare():
        def put(kx, vx, r0):
            n = kx.shape[0]
            low = lax.broadcasted_iota(jnp.int32, (n, LANES), 1) < HEAD_DIM
            ones = jnp.ones((V_ROWS - HEAD_DIM, n), BF16)
            for jp in range(N_KV_HEADS // 2):
                kp = kx[:, jp * LANES:(jp + 1) * LANES]
                k_lo = jnp.where(low, kp, 0.0)
                k_hi = jnp.where(low, 0.0, kp)
                placed = (k_lo, pltpu.roll(k_lo, HEAD_DIM, 1), pltpu.roll(k_hi, HEAD_DIM, 1), k_hi)
                for t, kk in enumerate(placed):
                    kop[4 * jp + t, r0:r0 + n, :] = kk.astype(BF16)
                vt = vx[:, jp * LANES:(jp + 1) * LANES].T.astype(BF16)
                for t in range(2):
                    vopt[2 * jp + t, 0:HEAD_DIM, r0:r0 + n] = vt[t * HEAD_DIM:(t + 1) * HEAD_DIM]
                    vopt[2 * jp + t, HEAD_DIM:V_ROWS, r0:r0 + n] = ones

        put(k_ref[0], v_ref[0], 0)
        if has_ctx:
            put(ck_ref[0], cv_ref[0], seq)
        f = f_ref[0].astype(BF16)
        g_scr[0:seq, :] = _dot(f, cbd_ref[0].astype(BF16)).astype(BF16)
        g_scr[seq:2 * seq, :] = _dot(f, cbd_ref[1].astype(BF16)).astype(BF16)

    r0 = pl.multiple_of(pl.program_id(1) * tq, tq)
    o_ref[0, :, 0:FOURIER_W] = _dot(cs_scr[pl.ds(r0, tq), :], g_scr[...]).astype(BF16)
    q = q_ref[0]
    n_pairs = N_Q_HEADS // 2
    kv_of = lambda qp: [(2 * qp + half) // (N_Q_HEADS // N_KV_HEADS) for half in range(2)]

    def pair_scores(qp):
        qpair = q[:, qp * LANES:(qp + 1) * LANES]
        return [_dot_nt(qpair, kop[2 * kv_of(qp)[half] + half]) for half in range(2)]

    scores_next = pair_scores(0)
    for qp in range(n_pairs):
        kv = kv_of(qp)
        scores = scores_next
        if qp + 1 < n_pairs:
            scores_next = pair_scores(qp + 1)
        probs = [jnp.exp(s - jnp.max(s, axis=-1, keepdims=True)).astype(BF16) for s in scores]
        outs = []
        for half in range(2):
            o = _dot_nt(vopt[kv[half]], probs[half])
            outs.append(o[0:HEAD_DIM] / o[HEAD_DIM:HEAD_DIM + 1])
        res = jnp.concatenate(outs, axis=0)
        o_ref[0, :, FOURIER_W + qp * LANES:FOURIER_W + (qp + 1) * LANES] = res.T.astype(BF16)


def _dft_tables(seq):
    s = np.arange(seq)
    ang = 2.0 * np.pi * ((s[:, None] * s[None, :]) % seq) / seq
    pos = np.concatenate([np.cos(ang), -np.sin(ang)], axis=1) / math.sqrt(seq)
    c = np.arange(FOURIER_GROUP_W)
    ang_c = 2.0 * np.pi * ((c[:, None] * c[None, :]) % FOURIER_GROUP_W) / FOURIER_GROUP_W
    eye = np.eye(FOURIER_W // FOURIER_GROUP_W)
    chan = np.stack([np.kron(eye, np.cos(ang_c)), np.kron(eye, np.sin(ang_c))]) / math.sqrt(FOURIER_GROUP_W)
    return jnp.asarray(pos, dtype=F32), jnp.asarray(chan, dtype=F32)


def _even_mixer(q, k, v, f, ctx):
    bsz, seq, _ = q.shape
    tq = TOKEN_BLOCK
    has_ctx = ctx is not None
    past = ctx[0].shape[1] if has_ctx else 0
    kt = seq + past
    pos_dft, chan_dft = _dft_tables(seq)
    full = lambda n, w: pl.BlockSpec((1, n, w), lambda b, i: (b, 0, 0))
    in_specs = [pl.BlockSpec((1, tq, Q_W), lambda b, i: (b, i, 0)), full(seq, KV_W), full(seq, KV_W)]
    args = [q, k, v]
    if has_ctx:
        in_specs += [full(past, KV_W), full(past, KV_W)]
        args += list(ctx)
    in_specs += [full(seq, FOURIER_W), _const_spec((seq, 2 * seq)),
                 _const_spec((2, FOURIER_W, FOURIER_W))]
    args += [f, pos_dft, chan_dft]
    return pl.pallas_call(
        functools.partial(_attn_kernel, seq=seq, has_ctx=has_ctx),
        grid=(bsz, seq // tq),
        in_specs=in_specs,
        out_specs=pl.BlockSpec((1, tq, D_MODEL), lambda b, i: (b, i, 0)),
        out_shape=jax.ShapeDtypeStruct((bsz, seq, D_MODEL), BF16),
        scratch_shapes=[pltpu.VMEM((2 * N_KV_HEADS, kt, LANES), BF16),
                        pltpu.VMEM((N_KV_HEADS, V_ROWS, kt), BF16),
                        pltpu.VMEM((2 * seq, FOURIER_W), BF16),
                        pltpu.VMEM((seq, 2 * seq), BF16),
                        pltpu.VMEM((2, kt, tq), F32),
                        pltpu.VMEM((2, kt, tq), BF16)],
        compiler_params=_params(("arbitrary", "arbitrary")),
        name="even_mixer",
    )(*args)


def _post_ffn_kernel(*refs, odd):
    if odd:
        (x_ref, ylru_ref, g_ref, yssd_ref, z_ref, sn_ref, mod_ref, wout_ref, gf_ref,
         w1_ref, w3_ref, w2_ref, o_ref) = refs
        y_lru = ylru_ref[0] * _gelu_tanh(g_ref[0])
        y = yssd_ref[0] * _silu(z_ref[0])
        ms = jnp.mean(y * y, axis=-1, keepdims=True)
        y_ssd = y * lax.rsqrt(ms + EPS) * sn_ref[...]
        mix = jnp.concatenate([y_lru, y_ssd], axis=-1).astype(BF16)
    else:
        x_ref, mix_ref, mod_ref, wout_ref, gf_ref, w1_ref, w3_ref, w2_ref, o_ref = refs
        mix = mix_ref[0]
    x = x_ref[0] + mod_ref[0, 2:3, :] * _dot(mix, wout_ref[...])
    h = _modnorm(x, gf_ref[...], mod_ref[0, 3:4, :], mod_ref[0, 4:5, :]).astype(BF16)
    a = (_silu(_dot(h, w1_ref[...])) * _dot(h, w3_ref[...])).astype(BF16)
    o_ref[0] = x + mod_ref[0, 5:6, :] * _dot(a, w2_ref[...])


def _post_ffn(x, mix_args, mod, w_out, g_ffn, w1, w3, w2, layer, ssd_norm=None):
    bsz, seq, d = x.shape
    tm = TOKEN_BLOCK
    odd = layer % 2 == 1
    tok = lambda w: pl.BlockSpec((1, tm, w), lambda b, i: (b, i, 0))
    slab = lambda w: pl.BlockSpec((None,) + w.shape[1:], lambda b, i: (layer, 0, 0), pipeline_mode=pl.Buffered(1))
    if odd:
        ylru, g, yssd, z = mix_args
        in_specs = [tok(d), tok(LRU_W), tok(LRU_W), tok(SSD_INNER), tok(SSD_INNER), _const_spec((1, SSD_INNER))]
        args = [x, ylru, g, yssd, z, ssd_norm]
    else:
        in_specs = [tok(d), tok(d)]
        args = [x, mix_args]
    in_specs += [_mod_spec(mod), _const_spec(w_out.shape), _const_spec((1, d)),
                 slab(w1), slab(w3), slab(w2)]
    args += [mod, w_out, g_ffn, w1, w3, w2]
    return pl.pallas_call(
        functools.partial(_post_ffn_kernel, odd=odd),
        grid=(bsz, seq // tm),
        in_specs=in_specs,
        out_specs=tok(d),
        out_shape=jax.ShapeDtypeStruct((bsz, seq, d), F32),
        compiler_params=_params(("arbitrary", "arbitrary")),
        name="post_ffn_odd" if odd else "post_ffn_even",
    )(*args)


def _odd_pre_kernel(x_ref, mod_ref, g_ref, w_ref, gate_ref, xl_ref, z_ref, xbc_ref, dt_ref):
    h = _modnorm(x_ref[0], g_ref[...], mod_ref[0, 0:1, :], mod_ref[0, 1:2, :])
    p = _dot(h.astype(BF16), w_ref[...])
    gate_ref[0] = p[:, 0:LRU_W]
    xl_ref[0] = p[:, LRU_W:2 * LRU_W]
    z_ref[0] = p[:, 2 * LRU_W:2 * LRU_W + SSD_INNER]
    xbc_ref[0] = p[:, 2 * LRU_W + SSD_INNER:2 * LRU_W + SSD_INNER + SSD_CONV_CH]
    dt_ref[0] = p[:, 2 * LRU_W + SSD_INNER + SSD_CONV_CH:]


def _odd_pre(x, mod, g, w_in):
    bsz, seq, d = x.shape
    tm = TOKEN_BLOCK
    tok = lambda w: pl.BlockSpec((1, tm, w), lambda b, i: (b, i, 0))
    widths = [LRU_W, LRU_W, SSD_INNER, SSD_CONV_CH, LANES]
    return pl.pallas_call(
        _odd_pre_kernel,
        grid=(bsz, seq // tm),
        in_specs=[tok(d), _mod_spec(mod), _const_spec((1, d)), _const_spec((d, ODD_IN_PAD))],
        out_specs=[tok(w) for w in widths],
        out_shape=[jax.ShapeDtypeStruct((bsz, seq, w), F32) for w in widths],
        compiler_params=_params(("arbitrary", "arbitrary")),
        name="odd_pre",
    )(x, mod, g, w_in)


def _conv_rows(x_ref, r0, rows, seq, w_ref, b_ref, sl):
    main = x_ref[0, pl.ds(r0, rows), sl]
    prev8 = x_ref[0, pl.ds(pl.multiple_of(jnp.maximum(r0 - SUBLANES, 0), SUBLANES), SUBLANES), sl]
    next8 = x_ref[0, pl.ds(pl.multiple_of(jnp.minimum(r0 + rows, seq - SUBLANES), SUBLANES), SUBLANES), sl]
    has_prev = r0 > 0
    has_next = r0 + rows < seq
    before = jnp.where(has_prev, prev8[SUBLANES - 1:SUBLANES, :], 0.0)
    after0 = jnp.where(has_next, next8[0:1, :], 0.0)
    after1 = jnp.where(has_next, next8[1:2, :], 0.0)
    row = lax.broadcasted_iota(jnp.int32, (rows, 1), 0)
    xm1 = jnp.where(row == 0, before, pltpu.roll(main, 1, 0))
    xp1 = jnp.where(row == rows - 1, after0, pltpu.roll(main, rows - 1, 0))
    xp2 = jnp.where(row == rows - 2, after0, jnp.where(row == rows - 1, after1, pltpu.roll(main, rows - 2, 0)))
    return (xm1 * w_ref[0:1, sl] + main * w_ref[1:2, sl] + xp1 * w_ref[2:3, sl] + xp2 * w_ref[3:4, sl]
            + b_ref[:, sl])


def _lru_kernel(xl_ref, cw_ref, cb_ref, wg_ref, bg_ref, lam_ref, h0_ref, y_ref, hout_ref, a_scr, b_scr, *, seq):
    rows = TOKEN_BLOCK
    sp = _softplus(-lam_ref[...])

    def gates(c, carry):
        r0 = pl.multiple_of(c * rows, rows)
        for lb in range(LRU_W // LANES):
            sl = slice(lb * LANES, (lb + 1) * LANES)
            xb = _conv_rows(xl_ref, r0, rows, seq, cw_ref, cb_ref, sl)
            gt = _dot(xb.astype(BF16), wg_ref[lb]) + bg_ref[lb]
            for d in range(2):
                r = _sigmoid_tanh(gt[:, 2 * d * LANES:(2 * d + 1) * LANES])
                i = _sigmoid_tanh(gt[:, (2 * d + 1) * LANES:(2 * d + 2) * LANES])
                log_a = -LRU_C * r * sp[d:d + 1, sl]
                a = jnp.exp(log_a)
                b = jnp.sqrt(jnp.tanh(-log_a) * (a * a + 1.0)) * (i * xb)
                a_scr[d, pl.ds(r0, rows), sl] = a
                b_scr[d, pl.ds(r0, rows), sl] = b
        y_ref[0, pl.ds(r0, rows), :] = jnp.zeros((rows, LRU_W), F32)
        return carry

    lax.fori_loop(0, seq // rows, gates, 0)

    nb = seq // SUBLANES
    row = lax.broadcasted_iota(jnp.int32, (SUBLANES, LRU_W), 0)

    def scan(i, carry):
        hf, hb = carry
        rf = pl.multiple_of(i * SUBLANES, SUBLANES)
        rb = pl.multiple_of((nb - 1 - i) * SUBLANES, SUBLANES)
        a = a_scr[0, pl.ds(rf, SUBLANES), :]
        b = b_scr[0, pl.ds(rf, SUBLANES), :]
        for k in (1, 2, 4):
            keep = row >= k
            b = a * jnp.where(keep, pltpu.roll(b, k, 0), 0.0) + b
            a = a * jnp.where(keep, pltpu.roll(a, k, 0), 1.0)
        h = a * hf + b
        y_ref[0, pl.ds(rf, SUBLANES), :] += h
        hf = h[SUBLANES - 1:SUBLANES, :]
        a = a_scr[1, pl.ds(rb, SUBLANES), :]
        b = b_scr[1, pl.ds(rb, SUBLANES), :]
        for k in (1, 2, 4):
            keep = row < SUBLANES - k
            b = a * jnp.where(keep, pltpu.roll(b, SUBLANES - k, 0), 0.0) + b
            a = a * jnp.where(keep, pltpu.roll(a, SUBLANES - k, 0), 1.0)
        h = a * hb + b
        y_ref[0, pl.ds(rb, SUBLANES), :] += h
        hb = h[0:1, :]
        return hf, hb

    hf, hb = lax.fori_loop(0, nb, scan, (h0_ref[0, 0:1, :], h0_ref[0, 1:2, :]))
    hout_ref[0, 0:1, :] = hf
    hout_ref[0, 1:2, :] = hb


def _lru(xl, cw, cb, wg, bg, lam, h0):
    bsz, seq, _ = xl.shape
    per_seq = lambda n, w: pl.BlockSpec((1, n, w), lambda b: (b, 0, 0))
    return pl.pallas_call(
        functools.partial(_lru_kernel, seq=seq),
        grid=(bsz,),
        in_specs=[per_seq(seq, LRU_W), _const_spec(cw.shape), _const_spec(cb.shape), _const_spec(wg.shape),
                  _const_spec(bg.shape), _const_spec(lam.shape), per_seq(2, LRU_W)],
        out_specs=[per_seq(seq, LRU_W), per_seq(2, LRU_W)],
        out_shape=[jax.ShapeDtypeStruct((bsz, seq, LRU_W), F32), jax.ShapeDtypeStruct((bsz, 2, LRU_W), F32)],
        scratch_shapes=[pltpu.VMEM((2, seq, LRU_W), F32), pltpu.VMEM((2, seq, LRU_W), F32)],
        compiler_params=_params(("arbitrary",)),
        name="rglru",
    )(xl, cw, cb, wg, bg, lam, h0)


def _ssd_kernel(xbc_ref, dt_ref, cw_ref, cb_ref, dtb_ref, alog_ref, dexp_ref, ef_ref, eb_ref, h0_ref,
                y_ref, hout_ref,
                xs_scr, b_scr, c_scr, cs_scr, dts_scr, cst_scr, dtt_scr, bt_scr, st_scr, *, seq):
    q = SSD_Q
    nc = seq // q
    lane = lax.broadcasted_iota(jnp.int32, (q, LANES), 1)
    qi = lax.broadcasted_iota(jnp.int32, (q, q), 0)
    ki = lax.broadcasted_iota(jnp.int32, (q, q), 1)
    tri_lower = jnp.where(ki <= qi, 1.0, 0.0).astype(BF16)
    tri_upper = jnp.where(ki >= qi, 1.0, 0.0).astype(BF16)
    a_neg = jnp.where(lane[0:1, :] < 2 * SSD_HEADS, -jnp.exp(alog_ref[...]), 0.0)

    def prepare(c, carry):
        r0 = pl.multiple_of(c * q, q)
        rows = pl.ds(r0, q)
        for lb in range(SSD_CONV_CH // LANES):
            sl = slice(lb * LANES, (lb + 1) * LANES)
            xc = _conv_rows(xbc_ref, r0, q, seq, cw_ref, cb_ref, sl)
            xc = xc * _sigmoid_tanh(xc)
            if lb < SSD_INNER // LANES:
                xs_scr[rows, sl] = xc
                y_ref[0, rows, sl] = dexp_ref[:, sl] * xc
            elif lb == SSD_INNER // LANES:
                b_scr[rows, :] = xc
                bt_scr[c] = xc.T
            else:
                c_scr[rows, :] = xc
        dts = _softplus(dt_ref[0, rows, :] + dtb_ref[...])
        da = dts * a_neg
        cs = jnp.where(lane < SSD_HEADS, _dot3(tri_lower, da), _dot3(tri_upper, da))
        cs_scr[rows, :] = cs
        dts_scr[rows, :] = dts
        cst_scr[c] = cs.T
        dtt_scr[c] = dts.T
        return carry

    lax.fori_loop(0, nc, prepare, 0)
    st_scr[...] = h0_ref[0]

    low = lane < SSD_STATE
    col = lax.broadcasted_iota(jnp.int32, (SSD_STATE, SSD_INNER), 1)
    first_group = col < SSD_INNER // 2

    def chunk(c, fwd):
        d = 0 if fwd else 1
        r0 = pl.multiple_of(c * q, q)
        rows = pl.ds(r0, q)
        cs = cs_scr[rows, :]
        dts = dts_scr[rows, :]
        cst = cst_scr[c]
        dtt = dtt_scr[c]
        xs = xs_scr[rows, :]
        xsb = xs.astype(BF16)
        bb = b_scr[rows, :].astype(BF16)
        cb_ = c_scr[rows, :].astype(BF16)
        zero = jnp.zeros_like(cb_)
        cbg = [_dot_nt(jnp.where(low, cb_, zero), bb), _dot_nt(jnp.where(low, zero, cb_), bb)]
        mask = (ki <= qi) if fwd else (ki >= qi)
        expand = ef_ref[...] if fwd else eb_ref[...]
        st = st_scr[d]
        hh = jnp.concatenate([jnp.where(first_group, st, 0.0), jnp.where(first_group, 0.0, st)], axis=0)
        tot = cs[q - 1:q, :] if fwd else cs[0:1, :]
        mine = (lane >= d * SSD_HEADS) & (lane < (d + 1) * SSD_HEADS)
        w = jnp.exp(jnp.where(mine, tot - cs, 0.0)) * dts
        dec = jnp.broadcast_to(jnp.exp(tot), (2 * SUBLANES, LANES))
        dec_hi = dec.astype(BF16)
        dec_lo = (dec - dec_hi.astype(F32)).astype(BF16)
        spread = _dot(jnp.concatenate([jnp.exp(cs).astype(BF16), w.astype(BF16), dec_hi, dec_lo], axis=0), expand)
        y = _dot(cb_, hh.astype(BF16)) * spread[0:q]
        pairs = []
        for pp in range(SSD_HEADS // 2):
            ms = []
            for h in (2 * pp, 2 * pp + 1):
                l = d * SSD_HEADS + h
                decay = jnp.where(mask, jnp.exp(cs[:, l:l + 1] - cst[l:l + 1, :]), 0.0)
                ms.append((cbg[pp // (SSD_HEADS // 4)] * decay * dtt[l:l + 1, :]).astype(BF16))
            xp = xsb[:, pp * LANES:(pp + 1) * LANES]
            zx = jnp.zeros_like(xp)
            x2 = jnp.concatenate([jnp.where(low, xp, zx), jnp.where(low, zx, xp)], axis=0)
            pairs.append(_dot(jnp.concatenate(ms, axis=1), x2))
        y_ref[0, rows, :] += y + jnp.concatenate(pairs, axis=1)
        xw = (xs * spread[q:2 * q]).astype(BF16)
        bt = bt_scr[c].astype(BF16)
        half_w = SSD_INNER // 2
        st_new = jnp.concatenate([_dot(bt[:SSD_STATE], xw[:, :half_w]), _dot(bt[SSD_STATE:], xw[:, half_w:])], axis=1)
        chunk_decay = spread[2 * q:2 * q + 1] + spread[2 * q + 2 * SUBLANES:2 * q + 2 * SUBLANES + 1]
        st_scr[d] = chunk_decay * st + st_new

    def body(i, carry):
        chunk(i, True)
        chunk(nc - 1 - i, False)
        return carry

    lax.fori_loop(0, nc, body, 0)
    hout_ref[0] = st_scr[...]


def _ssd(xbc, dt, cw, cb, dtb, alog, dexp, h0):
    bsz, seq, _ = xbc.shape
    nc = seq // SSD_Q
    expand = np.zeros((2, LANES, SSD_INNER), np.float32)
    for d in range(2):
        for h in range(SSD_HEADS):
            expand[d, d * SSD_HEADS + h, h * 64:(h + 1) * 64] = 1.0
    expand = jnp.asarray(expand, dtype=BF16)
    per_seq = lambda n, w: pl.BlockSpec((1, n, w), lambda b: (b, 0, 0))
    state_spec = pl.BlockSpec((1, 2, SSD_STATE, SSD_INNER), lambda b: (b, 0, 0, 0))
    return pl.pallas_call(
        functools.partial(_ssd_kernel, seq=seq),
        grid=(bsz,),
        in_specs=[per_seq(seq, SSD_CONV_CH), per_seq(seq, LANES), _const_spec(cw.shape), _const_spec(cb.shape),
                  _const_spec(dtb.shape), _const_spec(alog.shape), _const_spec(dexp.shape),
                  _const_spec((LANES, SSD_INNER)), _const_spec((LANES, SSD_INNER)), state_spec],
        out_specs=[per_seq(seq, SSD_INNER), state_spec],
        out_shape=[jax.ShapeDtypeStruct((bsz, seq, SSD_INNER), F32),
                   jax.ShapeDtypeStruct((bsz, 2, SSD_STATE, SSD_INNER), F32)],
        scratch_shapes=[pltpu.VMEM((seq, SSD_INNER), F32), pltpu.VMEM((seq, LANES), F32),
                        pltpu.VMEM((seq, LANES), F32), pltpu.VMEM((seq, LANES), F32),
                        pltpu.VMEM((seq, LANES), F32), pltpu.VMEM((nc, LANES, SSD_Q), F32),
                        pltpu.VMEM((nc, LANES, SSD_Q), F32), pltpu.VMEM((nc, LANES, SSD_Q), F32),
                        pltpu.VMEM((2, SSD_STATE, SSD_INNER), F32)],
        compiler_params=_params(("arbitrary",)),
        name="ssd",
    )(xbc, dt, cw, cb, dtb, alog, dexp, expand[0], expand[1], h0)


def _lru_gate_weights(wa, ba, wx, bx):
    eye = jnp.eye(2, dtype=F32)

    def pair_blocks(w):
        w4 = w.reshape(4, 2, 64, 64)
        return jnp.einsum("laij,ab->laibj", w4, eye).reshape(4, LANES, LANES)

    mats = [pair_blocks(w) for w in (wa[0], wx[0], wa[1], wx[1])]
    wg = jnp.concatenate(mats, axis=-1).astype(BF16)
    bias = [b.reshape(4, 1, LANES) for b in (ba[0], bx[0], ba[1], bx[1])]
    return wg, jnp.concatenate(bias, axis=-1)


def _trunk(x, mods, p, ctx):
    is_ctx = ctx is None
    bsz, seq, _ = x.shape
    f, q, k, v = _even_pre(x, mods[0], p["norm_mix"][0], p["w_in_even"], p["q_gain"], p["k_gain"], rope=not is_ctx)
    mix = _even_mixer(q, k, v, f, None if is_ctx else (ctx["k"], ctx["v"]))
    x = _post_ffn(x, mix, mods[0], p["w_out_even"], p["norm_ffn"][0], p["w1"], p["w3"], p["w2"], layer=0)
    gate, xl, z, xbc, dt = _odd_pre(x, mods[1], p["norm_mix"][1], p["w_in_odd"])
    if is_ctx:
        h0_lru = jnp.zeros((bsz, 2, LRU_W), F32)
        h0_ssd = jnp.zeros((bsz, 2, SSD_STATE, SSD_INNER), F32)
    else:
        h0_lru, h0_ssd = ctx["lru"], ctx["ssd"]
    ylru, lru_state = _lru(xl, p["conv_lru_w"], p["conv_lru_b"], p["lru_wg"], p["lru_bg"], p["lru_lambda"], h0_lru)
    yssd, ssd_state = _ssd(xbc, dt, p["conv_ssd_w"], p["conv_ssd_b"], p["ssd_dtb"], p["ssd_alog"], p["ssd_dexp"],
                           h0_ssd)
    x = _post_ffn(x, (ylru, gate, yssd, z), mods[1], p["w_out_odd"], p["norm_ffn"][1], p["w1"], p["w3"], p["w2"],
                  layer=1, ssd_norm=p["ssd_norm"])
    return x, k, v, lru_state, ssd_state


def kernel(x_prompt, x_sample, c, cache_k, cache_v, state_lru, state_ssd, c_ctx, w_ada, b_ada, norm_mix, norm_ffn,
           w_in_even, q_norm, k_norm, w_out_even, w_in_odd, conv_lru_w, conv_lru_b, lru_wa, lru_ba, lru_wx, lru_bx,
           lru_lambda, conv_ssd_w, conv_ssd_b, ssd_dt_bias, ssd_a_log, ssd_d, ssd_norm, w_out_odd, ffn_w1, ffn_w3,
           ffn_w2):
    bsz, seq, d = x_prompt.shape
    dec_b = x_sample.shape[0]
    past = cache_k.shape[2]

    cond_rows = 16
    cond = jnp.concatenate([c, c_ctx[None, :], jnp.zeros((cond_rows - dec_b - 1, d), F32)], axis=0)
    mod = _modulation(cond, w_ada, b_ada).reshape(2, cond_rows, 6, d)
    mods_sample = [mod[l, :dec_b] for l in range(2)]
    mods_prompt = [mod[l, dec_b:dec_b + 1] for l in range(2)]

    lru_wg, lru_bg = _lru_gate_weights(lru_wa[0], lru_ba[0], lru_wx[0], lru_bx[0])
    pad32 = lambda a: jnp.pad(a.reshape(1, 2 * SSD_HEADS), ((0, 0), (0, LANES - 2 * SSD_HEADS)))
    p = dict(
        norm_mix=norm_mix.reshape(2, 1, d), norm_ffn=norm_ffn.reshape(2, 1, d),
        w_in_even=w_in_even[0].astype(BF16),
        q_gain=jnp.tile(q_norm[0], N_Q_HEADS)[None, :], k_gain=jnp.tile(k_norm[0], N_KV_HEADS)[None, :],
        w_out_even=w_out_even[0].astype(BF16),
        w_in_odd=jnp.pad(w_in_odd[0].astype(BF16), ((0, 0), (0, ODD_IN_PAD - ODD_IN))),
        conv_lru_w=conv_lru_w[0], conv_lru_b=conv_lru_b[0][None, :],
        lru_wg=lru_wg, lru_bg=lru_bg, lru_lambda=lru_lambda[0],
        conv_ssd_w=conv_ssd_w[0], conv_ssd_b=conv_ssd_b[0][None, :],
        ssd_dtb=pad32(ssd_dt_bias[0]), ssd_alog=pad32(ssd_a_log[0]),
        ssd_dexp=jnp.repeat(ssd_d[0], SSD_INNER // SSD_HEADS)[None, :],
        ssd_norm=ssd_norm[0][None, :],
        w_out_odd=w_out_odd[0].astype(BF16),
        w1=ffn_w1.astype(BF16), w3=ffn_w3.astype(BF16), w2=ffn_w2.astype(BF16),
    )

    y_prompt, k_new, v_new, lru_new, ssd_new = _trunk(x_prompt, mods_prompt, p, None)
    ctx = dict(
        k=cache_k[:, 0].reshape(dec_b, past, KV_W), v=cache_v[:, 0].reshape(dec_b, past, KV_W),
        lru=state_lru[:, 0],
        ssd=state_ssd[:, 0].transpose(0, 1, 4, 2, 3).reshape(dec_b, 2, SSD_STATE, SSD_INNER),
    )
    y_sample, _, _, _, _ = _trunk(x_sample, mods_sample, p, ctx)

    new_k = k_new.reshape(bsz, 1, seq, N_KV_HEADS, HEAD_DIM)
    new_v = v_new.reshape(bsz, 1, seq, N_KV_HEADS, HEAD_DIM)
    new_lru = lru_new.reshape(bsz, 1, 2, LRU_W)
    new_ssd = (ssd_new.reshape(bsz, 2, SSD_STATE, SSD_HEADS, SSD_INNER // SSD_HEADS)
               .transpose(0, 1, 3, 4, 2).reshape(bsz, 1, 2, SSD_HEADS, SSD_INNER // SSD_HEADS, SSD_STATE))
    return (y_prompt, y_sample, new_k, new_v, new_lru, new_ssd)
```

```python
import functools
import math

import numpy as np
import jax
import jax.numpy as jnp
from jax import lax
from jax.experimental import pallas as pl
from jax.experimental.pallas import tpu as pltpu

F32 = jnp.float32
BF16 = jnp.bfloat16

D_MODEL = 1024
EPS = 1e-6
GRID_W = 64
HEAD_DIM = 64
N_Q_HEADS = 12
N_KV_HEADS = 4
FOURIER_W = 256
FOURIER_GROUP_W = 64
Q_W = N_Q_HEADS * HEAD_DIM
KV_W = N_KV_HEADS * HEAD_DIM
EVEN_IN = FOURIER_W + Q_W + 2 * KV_W
ROPE_THETA = 10000.0
AX_PAIRS = HEAD_DIM // 4
LRU_W = 512
LRU_C = 8.0
SSD_INNER = 1024
SSD_HEADS = 16
SSD_STATE = 64
SSD_CONV_CH = SSD_INNER + 4 * SSD_STATE
ODD_IN = 2 * LRU_W + SSD_INNER + SSD_CONV_CH + 2 * SSD_HEADS
ODD_IN_PAD = 3456
D_FF = 2816
LANES = 128
SUBLANES = 8
TOKEN_BLOCK = 512
ROW_TILE = 256
SSD_Q = 128
V_ROWS = HEAD_DIM + 16
VMEM_LIMIT = 56 * 1024 * 1024


def _params(sem, vmem=VMEM_LIMIT):
    return pltpu.CompilerParams(dimension_semantics=sem, vmem_limit_bytes=vmem)


def _const_spec(shape):
    nd = len(shape)
    return pl.BlockSpec(shape, lambda *_: (0,) * nd, pipeline_mode=pl.Buffered(1))


def _dot(a, b):
    return jnp.dot(a, b, preferred_element_type=F32)


def _dot_nt(a, b):
    return lax.dot_general(a, b, (((1,), (1,)), ((), ())), preferred_element_type=F32)


def _dot3(m, a):
    hi = a.astype(BF16)
    r1 = a - hi.astype(F32)
    mid = r1.astype(BF16)
    lo = (r1 - mid.astype(F32)).astype(BF16)
    return _dot(m, hi) + _dot(m, mid) + _dot(m, lo)


def _silu(x):
    return x * jax.nn.sigmoid(x)


def _sigmoid_tanh(x):
    return 0.5 * jnp.tanh(0.5 * x) + 0.5


def _softplus(x):
    return jnp.maximum(x, 0.0) + jnp.log1p(jnp.exp(-jnp.abs(x)))


def _gelu_tanh(x):
    c = math.sqrt(2.0 / math.pi)
    return 0.5 * x * (1.0 + jnp.tanh(c * (x + 0.044715 * (x * x * x))))


def _modnorm(x, g, shift, scale):
    ms = jnp.mean(x * x, axis=-1, keepdims=True)
    y = x * lax.rsqrt(ms + EPS) * g
    return y * (1.0 + scale) + shift


def _mod_kernel(cond_ref, w_ref, b_ref, o_ref):
    s = _silu(cond_ref[...])
    o_ref[0] = _dot(s.astype(BF16), w_ref[0].astype(BF16)) + b_ref[0]


def _modulation(cond, w_ada, b_ada):
    depth, d, n = w_ada.shape
    rows = cond.shape[0]
    tn = 1536
    return pl.pallas_call(
        _mod_kernel,
        grid=(depth, n // tn),
        in_specs=[pl.BlockSpec((rows, d), lambda l, j: (0, 0)),
                  pl.BlockSpec((1, d, tn), lambda l, j: (l, 0, j)),
                  pl.BlockSpec((1, 1, tn), lambda l, j: (l, 0, j))],
        out_specs=pl.BlockSpec((1, rows, tn), lambda l, j: (l, 0, j)),
        out_shape=jax.ShapeDtypeStruct((depth, rows, n), F32),
        compiler_params=_params(("arbitrary", "arbitrary")),
        name="modulation",
    )(cond, w_ada, b_ada.reshape(depth, 1, n))


def _mod_spec(mod):
    if mod.shape[0] == 1:
        return pl.BlockSpec((1, 6, D_MODEL), lambda b, i: (0, 0, 0))
    return pl.BlockSpec((1, 6, D_MODEL), lambda b, i: (b, 0, 0))


def _head_rmsnorm(x, gain):
    rows, width = x.shape
    low = lax.broadcasted_iota(jnp.int32, (rows, LANES), 1) < HEAD_DIM
    out = []
    for b in range(width // LANES):
        blk = x[:, b * LANES:(b + 1) * LANES]
        sq = blk * blk
        lo = jnp.sum(jnp.where(low, sq, 0.0), axis=-1, keepdims=True)
        hi = jnp.sum(jnp.where(low, 0.0, sq), axis=-1, keepdims=True)
        ms = jnp.where(low, lo, hi) * (1.0 / HEAD_DIM)
        out.append(blk * lax.rsqrt(ms + EPS) * gain[:, b * LANES:(b + 1) * LANES])
    return jnp.concatenate(out, axis=-1)


def _rope(x, cos, sin_signed):
    rows, width = x.shape
    lane = lax.broadcasted_iota(jnp.int32, (rows, LANES), 1)
    second = (lane % (2 * AX_PAIRS)) >= AX_PAIRS
    out = []
    for b in range(width // LANES):
        blk = x[:, b * LANES:(b + 1) * LANES]
        partner = jnp.where(second, pltpu.roll(blk, AX_PAIRS, 1), pltpu.roll(blk, LANES - AX_PAIRS, 1))
        out.append(blk * cos + partner * sin_signed)
    return jnp.concatenate(out, axis=-1)


def _even_pre_kernel(*refs, rope):
    if rope:
        x_ref, mod_ref, g_ref, w_ref, qg_ref, kg_ref, cos_ref, sin_ref, f_ref, q_ref, k_ref, v_ref = refs
    else:
        x_ref, mod_ref, g_ref, w_ref, qg_ref, kg_ref, f_ref, q_ref, k_ref, v_ref = refs
    h = _modnorm(x_ref[0], g_ref[...], mod_ref[0, 0:1, :], mod_ref[0, 1:2, :])
    p = _dot(h.astype(BF16), w_ref[...])
    f_ref[0] = p[:, :FOURIER_W]
    q = _head_rmsnorm(p[:, FOURIER_W:FOURIER_W + Q_W], qg_ref[...])
    k = _head_rmsnorm(p[:, FOURIER_W + Q_W:FOURIER_W + Q_W + KV_W], kg_ref[...])
    if rope:
        q = _rope(q, cos_ref[...], sin_ref[...])
        k = _rope(k, cos_ref[...], sin_ref[...])
    q_ref[0] = (q * (HEAD_DIM ** -0.5 * math.log2(math.e))).astype(BF16)
    k_ref[0] = k
    v_ref[0] = p[:, FOURIER_W + Q_W + KV_W:]


def _rope_tables(seq):
    pos = np.arange(seq)
    freqs = ROPE_THETA ** (-np.arange(AX_PAIRS, dtype=np.float64) / AX_PAIRS)
    ang = np.zeros((seq, HEAD_DIM))
    sign = np.zeros((HEAD_DIM,))
    for a, p_a in enumerate((pos // GRID_W, pos % GRID_W)):
        for t in range(2):
            lo = a * 2 * AX_PAIRS + t * AX_PAIRS
            ang[:, lo:lo + AX_PAIRS] = p_a[:, None] * freqs[None, :]
            sign[lo:lo + AX_PAIRS] = -1.0 if t == 0 else 1.0
    cos = np.tile(np.cos(ang), (1, 2)).astype(np.float32)
    sin = np.tile(np.sin(ang) * sign[None, :], (1, 2)).astype(np.float32)
    return jnp.asarray(cos), jnp.asarray(sin)


def _even_pre(x, mod, g, w_in, qg, kg, rope):
    bsz, seq, d = x.shape
    tm = ROW_TILE
    tok = lambda w: pl.BlockSpec((1, tm, w), lambda b, i: (b, i, 0))
    in_specs = [tok(d), _mod_spec(mod), _const_spec((1, d)), _const_spec((d, EVEN_IN)),
                _const_spec((1, Q_W)), _const_spec((1, KV_W))]
    args = [x, mod, g, w_in, qg, kg]
    if rope:
        cos, sin = _rope_tables(seq)
        in_specs += [pl.BlockSpec((tm, LANES), lambda b, i: (i, 0))] * 2
        args += [cos, sin]
    return pl.pallas_call(
        functools.partial(_even_pre_kernel, rope=rope),
        grid=(bsz, seq // tm),
        in_specs=in_specs,
        out_specs=[tok(FOURIER_W), tok(Q_W), tok(KV_W), tok(KV_W)],
        out_shape=[jax.ShapeDtypeStruct((bsz, seq, FOURIER_W), F32),
                   jax.ShapeDtypeStruct((bsz, seq, Q_W), BF16),
                   jax.ShapeDtypeStruct((bsz, seq, KV_W), F32),
                   jax.ShapeDtypeStruct((bsz, seq, KV_W), F32)],
        compiler_params=_params(("arbitrary", "arbitrary")),
        name="even_pre",
    )(*args)


def _attn_kernel(*refs, seq, has_ctx):
    if has_ctx:
        q_ref, k_ref, v_ref, ck_ref, cv_ref, f_ref, cs_ref, cbd_ref, o_ref, kop, vopt, g_scr, cs_scr, s_scr, p_scr = refs
    else:
        q_ref, k_ref, v_ref, f_ref, cs_ref, cbd_ref, o_ref, kop, vopt, g_scr, cs_scr, s_scr, p_scr = refs
    tq = q_ref.shape[1]

    @pl.when((pl.program_id(0) == 0) & (pl.program_id(1) == 0))
    def _round_tables():
        for r0 in range(0, seq, tq):
            cs_scr[r0:r0 + tq, :] = cs_ref[r0:r0 + tq, :].astype(BF16)

    @pl.when(pl.program_id(1) == 0)
    def _prepare():
        def put(kx, vx, r0):
            n = kx.shape[0]
            low = lax.broadcasted_iota(jnp.int32, (n, LANES), 1) < HEAD_DIM
            ones = jnp.ones((V_ROWS - HEAD_DIM, n), BF16)
            for jp in range(N_KV_HEADS // 2):
                kp = kx[:, jp * LANES:(jp + 1) * LANES]
                k_lo = jnp.where(low, kp, 0.0)
                k_hi = jnp.where(low, 0.0, kp)
                placed = (k_lo, pltpu.roll(k_lo, HEAD_DIM, 1), pltpu.roll(k_hi, HEAD_DIM, 1), k_hi)
                for t, kk in enumerate(placed):
                    kop[4 * jp + t, r0:r0 + n, :] = kk.astype(BF16)
                vt = vx[:, jp * LANES:(jp + 1) * LANES].T.astype(BF16)
                for t in range(2):
                    vopt[2 * jp + t, 0:HEAD_DIM, r0:r0 + n] = vt[t * HEAD_DIM:(t + 1) * HEAD_DIM]
                    vopt[2 * jp + t, HEAD_DIM:V_ROWS, r0:r0 + n] = ones

        put(k_ref[0], v_ref[0], 0)
        if has_ctx:
            put(ck_ref[0], cv_ref[0], seq)
        f = f_ref[0].astype(BF16)
        g_scr[0:seq, :] = _dot(f, cbd_ref[0].astype(BF16)).astype(BF16)
        g_scr[seq:2 * seq, :] = _dot(f, cbd_ref[1].astype(BF16)).astype(BF16)

    r0 = pl.multiple_of(pl.program_id(1) * tq, tq)
    o_ref[0, :, 0:FOURIER_W] = _dot(cs_scr[pl.ds(r0, tq), :], g_scr[...]).astype(BF16)
    q = q_ref[0]
    n_pairs = N_Q_HEADS // 2
    kv_of = lambda qp: [(2 * qp + half) // (N_Q_HEADS // N_KV_HEADS) for half in range(2)]

    def pair_scores(qp):
        qpair = q[:, qp * LANES:(qp + 1) * LANES]
        return [_dot_nt(qpair, kop[2 * kv_of(qp)[half] + half]) for half in range(2)]

    scores_next = pair_scores(0)
    for qp in range(n_pairs):
        kv = kv_of(qp)
        scores = scores_next
        if qp + 1 < n_pairs:
            scores_next = pair_scores(qp + 1)
        probs = [jnp.exp2(s - jnp.max(s, axis=-1, keepdims=True)).astype(BF16) for s in scores]
        outs = []
        for half in range(2):
            o = _dot_nt(vopt[kv[half]], probs[half])
            outs.append(o[0:HEAD_DIM] / o[HEAD_DIM:HEAD_DIM + 1])
        res = jnp.concatenate(outs, axis=0)
        o_ref[0, :, FOURIER_W + qp * LANES:FOURIER_W + (qp + 1) * LANES] = res.T.astype(BF16)


def _dft_tables(seq):
    s = np.arange(seq)
    ang = 2.0 * np.pi * ((s[:, None] * s[None, :]) % seq) / seq
    pos = np.concatenate([np.cos(ang), -np.sin(ang)], axis=1) / math.sqrt(seq)
    c = np.arange(FOURIER_GROUP_W)
    ang_c = 2.0 * np.pi * ((c[:, None] * c[None, :]) % FOURIER_GROUP_W) / FOURIER_GROUP_W
    eye = np.eye(FOURIER_W // FOURIER_GROUP_W)
    chan = np.stack([np.kron(eye, np.cos(ang_c)), np.kron(eye, np.sin(ang_c))]) / math.sqrt(FOURIER_GROUP_W)
    return jnp.asarray(pos, dtype=F32), jnp.asarray(chan, dtype=F32)


def _even_mixer(q, k, v, f, ctx):
    bsz, seq, _ = q.shape
    tq = ROW_TILE
    has_ctx = ctx is not None
    past = ctx[0].shape[1] if has_ctx else 0
    kt = seq + past
    pos_dft, chan_dft = _dft_tables(seq)
    full = lambda n, w: pl.BlockSpec((1, n, w), lambda b, i: (b, 0, 0))
    in_specs = [pl.BlockSpec((1, tq, Q_W), lambda b, i: (b, i, 0)), full(seq, KV_W), full(seq, KV_W)]
    args = [q, k, v]
    if has_ctx:
        in_specs += [full(past, KV_W), full(past, KV_W)]
        args += list(ctx)
    in_specs += [full(seq, FOURIER_W), _const_spec((seq, 2 * seq)),
                 _const_spec((2, FOURIER_W, FOURIER_W))]
    args += [f, pos_dft, chan_dft]
    return pl.pallas_call(
        functools.partial(_attn_kernel, seq=seq, has_ctx=has_ctx),
        grid=(bsz, seq // tq),
        in_specs=in_specs,
        out_specs=pl.BlockSpec((1, tq, D_MODEL), lambda b, i: (b, i, 0)),
        out_shape=jax.ShapeDtypeStruct((bsz, seq, D_MODEL), BF16),
        scratch_shapes=[pltpu.VMEM((2 * N_KV_HEADS, kt, LANES), BF16),
                        pltpu.VMEM((N_KV_HEADS, V_ROWS, kt), BF16),
                        pltpu.VMEM((2 * seq, FOURIER_W), BF16),
                        pltpu.VMEM((seq, 2 * seq), BF16),
                        pltpu.VMEM((2, kt, tq), F32),
                        pltpu.VMEM((2, kt, tq), BF16)],
        compiler_params=_params(("arbitrary", "arbitrary")),
        name="even_mixer",
    )(*args)


def _post_ffn_kernel(*refs, odd):
    if odd:
        (x_ref, ylru_ref, g_ref, yssd_ref, z_ref, sn_ref, mod_ref, wout_ref, gf_ref,
         w1_ref, w3_ref, w2_ref, o_ref) = refs
        y_lru = ylru_ref[0] * _gelu_tanh(g_ref[0])
        y = yssd_ref[0] * _silu(z_ref[0])
        ms = jnp.mean(y * y, axis=-1, keepdims=True)
        y_ssd = y * lax.rsqrt(ms + EPS) * sn_ref[...]
        mix = jnp.concatenate([y_lru, y_ssd], axis=-1).astype(BF16)
    else:
        x_ref, mix_ref, mod_ref, wout_ref, gf_ref, w1_ref, w3_ref, w2_ref, o_ref = refs
        mix = mix_ref[0]
    x = x_ref[0] + mod_ref[0, 2:3, :] * _dot(mix, wout_ref[...])
    h = _modnorm(x, gf_ref[...], mod_ref[0, 3:4, :], mod_ref[0, 4:5, :]).astype(BF16)
    a = (_silu(_dot(h, w1_ref[...])) * _dot(h, w3_ref[...])).astype(BF16)
    o_ref[0] = x + mod_ref[0, 5:6, :] * _dot(a, w2_ref[...])


def _post_ffn(x, mix_args, mod, w_out, g_ffn, w1, w3, w2, layer, ssd_norm=None):
    bsz, seq, d = x.shape
    tm = min(seq, TOKEN_BLOCK)
    odd = layer % 2 == 1
    tok = lambda w: pl.BlockSpec((1, tm, w), lambda b, i: (b, i, 0))
    slab = lambda w: pl.BlockSpec((None,) + w.shape[1:], lambda b, i: (layer, 0, 0), pipeline_mode=pl.Buffered(1))
    if odd:
        ylru, g, yssd, z = mix_args
        in_specs = [tok(d), tok(LRU_W), tok(LRU_W), tok(SSD_INNER), tok(SSD_INNER), _const_spec((1, SSD_INNER))]
        args = [x, ylru, g, yssd, z, ssd_norm]
    else:
        in_specs = [tok(d), tok(d)]
        args = [x, mix_args]
    in_specs += [_mod_spec(mod), _const_spec(w_out.shape), _const_spec((1, d)),
                 slab(w1), slab(w3), slab(w2)]
    args += [mod, w_out, g_ffn, w1, w3, w2]
    return pl.pallas_call(
        functools.partial(_post_ffn_kernel, odd=odd),
        grid=(bsz, seq // tm),
        in_specs=in_specs,
        out_specs=tok(d),
        out_shape=jax.ShapeDtypeStruct((bsz, seq, d), F32),
        compiler_params=_params(("arbitrary", "arbitrary")),
        name="post_ffn_odd" if odd else "post_ffn_even",
    )(*args)


def _odd_pre_kernel(x_ref, mod_ref, g_ref, w_ref, gate_ref, xl_ref, z_ref, xbc_ref, dt_ref):
    h = _modnorm(x_ref[0], g_ref[...], mod_ref[0, 0:1, :], mod_ref[0, 1:2, :])
    p = _dot(h.astype(BF16), w_ref[...])
    gate_ref[0] = p[:, 0:LRU_W]
    xl_ref[0] = p[:, LRU_W:2 * LRU_W]
    z_ref[0] = p[:, 2 * LRU_W:2 * LRU_W + SSD_INNER]
    xbc_ref[0] = p[:, 2 * LRU_W + SSD_INNER:2 * LRU_W + SSD_INNER + SSD_CONV_CH]
    dt_ref[0] = p[:, 2 * LRU_W + SSD_INNER + SSD_CONV_CH:]


def _odd_pre(x, mod, g, w_in):
    bsz, seq, d = x.shape
    tm = min(seq, TOKEN_BLOCK)
    tok = lambda w: pl.BlockSpec((1, tm, w), lambda b, i: (b, i, 0))
    widths = [LRU_W, LRU_W, SSD_INNER, SSD_CONV_CH, LANES]
    return pl.pallas_call(
        _odd_pre_kernel,
        grid=(bsz, seq // tm),
        in_specs=[tok(d), _mod_spec(mod), _const_spec((1, d)), _const_spec((d, ODD_IN_PAD))],
        out_specs=[tok(w) for w in widths],
        out_shape=[jax.ShapeDtypeStruct((bsz, seq, w), F32) for w in widths],
        compiler_params=_params(("arbitrary", "arbitrary")),
        name="odd_pre",
    )(x, mod, g, w_in)


def _conv_rows(x_ref, r0, rows, seq, w_ref, b_ref, sl):
    main = x_ref[0, pl.ds(r0, rows), sl]
    prev8 = x_ref[0, pl.ds(pl.multiple_of(jnp.maximum(r0 - SUBLANES, 0), SUBLANES), SUBLANES), sl]
    next8 = x_ref[0, pl.ds(pl.multiple_of(jnp.minimum(r0 + rows, seq - SUBLANES), SUBLANES), SUBLANES), sl]
    has_prev = r0 > 0
    has_next = r0 + rows < seq
    before = jnp.where(has_prev, prev8[SUBLANES - 1:SUBLANES, :], 0.0)
    after0 = jnp.where(has_next, next8[0:1, :], 0.0)
    after1 = jnp.where(has_next, next8[1:2, :], 0.0)
    row = lax.broadcasted_iota(jnp.int32, (rows, 1), 0)
    xm1 = jnp.where(row == 0, before, pltpu.roll(main, 1, 0))
    xp1 = jnp.where(row == rows - 1, after0, pltpu.roll(main, rows - 1, 0))
    xp2 = jnp.where(row == rows - 2, after0, jnp.where(row == rows - 1, after1, pltpu.roll(main, rows - 2, 0)))
    return (xm1 * w_ref[0:1, sl] + main * w_ref[1:2, sl] + xp1 * w_ref[2:3, sl] + xp2 * w_ref[3:4, sl]
            + b_ref[:, sl])


def _lru_kernel(*refs, seq, has_h0):
    if has_h0:
        xl_ref, cw_ref, cb_ref, wg_ref, bg_ref, lam_ref, h0_ref, y_ref, a_scr, b_scr = refs
    else:
        xl_ref, cw_ref, cb_ref, wg_ref, bg_ref, lam_ref, y_ref, hout_ref, a_scr, b_scr = refs
    rows = ROW_TILE
    rate = (0.5 * LRU_C) * _softplus(-lam_ref[...])

    def gates(c, carry):
        r0 = pl.multiple_of(c * rows, rows)
        for lb in range(LRU_W // LANES):
            sl = slice(lb * LANES, (lb + 1) * LANES)
            xb = _conv_rows(xl_ref, r0, rows, seq, cw_ref, cb_ref, sl)
            xh = 0.5 * xb
            gt = jnp.tanh(_dot(xb.astype(BF16), wg_ref[lb]) + bg_ref[lb])
            for d in range(2):
                neg_log_a = rate[d:d + 1, sl] * (gt[:, 2 * d * LANES:(2 * d + 1) * LANES] + 1.0)
                a = jnp.exp2(neg_log_a * (-math.log2(math.e)))
                g2 = jnp.tanh(neg_log_a) * (a * a + 1.0)
                root = jnp.where(g2 > 0.0, g2 * lax.rsqrt(g2), 0.0)
                b = root * ((gt[:, (2 * d + 1) * LANES:(2 * d + 2) * LANES] + 1.0) * xh)
                a_scr[d, pl.ds(r0, rows), sl] = a
                b_scr[d, pl.ds(r0, rows), sl] = b
        y_ref[0, pl.ds(r0, rows), :] = jnp.zeros((rows, LRU_W), F32)
        return carry

    lax.fori_loop(0, seq // rows, gates, 0)

    nb = seq // SUBLANES
    row = lax.broadcasted_iota(jnp.int32, (SUBLANES, LRU_W), 0)

    def scan(i, carry):
        hf, hb = carry
        rf = pl.multiple_of(i * SUBLANES, SUBLANES)
        rb = pl.multiple_of((nb - 1 - i) * SUBLANES, SUBLANES)
        a = a_scr[0, pl.ds(rf, SUBLANES), :]
        b = b_scr[0, pl.ds(rf, SUBLANES), :]
        for k in (1, 2, 4):
            keep = row >= k
            b = a * jnp.where(keep, pltpu.roll(b, k, 0), 0.0) + b
            a = a * jnp.where(keep, pltpu.roll(a, k, 0), 1.0)
        h = a * hf + b
        y_ref[0, pl.ds(rf, SUBLANES), :] += h
        hf = h[SUBLANES - 1:SUBLANES, :]
        a = a_scr[1, pl.ds(rb, SUBLANES), :]
        b = b_scr[1, pl.ds(rb, SUBLANES), :]
        for k in (1, 2, 4):
            keep = row < SUBLANES - k
            b = a * jnp.where(keep, pltpu.roll(b, SUBLANES - k, 0), 0.0) + b
            a = a * jnp.where(keep, pltpu.roll(a, SUBLANES - k, 0), 1.0)
        h = a * hb + b
        y_ref[0, pl.ds(rb, SUBLANES), :] += h
        hb = h[0:1, :]
        return hf, hb

    if has_h0:
        lax.fori_loop(0, nb, scan, (h0_ref[0, 0:1, :], h0_ref[0, 1:2, :]))
    else:
        zero = jnp.zeros((1, LRU_W), F32)
        hf, hb = lax.fori_loop(0, nb, scan, (zero, zero))
        hout_ref[0, 0:1, :] = hf
        hout_ref[0, 1:2, :] = hb


def _lru(xl, cw, cb, wg, bg, lam, h0):
    bsz, seq, _ = xl.shape
    has_h0 = h0 is not None
    per_seq = lambda n, w: pl.BlockSpec((1, n, w), lambda b: (b, 0, 0))
    in_specs = [per_seq(seq, LRU_W), _const_spec(cw.shape), _const_spec(cb.shape), _const_spec(wg.shape),
                _const_spec(bg.shape), _const_spec(lam.shape)]
    args = [xl, cw, cb, wg, bg, lam]
    out_specs = [per_seq(seq, LRU_W)]
    out_shape = [jax.ShapeDtypeStruct((bsz, seq, LRU_W), F32)]
    if has_h0:
        in_specs.append(per_seq(2, LRU_W))
        args.append(h0)
    else:
        out_specs.append(per_seq(2, LRU_W))
        out_shape.append(jax.ShapeDtypeStruct((bsz, 2, LRU_W), F32))
    outs = pl.pallas_call(
        functools.partial(_lru_kernel, seq=seq, has_h0=has_h0),
        grid=(bsz,),
        in_specs=in_specs,
        out_specs=out_specs,
        out_shape=out_shape,
        scratch_shapes=[pltpu.VMEM((2, seq, LRU_W), F32), pltpu.VMEM((2, seq, LRU_W), F32)],
        compiler_params=_params(("arbitrary",)),
        name="rglru",
    )(*args)
    return (outs[0], None) if has_h0 else tuple(outs)


def _ssd_kernel(*refs, seq, has_h0):
    xbc_ref, dt_ref, cw_ref, cb_ref, dtb_ref, alog_ref, dexp_ref, ef_ref, eb_ref = refs[:9]
    if has_h0:
        h0_ref, y_ref = refs[9:11]
    else:
        y_ref, hout_ref = refs[9:11]
    xs_scr, b_scr, c_scr, cs_scr, dts_scr, cst_scr, dtt_scr, bt_scr, st_scr = refs[11:]
    q = SSD_Q
    nc = seq // q
    lane = lax.broadcasted_iota(jnp.int32, (q, LANES), 1)
    qi = lax.broadcasted_iota(jnp.int32, (q, q), 0)
    ki = lax.broadcasted_iota(jnp.int32, (q, q), 1)
    tri_lower = jnp.where(ki <= qi, 1.0, 0.0).astype(BF16)
    tri_upper = jnp.where(ki >= qi, 1.0, 0.0).astype(BF16)
    a_neg = jnp.where(lane[0:1, :] < 2 * SSD_HEADS, -jnp.exp(alog_ref[...]), 0.0)

    def prepare(c, carry):
        r0 = pl.multiple_of(c * q, q)
        rows = pl.ds(r0, q)
        for lb in range(SSD_CONV_CH // LANES):
            sl = slice(lb * LANES, (lb + 1) * LANES)
            xc = _conv_rows(xbc_ref, r0, q, seq, cw_ref, cb_ref, sl)
            xc = xc * _sigmoid_tanh(xc)
            if lb < SSD_INNER // LANES:
                xs_scr[rows, sl] = xc
                y_ref[0, rows, sl] = dexp_ref[:, sl] * xc
            elif lb == SSD_INNER // LANES:
                b_scr[rows, :] = xc
                bt_scr[c] = xc.T
            else:
                c_scr[rows, :] = xc
        dts = _softplus(dt_ref[0, rows, :] + dtb_ref[...])
        da = dts * a_neg
        cs = jnp.where(lane < SSD_HEADS, _dot3(tri_lower, da), _dot3(tri_upper, da))
        cs_scr[rows, :] = cs
        dts_scr[rows, :] = dts
        cst_scr[c] = cs.T
        dtt_scr[c] = dts.T
        return carry

    lax.fori_loop(0, nc, prepare, 0)
    st_scr[...] = h0_ref[0] if has_h0 else jnp.zeros(st_scr.shape, F32)

    low = lane < SSD_STATE
    col = lax.broadcasted_iota(jnp.int32, (SSD_STATE, SSD_INNER), 1)
    first_group = col < SSD_INNER // 2

    def chunk(c, fwd):
        d = 0 if fwd else 1
        r0 = pl.multiple_of(c * q, q)
        rows = pl.ds(r0, q)
        cs = cs_scr[rows, :]
        dts = dts_scr[rows, :]
        cst = cst_scr[c]
        dtt = dtt_scr[c]
        xs = xs_scr[rows, :]
        xsb = xs.astype(BF16)
        bb = b_scr[rows, :].astype(BF16)
        cb_ = c_scr[rows, :].astype(BF16)
        zero = jnp.zeros_like(cb_)
        cbg = [_dot_nt(jnp.where(low, cb_, zero), bb), _dot_nt(jnp.where(low, zero, cb_), bb)]
        mask = (ki <= qi) if fwd else (ki >= qi)
        expand = ef_ref[...] if fwd else eb_ref[...]
        st = st_scr[d]
        hh = jnp.concatenate([jnp.where(first_group, st, 0.0), jnp.where(first_group, 0.0, st)], axis=0)
        tot = cs[q - 1:q, :] if fwd else cs[0:1, :]
        mine = (lane >= d * SSD_HEADS) & (lane < (d + 1) * SSD_HEADS)
        w = jnp.exp(jnp.where(mine, tot - cs, 0.0)) * dts
        dec = jnp.broadcast_to(jnp.exp(tot), (2 * SUBLANES, LANES))
        dec_hi = dec.astype(BF16)
        dec_lo = (dec - dec_hi.astype(F32)).astype(BF16)
        spread = _dot(jnp.concatenate([jnp.exp(cs).astype(BF16), w.astype(BF16), dec_hi, dec_lo], axis=0), expand)
        y = _dot(cb_, hh.astype(BF16)) * spread[0:q]
        pairs = []
        for pp in range(SSD_HEADS // 2):
            ms = []
            for h in (2 * pp, 2 * pp + 1):
                l = d * SSD_HEADS + h
                decay = jnp.where(mask, jnp.exp(cs[:, l:l + 1] - cst[l:l + 1, :]), 0.0)
                ms.append((cbg[pp // (SSD_HEADS // 4)] * decay * dtt[l:l + 1, :]).astype(BF16))
            xp = xsb[:, pp * LANES:(pp + 1) * LANES]
            zx = jnp.zeros_like(xp)
            x2 = jnp.concatenate([jnp.where(low, xp, zx), jnp.where(low, zx, xp)], axis=0)
            pairs.append(_dot(jnp.concatenate(ms, axis=1), x2))
        y_ref[0, rows, :] += y + jnp.concatenate(pairs, axis=1)
        xw = (xs * spread[q:2 * q]).astype(BF16)
        bt = bt_scr[c].astype(BF16)
        half_w = SSD_INNER // 2
        st_new = jnp.concatenate([_dot(bt[:SSD_STATE], xw[:, :half_w]), _dot(bt[SSD_STATE:], xw[:, half_w:])], axis=1)
        chunk_decay = spread[2 * q:2 * q + 1] + spread[2 * q + 2 * SUBLANES:2 * q + 2 * SUBLANES + 1]
        st_scr[d] = chunk_decay * st + st_new

    def body(i, carry):
        chunk(i, True)
        chunk(nc - 1 - i, False)
        return carry

    lax.fori_loop(0, nc, body, 0)
    if not has_h0:
        hout_ref[0] = st_scr[...]


def _ssd(xbc, dt, cw, cb, dtb, alog, dexp, h0):
    bsz, seq, _ = xbc.shape
    nc = seq // SSD_Q
    expand = np.zeros((2, LANES, SSD_INNER), np.float32)
    for d in range(2):
        for h in range(SSD_HEADS):
            expand[d, d * SSD_HEADS + h, h * 64:(h + 1) * 64] = 1.0
    expand = jnp.asarray(expand, dtype=BF16)
    per_seq = lambda n, w: pl.BlockSpec((1, n, w), lambda b: (b, 0, 0))
    state_spec = pl.BlockSpec((1, 2, SSD_STATE, SSD_INNER), lambda b: (b, 0, 0, 0))
    has_h0 = h0 is not None
    in_specs = [per_seq(seq, SSD_CONV_CH), per_seq(seq, LANES), _const_spec(cw.shape), _const_spec(cb.shape),
                _const_spec(dtb.shape), _const_spec(alog.shape), _const_spec(dexp.shape),
                _const_spec((LANES, SSD_INNER)), _const_spec((LANES, SSD_INNER))]
    args = [xbc, dt, cw, cb, dtb, alog, dexp, expand[0], expand[1]]
    out_specs = [per_seq(seq, SSD_INNER)]
    out_shape = [jax.ShapeDtypeStruct((bsz, seq, SSD_INNER), F32)]
    if has_h0:
        in_specs.append(state_spec)
        args.append(h0)
    else:
        out_specs.append(state_spec)
        out_shape.append(jax.ShapeDtypeStruct((bsz, 2, SSD_STATE, SSD_INNER), F32))
    outs = pl.pallas_call(
        functools.partial(_ssd_kernel, seq=seq, has_h0=has_h0),
        grid=(bsz,),
        in_specs=in_specs,
        out_specs=out_specs,
        out_shape=out_shape,
        scratch_shapes=[pltpu.VMEM((seq, SSD_INNER), F32), pltpu.VMEM((seq, LANES), F32),
                        pltpu.VMEM((seq, LANES), F32), pltpu.VMEM((seq, LANES), F32),
                        pltpu.VMEM((seq, LANES), F32), pltpu.VMEM((nc, LANES, SSD_Q), F32),
                        pltpu.VMEM((nc, LANES, SSD_Q), F32), pltpu.VMEM((nc, LANES, SSD_Q), F32),
                        pltpu.VMEM((2, SSD_STATE, SSD_INNER), F32)],
        compiler_params=_params(("arbitrary",)),
        name="ssd",
    )(*args)
    return (outs[0], None) if has_h0 else tuple(outs)


def _lru_gate_weights(wa, ba, wx, bx):
    eye = 0.5 * jnp.eye(2, dtype=F32)

    def pair_blocks(w):
        w4 = w.reshape(4, 2, 64, 64)
        return jnp.einsum("laij,ab->laibj", w4, eye).reshape(4, LANES, LANES)

    mats = [pair_blocks(w) for w in (wa[0], wx[0], wa[1], wx[1])]
    wg = jnp.concatenate(mats, axis=-1).astype(BF16)
    bias = [0.5 * b.reshape(4, 1, LANES) for b in (ba[0], bx[0], ba[1], bx[1])]
    return wg, jnp.concatenate(bias, axis=-1)


def _trunk(x, mods, p, ctx):
    is_ctx = ctx is None
    bsz, seq, _ = x.shape
    f, q, k, v = _even_pre(x, mods[0], p["norm_mix"][0], p["w_in_even"], p["q_gain"], p["k_gain"], rope=not is_ctx)
    mix = _even_mixer(q, k, v, f, None if is_ctx else (ctx["k"], ctx["v"]))
    x = _post_ffn(x, mix, mods[0], p["w_out_even"], p["norm_ffn"][0], p["w1"], p["w3"], p["w2"], layer=0)
    gate, xl, z, xbc, dt = _odd_pre(x, mods[1], p["norm_mix"][1], p["w_in_odd"])
    h0_lru, h0_ssd = (None, None) if is_ctx else (ctx["lru"], ctx["ssd"])
    ylru, lru_state = _lru(xl, p["conv_lru_w"], p["conv_lru_b"], p["lru_wg"], p["lru_bg"], p["lru_lambda"], h0_lru)
    yssd, ssd_state = _ssd(xbc, dt, p["conv_ssd_w"], p["conv_ssd_b"], p["ssd_dtb"], p["ssd_alog"], p["ssd_dexp"],
                           h0_ssd)
    x = _post_ffn(x, (ylru, gate, yssd, z), mods[1], p["w_out_odd"], p["norm_ffn"][1], p["w1"], p["w3"], p["w2"],
                  layer=1, ssd_norm=p["ssd_norm"])
    return x, k, v, lru_state, ssd_state


def kernel(x_prompt, x_sample, c, cache_k, cache_v, state_lru, state_ssd, c_ctx, w_ada, b_ada, norm_mix, norm_ffn,
           w_in_even, q_norm, k_norm, w_out_even, w_in_odd, conv_lru_w, conv_lru_b, lru_wa, lru_ba, lru_wx, lru_bx,
           lru_lambda, conv_ssd_w, conv_ssd_b, ssd_dt_bias, ssd_a_log, ssd_d, ssd_norm, w_out_odd, ffn_w1, ffn_w3,
           ffn_w2):
    bsz, seq, d = x_prompt.shape
    dec_b = x_sample.shape[0]
    past = cache_k.shape[2]

    cond_rows = 16
    cond = jnp.concatenate([c, c_ctx[None, :], jnp.zeros((cond_rows - dec_b - 1, d), F32)], axis=0)
    mod = _modulation(cond, w_ada, b_ada).reshape(2, cond_rows, 6, d)
    mods_sample = [mod[l, :dec_b] for l in range(2)]
    mods_prompt = [mod[l, dec_b:dec_b + 1] for l in range(2)]

    lru_wg, lru_bg = _lru_gate_weights(lru_wa[0], lru_ba[0], lru_wx[0], lru_bx[0])
    pad32 = lambda a: jnp.pad(a.reshape(1, 2 * SSD_HEADS), ((0, 0), (0, LANES - 2 * SSD_HEADS)))
    p = dict(
        norm_mix=norm_mix.reshape(2, 1, d), norm_ffn=norm_ffn.reshape(2, 1, d),
        w_in_even=w_in_even[0].astype(BF16),
        q_gain=jnp.tile(q_norm[0], N_Q_HEADS)[None, :], k_gain=jnp.tile(k_norm[0], N_KV_HEADS)[None, :],
        w_out_even=w_out_even[0].astype(BF16),
        w_in_odd=jnp.pad(w_in_odd[0].astype(BF16), ((0, 0), (0, ODD_IN_PAD - ODD_IN))),
        conv_lru_w=conv_lru_w[0], conv_lru_b=conv_lru_b[0][None, :],
        lru_wg=lru_wg, lru_bg=lru_bg, lru_lambda=lru_lambda[0],
        conv_ssd_w=conv_ssd_w[0], conv_ssd_b=conv_ssd_b[0][None, :],
        ssd_dtb=pad32(ssd_dt_bias[0]), ssd_alog=pad32(ssd_a_log[0]),
        ssd_dexp=jnp.repeat(ssd_d[0], SSD_INNER // SSD_HEADS)[None, :],
        ssd_norm=ssd_norm[0][None, :],
        w_out_odd=w_out_odd[0].astype(BF16),
        w1=ffn_w1.astype(BF16), w3=ffn_w3.astype(BF16), w2=ffn_w2.astype(BF16),
    )

    y_prompt, k_new, v_new, lru_new, ssd_new = _trunk(x_prompt, mods_prompt, p, None)
    ctx = dict(
        k=cache_k[:, 0].reshape(dec_b, past, KV_W), v=cache_v[:, 0].reshape(dec_b, past, KV_W),
        lru=state_lru[:, 0],
        ssd=state_ssd[:, 0].transpose(0, 1, 4, 2, 3).reshape(dec_b, 2, SSD_STATE, SSD_INNER),
    )
    y_sample, _, _, _, _ = _trunk(x_sample, mods_sample, p, ctx)

    new_k = k_new.reshape(bsz, 1, seq, N_KV_HEADS, HEAD_DIM)
    new_v = v_new.reshape(bsz, 1, seq, N_KV_HEADS, HEAD_DIM)
    new_lru = lru_new.reshape(bsz, 1, 2, LRU_W)
    new_ssd = (ssd_new.reshape(bsz, 2, SSD_STATE, SSD_HEADS, SSD_INNER // SSD_HEADS)
               .transpose(0, 1, 3, 4, 2).reshape(bsz, 1, 2, SSD_HEADS, SSD_INNER // SSD_HEADS, SSD_STATE))
    return (y_prompt, y_sample, new_k, new_v, new_lru, new_ssd)
```

```python
import functools
import math

import numpy as np
import jax
import jax.numpy as jnp
from jax import lax
from jax.experimental import pallas as pl
from jax.experimental.pallas import tpu as pltpu

F32 = jnp.float32
BF16 = jnp.bfloat16

D_MODEL = 1024
EPS = 1e-6
GRID_W = 64
HEAD_DIM = 64
N_Q_HEADS = 12
N_KV_HEADS = 4
FOURIER_W = 256
FOURIER_GROUP_W = 64
Q_W = N_Q_HEADS * HEAD_DIM
KV_W = N_KV_HEADS * HEAD_DIM
EVEN_IN = FOURIER_W + Q_W + 2 * KV_W
ROPE_THETA = 10000.0
AX_PAIRS = HEAD_DIM // 4
LRU_W = 512
LRU_C = 8.0
SSD_INNER = 1024
SSD_HEADS = 16
SSD_STATE = 64
SSD_CONV_CH = SSD_INNER + 4 * SSD_STATE
ODD_IN = 2 * LRU_W + SSD_INNER + SSD_CONV_CH + 2 * SSD_HEADS
ODD_IN_PAD = 3456
D_FF = 2816
LANES = 128
SUBLANES = 8
TOKEN_BLOCK = 512
ROW_TILE = 256
SSD_Q = 128
V_ROWS = HEAD_DIM + 16
VMEM_LIMIT = 56 * 1024 * 1024


def _params(sem, vmem=VMEM_LIMIT):
    return pltpu.CompilerParams(dimension_semantics=sem, vmem_limit_bytes=vmem)


def _const_spec(shape):
    nd = len(shape)
    return pl.BlockSpec(shape, lambda *_: (0,) * nd, pipeline_mode=pl.Buffered(1))


def _dot(a, b):
    return jnp.dot(a, b, preferred_element_type=F32)


def _dot_nt(a, b):
    return lax.dot_general(a, b, (((1,), (1,)), ((), ())), preferred_element_type=F32)


def _dot3(m, a):
    hi = a.astype(BF16)
    r1 = a - hi.astype(F32)
    mid = r1.astype(BF16)
    lo = (r1 - mid.astype(F32)).astype(BF16)
    return _dot(m, hi) + _dot(m, mid) + _dot(m, lo)


def _silu(x):
    return x * jax.nn.sigmoid(x)


def _sigmoid_tanh(x):
    return 0.5 * jnp.tanh(0.5 * x) + 0.5


def _softplus(x):
    return jnp.maximum(x, 0.0) + jnp.log1p(jnp.exp(-jnp.abs(x)))


def _gelu_tanh(x):
    c = math.sqrt(2.0 / math.pi)
    return 0.5 * x * (1.0 + jnp.tanh(c * (x + 0.044715 * (x * x * x))))


def _modnorm(x, g, shift, scale):
    ms = jnp.mean(x * x, axis=-1, keepdims=True)
    y = x * lax.rsqrt(ms + EPS) * g
    return y * (1.0 + scale) + shift


def _mod_kernel(cond_ref, w_ref, b_ref, o_ref):
    s = _silu(cond_ref[...])
    o_ref[0] = _dot(s.astype(BF16), w_ref[0].astype(BF16)) + b_ref[0]


def _modulation(cond, w_ada, b_ada):
    depth, d, n = w_ada.shape
    rows = cond.shape[0]
    tn = 1536
    return pl.pallas_call(
        _mod_kernel,
        grid=(depth, n // tn),
        in_specs=[pl.BlockSpec((rows, d), lambda l, j: (0, 0)),
                  pl.BlockSpec((1, d, tn), lambda l, j: (l, 0, j)),
                  pl.BlockSpec((1, 1, tn), lambda l, j: (l, 0, j))],
        out_specs=pl.BlockSpec((1, rows, tn), lambda l, j: (l, 0, j)),
        out_shape=jax.ShapeDtypeStruct((depth, rows, n), F32),
        compiler_params=_params(("arbitrary", "arbitrary")),
        name="modulation",
    )(cond, w_ada, b_ada.reshape(depth, 1, n))


def _mod_spec(mod):
    if mod.shape[0] == 1:
        return pl.BlockSpec((1, 6, D_MODEL), lambda b, i: (0, 0, 0))
    return pl.BlockSpec((1, 6, D_MODEL), lambda b, i: (b, 0, 0))


def _head_rmsnorm(x, gain):
    rows, width = x.shape
    low = lax.broadcasted_iota(jnp.int32, (rows, LANES), 1) < HEAD_DIM
    out = []
    for b in range(width // LANES):
        blk = x[:, b * LANES:(b + 1) * LANES]
        sq = blk * blk
        lo = jnp.sum(jnp.where(low, sq, 0.0), axis=-1, keepdims=True)
        hi = jnp.sum(jnp.where(low, 0.0, sq), axis=-1, keepdims=True)
        ms = jnp.where(low, lo, hi) * (1.0 / HEAD_DIM)
        out.append(blk * lax.rsqrt(ms + EPS) * gain[:, b * LANES:(b + 1) * LANES])
    return jnp.concatenate(out, axis=-1)


def _rope(x, cos, sin_signed):
    rows, width = x.shape
    lane = lax.broadcasted_iota(jnp.int32, (rows, LANES), 1)
    second = (lane % (2 * AX_PAIRS)) >= AX_PAIRS
    out = []
    for b in range(width // LANES):
        blk = x[:, b * LANES:(b + 1) * LANES]
        partner = jnp.where(second, pltpu.roll(blk, AX_PAIRS, 1), pltpu.roll(blk, LANES - AX_PAIRS, 1))
        out.append(blk * cos + partner * sin_signed)
    return jnp.concatenate(out, axis=-1)


def _even_pre_kernel(*refs, rope):
    if rope:
        x_ref, mod_ref, g_ref, w_ref, qg_ref, kg_ref, cos_ref, sin_ref, f_ref, q_ref, k_ref, v_ref = refs
    else:
        x_ref, mod_ref, g_ref, w_ref, qg_ref, kg_ref, f_ref, q_ref, k_ref, v_ref = refs
    h = _modnorm(x_ref[0], g_ref[...], mod_ref[0, 0:1, :], mod_ref[0, 1:2, :])
    p = _dot(h.astype(BF16), w_ref[...])
    f_ref[0] = p[:, :FOURIER_W]
    q = _head_rmsnorm(p[:, FOURIER_W:FOURIER_W + Q_W], qg_ref[...])
    k = _head_rmsnorm(p[:, FOURIER_W + Q_W:FOURIER_W + Q_W + KV_W], kg_ref[...])
    if rope:
        q = _rope(q, cos_ref[...], sin_ref[...])
        k = _rope(k, cos_ref[...], sin_ref[...])
    q_ref[0] = (q * (HEAD_DIM ** -0.5 * math.log2(math.e))).astype(BF16)
    k_ref[0] = k
    v_ref[0] = p[:, FOURIER_W + Q_W + KV_W:]


def _rope_tables(seq):
    pos = np.arange(seq)
    freqs = ROPE_THETA ** (-np.arange(AX_PAIRS, dtype=np.float64) / AX_PAIRS)
    ang = np.zeros((seq, HEAD_DIM))
    sign = np.zeros((HEAD_DIM,))
    for a, p_a in enumerate((pos // GRID_W, pos % GRID_W)):
        for t in range(2):
            lo = a * 2 * AX_PAIRS + t * AX_PAIRS
            ang[:, lo:lo + AX_PAIRS] = p_a[:, None] * freqs[None, :]
            sign[lo:lo + AX_PAIRS] = -1.0 if t == 0 else 1.0
    cos = np.tile(np.cos(ang), (1, 2)).astype(np.float32)
    sin = np.tile(np.sin(ang) * sign[None, :], (1, 2)).astype(np.float32)
    return jnp.asarray(cos), jnp.asarray(sin)


def _even_pre(x, mod, g, w_in, qg, kg, rope):
    bsz, seq, d = x.shape
    tm = ROW_TILE
    tok = lambda w: pl.BlockSpec((1, tm, w), lambda b, i: (b, i, 0))
    in_specs = [tok(d), _mod_spec(mod), _const_spec((1, d)), _const_spec((d, EVEN_IN)),
                _const_spec((1, Q_W)), _const_spec((1, KV_W))]
    args = [x, mod, g, w_in, qg, kg]
    if rope:
        cos, sin = _rope_tables(seq)
        in_specs += [pl.BlockSpec((tm, LANES), lambda b, i: (i, 0))] * 2
        args += [cos, sin]
    return pl.pallas_call(
        functools.partial(_even_pre_kernel, rope=rope),
        grid=(bsz, seq // tm),
        in_specs=in_specs,
        out_specs=[tok(FOURIER_W), tok(Q_W), tok(KV_W), tok(KV_W)],
        out_shape=[jax.ShapeDtypeStruct((bsz, seq, FOURIER_W), F32),
                   jax.ShapeDtypeStruct((bsz, seq, Q_W), BF16),
                   jax.ShapeDtypeStruct((bsz, seq, KV_W), F32),
                   jax.ShapeDtypeStruct((bsz, seq, KV_W), F32)],
        compiler_params=_params(("arbitrary", "arbitrary")),
        name="even_pre",
    )(*args)


def _attn_kernel(*refs, seq, has_ctx):
    if has_ctx:
        q_ref, k_ref, v_ref, ck_ref, cv_ref, f_ref, cs_ref, cbd_ref, o_ref, kop, vopt, g_scr, cs_scr = refs
    else:
        q_ref, k_ref, v_ref, f_ref, cs_ref, cbd_ref, o_ref, kop, vopt, g_scr, cs_scr = refs
    tq = q_ref.shape[1]

    @pl.when((pl.program_id(0) == 0) & (pl.program_id(1) == 0))
    def _round_tables():
        for r0 in range(0, seq, tq):
            cs_scr[r0:r0 + tq, :] = cs_ref[r0:r0 + tq, :].astype(BF16)

    @pl.when(pl.program_id(1) == 0)
    def _prepare():
        def put(kx, vx, r0):
            n = kx.shape[0]
            low = lax.broadcasted_iota(jnp.int32, (n, LANES), 1) < HEAD_DIM
            ones = jnp.ones((V_ROWS - HEAD_DIM, n), BF16)
            for jp in range(N_KV_HEADS // 2):
                kp = kx[:, jp * LANES:(jp + 1) * LANES]
                k_lo = jnp.where(low, kp, 0.0)
                k_hi = jnp.where(low, 0.0, kp)
                placed = (k_lo, pltpu.roll(k_lo, HEAD_DIM, 1), pltpu.roll(k_hi, HEAD_DIM, 1), k_hi)
                for t, kk in enumerate(placed):
                    kop[4 * jp + t, r0:r0 + n, :] = kk.astype(BF16)
                vt = vx[:, jp * LANES:(jp + 1) * LANES].T.astype(BF16)
                for t in range(2):
                    vopt[2 * jp + t, 0:HEAD_DIM, r0:r0 + n] = vt[t * HEAD_DIM:(t + 1) * HEAD_DIM]
                    vopt[2 * jp + t, HEAD_DIM:V_ROWS, r0:r0 + n] = ones

        put(k_ref[0], v_ref[0], 0)
        if has_ctx:
            put(ck_ref[0], cv_ref[0], seq)
        f = f_ref[0].astype(BF16)
        g_scr[0:seq, :] = _dot(f, cbd_ref[0].astype(BF16)).astype(BF16)
        g_scr[seq:2 * seq, :] = _dot(f, cbd_ref[1].astype(BF16)).astype(BF16)

    r0 = pl.multiple_of(pl.program_id(1) * tq, tq)
    o_ref[0, :, 0:FOURIER_W] = _dot(cs_scr[pl.ds(r0, tq), :], g_scr[...]).astype(BF16)
    q = q_ref[0]
    n_pairs = N_Q_HEADS // 2
    kv_of = lambda qp: [(2 * qp + half) // (N_Q_HEADS // N_KV_HEADS) for half in range(2)]

    def pair_scores(qp):
        qpair = q[:, qp * LANES:(qp + 1) * LANES]
        return [_dot_nt(qpair, kop[2 * kv_of(qp)[half] + half]) for half in range(2)]

    scores_next = pair_scores(0)
    for qp in range(n_pairs):
        kv = kv_of(qp)
        scores = scores_next
        if qp + 1 < n_pairs:
            scores_next = pair_scores(qp + 1)
        probs = [jnp.exp2(s - jnp.max(s, axis=-1, keepdims=True)).astype(BF16) for s in scores]
        outs = []
        for half in range(2):
            o = _dot_nt(vopt[kv[half]], probs[half])
            outs.append(o[0:HEAD_DIM] / o[HEAD_DIM:HEAD_DIM + 1])
        res = jnp.concatenate(outs, axis=0)
        o_ref[0, :, FOURIER_W + qp * LANES:FOURIER_W + (qp + 1) * LANES] = res.T.astype(BF16)


def _dft_tables(seq):
    s = np.arange(seq)
    ang = 2.0 * np.pi * ((s[:, None] * s[None, :]) % seq) / seq
    pos = np.concatenate([np.cos(ang), -np.sin(ang)], axis=1) / math.sqrt(seq)
    c = np.arange(FOURIER_GROUP_W)
    ang_c = 2.0 * np.pi * ((c[:, None] * c[None, :]) % FOURIER_GROUP_W) / FOURIER_GROUP_W
    eye = np.eye(FOURIER_W // FOURIER_GROUP_W)
    chan = np.stack([np.kron(eye, np.cos(ang_c)), np.kron(eye, np.sin(ang_c))]) / math.sqrt(FOURIER_GROUP_W)
    return jnp.asarray(pos, dtype=F32), jnp.asarray(chan, dtype=F32)


def _even_mixer(q, k, v, f, ctx):
    bsz, seq, _ = q.shape
    tq = ROW_TILE
    has_ctx = ctx is not None
    past = ctx[0].shape[1] if has_ctx else 0
    kt = seq + past
    pos_dft, chan_dft = _dft_tables(seq)
    full = lambda n, w: pl.BlockSpec((1, n, w), lambda b, i: (b, 0, 0))
    in_specs = [pl.BlockSpec((1, tq, Q_W), lambda b, i: (b, i, 0)), full(seq, KV_W), full(seq, KV_W)]
    args = [q, k, v]
    if has_ctx:
        in_specs += [full(past, KV_W), full(past, KV_W)]
        args += list(ctx)
    in_specs += [full(seq, FOURIER_W), _const_spec((seq, 2 * seq)),
                 _const_spec((2, FOURIER_W, FOURIER_W))]
    args += [f, pos_dft, chan_dft]
    return pl.pallas_call(
        functools.partial(_attn_kernel, seq=seq, has_ctx=has_ctx),
        grid=(bsz, seq // tq),
        in_specs=in_specs,
        out_specs=pl.BlockSpec((1, tq, D_MODEL), lambda b, i: (b, i, 0)),
        out_shape=jax.ShapeDtypeStruct((bsz, seq, D_MODEL), BF16),
        scratch_shapes=[pltpu.VMEM((2 * N_KV_HEADS, kt, LANES), BF16),
                        pltpu.VMEM((N_KV_HEADS, V_ROWS, kt), BF16),
                        pltpu.VMEM((2 * seq, FOURIER_W), BF16),
                        pltpu.VMEM((seq, 2 * seq), BF16)],
        compiler_params=_params(("arbitrary", "arbitrary")),
        name="even_mixer",
    )(*args)


def _post_ffn_kernel(*refs, odd):
    if odd:
        (x_ref, ylru_ref, g_ref, yssd_ref, z_ref, sn_ref, mod_ref, wout_ref, gf_ref,
         w1_ref, w3_ref, w2_ref, o_ref) = refs
        y_lru = ylru_ref[0] * _gelu_tanh(g_ref[0])
        y = yssd_ref[0] * _silu(z_ref[0])
        ms = jnp.mean(y * y, axis=-1, keepdims=True)
        y_ssd = y * lax.rsqrt(ms + EPS) * sn_ref[...]
        mix = jnp.concatenate([y_lru, y_ssd], axis=-1).astype(BF16)
    else:
        x_ref, mix_ref, mod_ref, wout_ref, gf_ref, w1_ref, w3_ref, w2_ref, o_ref = refs
        mix = mix_ref[0]
    x = x_ref[0] + mod_ref[0, 2:3, :] * _dot(mix, wout_ref[...])
    h = _modnorm(x, gf_ref[...], mod_ref[0, 3:4, :], mod_ref[0, 4:5, :]).astype(BF16)
    a = (_silu(_dot(h, w1_ref[...])) * _dot(h, w3_ref[...])).astype(BF16)
    o_ref[0] = x + mod_ref[0, 5:6, :] * _dot(a, w2_ref[...])


def _post_ffn(x, mix_args, mod, w_out, g_ffn, w1, w3, w2, layer, ssd_norm=None):
    bsz, seq, d = x.shape
    tm = min(seq, TOKEN_BLOCK)
    odd = layer % 2 == 1
    tok = lambda w: pl.BlockSpec((1, tm, w), lambda b, i: (b, i, 0))
    slab = lambda w: pl.BlockSpec((None,) + w.shape[1:], lambda b, i: (layer, 0, 0), pipeline_mode=pl.Buffered(1))
    if odd:
        ylru, g, yssd, z = mix_args
        in_specs = [tok(d), tok(LRU_W), tok(LRU_W), tok(SSD_INNER), tok(SSD_INNER), _const_spec((1, SSD_INNER))]
        args = [x, ylru, g, yssd, z, ssd_norm]
    else:
        in_specs = [tok(d), tok(d)]
        args = [x, mix_args]
    in_specs += [_mod_spec(mod), _const_spec(w_out.shape), _const_spec((1, d)),
                 slab(w1), slab(w3), slab(w2)]
    args += [mod, w_out, g_ffn, w1, w3, w2]
    return pl.pallas_call(
        functools.partial(_post_ffn_kernel, odd=odd),
        grid=(bsz, seq // tm),
        in_specs=in_specs,
        out_specs=tok(d),
        out_shape=jax.ShapeDtypeStruct((bsz, seq, d), F32),
        compiler_params=_params(("arbitrary", "arbitrary")),
        name="post_ffn_odd" if odd else "post_ffn_even",
    )(*args)


XL_LO = LRU_W
Z_LO = 2 * LRU_W
XBC_LO = 2 * LRU_W + SSD_INNER
DT_LO = XBC_LO + SSD_CONV_CH


def _conv_block(main, edges, w, b):
    before, after0, after1 = edges
    rows = main.shape[0]
    row = lax.broadcasted_iota(jnp.int32, (rows, 1), 0)
    xm1 = jnp.where(row == 0, before, pltpu.roll(main, 1, 0))
    xp1 = jnp.where(row == rows - 1, after0, pltpu.roll(main, rows - 1, 0))
    xp2 = jnp.where(row == rows - 2, after0, jnp.where(row == rows - 1, after1, pltpu.roll(main, rows - 2, 0)))
    return xm1 * w[0:1] + main * w[1:2] + xp1 * w[2:3] + xp2 * w[3:4] + b


def _odd_pre_kernel(x_ref, xprev_ref, xnext_ref, mod_ref, g_ref, w_ref, cwl_ref, cbl_ref, wg_ref, bg_ref, lam_ref,
                    cws_ref, cbs_ref, dtb_ref, alog_ref,
                    gate_ref, z_ref, a_ref, b_ref, xs_ref, bm_ref, cm_ref, cs_ref, dts_ref, cst_ref, dtt_ref,
                    bt_ref):
    i = pl.program_id(1)
    g, shift, scale = g_ref[...], mod_ref[0, 0:1, :], mod_ref[0, 1:2, :]
    h = _modnorm(x_ref[0], g, shift, scale).astype(BF16)
    tm = h.shape[0]
    wide = 2 * LANES
    proj = lambda lo: _dot(h, w_ref[:, lo:lo + wide])
    lanes_of = lambda pieces, lb: pieces[lb // 2][:, (lb % 2) * LANES:(lb % 2 + 1) * LANES]
    p_xl = [proj(XL_LO + t * wide) for t in range(LRU_W // wide)]
    p_xbc = []

    halo = jnp.concatenate([xprev_ref[0], xnext_ref[0]], axis=0)
    hh = _modnorm(halo, g, shift, scale).astype(BF16)
    has_prev = i > 0
    has_next = i < pl.num_programs(1) - 1

    def edges(cols):
        ph = _dot(hh, w_ref[:, cols])
        return (jnp.where(has_prev, ph[SUBLANES - 1:SUBLANES], 0.0),
                jnp.where(has_next, ph[SUBLANES:SUBLANES + 1], 0.0),
                jnp.where(has_next, ph[SUBLANES + 1:SUBLANES + 2], 0.0))

    rate = (0.5 * LRU_C) * _softplus(-lam_ref[...])
    e_l = edges(slice(XL_LO, Z_LO))
    n_lru = LRU_W // LANES
    n_xbc = SSD_CONV_CH // wide
    for lb in range(n_lru):
        sl = slice(lb * LANES, (lb + 1) * LANES)
        for t in range(lb * n_xbc // n_lru, (lb + 1) * n_xbc // n_lru):
            p_xbc.append(proj(XBC_LO + t * wide))
        xb = _conv_block(lanes_of(p_xl, lb), [e[:, sl] for e in e_l],
                         cwl_ref[:, sl], cbl_ref[:, sl])
        xh = 0.5 * xb
        gt = jnp.tanh(_dot(xb.astype(BF16), wg_ref[lb]) + bg_ref[lb])
        for d in range(2):
            neg_log_a = rate[d:d + 1, sl] * (gt[:, 2 * d * LANES:(2 * d + 1) * LANES] + 1.0)
            a = jnp.exp2(neg_log_a * (-math.log2(math.e)))
            g2 = jnp.tanh(neg_log_a) * (a * a + 1.0)
            root = jnp.where(g2 > 0.0, g2 * lax.rsqrt(g2), 0.0)
            a_ref[0, d, :, sl] = a
            b_ref[0, d, :, sl] = root * ((gt[:, (2 * d + 1) * LANES:(2 * d + 2) * LANES] + 1.0) * xh)

    nchunk = tm // SSD_Q
    e_s = edges(slice(XBC_LO, DT_LO))
    plain = ([(gate_ref, t * wide, t * wide) for t in range(LRU_W // wide)]
             + [(z_ref, Z_LO + t * wide, t * wide) for t in range(SSD_INNER // wide)])
    n_ssd = SSD_CONV_CH // LANES
    for lb in range(n_ssd):
        sl = slice(lb * LANES, (lb + 1) * LANES)
        for dst, lo, col in plain[lb * len(plain) // n_ssd:(lb + 1) * len(plain) // n_ssd]:
            dst[0, :, col:col + wide] = proj(lo)
        xc = _conv_block(lanes_of(p_xbc, lb), [e[:, sl] for e in e_s], cws_ref[:, sl], cbs_ref[:, sl])
        xc = xc * _sigmoid_tanh(xc)
        if lb < SSD_INNER // LANES:
            xs_ref[0, :, sl] = xc
        elif lb == SSD_INNER // LANES:
            bm_ref[0] = xc
            for c in range(nchunk):
                bt_ref[0, c] = xc[c * SSD_Q:(c + 1) * SSD_Q].T
        else:
            cm_ref[0] = xc
    lane = lax.broadcasted_iota(jnp.int32, (SSD_Q, LANES), 1)
    qi = lax.broadcasted_iota(jnp.int32, (SSD_Q, SSD_Q), 0)
    ki = lax.broadcasted_iota(jnp.int32, (SSD_Q, SSD_Q), 1)
    tri_lower = jnp.where(ki <= qi, 1.0, 0.0).astype(BF16)
    tri_upper = jnp.where(ki >= qi, 1.0, 0.0).astype(BF16)
    a_neg = jnp.where(lane[0:1, :] < 2 * SSD_HEADS, -jnp.exp(alog_ref[...]), 0.0)
    dts = _softplus(_dot(h, w_ref[:, DT_LO:]) + dtb_ref[...])
    dts_ref[0] = dts
    for c in range(nchunk):
        rows = slice(c * SSD_Q, (c + 1) * SSD_Q)
        da = dts[rows] * a_neg
        cs = jnp.where(lane < SSD_HEADS, _dot3(tri_lower, da), _dot3(tri_upper, da))
        cs_ref[0, rows, :] = cs
        cst_ref[0, c] = cs.T
        dtt_ref[0, c] = dts[rows].T


def _odd_pre(x, mod, g, w_in, lru, ssd):
    bsz, seq, d = x.shape
    tm = min(seq, TOKEN_BLOCK)
    per8 = tm // SUBLANES
    tok = lambda w: pl.BlockSpec((1, tm, w), lambda b, i: (b, i, 0))
    both = pl.BlockSpec((1, 2, tm, LRU_W), lambda b, i: (b, 0, i, 0))
    chunked = pl.BlockSpec((1, tm // SSD_Q, LANES, SSD_Q), lambda b, i: (b, i, 0, 0))
    consts = list(lru) + list(ssd)
    out_specs = [tok(LRU_W), tok(SSD_INNER), both, both, tok(SSD_INNER)] + [tok(LANES)] * 4 + [chunked] * 3
    row = lambda w: jax.ShapeDtypeStruct((bsz, seq, w), F32)
    out_shape = ([row(LRU_W), row(SSD_INNER)] + [jax.ShapeDtypeStruct((bsz, 2, seq, LRU_W), F32)] * 2
                 + [row(SSD_INNER)] + [row(LANES)] * 4
                 + [jax.ShapeDtypeStruct((bsz, seq // SSD_Q, LANES, SSD_Q), F32)] * 3)
    return pl.pallas_call(
        _odd_pre_kernel,
        grid=(bsz, seq // tm),
        in_specs=[tok(d),
                  pl.BlockSpec((1, SUBLANES, d), lambda b, i: (b, jnp.maximum(i * per8 - 1, 0), 0)),
                  pl.BlockSpec((1, SUBLANES, d), lambda b, i: (b, jnp.minimum((i + 1) * per8, seq // SUBLANES - 1), 0)),
                  _mod_spec(mod), _const_spec((1, d)), _const_spec((d, ODD_IN_PAD))]
                 + [_const_spec(c.shape) for c in consts],
        out_specs=out_specs,
        out_shape=out_shape,
        compiler_params=_params(("arbitrary", "arbitrary")),
        name="odd_pre",
    )(x, x, x, mod, g, w_in, *consts)


def _lru_kernel(*refs, seq, has_h0):
    if has_h0:
        a_ref, b_ref, h0_ref, y_ref = refs
    else:
        a_ref, b_ref, y_ref, hout_ref = refs
    y_ref[0] = jnp.zeros((seq, LRU_W), F32)

    nb = seq // SUBLANES
    row = lax.broadcasted_iota(jnp.int32, (SUBLANES, LRU_W), 0)

    def scan(i, carry):
        hf, hb = carry
        rf = pl.multiple_of(i * SUBLANES, SUBLANES)
        rb = pl.multiple_of((nb - 1 - i) * SUBLANES, SUBLANES)
        a = a_ref[0, 0, pl.ds(rf, SUBLANES), :]
        b = b_ref[0, 0, pl.ds(rf, SUBLANES), :]
        for k in (1, 2, 4):
            keep = row >= k
            b = a * jnp.where(keep, pltpu.roll(b, k, 0), 0.0) + b
            a = a * jnp.where(keep, pltpu.roll(a, k, 0), 1.0)
        h = a * hf + b
        y_ref[0, pl.ds(rf, SUBLANES), :] += h
        hf = h[SUBLANES - 1:SUBLANES, :]
        a = a_ref[0, 1, pl.ds(rb, SUBLANES), :]
        b = b_ref[0, 1, pl.ds(rb, SUBLANES), :]
        for k in (1, 2, 4):
            keep = row < SUBLANES - k
            b = a * jnp.where(keep, pltpu.roll(b, SUBLANES - k, 0), 0.0) + b
            a = a * jnp.where(keep, pltpu.roll(a, SUBLANES - k, 0), 1.0)
        h = a * hb + b
        y_ref[0, pl.ds(rb, SUBLANES), :] += h
        hb = h[0:1, :]
        return hf, hb

    if has_h0:
        lax.fori_loop(0, nb, scan, (h0_ref[0, 0:1, :], h0_ref[0, 1:2, :]))
    else:
        zero = jnp.zeros((1, LRU_W), F32)
        hf, hb = lax.fori_loop(0, nb, scan, (zero, zero))
        hout_ref[0, 0:1, :] = hf
        hout_ref[0, 1:2, :] = hb


def _lru(a, b, h0):
    bsz, _, seq, _ = a.shape
    has_h0 = h0 is not None
    per_seq = lambda n, w: pl.BlockSpec((1, n, w), lambda b: (b, 0, 0))
    coef = pl.BlockSpec((1, 2, seq, LRU_W), lambda b: (b, 0, 0, 0))
    in_specs = [coef, coef]
    args = [a, b]
    out_specs = [per_seq(seq, LRU_W)]
    out_shape = [jax.ShapeDtypeStruct((bsz, seq, LRU_W), F32)]
    if has_h0:
        in_specs.append(per_seq(2, LRU_W))
        args.append(h0)
    else:
        out_specs.append(per_seq(2, LRU_W))
        out_shape.append(jax.ShapeDtypeStruct((bsz, 2, LRU_W), F32))
    outs = pl.pallas_call(
        functools.partial(_lru_kernel, seq=seq, has_h0=has_h0),
        grid=(bsz,),
        in_specs=in_specs,
        out_specs=out_specs,
        out_shape=out_shape,
        compiler_params=_params(("arbitrary",)),
        name="rglru",
    )(*args)
    return (outs[0], None) if has_h0 else tuple(outs)


def _ssd_kernel(*refs, seq, has_h0):
    xs_ref, bm_ref, cm_ref, cs_ref, dts_ref, cst_ref, dtt_ref, bt_ref, dexp_ref, ef_ref, eb_ref = refs[:11]
    if has_h0:
        h0_ref, y_ref, st_scr = refs[11:]
    else:
        y_ref, hout_ref, st_scr = refs[11:]
    q = SSD_Q
    nc = seq // q
    lane = lax.broadcasted_iota(jnp.int32, (q, LANES), 1)
    qi = lax.broadcasted_iota(jnp.int32, (q, q), 0)
    ki = lax.broadcasted_iota(jnp.int32, (q, q), 1)

    def skip_term(c, carry):
        rows = pl.ds(pl.multiple_of(c * q, q), q)
        y_ref[0, rows, :] = dexp_ref[...] * xs_ref[0, rows, :]
        return carry

    lax.fori_loop(0, nc, skip_term, 0)
    st_scr[...] = h0_ref[0] if has_h0 else jnp.zeros(st_scr.shape, F32)

    low = lane < SSD_STATE
    col = lax.broadcasted_iota(jnp.int32, (SSD_STATE, SSD_INNER), 1)
    first_group = col < SSD_INNER // 2

    def chunk(c, fwd):
        d = 0 if fwd else 1
        r0 = pl.multiple_of(c * q, q)
        rows = pl.ds(r0, q)
        cs = cs_ref[0, rows, :]
        dts = dts_ref[0, rows, :]
        cst = cst_ref[0, c]
        dtt = dtt_ref[0, c]
        xs = xs_ref[0, rows, :]
        xsb = xs.astype(BF16)
        bb = bm_ref[0, rows, :].astype(BF16)
        cb_ = cm_ref[0, rows, :].astype(BF16)
        zero = jnp.zeros_like(cb_)
        cbg = [_dot_nt(jnp.where(low, cb_, zero), bb), _dot_nt(jnp.where(low, zero, cb_), bb)]
        mask = (ki <= qi) if fwd else (ki >= qi)
        expand = ef_ref[...] if fwd else eb_ref[...]
        st = st_scr[d]
        hh = jnp.concatenate([jnp.where(first_group, st, 0.0), jnp.where(first_group, 0.0, st)], axis=0)
        tot = cs[q - 1:q, :] if fwd else cs[0:1, :]
        mine = (lane >= d * SSD_HEADS) & (lane < (d + 1) * SSD_HEADS)
        w = jnp.exp(jnp.where(mine, tot - cs, 0.0)) * dts
        dec = jnp.broadcast_to(jnp.exp(tot), (2 * SUBLANES, LANES))
        dec_hi = dec.astype(BF16)
        dec_lo = (dec - dec_hi.astype(F32)).astype(BF16)
        spread = _dot(jnp.concatenate([jnp.exp(cs).astype(BF16), w.astype(BF16), dec_hi, dec_lo], axis=0), expand)
        y = _dot(cb_, hh.astype(BF16)) * spread[0:q]
        pairs = []
        for pp in range(SSD_HEADS // 2):
            ms = []
            for h in (2 * pp, 2 * pp + 1):
                l = d * SSD_HEADS + h
                decay = jnp.where(mask, jnp.exp(cs[:, l:l + 1] - cst[l:l + 1, :]), 0.0)
                ms.append((cbg[pp // (SSD_HEADS // 4)] * decay * dtt[l:l + 1, :]).astype(BF16))
            xp = xsb[:, pp * LANES:(pp + 1) * LANES]
            zx = jnp.zeros_like(xp)
            x2 = jnp.concatenate([jnp.where(low, xp, zx), jnp.where(low, zx, xp)], axis=0)
            pairs.append(_dot(jnp.concatenate(ms, axis=1), x2))
        y_ref[0, rows, :] += y + jnp.concatenate(pairs, axis=1)
        xw = (xs * spread[q:2 * q]).astype(BF16)
        bt = bt_ref[0, c].astype(BF16)
        half_w = SSD_INNER // 2
        st_new = jnp.concatenate([_dot(bt[:SSD_STATE], xw[:, :half_w]), _dot(bt[SSD_STATE:], xw[:, half_w:])], axis=1)
        chunk_decay = spread[2 * q:2 * q + 1] + spread[2 * q + 2 * SUBLANES:2 * q + 2 * SUBLANES + 1]
        st_scr[d] = chunk_decay * st + st_new

    def body(i, carry):
        chunk(i, True)
        chunk(nc - 1 - i, False)
        return carry

    lax.fori_loop(0, nc, body, 0)
    if not has_h0:
        hout_ref[0] = st_scr[...]


def _ssd(xs, bm, cm, cs, dts, cst, dtt, bt, dexp, h0):
    bsz, seq, _ = xs.shape
    nc = seq // SSD_Q
    expand = np.zeros((2, LANES, SSD_INNER), np.float32)
    for d in range(2):
        for h in range(SSD_HEADS):
            expand[d, d * SSD_HEADS + h, h * 64:(h + 1) * 64] = 1.0
    expand = jnp.asarray(expand, dtype=BF16)
    per_seq = lambda n, w: pl.BlockSpec((1, n, w), lambda b: (b, 0, 0))
    state_spec = pl.BlockSpec((1, 2, SSD_STATE, SSD_INNER), lambda b: (b, 0, 0, 0))
    has_h0 = h0 is not None
    chunked = pl.BlockSpec((1, nc, LANES, SSD_Q), lambda b: (b, 0, 0, 0))
    in_specs = ([per_seq(seq, SSD_INNER)] + [per_seq(seq, LANES)] * 4 + [chunked] * 3
                + [_const_spec(dexp.shape), _const_spec((LANES, SSD_INNER)), _const_spec((LANES, SSD_INNER))])
    args = [xs, bm, cm, cs, dts, cst, dtt, bt, dexp, expand[0], expand[1]]
    out_specs = [per_seq(seq, SSD_INNER)]
    out_shape = [jax.ShapeDtypeStruct((bsz, seq, SSD_INNER), F32)]
    if has_h0:
        in_specs.append(state_spec)
        args.append(h0)
    else:
        out_specs.append(state_spec)
        out_shape.append(jax.ShapeDtypeStruct((bsz, 2, SSD_STATE, SSD_INNER), F32))
    outs = pl.pallas_call(
        functools.partial(_ssd_kernel, seq=seq, has_h0=has_h0),
        grid=(bsz,),
        in_specs=in_specs,
        out_specs=out_specs,
        out_shape=out_shape,
        scratch_shapes=[pltpu.VMEM((2, SSD_STATE, SSD_INNER), F32)],
        compiler_params=_params(("arbitrary",)),
        name="ssd",
    )(*args)
    return (outs[0], None) if has_h0 else tuple(outs)


def _lru_gate_weights(wa, ba, wx, bx):
    eye = 0.5 * jnp.eye(2, dtype=F32)

    def pair_blocks(w):
        w4 = w.reshape(4, 2, 64, 64)
        return jnp.einsum("laij,ab->laibj", w4, eye).reshape(4, LANES, LANES)

    mats = [pair_blocks(w) for w in (wa[0], wx[0], wa[1], wx[1])]
    wg = jnp.concatenate(mats, axis=-1).astype(BF16)
    bias = [0.5 * b.reshape(4, 1, LANES) for b in (ba[0], bx[0], ba[1], bx[1])]
    return wg, jnp.concatenate(bias, axis=-1)


def _trunk(x, mods, p, ctx):
    is_ctx = ctx is None
    bsz, seq, _ = x.shape
    f, q, k, v = _even_pre(x, mods[0], p["norm_mix"][0], p["w_in_even"], p["q_gain"], p["k_gain"], rope=not is_ctx)
    mix = _even_mixer(q, k, v, f, None if is_ctx else (ctx["k"], ctx["v"]))
    x = _post_ffn(x, mix, mods[0], p["w_out_even"], p["norm_ffn"][0], p["w1"], p["w3"], p["w2"], layer=0)
    gate, z, lru_a, lru_b, *ssd_in = _odd_pre(
        x, mods[1], p["norm_mix"][1], p["w_in_odd"],
        lru=(p["conv_lru_w"], p["conv_lru_b"], p["lru_wg"], p["lru_bg"], p["lru_lambda"]),
        ssd=(p["conv_ssd_w"], p["conv_ssd_b"], p["ssd_dtb"], p["ssd_alog"]))
    h0_lru, h0_ssd = (None, None) if is_ctx else (ctx["lru"], ctx["ssd"])
    ylru, lru_state = _lru(lru_a, lru_b, h0_lru)
    yssd, ssd_state = _ssd(*ssd_in, p["ssd_dexp"], h0_ssd)
    x = _post_ffn(x, (ylru, gate, yssd, z), mods[1], p["w_out_odd"], p["norm_ffn"][1], p["w1"], p["w3"], p["w2"],
                  layer=1, ssd_norm=p["ssd_norm"])
    return x, k, v, lru_state, ssd_state


def kernel(x_prompt, x_sample, c, cache_k, cache_v, state_lru, state_ssd, c_ctx, w_ada, b_ada, norm_mix, norm_ffn,
           w_in_even, q_norm, k_norm, w_out_even, w_in_odd, conv_lru_w, conv_lru_b, lru_wa, lru_ba, lru_wx, lru_bx,
           lru_lambda, conv_ssd_w, conv_ssd_b, ssd_dt_bias, ssd_a_log, ssd_d, ssd_norm, w_out_odd, ffn_w1, ffn_w3,
           ffn_w2):
    bsz, seq, d = x_prompt.shape
    dec_b = x_sample.shape[0]
    past = cache_k.shape[2]

    cond_rows = 16
    cond = jnp.concatenate([c, c_ctx[None, :], jnp.zeros((cond_rows - dec_b - 1, d), F32)], axis=0)
    mod = _modulation(cond, w_ada, b_ada).reshape(2, cond_rows, 6, d)
    mods_sample = [mod[l, :dec_b] for l in range(2)]
    mods_prompt = [mod[l, dec_b:dec_b + 1] for l in range(2)]

    lru_wg, lru_bg = _lru_gate_weights(lru_wa[0], lru_ba[0], lru_wx[0], lru_bx[0])
    pad32 = lambda a: jnp.pad(a.reshape(1, 2 * SSD_HEADS), ((0, 0), (0, LANES - 2 * SSD_HEADS)))
    p = dict(
        norm_mix=norm_mix.reshape(2, 1, d), norm_ffn=norm_ffn.reshape(2, 1, d),
        w_in_even=w_in_even[0].astype(BF16),
        q_gain=jnp.tile(q_norm[0], N_Q_HEADS)[None, :], k_gain=jnp.tile(k_norm[0], N_KV_HEADS)[None, :],
        w_out_even=w_out_even[0].astype(BF16),
        w_in_odd=jnp.pad(w_in_odd[0].astype(BF16), ((0, 0), (0, ODD_IN_PAD - ODD_IN))),
        conv_lru_w=conv_lru_w[0], conv_lru_b=conv_lru_b[0][None, :],
        lru_wg=lru_wg, lru_bg=lru_bg, lru_lambda=lru_lambda[0],
        conv_ssd_w=conv_ssd_w[0], conv_ssd_b=conv_ssd_b[0][None, :],
        ssd_dtb=pad32(ssd_dt_bias[0]), ssd_alog=pad32(ssd_a_log[0]),
        ssd_dexp=jnp.repeat(ssd_d[0], SSD_INNER // SSD_HEADS)[None, :],
        ssd_norm=ssd_norm[0][None, :],
        w_out_odd=w_out_odd[0].astype(BF16),
        w1=ffn_w1.astype(BF16), w3=ffn_w3.astype(BF16), w2=ffn_w2.astype(BF16),
    )

    y_prompt, k_new, v_new, lru_new, ssd_new = _trunk(x_prompt, mods_prompt, p, None)
    ctx = dict(
        k=cache_k[:, 0].reshape(dec_b, past, KV_W), v=cache_v[:, 0].reshape(dec_b, past, KV_W),
        lru=state_lru[:, 0],
        ssd=state_ssd[:, 0].transpose(0, 1, 4, 2, 3).reshape(dec_b, 2, SSD_STATE, SSD_INNER),
    )
    y_sample, _, _, _, _ = _trunk(x_sample, mods_sample, p, ctx)

    new_k = k_new.reshape(bsz, 1, seq, N_KV_HEADS, HEAD_DIM)
    new_v = v_new.reshape(bsz, 1, seq, N_KV_HEADS, HEAD_DIM)
    new_lru = lru_new.reshape(bsz, 1, 2, LRU_W)
    new_ssd = (ssd_new.reshape(bsz, 2, SSD_STATE, SSD_HEADS, SSD_INNER // SSD_HEADS)
               .transpose(0, 1, 3, 4, 2).reshape(bsz, 1, 2, SSD_HEADS, SSD_INNER // SSD_HEADS, SSD_STATE))
    return (y_prompt, y_sample, new_k, new_v, new_lru, new_ssd)
```

```python
import functools
import math

import numpy as np
import jax
import jax.numpy as jnp
from jax import lax
from jax.experimental import pallas as pl
from jax.experimental.pallas import tpu as pltpu

F32 = jnp.float32
BF16 = jnp.bfloat16

D_MODEL = 1024
EPS = 1e-6
GRID_W = 64
HEAD_DIM = 64
N_Q_HEADS = 12
N_KV_HEADS = 4
FOURIER_W = 256
FOURIER_GROUP_W = 64
Q_W = N_Q_HEADS * HEAD_DIM
KV_W = N_KV_HEADS * HEAD_DIM
EVEN_IN = FOURIER_W + Q_W + 2 * KV_W
ROPE_THETA = 10000.0
AX_PAIRS = HEAD_DIM // 4
LRU_W = 512
LRU_C = 8.0
SSD_INNER = 1024
SSD_HEADS = 16
SSD_STATE = 64
SSD_CONV_CH = SSD_INNER + 4 * SSD_STATE
ODD_IN = 2 * LRU_W + SSD_INNER + SSD_CONV_CH + 2 * SSD_HEADS
D_FF = 2816
LANES = 128
SUBLANES = 8
TOKEN_BLOCK = 512
ROW_TILE = 256
SSD_Q = 128
V_ROWS = HEAD_DIM + 16
VMEM_LIMIT = 56 * 1024 * 1024


def _params(sem, vmem=VMEM_LIMIT):
    return pltpu.CompilerParams(dimension_semantics=sem, vmem_limit_bytes=vmem)


def _const_spec(shape):
    nd = len(shape)
    return pl.BlockSpec(shape, lambda *_: (0,) * nd, pipeline_mode=pl.Buffered(1))


def _dot(a, b):
    return jnp.dot(a, b, preferred_element_type=F32)


def _dot_nt(a, b):
    return lax.dot_general(a, b, (((1,), (1,)), ((), ())), preferred_element_type=F32)


def _dot3(m, a):
    hi = a.astype(BF16)
    r1 = a - hi.astype(F32)
    mid = r1.astype(BF16)
    lo = (r1 - mid.astype(F32)).astype(BF16)
    return _dot(m, hi) + _dot(m, mid) + _dot(m, lo)


def _silu(x):
    return x * jax.nn.sigmoid(x)


def _sigmoid_tanh(x):
    return 0.5 * jnp.tanh(0.5 * x) + 0.5


def _softplus(x):
    return jnp.maximum(x, 0.0) + jnp.log1p(jnp.exp(-jnp.abs(x)))


def _gelu_tanh(x):
    c = math.sqrt(2.0 / math.pi)
    return 0.5 * x * (1.0 + jnp.tanh(c * (x + 0.044715 * (x * x * x))))


def _modnorm(x, g, shift, scale):
    ms = jnp.mean(x * x, axis=-1, keepdims=True)
    y = x * lax.rsqrt(ms + EPS) * g
    return y * (1.0 + scale) + shift


def _mod_kernel(cond_ref, w_ref, b_ref, o_ref):
    s = _silu(cond_ref[...])
    o_ref[0] = _dot(s.astype(BF16), w_ref[0].astype(BF16)) + b_ref[0]


def _modulation(cond, w_ada, b_ada):
    depth, d, n = w_ada.shape
    rows = cond.shape[0]
    tn = 1536
    return pl.pallas_call(
        _mod_kernel,
        grid=(depth, n // tn),
        in_specs=[pl.BlockSpec((rows, d), lambda l, j: (0, 0)),
                  pl.BlockSpec((1, d, tn), lambda l, j: (l, 0, j)),
                  pl.BlockSpec((1, 1, tn), lambda l, j: (l, 0, j))],
        out_specs=pl.BlockSpec((1, rows, tn), lambda l, j: (l, 0, j)),
        out_shape=jax.ShapeDtypeStruct((depth, rows, n), F32),
        compiler_params=_params(("arbitrary", "arbitrary")),
        name="modulation",
    )(cond, w_ada, b_ada.reshape(depth, 1, n))


def _mod_spec(mod):
    if mod.shape[0] == 1:
        return pl.BlockSpec((1, 6, D_MODEL), lambda b, i: (0, 0, 0))
    return pl.BlockSpec((1, 6, D_MODEL), lambda b, i: (b, 0, 0))


def _head_rmsnorm(x, gain):
    rows, width = x.shape
    low = lax.broadcasted_iota(jnp.int32, (rows, LANES), 1) < HEAD_DIM
    out = []
    for b in range(width // LANES):
        blk = x[:, b * LANES:(b + 1) * LANES]
        sq = blk * blk
        lo = jnp.sum(jnp.where(low, sq, 0.0), axis=-1, keepdims=True)
        hi = jnp.sum(jnp.where(low, 0.0, sq), axis=-1, keepdims=True)
        ms = jnp.where(low, lo, hi) * (1.0 / HEAD_DIM)
        out.append(blk * lax.rsqrt(ms + EPS) * gain[:, b * LANES:(b + 1) * LANES])
    return jnp.concatenate(out, axis=-1)


def _rope(x, cos, sin_signed):
    rows, width = x.shape
    lane = lax.broadcasted_iota(jnp.int32, (rows, LANES), 1)
    second = (lane % (2 * AX_PAIRS)) >= AX_PAIRS
    out = []
    for b in range(width // LANES):
        blk = x[:, b * LANES:(b + 1) * LANES]
        partner = jnp.where(second, pltpu.roll(blk, AX_PAIRS, 1), pltpu.roll(blk, LANES - AX_PAIRS, 1))
        out.append(blk * cos + partner * sin_signed)
    return jnp.concatenate(out, axis=-1)


def _even_pre_kernel(*refs, rope):
    if rope:
        x_ref, mod_ref, g_ref, w_ref, qg_ref, kg_ref, cos_ref, sin_ref, f_ref, q_ref, k_ref, v_ref = refs
    else:
        x_ref, mod_ref, g_ref, w_ref, qg_ref, kg_ref, f_ref, q_ref, k_ref, v_ref = refs
    h = _modnorm(x_ref[0], g_ref[...], mod_ref[0, 0:1, :], mod_ref[0, 1:2, :])
    p = _dot(h.astype(BF16), w_ref[...])
    f_ref[0] = p[:, :FOURIER_W]
    q = _head_rmsnorm(p[:, FOURIER_W:FOURIER_W + Q_W], qg_ref[...])
    k = _head_rmsnorm(p[:, FOURIER_W + Q_W:FOURIER_W + Q_W + KV_W], kg_ref[...])
    if rope:
        q = _rope(q, cos_ref[...], sin_ref[...])
        k = _rope(k, cos_ref[...], sin_ref[...])
    q_ref[0] = (q * (HEAD_DIM ** -0.5 * math.log2(math.e))).astype(BF16)
    k_ref[0] = k
    v_ref[0] = p[:, FOURIER_W + Q_W + KV_W:]


def _rope_tables(seq):
    pos = np.arange(seq)
    freqs = ROPE_THETA ** (-np.arange(AX_PAIRS, dtype=np.float64) / AX_PAIRS)
    ang = np.zeros((seq, HEAD_DIM))
    sign = np.zeros((HEAD_DIM,))
    for a, p_a in enumerate((pos // GRID_W, pos % GRID_W)):
        for t in range(2):
            lo = a * 2 * AX_PAIRS + t * AX_PAIRS
            ang[:, lo:lo + AX_PAIRS] = p_a[:, None] * freqs[None, :]
            sign[lo:lo + AX_PAIRS] = -1.0 if t == 0 else 1.0
    cos = np.tile(np.cos(ang), (1, 2)).astype(np.float32)
    sin = np.tile(np.sin(ang) * sign[None, :], (1, 2)).astype(np.float32)
    return jnp.asarray(cos), jnp.asarray(sin)


def _even_pre(x, mod, g, w_in, qg, kg, rope):
    bsz, seq, d = x.shape
    tm = ROW_TILE
    tok = lambda w: pl.BlockSpec((1, tm, w), lambda b, i: (b, i, 0))
    in_specs = [tok(d), _mod_spec(mod), _const_spec((1, d)), _const_spec((d, EVEN_IN)),
                _const_spec((1, Q_W)), _const_spec((1, KV_W))]
    args = [x, mod, g, w_in, qg, kg]
    if rope:
        cos, sin = _rope_tables(seq)
        in_specs += [pl.BlockSpec((tm, LANES), lambda b, i: (i, 0))] * 2
        args += [cos, sin]
    return pl.pallas_call(
        functools.partial(_even_pre_kernel, rope=rope),
        grid=(bsz, seq // tm),
        in_specs=in_specs,
        out_specs=[tok(FOURIER_W), tok(Q_W), tok(KV_W), tok(KV_W)],
        out_shape=[jax.ShapeDtypeStruct((bsz, seq, FOURIER_W), F32),
                   jax.ShapeDtypeStruct((bsz, seq, Q_W), BF16),
                   jax.ShapeDtypeStruct((bsz, seq, KV_W), F32),
                   jax.ShapeDtypeStruct((bsz, seq, KV_W), F32)],
        compiler_params=_params(("arbitrary", "arbitrary")),
        name="even_pre",
    )(*args)


def _attn_kernel(*refs, seq, has_ctx):
    if has_ctx:
        q_ref, k_ref, v_ref, ck_ref, cv_ref, f_ref, cs_ref, cbd_ref, o_ref, kop, vopt, g_scr, cs_scr = refs
    else:
        q_ref, k_ref, v_ref, f_ref, cs_ref, cbd_ref, o_ref, kop, vopt, g_scr, cs_scr = refs
    tq = q_ref.shape[1]

    @pl.when((pl.program_id(0) == 0) & (pl.program_id(1) == 0))
    def _round_tables():
        for r0 in range(0, seq, tq):
            cs_scr[r0:r0 + tq, :] = cs_ref[r0:r0 + tq, :].astype(BF16)

    @pl.when(pl.program_id(1) == 0)
    def _prepare():
        def put(kx, vx, r0):
            n = kx.shape[0]
            low = lax.broadcasted_iota(jnp.int32, (n, LANES), 1) < HEAD_DIM
            ones = jnp.ones((V_ROWS - HEAD_DIM, n), BF16)
            for jp in range(N_KV_HEADS // 2):
                kp = kx[:, jp * LANES:(jp + 1) * LANES]
                k_lo = jnp.where(low, kp, 0.0)
                k_hi = jnp.where(low, 0.0, kp)
                placed = (k_lo, pltpu.roll(k_lo, HEAD_DIM, 1), pltpu.roll(k_hi, HEAD_DIM, 1), k_hi)
                for t, kk in enumerate(placed):
                    kop[4 * jp + t, r0:r0 + n, :] = kk.astype(BF16)
                vt = vx[:, jp * LANES:(jp + 1) * LANES].T.astype(BF16)
                for t in range(2):
                    vopt[2 * jp + t, 0:HEAD_DIM, r0:r0 + n] = vt[t * HEAD_DIM:(t + 1) * HEAD_DIM]
                    vopt[2 * jp + t, HEAD_DIM:V_ROWS, r0:r0 + n] = ones

        put(k_ref[0], v_ref[0], 0)
        if has_ctx:
            put(ck_ref[0], cv_ref[0], seq)
        f = f_ref[0].astype(BF16)
        g_scr[0:seq, :] = _dot(f, cbd_ref[0].astype(BF16)).astype(BF16)
        g_scr[seq:2 * seq, :] = _dot(f, cbd_ref[1].astype(BF16)).astype(BF16)

    r0 = pl.multiple_of(pl.program_id(1) * tq, tq)
    o_ref[0, :, 0:FOURIER_W] = _dot(cs_scr[pl.ds(r0, tq), :], g_scr[...]).astype(BF16)
    q = q_ref[0]
    n_pairs = N_Q_HEADS // 2
    kv_of = lambda qp: [(2 * qp + half) // (N_Q_HEADS // N_KV_HEADS) for half in range(2)]

    def pair_scores(qp):
        qpair = q[:, qp * LANES:(qp + 1) * LANES]
        return [_dot_nt(qpair, kop[2 * kv_of(qp)[half] + half]) for half in range(2)]

    scores_next = pair_scores(0)
    for qp in range(n_pairs):
        kv = kv_of(qp)
        scores = scores_next
        if qp + 1 < n_pairs:
            scores_next = pair_scores(qp + 1)
        probs = [jnp.exp2(s - jnp.max(s, axis=-1, keepdims=True)).astype(BF16) for s in scores]
        outs = []
        for half in range(2):
            o = _dot_nt(vopt[kv[half]], probs[half])
            outs.append(o[0:HEAD_DIM] / o[HEAD_DIM:HEAD_DIM + 1])
        res = jnp.concatenate(outs, axis=0)
        o_ref[0, :, FOURIER_W + qp * LANES:FOURIER_W + (qp + 1) * LANES] = res.T.astype(BF16)


def _dft_tables(seq):
    s = np.arange(seq)
    ang = 2.0 * np.pi * ((s[:, None] * s[None, :]) % seq) / seq
    pos = np.concatenate([np.cos(ang), -np.sin(ang)], axis=1) / math.sqrt(seq)
    c = np.arange(FOURIER_GROUP_W)
    ang_c = 2.0 * np.pi * ((c[:, None] * c[None, :]) % FOURIER_GROUP_W) / FOURIER_GROUP_W
    eye = np.eye(FOURIER_W // FOURIER_GROUP_W)
    chan = np.stack([np.kron(eye, np.cos(ang_c)), np.kron(eye, np.sin(ang_c))]) / math.sqrt(FOURIER_GROUP_W)
    return jnp.asarray(pos, dtype=F32), jnp.asarray(chan, dtype=F32)


def _even_mixer(q, k, v, f, ctx):
    bsz, seq, _ = q.shape
    tq = ROW_TILE
    has_ctx = ctx is not None
    past = ctx[0].shape[1] if has_ctx else 0
    kt = seq + past
    pos_dft, chan_dft = _dft_tables(seq)
    full = lambda n, w: pl.BlockSpec((1, n, w), lambda b, i: (b, 0, 0))
    in_specs = [pl.BlockSpec((1, tq, Q_W), lambda b, i: (b, i, 0)), full(seq, KV_W), full(seq, KV_W)]
    args = [q, k, v]
    if has_ctx:
        in_specs += [full(past, KV_W), full(past, KV_W)]
        args += list(ctx)
    in_specs += [full(seq, FOURIER_W), _const_spec((seq, 2 * seq)),
                 _const_spec((2, FOURIER_W, FOURIER_W))]
    args += [f, pos_dft, chan_dft]
    return pl.pallas_call(
        functools.partial(_attn_kernel, seq=seq, has_ctx=has_ctx),
        grid=(bsz, seq // tq),
        in_specs=in_specs,
        out_specs=pl.BlockSpec((1, tq, D_MODEL), lambda b, i: (b, i, 0)),
        out_shape=jax.ShapeDtypeStruct((bsz, seq, D_MODEL), BF16),
        scratch_shapes=[pltpu.VMEM((2 * N_KV_HEADS, kt, LANES), BF16),
                        pltpu.VMEM((N_KV_HEADS, V_ROWS, kt), BF16),
                        pltpu.VMEM((2 * seq, FOURIER_W), BF16),
                        pltpu.VMEM((seq, 2 * seq), BF16)],
        compiler_params=_params(("arbitrary", "arbitrary")),
        name="even_mixer",
    )(*args)


def _post_ffn_kernel(*refs, odd):
    if odd:
        (x_ref, ylru_ref, g_ref, yssd_ref, z_ref, sn_ref, mod_ref, wout_ref, gf_ref,
         w1_ref, w3_ref, w2_ref, o_ref) = refs
        y_lru = ylru_ref[0] * _gelu_tanh(g_ref[0])
        y = yssd_ref[0] * _silu(z_ref[0])
        ms = jnp.mean(y * y, axis=-1, keepdims=True)
        y_ssd = y * lax.rsqrt(ms + EPS) * sn_ref[...]
        mix = jnp.concatenate([y_lru, y_ssd], axis=-1).astype(BF16)
    else:
        x_ref, mix_ref, mod_ref, wout_ref, gf_ref, w1_ref, w3_ref, w2_ref, o_ref = refs
        mix = mix_ref[0]
    x = x_ref[0] + mod_ref[0, 2:3, :] * _dot(mix, wout_ref[...])
    h = _modnorm(x, gf_ref[...], mod_ref[0, 3:4, :], mod_ref[0, 4:5, :]).astype(BF16)
    a = (_silu(_dot(h, w1_ref[...])) * _dot(h, w3_ref[...])).astype(BF16)
    o_ref[0] = x + mod_ref[0, 5:6, :] * _dot(a, w2_ref[...])


def _post_ffn(x, mix_args, mod, w_out, g_ffn, w1, w3, w2, layer, ssd_norm=None):
    bsz, seq, d = x.shape
    if mod.shape[0] == 1 and seq < TOKEN_BLOCK:
        fold = TOKEN_BLOCK // seq
        merge = lambda t: t.reshape(bsz // fold, fold * seq, t.shape[-1])
        out = _post_ffn(merge(x), jax.tree.map(merge, mix_args), mod, w_out, g_ffn, w1, w3, w2, layer, ssd_norm)
        return out.reshape(bsz, seq, d)
    tm = min(seq, TOKEN_BLOCK)
    odd = layer % 2 == 1
    tok = lambda w: pl.BlockSpec((1, tm, w), lambda b, i: (b, i, 0))
    slab = lambda w: pl.BlockSpec((None,) + w.shape[1:], lambda b, i: (layer, 0, 0), pipeline_mode=pl.Buffered(1))
    if odd:
        ylru, g, yssd, z = mix_args
        in_specs = [tok(d), tok(LRU_W), tok(LRU_W), tok(SSD_INNER), tok(SSD_INNER), _const_spec((1, SSD_INNER))]
        args = [x, ylru, g, yssd, z, ssd_norm]
    else:
        in_specs = [tok(d), tok(d)]
        args = [x, mix_args]
    in_specs += [_mod_spec(mod), _const_spec(w_out.shape), _const_spec((1, d)),
                 slab(w1), slab(w3), slab(w2)]
    args += [mod, w_out, g_ffn, w1, w3, w2]
    return pl.pallas_call(
        functools.partial(_post_ffn_kernel, odd=odd),
        grid=(bsz, seq // tm),
        in_specs=in_specs,
        out_specs=tok(d),
        out_shape=jax.ShapeDtypeStruct((bsz, seq, d), F32),
        compiler_params=_params(("arbitrary", "arbitrary")),
        name="post_ffn_odd" if odd else "post_ffn_even",
    )(*args)


XL_LO = LRU_W
Z_LO = 2 * LRU_W
XBC_LO = 2 * LRU_W + SSD_INNER
DT_LO = XBC_LO + SSD_CONV_CH


def _conv_block(main, edges, w, b):
    before, after0, after1 = edges
    rows = main.shape[0]
    row = lax.broadcasted_iota(jnp.int32, (rows, 1), 0)
    xm1 = jnp.where(row == 0, before, pltpu.roll(main, 1, 0))
    xp1 = jnp.where(row == rows - 1, after0, pltpu.roll(main, rows - 1, 0))
    xp2 = jnp.where(row == rows - 2, after0, jnp.where(row == rows - 1, after1, pltpu.roll(main, rows - 2, 0)))
    return xm1 * w[0:1] + main * w[1:2] + xp1 * w[2:3] + xp2 * w[3:4] + b


def _odd_pre_kernel(x_ref, xprev_ref, xnext_ref, mod_ref, g_ref, w_ref, wdt_ref, cwl_ref, cbl_ref, wg_ref, bg_ref,
                    lam_ref, gate_ref, z_ref, a_ref, b_ref, xbc_ref, dt_ref):
    i = pl.program_id(1)
    g, shift, scale = g_ref[...], mod_ref[0, 0:1, :], mod_ref[0, 1:2, :]
    h = _modnorm(x_ref[0], g, shift, scale).astype(BF16)
    wide = 2 * LANES
    proj = lambda lo: _dot(h, w_ref[:, lo:lo + wide])
    p_xl = [proj(XL_LO + t * wide) for t in range(LRU_W // wide)]
    others = ([(gate_ref, t * wide, t * wide) for t in range(LRU_W // wide)]
              + [(z_ref, Z_LO + t * wide, t * wide) for t in range(SSD_INNER // wide)]
              + [(xbc_ref, XBC_LO + t * wide, t * wide) for t in range(SSD_CONV_CH // wide)])

    halo = jnp.concatenate([xprev_ref[0], xnext_ref[0]], axis=0)
    ph = _dot(_modnorm(halo, g, shift, scale).astype(BF16), w_ref[:, XL_LO:Z_LO])
    has_prev = i > 0
    has_next = i < pl.num_programs(1) - 1
    e_l = (jnp.where(has_prev, ph[SUBLANES - 1:SUBLANES], 0.0),
           jnp.where(has_next, ph[SUBLANES:SUBLANES + 1], 0.0),
           jnp.where(has_next, ph[SUBLANES + 1:SUBLANES + 2], 0.0))

    rate = (0.5 * LRU_C) * _softplus(-lam_ref[...])
    n_lru = LRU_W // LANES
    for lb in range(n_lru):
        sl = slice(lb * LANES, (lb + 1) * LANES)
        for dst, lo, col in others[lb * len(others) // n_lru:(lb + 1) * len(others) // n_lru]:
            dst[0, :, col:col + wide] = proj(lo)
        xb = _conv_block(p_xl[lb // 2][:, (lb % 2) * LANES:(lb % 2 + 1) * LANES], [e[:, sl] for e in e_l],
                         cwl_ref[:, sl], cbl_ref[:, sl])
        xh = 0.5 * xb
        gt = jnp.tanh(_dot(xb.astype(BF16), wg_ref[lb]) + bg_ref[lb])
        for d in range(2):
            neg_log_a = rate[d:d + 1, sl] * (gt[:, 2 * d * LANES:(2 * d + 1) * LANES] + 1.0)
            a = jnp.exp2(neg_log_a * (-math.log2(math.e)))
            g2 = jnp.tanh(neg_log_a) * (a * a + 1.0)
            root = jnp.where(g2 > 0.0, g2 * lax.rsqrt(g2), 0.0)
            a_ref[0, d, :, sl] = a
            b_ref[0, d, :, sl] = root * ((gt[:, (2 * d + 1) * LANES:(2 * d + 2) * LANES] + 1.0) * xh)
    dt_ref[0] = _dot(h, wdt_ref[...])


def _odd_pre(x, mod, g, w_in, w_dt, lru):
    bsz, seq, d = x.shape
    tm = min(seq, TOKEN_BLOCK)
    per8 = tm // SUBLANES
    tok = lambda w: pl.BlockSpec((1, tm, w), lambda b, i: (b, i, 0))
    both = pl.BlockSpec((1, 2, tm, LRU_W), lambda b, i: (b, 0, i, 0))
    consts = list(lru)
    out_specs = [tok(LRU_W), tok(SSD_INNER), both, both, tok(SSD_CONV_CH), tok(LANES)]
    row = lambda w: jax.ShapeDtypeStruct((bsz, seq, w), F32)
    out_shape = ([row(LRU_W), row(SSD_INNER)] + [jax.ShapeDtypeStruct((bsz, 2, seq, LRU_W), F32)] * 2
                 + [row(SSD_CONV_CH), row(LANES)])
    return pl.pallas_call(
        _odd_pre_kernel,
        grid=(bsz, seq // tm),
        in_specs=[tok(d),
                  pl.BlockSpec((1, SUBLANES, d), lambda b, i: (b, jnp.maximum(i * per8 - 1, 0), 0)),
                  pl.BlockSpec((1, SUBLANES, d), lambda b, i: (b, jnp.minimum((i + 1) * per8, seq // SUBLANES - 1), 0)),
                  _mod_spec(mod), _const_spec((1, d)), _const_spec(w_in.shape), _const_spec(w_dt.shape)]
                 + [_const_spec(c.shape) for c in consts],
        out_specs=out_specs,
        out_shape=out_shape,
        compiler_params=_params(("arbitrary", "arbitrary")),
        name="odd_pre",
    )(x, x, x, mod, g, w_in, w_dt, *consts)


def _lru_kernel(*refs, seq, has_h0):
    if has_h0:
        a_ref, b_ref, h0_ref, y_ref = refs
    else:
        a_ref, b_ref, y_ref, hout_ref = refs
    y_ref[0] = jnp.zeros((seq, LRU_W), F32)

    nb = seq // SUBLANES
    row = lax.broadcasted_iota(jnp.int32, (SUBLANES, LRU_W), 0)

    def scan(i, carry):
        hf, hb = carry
        rf = pl.multiple_of(i * SUBLANES, SUBLANES)
        rb = pl.multiple_of((nb - 1 - i) * SUBLANES, SUBLANES)
        a = a_ref[0, 0, pl.ds(rf, SUBLANES), :]
        b = b_ref[0, 0, pl.ds(rf, SUBLANES), :]
        for k in (1, 2, 4):
            keep = row >= k
            b = a * jnp.where(keep, pltpu.roll(b, k, 0), 0.0) + b
            a = a * jnp.where(keep, pltpu.roll(a, k, 0), 1.0)
        h = a * hf + b
        y_ref[0, pl.ds(rf, SUBLANES), :] += h
        hf = h[SUBLANES - 1:SUBLANES, :]
        a = a_ref[0, 1, pl.ds(rb, SUBLANES), :]
        b = b_ref[0, 1, pl.ds(rb, SUBLANES), :]
        for k in (1, 2, 4):
            keep = row < SUBLANES - k
            b = a * jnp.where(keep, pltpu.roll(b, SUBLANES - k, 0), 0.0) + b
            a = a * jnp.where(keep, pltpu.roll(a, SUBLANES - k, 0), 1.0)
        h = a * hb + b
        y_ref[0, pl.ds(rb, SUBLANES), :] += h
        hb = h[0:1, :]
        return hf, hb

    if has_h0:
        lax.fori_loop(0, nb, scan, (h0_ref[0, 0:1, :], h0_ref[0, 1:2, :]))
    else:
        zero = jnp.zeros((1, LRU_W), F32)
        hf, hb = lax.fori_loop(0, nb, scan, (zero, zero))
        hout_ref[0, 0:1, :] = hf
        hout_ref[0, 1:2, :] = hb


def _lru(a, b, h0):
    bsz, _, seq, _ = a.shape
    has_h0 = h0 is not None
    per_seq = lambda n, w: pl.BlockSpec((1, n, w), lambda b: (b, 0, 0))
    coef = pl.BlockSpec((1, 2, seq, LRU_W), lambda b: (b, 0, 0, 0))
    in_specs = [coef, coef]
    args = [a, b]
    out_specs = [per_seq(seq, LRU_W)]
    out_shape = [jax.ShapeDtypeStruct((bsz, seq, LRU_W), F32)]
    if has_h0:
        in_specs.append(per_seq(2, LRU_W))
        args.append(h0)
    else:
        out_specs.append(per_seq(2, LRU_W))
        out_shape.append(jax.ShapeDtypeStruct((bsz, 2, LRU_W), F32))
    outs = pl.pallas_call(
        functools.partial(_lru_kernel, seq=seq, has_h0=has_h0),
        grid=(bsz,),
        in_specs=in_specs,
        out_specs=out_specs,
        out_shape=out_shape,
        compiler_params=_params(("arbitrary",)),
        name="rglru",
    )(*args)
    return (outs[0], None) if has_h0 else tuple(outs)


def _ssd_kernel(*refs, seq, has_h0):
    xbc_ref, dt_ref, cw_ref, cb_ref, dtb_ref, alog_ref, dexp_ref, ef_ref, eb_ref = refs[:9]
    if has_h0:
        h0_ref, y_ref = refs[9:11]
    else:
        y_ref, hout_ref = refs[9:11]
    xs_ref, bm_ref, cm_ref, cs_ref, dts_ref, cst_ref, dtt_ref, bt_ref, st_scr = refs[11:]
    q = SSD_Q
    nc = seq // q
    lane = lax.broadcasted_iota(jnp.int32, (q, LANES), 1)
    qi = lax.broadcasted_iota(jnp.int32, (q, q), 0)
    ki = lax.broadcasted_iota(jnp.int32, (q, q), 1)
    tri_lower = jnp.where(ki <= qi, 1.0, 0.0).astype(BF16)
    tri_upper = jnp.where(ki >= qi, 1.0, 0.0).astype(BF16)
    a_neg = jnp.where(lane[0:1, :] < 2 * SSD_HEADS, -jnp.exp(alog_ref[...]), 0.0)

    def prepare(c):
        r0 = pl.multiple_of(c * q, q)
        rows = pl.ds(r0, q)
        prev8 = pl.ds(pl.multiple_of(jnp.maximum(r0 - SUBLANES, 0), SUBLANES), SUBLANES)
        next8 = pl.ds(pl.multiple_of(jnp.minimum(r0 + q, seq - SUBLANES), SUBLANES), SUBLANES)
        has_prev = r0 > 0
        has_next = r0 + q < seq
        for lb in range(SSD_CONV_CH // LANES):
            sl = slice(lb * LANES, (lb + 1) * LANES)
            before = xbc_ref[0, prev8, sl]
            after = xbc_ref[0, next8, sl]
            edges = (jnp.where(has_prev, before[SUBLANES - 1:SUBLANES], 0.0),
                     jnp.where(has_next, after[0:1], 0.0), jnp.where(has_next, after[1:2], 0.0))
            xc = _conv_block(xbc_ref[0, rows, sl], edges, cw_ref[:, sl], cb_ref[:, sl])
            xc = xc * _sigmoid_tanh(xc)
            if lb < SSD_INNER // LANES:
                xs_ref[rows, sl] = xc
                y_ref[0, rows, sl] = dexp_ref[:, sl] * xc
            elif lb == SSD_INNER // LANES:
                bm_ref[rows, :] = xc
                bt_ref[c] = xc.T
            else:
                cm_ref[rows, :] = xc
        dts = _softplus(dt_ref[0, rows, :] + dtb_ref[...])
        da = dts * a_neg
        cs = jnp.where(lane < SSD_HEADS, _dot3(tri_lower, da), _dot3(tri_upper, da))
        cs_ref[rows, :] = cs
        dts_ref[rows, :] = dts
        cst_ref[c] = cs.T
        dtt_ref[c] = dts.T

    st_scr[...] = h0_ref[0] if has_h0 else jnp.zeros(st_scr.shape, F32)

    low = lane < SSD_STATE
    col = lax.broadcasted_iota(jnp.int32, (SSD_STATE, SSD_INNER), 1)
    first_group = col < SSD_INNER // 2

    def chunk(c, fwd):
        d = 0 if fwd else 1
        r0 = pl.multiple_of(c * q, q)
        rows = pl.ds(r0, q)
        cs = cs_ref[rows, :]
        dts = dts_ref[rows, :]
        cst = cst_ref[c]
        dtt = dtt_ref[c]
        xs = xs_ref[rows, :]
        xsb = xs.astype(BF16)
        bb = bm_ref[rows, :].astype(BF16)
        cb_ = cm_ref[rows, :].astype(BF16)
        zero = jnp.zeros_like(cb_)
        cbg = [_dot_nt(jnp.where(low, cb_, zero), bb), _dot_nt(jnp.where(low, zero, cb_), bb)]
        mask = (ki <= qi) if fwd else (ki >= qi)
        expand = ef_ref[...] if fwd else eb_ref[...]
        st = st_scr[d]
        hh = jnp.concatenate([jnp.where(first_group, st, 0.0), jnp.where(first_group, 0.0, st)], axis=0)
        tot = cs[q - 1:q, :] if fwd else cs[0:1, :]
        mine = (lane >= d * SSD_HEADS) & (lane < (d + 1) * SSD_HEADS)
        w = jnp.exp(jnp.where(mine, tot - cs, 0.0)) * dts
        dec = jnp.broadcast_to(jnp.exp(tot), (2 * SUBLANES, LANES))
        dec_hi = dec.astype(BF16)
        dec_lo = (dec - dec_hi.astype(F32)).astype(BF16)
        spread = _dot(jnp.concatenate([jnp.exp(cs).astype(BF16), w.astype(BF16), dec_hi, dec_lo], axis=0), expand)
        y = _dot(cb_, hh.astype(BF16)) * spread[0:q]
        pairs = []
        for pp in range(SSD_HEADS // 2):
            ms = []
            for h in (2 * pp, 2 * pp + 1):
                l = d * SSD_HEADS + h
                decay = jnp.where(mask, jnp.exp(cs[:, l:l + 1] - cst[l:l + 1, :]), 0.0)
                ms.append((cbg[pp // (SSD_HEADS // 4)] * decay * dtt[l:l + 1, :]).astype(BF16))
            xp = xsb[:, pp * LANES:(pp + 1) * LANES]
            zx = jnp.zeros_like(xp)
            x2 = jnp.concatenate([jnp.where(low, xp, zx), jnp.where(low, zx, xp)], axis=0)
            pairs.append(_dot(jnp.concatenate(ms, axis=1), x2))
        y_ref[0, rows, :] += y + jnp.concatenate(pairs, axis=1)
        xw = (xs * spread[q:2 * q]).astype(BF16)
        bt = bt_ref[c].astype(BF16)
        half_w = SSD_INNER // 2
        st_new = jnp.concatenate([_dot(bt[:SSD_STATE], xw[:, :half_w]), _dot(bt[SSD_STATE:], xw[:, half_w:])], axis=1)
        chunk_decay = spread[2 * q:2 * q + 1] + spread[2 * q + 2 * SUBLANES:2 * q + 2 * SUBLANES + 1]
        st_scr[d] = chunk_decay * st + st_new

    def prepare_all(c, carry):
        prepare(c)
        return carry

    def both_directions(i, carry):
        chunk(i, True)
        chunk(nc - 1 - i, False)
        return carry

    lax.fori_loop(0, nc, prepare_all, 0)
    lax.fori_loop(0, nc, both_directions, 0)
    if not has_h0:
        hout_ref[0] = st_scr[...]


def _ssd(xbc, dt, cw, cb, dtb, alog, dexp, h0):
    bsz, seq, _ = xbc.shape
    nc = seq // SSD_Q
    expand = np.zeros((2, LANES, SSD_INNER), np.float32)
    for d in range(2):
        for h in range(SSD_HEADS):
            expand[d, d * SSD_HEADS + h, h * 64:(h + 1) * 64] = 1.0
    expand = jnp.asarray(expand, dtype=BF16)
    per_seq = lambda n, w: pl.BlockSpec((1, n, w), lambda b: (b, 0, 0))
    state_spec = pl.BlockSpec((1, 2, SSD_STATE, SSD_INNER), lambda b: (b, 0, 0, 0))
    has_h0 = h0 is not None
    in_specs = [per_seq(seq, SSD_CONV_CH), per_seq(seq, LANES), _const_spec(cw.shape), _const_spec(cb.shape),
                _const_spec(dtb.shape), _const_spec(alog.shape), _const_spec(dexp.shape),
                _const_spec((LANES, SSD_INNER)), _const_spec((LANES, SSD_INNER))]
    args = [xbc, dt, cw, cb, dtb, alog, dexp, expand[0], expand[1]]
    out_specs = [per_seq(seq, SSD_INNER)]
    out_shape = [jax.ShapeDtypeStruct((bsz, seq, SSD_INNER), F32)]
    if has_h0:
        in_specs.append(state_spec)
        args.append(h0)
    else:
        out_specs.append(state_spec)
        out_shape.append(jax.ShapeDtypeStruct((bsz, 2, SSD_STATE, SSD_INNER), F32))
    outs = pl.pallas_call(
        functools.partial(_ssd_kernel, seq=seq, has_h0=has_h0),
        grid=(bsz,),
        in_specs=in_specs,
        out_specs=out_specs,
        out_shape=out_shape,
        scratch_shapes=[pltpu.VMEM((seq, SSD_INNER), F32)] + [pltpu.VMEM((seq, LANES), F32)] * 4
                       + [pltpu.VMEM((nc, LANES, SSD_Q), F32)] * 3 + [pltpu.VMEM((2, SSD_STATE, SSD_INNER), F32)],
        compiler_params=_params(("arbitrary",)),
        name="ssd",
    )(*args)
    return (outs[0], None) if has_h0 else tuple(outs)


def _lru_gate_weights(wa, ba, wx, bx):
    eye = 0.5 * jnp.eye(2, dtype=F32)

    def pair_blocks(w):
        w4 = w.reshape(4, 2, 64, 64)
        return jnp.einsum("laij,ab->laibj", w4, eye).reshape(4, LANES, LANES)

    mats = [pair_blocks(w) for w in (wa[0], wx[0], wa[1], wx[1])]
    wg = jnp.concatenate(mats, axis=-1).astype(BF16)
    bias = [0.5 * b.reshape(4, 1, LANES) for b in (ba[0], bx[0], ba[1], bx[1])]
    return wg, jnp.concatenate(bias, axis=-1)


def _trunk(x, mods, p, ctx):
    is_ctx = ctx is None
    bsz, seq, _ = x.shape
    f, q, k, v = _even_pre(x, mods[0], p["norm_mix"][0], p["w_in_even"], p["q_gain"], p["k_gain"], rope=not is_ctx)
    mix = _even_mixer(q, k, v, f, None if is_ctx else (ctx["k"], ctx["v"]))
    x = _post_ffn(x, mix, mods[0], p["w_out_even"], p["norm_ffn"][0], p["w1"], p["w3"], p["w2"], layer=0)
    gate, z, lru_a, lru_b, xbc, dt = _odd_pre(
        x, mods[1], p["norm_mix"][1], p["w_in_odd"], p["w_dt_odd"],
        lru=(p["conv_lru_w"], p["conv_lru_b"], p["lru_wg"], p["lru_bg"], p["lru_lambda"]))
    h0_lru, h0_ssd = (None, None) if is_ctx else (ctx["lru"], ctx["ssd"])
    ylru, lru_state = _lru(lru_a, lru_b, h0_lru)
    yssd, ssd_state = _ssd(xbc, dt, p["conv_ssd_w"], p["conv_ssd_b"], p["ssd_dtb"], p["ssd_alog"], p["ssd_dexp"],
                           h0_ssd)
    x = _post_ffn(x, (ylru, gate, yssd, z), mods[1], p["w_out_odd"], p["norm_ffn"][1], p["w1"], p["w3"], p["w2"],
                  layer=1, ssd_norm=p["ssd_norm"])
    return x, k, v, lru_state, ssd_state


def kernel(x_prompt, x_sample, c, cache_k, cache_v, state_lru, state_ssd, c_ctx, w_ada, b_ada, norm_mix, norm_ffn,
           w_in_even, q_norm, k_norm, w_out_even, w_in_odd, conv_lru_w, conv_lru_b, lru_wa, lru_ba, lru_wx, lru_bx,
           lru_lambda, conv_ssd_w, conv_ssd_b, ssd_dt_bias, ssd_a_log, ssd_d, ssd_norm, w_out_odd, ffn_w1, ffn_w3,
           ffn_w2):
    bsz, seq, d = x_prompt.shape
    dec_b = x_sample.shape[0]
    past = cache_k.shape[2]

    cond_rows = 16
    cond = jnp.concatenate([c, c_ctx[None, :], jnp.zeros((cond_rows - dec_b - 1, d), F32)], axis=0)
    mod = _modulation(cond, w_ada, b_ada).reshape(2, cond_rows, 6, d)
    mods_sample = [mod[l, :dec_b] for l in range(2)]
    mods_prompt = [mod[l, dec_b:dec_b + 1] for l in range(2)]

    lru_wg, lru_bg = _lru_gate_weights(lru_wa[0], lru_ba[0], lru_wx[0], lru_bx[0])
    pad32 = lambda a: jnp.pad(a.reshape(1, 2 * SSD_HEADS), ((0, 0), (0, LANES - 2 * SSD_HEADS)))
    p = dict(
        norm_mix=norm_mix.reshape(2, 1, d), norm_ffn=norm_ffn.reshape(2, 1, d),
        w_in_even=w_in_even[0].astype(BF16),
        q_gain=jnp.tile(q_norm[0], N_Q_HEADS)[None, :], k_gain=jnp.tile(k_norm[0], N_KV_HEADS)[None, :],
        w_out_even=w_out_even[0].astype(BF16),
        w_in_odd=w_in_odd[0, :, :DT_LO].astype(BF16),
        w_dt_odd=jnp.pad(w_in_odd[0, :, DT_LO:].astype(BF16), ((0, 0), (0, LANES - 2 * SSD_HEADS))),
        conv_lru_w=conv_lru_w[0], conv_lru_b=conv_lru_b[0][None, :],
        lru_wg=lru_wg, lru_bg=lru_bg, lru_lambda=lru_lambda[0],
        conv_ssd_w=conv_ssd_w[0], conv_ssd_b=conv_ssd_b[0][None, :],
        ssd_dtb=pad32(ssd_dt_bias[0]), ssd_alog=pad32(ssd_a_log[0]),
        ssd_dexp=jnp.repeat(ssd_d[0], SSD_INNER // SSD_HEADS)[None, :],
        ssd_norm=ssd_norm[0][None, :],
        w_out_odd=w_out_odd[0].astype(BF16),
        w1=ffn_w1.astype(BF16), w3=ffn_w3.astype(BF16), w2=ffn_w2.astype(BF16),
    )

    y_prompt, k_new, v_new, lru_new, ssd_new = _trunk(x_prompt, mods_prompt, p, None)
    ctx = dict(
        k=cache_k[:, 0].reshape(dec_b, past, KV_W), v=cache_v[:, 0].reshape(dec_b, past, KV_W),
        lru=state_lru[:, 0],
        ssd=state_ssd[:, 0].transpose(0, 1, 4, 2, 3).reshape(dec_b, 2, SSD_STATE, SSD_INNER),
    )
    y_sample, _, _, _, _ = _trunk(x_sample, mods_sample, p, ctx)

    new_k = k_new.reshape(bsz, 1, seq, N_KV_HEADS, HEAD_DIM)
    new_v = v_new.reshape(bsz, 1, seq, N_KV_HEADS, HEAD_DIM)
    new_lru = lru_new.reshape(bsz, 1, 2, LRU_W)
    new_ssd = (ssd_new.reshape(bsz, 2, SSD_STATE, SSD_HEADS, SSD_INNER // SSD_HEADS)
               .transpose(0, 1, 3, 4, 2).reshape(bsz, 1, 2, SSD_HEADS, SSD_INNER // SSD_HEADS, SSD_STATE))
    return (y_prompt, y_sample, new_k, new_v, new_lru, new_ssd)
```

```python
import functools
import math

import numpy as np
import jax
import jax.numpy as jnp
from jax import lax
from jax.experimental import pallas as pl
from jax.experimental.pallas import tpu as pltpu

F32 = jnp.float32
BF16 = jnp.bfloat16

D_MODEL = 1024
EPS = 1e-6
GRID_W = 64
HEAD_DIM = 64
N_Q_HEADS = 12
N_KV_HEADS = 4
FOURIER_W = 256
FOURIER_GROUP_W = 64
Q_W = N_Q_HEADS * HEAD_DIM
KV_W = N_KV_HEADS * HEAD_DIM
EVEN_IN = FOURIER_W + Q_W + 2 * KV_W
ROPE_THETA = 10000.0
AX_PAIRS = HEAD_DIM // 4
LRU_W = 512
LRU_C = 8.0
SSD_INNER = 1024
SSD_HEADS = 16
SSD_STATE = 64
SSD_CONV_CH = SSD_INNER + 4 * SSD_STATE
ODD_IN = 2 * LRU_W + SSD_INNER + SSD_CONV_CH + 2 * SSD_HEADS
D_FF = 2816
LANES = 128
SUBLANES = 8
TOKEN_BLOCK = 512
ROW_TILE = 256
SSD_Q = 128
SCAN_UNROLL = 4
V_ROWS = HEAD_DIM + 16
VMEM_LIMIT = 56 * 1024 * 1024


def _params(sem, vmem=VMEM_LIMIT):
    return pltpu.CompilerParams(dimension_semantics=sem, vmem_limit_bytes=vmem)


def _const_spec(shape):
    nd = len(shape)
    return pl.BlockSpec(shape, lambda *_: (0,) * nd, pipeline_mode=pl.Buffered(1))


def _dot(a, b):
    return jnp.dot(a, b, preferred_element_type=F32)


def _dot_nt(a, b):
    return lax.dot_general(a, b, (((1,), (1,)), ((), ())), preferred_element_type=F32)


def _dot3(m, a):
    hi = a.astype(BF16)
    r1 = a - hi.astype(F32)
    mid = r1.astype(BF16)
    lo = (r1 - mid.astype(F32)).astype(BF16)
    return _dot(m, hi) + _dot(m, mid) + _dot(m, lo)


def _silu(x):
    return x * jax.nn.sigmoid(x)


def _sigmoid_tanh(x):
    return 0.5 * jnp.tanh(0.5 * x) + 0.5


def _softplus(x):
    return jnp.maximum(x, 0.0) + jnp.log1p(jnp.exp(-jnp.abs(x)))


def _gelu_tanh(x):
    c = math.sqrt(2.0 / math.pi)
    return 0.5 * x * (1.0 + jnp.tanh(c * (x + 0.044715 * (x * x * x))))


def _modnorm(x, g, shift, scale):
    ms = jnp.mean(x * x, axis=-1, keepdims=True)
    y = x * lax.rsqrt(ms + EPS) * g
    return y * (1.0 + scale) + shift


def _mod_kernel(cond_ref, w_ref, b_ref, o_ref):
    s = _silu(cond_ref[...])
    o_ref[0] = _dot(s.astype(BF16), w_ref[0].astype(BF16)) + b_ref[0]


def _modulation(cond, w_ada, b_ada):
    depth, d, n = w_ada.shape
    rows = cond.shape[0]
    tn = 1536
    return pl.pallas_call(
        _mod_kernel,
        grid=(depth, n // tn),
        in_specs=[pl.BlockSpec((rows, d), lambda l, j: (0, 0)),
                  pl.BlockSpec((1, d, tn), lambda l, j: (l, 0, j)),
                  pl.BlockSpec((1, 1, tn), lambda l, j: (l, 0, j))],
        out_specs=pl.BlockSpec((1, rows, tn), lambda l, j: (l, 0, j)),
        out_shape=jax.ShapeDtypeStruct((depth, rows, n), F32),
        compiler_params=_params(("arbitrary", "arbitrary")),
        name="modulation",
    )(cond, w_ada, b_ada.reshape(depth, 1, n))


def _mod_spec(mod):
    if mod.shape[0] == 1:
        return pl.BlockSpec((1, 6, D_MODEL), lambda b, i: (0, 0, 0))
    return pl.BlockSpec((1, 6, D_MODEL), lambda b, i: (b, 0, 0))


def _head_rmsnorm(x, gain):
    rows, width = x.shape
    low = lax.broadcasted_iota(jnp.int32, (rows, LANES), 1) < HEAD_DIM
    out = []
    for b in range(width // LANES):
        blk = x[:, b * LANES:(b + 1) * LANES]
        sq = blk * blk
        lo = jnp.sum(jnp.where(low, sq, 0.0), axis=-1, keepdims=True)
        hi = jnp.sum(jnp.where(low, 0.0, sq), axis=-1, keepdims=True)
        ms = jnp.where(low, lo, hi) * (1.0 / HEAD_DIM)
        out.append(blk * lax.rsqrt(ms + EPS) * gain[:, b * LANES:(b + 1) * LANES])
    return jnp.concatenate(out, axis=-1)


def _rope(x, cos, sin_signed):
    rows, width = x.shape
    lane = lax.broadcasted_iota(jnp.int32, (rows, LANES), 1)
    second = (lane % (2 * AX_PAIRS)) >= AX_PAIRS
    out = []
    for b in range(width // LANES):
        blk = x[:, b * LANES:(b + 1) * LANES]
        partner = jnp.where(second, pltpu.roll(blk, AX_PAIRS, 1), pltpu.roll(blk, LANES - AX_PAIRS, 1))
        out.append(blk * cos + partner * sin_signed)
    return jnp.concatenate(out, axis=-1)


def _even_pre_kernel(*refs, rope):
    if rope:
        x_ref, mod_ref, g_ref, w_ref, qg_ref, kg_ref, cos_ref, sin_ref, f_ref, q_ref, k_ref, v_ref = refs
    else:
        x_ref, mod_ref, g_ref, w_ref, qg_ref, kg_ref, f_ref, q_ref, k_ref, v_ref = refs
    h = _modnorm(x_ref[0], g_ref[...], mod_ref[0, 0:1, :], mod_ref[0, 1:2, :])
    p = _dot(h.astype(BF16), w_ref[...])
    f_ref[0] = p[:, :FOURIER_W]
    q = _head_rmsnorm(p[:, FOURIER_W:FOURIER_W + Q_W], qg_ref[...])
    k = _head_rmsnorm(p[:, FOURIER_W + Q_W:FOURIER_W + Q_W + KV_W], kg_ref[...])
    if rope:
        q = _rope(q, cos_ref[...], sin_ref[...])
        k = _rope(k, cos_ref[...], sin_ref[...])
    q_ref[0] = (q * (HEAD_DIM ** -0.5 * math.log2(math.e))).astype(BF16)
    k_ref[0] = k
    v_ref[0] = p[:, FOURIER_W + Q_W + KV_W:]


def _rope_tables(seq):
    pos = np.arange(seq)
    freqs = ROPE_THETA ** (-np.arange(AX_PAIRS, dtype=np.float64) / AX_PAIRS)
    ang = np.zeros((seq, HEAD_DIM))
    sign = np.zeros((HEAD_DIM,))
    for a, p_a in enumerate((pos // GRID_W, pos % GRID_W)):
        for t in range(2):
            lo = a * 2 * AX_PAIRS + t * AX_PAIRS
            ang[:, lo:lo + AX_PAIRS] = p_a[:, None] * freqs[None, :]
            sign[lo:lo + AX_PAIRS] = -1.0 if t == 0 else 1.0
    cos = np.tile(np.cos(ang), (1, 2)).astype(np.float32)
    sin = np.tile(np.sin(ang) * sign[None, :], (1, 2)).astype(np.float32)
    return jnp.asarray(cos), jnp.asarray(sin)


def _even_pre(x, mod, g, w_in, qg, kg, rope):
    bsz, seq, d = x.shape
    tm = ROW_TILE
    tok = lambda w: pl.BlockSpec((1, tm, w), lambda b, i: (b, i, 0))
    in_specs = [tok(d), _mod_spec(mod), _const_spec((1, d)), _const_spec((d, EVEN_IN)),
                _const_spec((1, Q_W)), _const_spec((1, KV_W))]
    args = [x, mod, g, w_in, qg, kg]
    if rope:
        cos, sin = _rope_tables(seq)
        in_specs += [pl.BlockSpec((tm, LANES), lambda b, i: (i, 0))] * 2
        args += [cos, sin]
    return pl.pallas_call(
        functools.partial(_even_pre_kernel, rope=rope),
        grid=(bsz, seq // tm),
        in_specs=in_specs,
        out_specs=[tok(FOURIER_W), tok(Q_W), tok(KV_W), tok(KV_W)],
        out_shape=[jax.ShapeDtypeStruct((bsz, seq, FOURIER_W), F32),
                   jax.ShapeDtypeStruct((bsz, seq, Q_W), BF16),
                   jax.ShapeDtypeStruct((bsz, seq, KV_W), F32),
                   jax.ShapeDtypeStruct((bsz, seq, KV_W), F32)],
        compiler_params=_params(("arbitrary", "arbitrary")),
        name="even_pre",
    )(*args)


def _attn_kernel(*refs, seq, has_ctx):
    if has_ctx:
        q_ref, k_ref, v_ref, ck_ref, cv_ref, f_ref, cs_ref, cbd_ref, o_ref, kop, vopt, g_scr, cs_scr = refs
    else:
        q_ref, k_ref, v_ref, f_ref, cs_ref, cbd_ref, o_ref, kop, vopt, g_scr, cs_scr = refs
    tq = q_ref.shape[1]

    @pl.when((pl.program_id(0) == 0) & (pl.program_id(1) == 0))
    def _round_tables():
        for r0 in range(0, seq, tq):
            cs_scr[r0:r0 + tq, :] = cs_ref[r0:r0 + tq, :].astype(BF16)

    @pl.when(pl.program_id(1) == 0)
    def _prepare():
        def put(kx, vx, r0):
            n = kx.shape[0]
            low = lax.broadcasted_iota(jnp.int32, (n, LANES), 1) < HEAD_DIM
            ones = jnp.ones((V_ROWS - HEAD_DIM, n), BF16)
            for jp in range(N_KV_HEADS // 2):
                kp = kx[:, jp * LANES:(jp + 1) * LANES]
                k_lo = jnp.where(low, kp, 0.0)
                k_hi = jnp.where(low, 0.0, kp)
                placed = (k_lo, pltpu.roll(k_lo, HEAD_DIM, 1), pltpu.roll(k_hi, HEAD_DIM, 1), k_hi)
                for t, kk in enumerate(placed):
                    kop[4 * jp + t, r0:r0 + n, :] = kk.astype(BF16)
                vt = vx[:, jp * LANES:(jp + 1) * LANES].T.astype(BF16)
                for t in range(2):
                    vopt[2 * jp + t, 0:HEAD_DIM, r0:r0 + n] = vt[t * HEAD_DIM:(t + 1) * HEAD_DIM]
                    vopt[2 * jp + t, HEAD_DIM:V_ROWS, r0:r0 + n] = ones

        put(k_ref[0], v_ref[0], 0)
        if has_ctx:
            put(ck_ref[0], cv_ref[0], seq)
        f = f_ref[0].astype(BF16)
        g_scr[0:seq, :] = _dot(f, cbd_ref[0].astype(BF16)).astype(BF16)
        g_scr[seq:2 * seq, :] = _dot(f, cbd_ref[1].astype(BF16)).astype(BF16)

    r0 = pl.multiple_of(pl.program_id(1) * tq, tq)
    o_ref[0, :, 0:FOURIER_W] = _dot(cs_scr[pl.ds(r0, tq), :], g_scr[...]).astype(BF16)
    q = q_ref[0]
    n_pairs = N_Q_HEADS // 2
    kv_of = lambda qp: [(2 * qp + half) // (N_Q_HEADS // N_KV_HEADS) for half in range(2)]

    def pair_scores(qp):
        qpair = q[:, qp * LANES:(qp + 1) * LANES]
        return [_dot_nt(qpair, kop[2 * kv_of(qp)[half] + half]) for half in range(2)]

    scores_next = pair_scores(0)
    for qp in range(n_pairs):
        kv = kv_of(qp)
        scores = scores_next
        if qp + 1 < n_pairs:
            scores_next = pair_scores(qp + 1)
        probs = [jnp.exp2(s - jnp.max(s, axis=-1, keepdims=True)).astype(BF16) for s in scores]
        outs = []
        for half in range(2):
            o = _dot_nt(vopt[kv[half]], probs[half])
            outs.append(o[0:HEAD_DIM] / o[HEAD_DIM:HEAD_DIM + 1])
        res = jnp.concatenate(outs, axis=0)
        o_ref[0, :, FOURIER_W + qp * LANES:FOURIER_W + (qp + 1) * LANES] = res.T.astype(BF16)


def _dft_tables(seq):
    s = np.arange(seq)
    ang = 2.0 * np.pi * ((s[:, None] * s[None, :]) % seq) / seq
    pos = np.concatenate([np.cos(ang), -np.sin(ang)], axis=1) / math.sqrt(seq)
    c = np.arange(FOURIER_GROUP_W)
    ang_c = 2.0 * np.pi * ((c[:, None] * c[None, :]) % FOURIER_GROUP_W) / FOURIER_GROUP_W
    eye = np.eye(FOURIER_W // FOURIER_GROUP_W)
    chan = np.stack([np.kron(eye, np.cos(ang_c)), np.kron(eye, np.sin(ang_c))]) / math.sqrt(FOURIER_GROUP_W)
    return jnp.asarray(pos, dtype=F32), jnp.asarray(chan, dtype=F32)


def _even_mixer(q, k, v, f, ctx):
    bsz, seq, _ = q.shape
    tq = ROW_TILE
    has_ctx = ctx is not None
    past = ctx[0].shape[1] if has_ctx else 0
    kt = seq + past
    pos_dft, chan_dft = _dft_tables(seq)
    full = lambda n, w: pl.BlockSpec((1, n, w), lambda b, i: (b, 0, 0))
    in_specs = [pl.BlockSpec((1, tq, Q_W), lambda b, i: (b, i, 0)), full(seq, KV_W), full(seq, KV_W)]
    args = [q, k, v]
    if has_ctx:
        in_specs += [full(past, KV_W), full(past, KV_W)]
        args += list(ctx)
    in_specs += [full(seq, FOURIER_W), _const_spec((seq, 2 * seq)),
                 _const_spec((2, FOURIER_W, FOURIER_W))]
    args += [f, pos_dft, chan_dft]
    return pl.pallas_call(
        functools.partial(_attn_kernel, seq=seq, has_ctx=has_ctx),
        grid=(bsz, seq // tq),
        in_specs=in_specs,
        out_specs=pl.BlockSpec((1, tq, D_MODEL), lambda b, i: (b, i, 0)),
        out_shape=jax.ShapeDtypeStruct((bsz, seq, D_MODEL), BF16),
        scratch_shapes=[pltpu.VMEM((2 * N_KV_HEADS, kt, LANES), BF16),
                        pltpu.VMEM((N_KV_HEADS, V_ROWS, kt), BF16),
                        pltpu.VMEM((2 * seq, FOURIER_W), BF16),
                        pltpu.VMEM((seq, 2 * seq), BF16)],
        compiler_params=_params(("arbitrary", "arbitrary")),
        name="even_mixer",
    )(*args)


def _post_ffn_kernel(*refs, odd):
    if odd:
        (x_ref, ylru_ref, g_ref, yssd_ref, z_ref, sn_ref, mod_ref, wout_ref, gf_ref,
         w1_ref, w3_ref, w2_ref, o_ref) = refs
        y_lru = ylru_ref[0] * _gelu_tanh(g_ref[0])
        y = yssd_ref[0] * _silu(z_ref[0])
        ms = jnp.mean(y * y, axis=-1, keepdims=True)
        y_ssd = y * lax.rsqrt(ms + EPS) * sn_ref[...]
        mix = jnp.concatenate([y_lru, y_ssd], axis=-1).astype(BF16)
    else:
        x_ref, mix_ref, mod_ref, wout_ref, gf_ref, w1_ref, w3_ref, w2_ref, o_ref = refs
        mix = mix_ref[0]
    x = x_ref[0] + mod_ref[0, 2:3, :] * _dot(mix, wout_ref[...])
    h = _modnorm(x, gf_ref[...], mod_ref[0, 3:4, :], mod_ref[0, 4:5, :]).astype(BF16)
    a = (_silu(_dot(h, w1_ref[...])) * _dot(h, w3_ref[...])).astype(BF16)
    o_ref[0] = x + mod_ref[0, 5:6, :] * _dot(a, w2_ref[...])


def _post_ffn(x, mix_args, mod, w_out, g_ffn, w1, w3, w2, layer, ssd_norm=None):
    bsz, seq, d = x.shape
    if mod.shape[0] == 1 and seq < TOKEN_BLOCK:
        fold = TOKEN_BLOCK // seq
        merge = lambda t: t.reshape(bsz // fold, fold * seq, t.shape[-1])
        out = _post_ffn(merge(x), jax.tree.map(merge, mix_args), mod, w_out, g_ffn, w1, w3, w2, layer, ssd_norm)
        return out.reshape(bsz, seq, d)
    tm = min(seq, TOKEN_BLOCK)
    odd = layer % 2 == 1
    tok = lambda w: pl.BlockSpec((1, tm, w), lambda b, i: (b, i, 0))
    slab = lambda w: pl.BlockSpec((None,) + w.shape[1:], lambda b, i: (layer, 0, 0), pipeline_mode=pl.Buffered(1))
    if odd:
        ylru, g, yssd, z = mix_args
        in_specs = [tok(d), tok(LRU_W), tok(LRU_W), tok(SSD_INNER), tok(SSD_INNER), _const_spec((1, SSD_INNER))]
        args = [x, ylru, g, yssd, z, ssd_norm]
    else:
        in_specs = [tok(d), tok(d)]
        args = [x, mix_args]
    in_specs += [_mod_spec(mod), _const_spec(w_out.shape), _const_spec((1, d)),
                 slab(w1), slab(w3), slab(w2)]
    args += [mod, w_out, g_ffn, w1, w3, w2]
    return pl.pallas_call(
        functools.partial(_post_ffn_kernel, odd=odd),
        grid=(bsz, seq // tm),
        in_specs=in_specs,
        out_specs=tok(d),
        out_shape=jax.ShapeDtypeStruct((bsz, seq, d), F32),
        compiler_params=_params(("arbitrary", "arbitrary")),
        name="post_ffn_odd" if odd else "post_ffn_even",
    )(*args)


XL_LO = LRU_W
Z_LO = 2 * LRU_W
XBC_LO = 2 * LRU_W + SSD_INNER
DT_LO = XBC_LO + SSD_CONV_CH


def _conv_block(main, edges, w, b):
    before, after0, after1 = edges
    rows = main.shape[0]
    row = lax.broadcasted_iota(jnp.int32, (SUBLANES, 1), 0)
    last = rows - SUBLANES

    def patch_first(x, fix):
        return jnp.concatenate([fix(x[:SUBLANES]), x[SUBLANES:]], axis=0)

    def patch_last(x, fix):
        return jnp.concatenate([x[:last], fix(x[last:])], axis=0)

    xm1 = patch_first(pltpu.roll(main, 1, 0), lambda t: jnp.where(row == 0, before, t))
    xp1 = patch_last(pltpu.roll(main, rows - 1, 0), lambda t: jnp.where(row == SUBLANES - 1, after0, t))
    xp2 = patch_last(pltpu.roll(main, rows - 2, 0),
                     lambda t: jnp.where(row == SUBLANES - 2, after0, jnp.where(row == SUBLANES - 1, after1, t)))
    return xm1 * w[0:1] + main * w[1:2] + xp1 * w[2:3] + xp2 * w[3:4] + b


def _odd_pre_kernel(x_ref, xprev_ref, xnext_ref, mod_ref, g_ref, w_ref, wdt_ref, cwl_ref, cbl_ref, wg_ref, bg_ref,
                    lam_ref, gate_ref, z_ref, a_ref, b_ref, xbc_ref, dt_ref):
    i = pl.program_id(1)
    g, shift, scale = g_ref[...], mod_ref[0, 0:1, :], mod_ref[0, 1:2, :]
    h = _modnorm(x_ref[0], g, shift, scale).astype(BF16)
    wide = 2 * LANES
    proj = lambda lo: _dot(h, w_ref[:, lo:lo + wide])
    p_xl = [proj(XL_LO + t * wide) for t in range(LRU_W // wide)]
    others = ([(gate_ref, t * wide, t * wide) for t in range(LRU_W // wide)]
              + [(z_ref, Z_LO + t * wide, t * wide) for t in range(SSD_INNER // wide)]
              + [(xbc_ref, XBC_LO + t * wide, t * wide) for t in range(SSD_CONV_CH // wide)])

    halo = jnp.concatenate([xprev_ref[0], xnext_ref[0]], axis=0)
    ph = _dot(_modnorm(halo, g, shift, scale).astype(BF16), w_ref[:, XL_LO:Z_LO])
    has_prev = i > 0
    has_next = i < pl.num_programs(1) - 1
    e_l = (jnp.where(has_prev, ph[SUBLANES - 1:SUBLANES], 0.0),
           jnp.where(has_next, ph[SUBLANES:SUBLANES + 1], 0.0),
           jnp.where(has_next, ph[SUBLANES + 1:SUBLANES + 2], 0.0))

    rate = (0.5 * LRU_C) * _softplus(-lam_ref[...])
    n_lru = LRU_W // LANES
    for lb in range(n_lru):
        sl = slice(lb * LANES, (lb + 1) * LANES)
        for dst, lo, col in others[lb * len(others) // n_lru:(lb + 1) * len(others) // n_lru]:
            dst[0, :, col:col + wide] = proj(lo)
        xb = _conv_block(p_xl[lb // 2][:, (lb % 2) * LANES:(lb % 2 + 1) * LANES], [e[:, sl] for e in e_l],
                         cwl_ref[:, sl], cbl_ref[:, sl])
        xh = 0.5 * xb
        gt = jnp.tanh(_dot(xb.astype(BF16), wg_ref[lb]) + bg_ref[lb])
        for d in range(2):
            neg_log_a = rate[d:d + 1, sl] * (gt[:, 2 * d * LANES:(2 * d + 1) * LANES] + 1.0)
            a = jnp.exp2(neg_log_a * (-math.log2(math.e)))
            g2 = jnp.tanh(neg_log_a) * (a * a + 1.0)
            root = jnp.where(g2 > 0.0, g2 * lax.rsqrt(g2), 0.0)
            a_ref[0, d, :, sl] = a
            b_ref[0, d, :, sl] = root * ((gt[:, (2 * d + 1) * LANES:(2 * d + 2) * LANES] + 1.0) * xh)
    dt_ref[0] = _dot(h, wdt_ref[...])


def _odd_pre(x, mod, g, w_in, w_dt, lru):
    bsz, seq, d = x.shape
    tm = min(seq, TOKEN_BLOCK)
    per8 = tm // SUBLANES
    tok = lambda w: pl.BlockSpec((1, tm, w), lambda b, i: (b, i, 0))
    both = pl.BlockSpec((1, 2, tm, LRU_W), lambda b, i: (b, 0, i, 0))
    consts = list(lru)
    out_specs = [tok(LRU_W), tok(SSD_INNER), both, both, tok(SSD_CONV_CH), tok(LANES)]
    row = lambda w: jax.ShapeDtypeStruct((bsz, seq, w), F32)
    out_shape = ([row(LRU_W), row(SSD_INNER)] + [jax.ShapeDtypeStruct((bsz, 2, seq, LRU_W), F32)] * 2
                 + [row(SSD_CONV_CH), row(LANES)])
    return pl.pallas_call(
        _odd_pre_kernel,
        grid=(bsz, seq // tm),
        in_specs=[tok(d),
                  pl.BlockSpec((1, SUBLANES, d), lambda b, i: (b, jnp.maximum(i * per8 - 1, 0), 0)),
                  pl.BlockSpec((1, SUBLANES, d), lambda b, i: (b, jnp.minimum((i + 1) * per8, seq // SUBLANES - 1), 0)),
                  _mod_spec(mod), _const_spec((1, d)), _const_spec(w_in.shape), _const_spec(w_dt.shape)]
                 + [_const_spec(c.shape) for c in consts],
        out_specs=out_specs,
        out_shape=out_shape,
        compiler_params=_params(("arbitrary", "arbitrary")),
        name="odd_pre",
    )(x, x, x, mod, g, w_in, w_dt, *consts)


def _lru_kernel(*refs, seq, has_h0):
    if has_h0:
        a_ref, b_ref, h0_ref, y_ref = refs
    else:
        a_ref, b_ref, y_ref, hout_ref = refs
    y_ref[0] = jnp.zeros((seq, LRU_W), F32)

    nb = seq // SUBLANES
    row = lax.broadcasted_iota(jnp.int32, (SUBLANES, LRU_W), 0)

    def scan(i, carry):
        hf, hb = carry
        rf = pl.multiple_of(i * SUBLANES, SUBLANES)
        rb = pl.multiple_of((nb - 1 - i) * SUBLANES, SUBLANES)
        a = a_ref[0, 0, pl.ds(rf, SUBLANES), :]
        b = b_ref[0, 0, pl.ds(rf, SUBLANES), :]
        for k in (1, 2, 4):
            keep = row >= k
            b = a * jnp.where(keep, pltpu.roll(b, k, 0), 0.0) + b
            a = a * jnp.where(keep, pltpu.roll(a, k, 0), 1.0)
        h = a * hf + b
        y_ref[0, pl.ds(rf, SUBLANES), :] += h
        hf = h[SUBLANES - 1:SUBLANES, :]
        a = a_ref[0, 1, pl.ds(rb, SUBLANES), :]
        b = b_ref[0, 1, pl.ds(rb, SUBLANES), :]
        for k in (1, 2, 4):
            keep = row < SUBLANES - k
            b = a * jnp.where(keep, pltpu.roll(b, SUBLANES - k, 0), 0.0) + b
            a = a * jnp.where(keep, pltpu.roll(a, SUBLANES - k, 0), 1.0)
        h = a * hb + b
        y_ref[0, pl.ds(rb, SUBLANES), :] += h
        hb = h[0:1, :]
        return hf, hb

    if has_h0:
        lax.fori_loop(0, nb, scan, (h0_ref[0, 0:1, :], h0_ref[0, 1:2, :]), unroll=SCAN_UNROLL)
    else:
        zero = jnp.zeros((1, LRU_W), F32)
        hf, hb = lax.fori_loop(0, nb, scan, (zero, zero), unroll=SCAN_UNROLL)
        hout_ref[0, 0:1, :] = hf
        hout_ref[0, 1:2, :] = hb


def _lru(a, b, h0):
    bsz, _, seq, _ = a.shape
    has_h0 = h0 is not None
    per_seq = lambda n, w: pl.BlockSpec((1, n, w), lambda b: (b, 0, 0))
    coef = pl.BlockSpec((1, 2, seq, LRU_W), lambda b: (b, 0, 0, 0))
    in_specs = [coef, coef]
    args = [a, b]
    out_specs = [per_seq(seq, LRU_W)]
    out_shape = [jax.ShapeDtypeStruct((bsz, seq, LRU_W), F32)]
    if has_h0:
        in_specs.append(per_seq(2, LRU_W))
        args.append(h0)
    else:
        out_specs.append(per_seq(2, LRU_W))
        out_shape.append(jax.ShapeDtypeStruct((bsz, 2, LRU_W), F32))
    outs = pl.pallas_call(
        functools.partial(_lru_kernel, seq=seq, has_h0=has_h0),
        grid=(bsz,),
        in_specs=in_specs,
        out_specs=out_specs,
        out_shape=out_shape,
        compiler_params=_params(("arbitrary",)),
        name="rglru",
    )(*args)
    return (outs[0], None) if has_h0 else tuple(outs)


def _ssd_kernel(*refs, seq, has_h0):
    xbc_ref, dt_ref, cw_ref, cb_ref, dtb_ref, alog_ref, dexp_ref, ef_ref, eb_ref = refs[:9]
    if has_h0:
        h0_ref, y_ref = refs[9:11]
    else:
        y_ref, hout_ref = refs[9:11]
    xs_ref, bm_ref, cm_ref, cs_ref, dts_ref, cst_ref, dtt_ref, bt_ref, st_scr = refs[11:]
    q = SSD_Q
    nc = seq // q
    lane = lax.broadcasted_iota(jnp.int32, (q, LANES), 1)
    qi = lax.broadcasted_iota(jnp.int32, (q, q), 0)
    ki = lax.broadcasted_iota(jnp.int32, (q, q), 1)
    tri_lower = jnp.where(ki <= qi, 1.0, 0.0).astype(BF16)
    tri_upper = jnp.where(ki >= qi, 1.0, 0.0).astype(BF16)
    a_log2 = jnp.where(lane[0:1, :] < 2 * SSD_HEADS, -math.log2(math.e) * jnp.exp(alog_ref[...]), 0.0)

    def prepare(c):
        r0 = pl.multiple_of(c * q, q)
        rows = pl.ds(r0, q)
        prev8 = pl.ds(pl.multiple_of(jnp.maximum(r0 - SUBLANES, 0), SUBLANES), SUBLANES)
        next8 = pl.ds(pl.multiple_of(jnp.minimum(r0 + q, seq - SUBLANES), SUBLANES), SUBLANES)
        has_prev = r0 > 0
        has_next = r0 + q < seq
        for lb in range(SSD_CONV_CH // LANES):
            sl = slice(lb * LANES, (lb + 1) * LANES)
            before = xbc_ref[0, prev8, sl]
            after = xbc_ref[0, next8, sl]
            edges = (jnp.where(has_prev, before[SUBLANES - 1:SUBLANES], 0.0),
                     jnp.where(has_next, after[0:1], 0.0), jnp.where(has_next, after[1:2], 0.0))
            xc = _conv_block(xbc_ref[0, rows, sl], edges, cw_ref[:, sl], cb_ref[:, sl])
            xc = xc * _sigmoid_tanh(xc)
            if lb < SSD_INNER // LANES:
                xs_ref[rows, sl] = xc
                y_ref[0, rows, sl] = dexp_ref[:, sl] * xc
            elif lb == SSD_INNER // LANES:
                bm_ref[rows, :] = xc
                bt_ref[c] = xc.T
            else:
                cm_ref[rows, :] = xc
        dts = _softplus(dt_ref[0, rows, :] + dtb_ref[...])
        da = dts * a_log2
        cs = jnp.where(lane < SSD_HEADS, _dot3(tri_lower, da), _dot3(tri_upper, da))
        cs_ref[rows, :] = cs
        dts_ref[rows, :] = dts
        cst_ref[c] = cs.T
        dtt_ref[c] = dts.T

    st_scr[...] = h0_ref[0] if has_h0 else jnp.zeros(st_scr.shape, F32)

    low = lane < SSD_STATE
    col = lax.broadcasted_iota(jnp.int32, (SSD_STATE, SSD_INNER), 1)
    first_group = col < SSD_INNER // 2

    def chunk(c, fwd):
        d = 0 if fwd else 1
        r0 = pl.multiple_of(c * q, q)
        rows = pl.ds(r0, q)
        cs = cs_ref[rows, :]
        dts = dts_ref[rows, :]
        cst = cst_ref[c]
        dtt = dtt_ref[c]
        xs = xs_ref[rows, :]
        xsb = xs.astype(BF16)
        bb = bm_ref[rows, :].astype(BF16)
        cb_ = cm_ref[rows, :].astype(BF16)
        zero = jnp.zeros_like(cb_)
        cbg = [_dot_nt(jnp.where(low, cb_, zero), bb), _dot_nt(jnp.where(low, zero, cb_), bb)]
        mask = (ki <= qi) if fwd else (ki >= qi)
        expand = ef_ref[...] if fwd else eb_ref[...]
        st = st_scr[d]
        hh = jnp.concatenate([jnp.where(first_group, st, 0.0), jnp.where(first_group, 0.0, st)], axis=0)
        tot = cs[q - 1:q, :] if fwd else cs[0:1, :]
        mine = (lane >= d * SSD_HEADS) & (lane < (d + 1) * SSD_HEADS)
        w = jnp.exp2(jnp.where(mine, tot - cs, 0.0)) * dts
        dec = jnp.broadcast_to(jnp.exp2(tot), (2 * SUBLANES, LANES))
        dec_hi = dec.astype(BF16)
        dec_lo = (dec - dec_hi.astype(F32)).astype(BF16)
        spread = _dot(jnp.concatenate([jnp.exp2(cs).astype(BF16), w.astype(BF16), dec_hi, dec_lo], axis=0), expand)
        y = _dot(cb_, hh.astype(BF16)) * spread[0:q]
        pairs = []
        for pp in range(SSD_HEADS // 2):
            ms = []
            for h in (2 * pp, 2 * pp + 1):
                l = d * SSD_HEADS + h
                decay = jnp.where(mask, jnp.exp2(cs[:, l:l + 1] - cst[l:l + 1, :]), 0.0)
                ms.append((cbg[pp // (SSD_HEADS // 4)] * decay * dtt[l:l + 1, :]).astype(BF16))
            xp = xsb[:, pp * LANES:(pp + 1) * LANES]
            zx = jnp.zeros_like(xp)
            x2 = jnp.concatenate([jnp.where(low, xp, zx), jnp.where(low, zx, xp)], axis=0)
            pairs.append(_dot(jnp.concatenate(ms, axis=1), x2))
        y_ref[0, rows, :] += y + jnp.concatenate(pairs, axis=1)
        xw = (xs * spread[q:2 * q]).astype(BF16)
        bt = bt_ref[c].astype(BF16)
        half_w = SSD_INNER // 2
        st_new = jnp.concatenate([_dot(bt[:SSD_STATE], xw[:, :half_w]), _dot(bt[SSD_STATE:], xw[:, half_w:])], axis=1)
        chunk_decay = spread[2 * q:2 * q + 1] + spread[2 * q + 2 * SUBLANES:2 * q + 2 * SUBLANES + 1]
        st_scr[d] = chunk_decay * st + st_new

    def prepare_all(c, carry):
        prepare(c)
        return carry

    def both_directions(i, carry):
        chunk(i, True)
        chunk(nc - 1 - i, False)
        return carry

    lax.fori_loop(0, nc, prepare_all, 0)
    lax.fori_loop(0, nc, both_directions, 0)
    if not has_h0:
        hout_ref[0] = st_scr[...]


def _ssd(xbc, dt, cw, cb, dtb, alog, dexp, h0):
    bsz, seq, _ = xbc.shape
    nc = seq // SSD_Q
    expand = np.zeros((2, LANES, SSD_INNER), np.float32)
    for d in range(2):
        for h in range(SSD_HEADS):
            expand[d, d * SSD_HEADS + h, h * 64:(h + 1) * 64] = 1.0
    expand = jnp.asarray(expand, dtype=BF16)
    per_seq = lambda n, w: pl.BlockSpec((1, n, w), lambda b: (b, 0, 0))
    state_spec = pl.BlockSpec((1, 2, SSD_STATE, SSD_INNER), lambda b: (b, 0, 0, 0))
    has_h0 = h0 is not None
    in_specs = [per_seq(seq, SSD_CONV_CH), per_seq(seq, LANES), _const_spec(cw.shape), _const_spec(cb.shape),
                _const_spec(dtb.shape), _const_spec(alog.shape), _const_spec(dexp.shape),
                _const_spec((LANES, SSD_INNER)), _const_spec((LANES, SSD_INNER))]
    args = [xbc, dt, cw, cb, dtb, alog, dexp, expand[0], expand[1]]
    out_specs = [per_seq(seq, SSD_INNER)]
    out_shape = [jax.ShapeDtypeStruct((bsz, seq, SSD_INNER), F32)]
    if has_h0:
        in_specs.append(state_spec)
        args.append(h0)
    else:
        out_specs.append(state_spec)
        out_shape.append(jax.ShapeDtypeStruct((bsz, 2, SSD_STATE, SSD_INNER), F32))
    outs = pl.pallas_call(
        functools.partial(_ssd_kernel, seq=seq, has_h0=has_h0),
        grid=(bsz,),
        in_specs=in_specs,
        out_specs=out_specs,
        out_shape=out_shape,
        scratch_shapes=[pltpu.VMEM((seq, SSD_INNER), F32)] + [pltpu.VMEM((seq, LANES), F32)] * 4
                       + [pltpu.VMEM((nc, LANES, SSD_Q), F32)] * 3 + [pltpu.VMEM((2, SSD_STATE, SSD_INNER), F32)],
        compiler_params=_params(("arbitrary",)),
        name="ssd",
    )(*args)
    return (outs[0], None) if has_h0 else tuple(outs)


def _lru_gate_weights(wa, ba, wx, bx):
    eye = 0.5 * jnp.eye(2, dtype=F32)

    def pair_blocks(w):
        w4 = w.reshape(4, 2, 64, 64)
        return jnp.einsum("laij,ab->laibj", w4, eye).reshape(4, LANES, LANES)

    mats = [pair_blocks(w) for w in (wa[0], wx[0], wa[1], wx[1])]
    wg = jnp.concatenate(mats, axis=-1).astype(BF16)
    bias = [0.5 * b.reshape(4, 1, LANES) for b in (ba[0], bx[0], ba[1], bx[1])]
    return wg, jnp.concatenate(bias, axis=-1)


def _trunk(x, mods, p, ctx):
    is_ctx = ctx is None
    bsz, seq, _ = x.shape
    f, q, k, v = _even_pre(x, mods[0], p["norm_mix"][0], p["w_in_even"], p["q_gain"], p["k_gain"], rope=not is_ctx)
    mix = _even_mixer(q, k, v, f, None if is_ctx else (ctx["k"], ctx["v"]))
    x = _post_ffn(x, mix, mods[0], p["w_out_even"], p["norm_ffn"][0], p["w1"], p["w3"], p["w2"], layer=0)
    gate, z, lru_a, lru_b, xbc, dt = _odd_pre(
        x, mods[1], p["norm_mix"][1], p["w_in_odd"], p["w_dt_odd"],
        lru=(p["conv_lru_w"], p["conv_lru_b"], p["lru_wg"], p["lru_bg"], p["lru_lambda"]))
    h0_lru, h0_ssd = (None, None) if is_ctx else (ctx["lru"], ctx["ssd"])
    ylru, lru_state = _lru(lru_a, lru_b, h0_lru)
    yssd, ssd_state = _ssd(xbc, dt, p["conv_ssd_w"], p["conv_ssd_b"], p["ssd_dtb"], p["ssd_alog"], p["ssd_dexp"],
                           h0_ssd)
    x = _post_ffn(x, (ylru, gate, yssd, z), mods[1], p["w_out_odd"], p["norm_ffn"][1], p["w1"], p["w3"], p["w2"],
                  layer=1, ssd_norm=p["ssd_norm"])
    return x, k, v, lru_state, ssd_state


def kernel(x_prompt, x_sample, c, cache_k, cache_v, state_lru, state_ssd, c_ctx, w_ada, b_ada, norm_mix, norm_ffn,
           w_in_even, q_norm, k_norm, w_out_even, w_in_odd, conv_lru_w, conv_lru_b, lru_wa, lru_ba, lru_wx, lru_bx,
           lru_lambda, conv_ssd_w, conv_ssd_b, ssd_dt_bias, ssd_a_log, ssd_d, ssd_norm, w_out_odd, ffn_w1, ffn_w3,
           ffn_w2):
    bsz, seq, d = x_prompt.shape
    dec_b = x_sample.shape[0]
    past = cache_k.shape[2]

    cond_rows = 16
    cond = jnp.concatenate([c, c_ctx[None, :], jnp.zeros((cond_rows - dec_b - 1, d), F32)], axis=0)
    mod = _modulation(cond, w_ada, b_ada).reshape(2, cond_rows, 6, d)
    mods_sample = [mod[l, :dec_b] for l in range(2)]
    mods_prompt = [mod[l, dec_b:dec_b + 1] for l in range(2)]

    lru_wg, lru_bg = _lru_gate_weights(lru_wa[0], lru_ba[0], lru_wx[0], lru_bx[0])
    pad32 = lambda a: jnp.pad(a.reshape(1, 2 * SSD_HEADS), ((0, 0), (0, LANES - 2 * SSD_HEADS)))
    p = dict(
        norm_mix=norm_mix.reshape(2, 1, d), norm_ffn=norm_ffn.reshape(2, 1, d),
        w_in_even=w_in_even[0].astype(BF16),
        q_gain=jnp.tile(q_norm[0], N_Q_HEADS)[None, :], k_gain=jnp.tile(k_norm[0], N_KV_HEADS)[None, :],
        w_out_even=w_out_even[0].astype(BF16),
        w_in_odd=w_in_odd[0].astype(BF16),
        w_dt_odd=jnp.pad(w_in_odd[0, :, DT_LO:], ((0, 0), (0, LANES - 2 * SSD_HEADS))).astype(BF16),
        conv_lru_w=conv_lru_w[0], conv_lru_b=conv_lru_b[0][None, :],
        lru_wg=lru_wg, lru_bg=lru_bg, lru_lambda=lru_lambda[0],
        conv_ssd_w=conv_ssd_w[0], conv_ssd_b=conv_ssd_b[0][None, :],
        ssd_dtb=pad32(ssd_dt_bias[0]), ssd_alog=pad32(ssd_a_log[0]),
        ssd_dexp=jnp.repeat(ssd_d[0], SSD_INNER // SSD_HEADS)[None, :],
        ssd_norm=ssd_norm[0][None, :],
        w_out_odd=w_out_odd[0].astype(BF16),
        w1=ffn_w1.astype(BF16), w3=ffn_w3.astype(BF16), w2=ffn_w2.astype(BF16),
    )

    y_prompt, k_new, v_new, lru_new, ssd_new = _trunk(x_prompt, mods_prompt, p, None)
    ctx = dict(
        k=cache_k[:, 0].reshape(dec_b, past, KV_W), v=cache_v[:, 0].reshape(dec_b, past, KV_W),
        lru=state_lru[:, 0],
        ssd=state_ssd[:, 0].transpose(0, 1, 4, 2, 3).reshape(dec_b, 2, SSD_STATE, SSD_INNER),
    )
    y_sample, _, _, _, _ = _trunk(x_sample, mods_sample, p, ctx)

    new_k = k_new.reshape(bsz, 1, seq, N_KV_HEADS, HEAD_DIM)
    new_v = v_new.reshape(bsz, 1, seq, N_KV_HEADS, HEAD_DIM)
    new_lru = lru_new.reshape(bsz, 1, 2, LRU_W)
    new_ssd = (ssd_new.reshape(bsz, 2, SSD_STATE, SSD_HEADS, SSD_INNER // SSD_HEADS)
               .transpose(0, 1, 3, 4, 2).reshape(bsz, 1, 2, SSD_HEADS, SSD_INNER // SSD_HEADS, SSD_STATE))
    return (y_prompt, y_sample, new_k, new_v, new_lru, new_ssd)
```

```python
import functools
import math

import numpy as np
import jax
import jax.numpy as jnp
from jax import lax
from jax.experimental import pallas as pl
from jax.experimental.pallas import tpu as pltpu

F32 = jnp.float32
BF16 = jnp.bfloat16

D_MODEL = 1024
EPS = 1e-6
GRID_W = 64
HEAD_DIM = 64
N_Q_HEADS = 12
N_KV_HEADS = 4
FOURIER_W = 256
FOURIER_GROUP_W = 64
Q_W = N_Q_HEADS * HEAD_DIM
KV_W = N_KV_HEADS * HEAD_DIM
EVEN_IN = FOURIER_W + Q_W + 2 * KV_W
ROPE_THETA = 10000.0
AX_PAIRS = HEAD_DIM // 4
LRU_W = 512
LRU_C = 8.0
SSD_INNER = 1024
SSD_HEADS = 16
SSD_STATE = 64
SSD_CONV_CH = SSD_INNER + 4 * SSD_STATE
ODD_IN = 2 * LRU_W + SSD_INNER + SSD_CONV_CH + 2 * SSD_HEADS
D_FF = 2816
LANES = 128
SUBLANES = 8
TOKEN_BLOCK = 512
ROW_TILE = 256
SSD_Q = 128
V_ROWS = HEAD_DIM + 16
VMEM_LIMIT = 56 * 1024 * 1024


def _params(sem, vmem=VMEM_LIMIT):
    return pltpu.CompilerParams(dimension_semantics=sem, vmem_limit_bytes=vmem)


def _const_spec(shape):
    nd = len(shape)
    return pl.BlockSpec(shape, lambda *_: (0,) * nd, pipeline_mode=pl.Buffered(1))


def _dot(a, b):
    return jnp.dot(a, b, preferred_element_type=F32)


def _dot_nt(a, b):
    return lax.dot_general(a, b, (((1,), (1,)), ((), ())), preferred_element_type=F32)


def _dot3(m, a):
    hi = a.astype(BF16)
    r1 = a - hi.astype(F32)
    mid = r1.astype(BF16)
    lo = (r1 - mid.astype(F32)).astype(BF16)
    return _dot(m, hi) + _dot(m, mid) + _dot(m, lo)


def _silu(x):
    return x * jax.nn.sigmoid(x)


def _sigmoid_tanh(x):
    return 0.5 * jnp.tanh(0.5 * x) + 0.5


def _softplus(x):
    return jnp.maximum(x, 0.0) + jnp.log1p(jnp.exp(-jnp.abs(x)))


def _gelu_tanh(x):
    c = math.sqrt(2.0 / math.pi)
    return 0.5 * x * (1.0 + jnp.tanh(c * (x + 0.044715 * (x * x * x))))


def _modnorm(x, g, shift, scale):
    ms = jnp.mean(x * x, axis=-1, keepdims=True)
    y = x * lax.rsqrt(ms + EPS) * g
    return y * (1.0 + scale) + shift


def _mod_kernel(cond_ref, w_ref, b_ref, o_ref):
    s = _silu(cond_ref[...])
    o_ref[0] = _dot(s.astype(BF16), w_ref[0].astype(BF16)) + b_ref[0]


def _modulation(cond, w_ada, b_ada):
    depth, d, n = w_ada.shape
    rows = cond.shape[0]
    tn = 1536
    return pl.pallas_call(
        _mod_kernel,
        grid=(depth, n // tn),
        in_specs=[pl.BlockSpec((rows, d), lambda l, j: (0, 0)),
                  pl.BlockSpec((1, d, tn), lambda l, j: (l, 0, j)),
                  pl.BlockSpec((1, 1, tn), lambda l, j: (l, 0, j))],
        out_specs=pl.BlockSpec((1, rows, tn), lambda l, j: (l, 0, j)),
        out_shape=jax.ShapeDtypeStruct((depth, rows, n), F32),
        compiler_params=_params(("arbitrary", "arbitrary")),
        name="modulation",
    )(cond, w_ada, b_ada.reshape(depth, 1, n))


def _mod_spec(mod):
    if mod.shape[0] == 1:
        return pl.BlockSpec((1, 6, D_MODEL), lambda b, i: (0, 0, 0))
    return pl.BlockSpec((1, 6, D_MODEL), lambda b, i: (b, 0, 0))


def _head_rmsnorm(x, gain):
    rows, width = x.shape
    low = lax.broadcasted_iota(jnp.int32, (rows, LANES), 1) < HEAD_DIM
    out = []
    for b in range(width // LANES):
        blk = x[:, b * LANES:(b + 1) * LANES]
        sq = blk * blk
        lo = jnp.sum(jnp.where(low, sq, 0.0), axis=-1, keepdims=True)
        hi = jnp.sum(jnp.where(low, 0.0, sq), axis=-1, keepdims=True)
        ms = jnp.where(low, lo, hi) * (1.0 / HEAD_DIM)
        out.append(blk * lax.rsqrt(ms + EPS) * gain[:, b * LANES:(b + 1) * LANES])
    return jnp.concatenate(out, axis=-1)


def _rope(x, cos, sin_signed):
    rows, width = x.shape
    lane = lax.broadcasted_iota(jnp.int32, (rows, LANES), 1)
    second = (lane % (2 * AX_PAIRS)) >= AX_PAIRS
    out = []
    for b in range(width // LANES):
        blk = x[:, b * LANES:(b + 1) * LANES]
        partner = jnp.where(second, pltpu.roll(blk, AX_PAIRS, 1), pltpu.roll(blk, LANES - AX_PAIRS, 1))
        out.append(blk * cos + partner * sin_signed)
    return jnp.concatenate(out, axis=-1)


def _even_pre_kernel(*refs, rope):
    if rope:
        x_ref, mod_ref, g_ref, w_ref, qg_ref, kg_ref, cos_ref, sin_ref, f_ref, q_ref, k_ref, v_ref = refs
    else:
        x_ref, mod_ref, g_ref, w_ref, qg_ref, kg_ref, f_ref, q_ref, k_ref, v_ref = refs
    h = _modnorm(x_ref[0], g_ref[...], mod_ref[0, 0:1, :], mod_ref[0, 1:2, :])
    p = _dot(h.astype(BF16), w_ref[...])
    f_ref[0] = p[:, :FOURIER_W]
    q = _head_rmsnorm(p[:, FOURIER_W:FOURIER_W + Q_W], qg_ref[...])
    k = _head_rmsnorm(p[:, FOURIER_W + Q_W:FOURIER_W + Q_W + KV_W], kg_ref[...])
    if rope:
        q = _rope(q, cos_ref[...], sin_ref[...])
        k = _rope(k, cos_ref[...], sin_ref[...])
    q_ref[0] = (q * (HEAD_DIM ** -0.5 * math.log2(math.e))).astype(BF16)
    k_ref[0] = k
    v_ref[0] = p[:, FOURIER_W + Q_W + KV_W:]


def _rope_tables(seq):
    pos = np.arange(seq)
    freqs = ROPE_THETA ** (-np.arange(AX_PAIRS, dtype=np.float64) / AX_PAIRS)
    ang = np.zeros((seq, HEAD_DIM))
    sign = np.zeros((HEAD_DIM,))
    for a, p_a in enumerate((pos // GRID_W, pos % GRID_W)):
        for t in range(2):
            lo = a * 2 * AX_PAIRS + t * AX_PAIRS
            ang[:, lo:lo + AX_PAIRS] = p_a[:, None] * freqs[None, :]
            sign[lo:lo + AX_PAIRS] = -1.0 if t == 0 else 1.0
    cos = np.tile(np.cos(ang), (1, 2)).astype(np.float32)
    sin = np.tile(np.sin(ang) * sign[None, :], (1, 2)).astype(np.float32)
    return jnp.asarray(cos), jnp.asarray(sin)


def _even_pre(x, mod, g, w_in, qg, kg, rope):
    bsz, seq, d = x.shape
    tm = ROW_TILE
    tok = lambda w: pl.BlockSpec((1, tm, w), lambda b, i: (b, i, 0))
    in_specs = [tok(d), _mod_spec(mod), _const_spec((1, d)), _const_spec((d, EVEN_IN)),
                _const_spec((1, Q_W)), _const_spec((1, KV_W))]
    args = [x, mod, g, w_in, qg, kg]
    if rope:
        cos, sin = _rope_tables(seq)
        in_specs += [pl.BlockSpec((tm, LANES), lambda b, i: (i, 0))] * 2
        args += [cos, sin]
    return pl.pallas_call(
        functools.partial(_even_pre_kernel, rope=rope),
        grid=(bsz, seq // tm),
        in_specs=in_specs,
        out_specs=[tok(FOURIER_W), tok(Q_W), tok(KV_W), tok(KV_W)],
        out_shape=[jax.ShapeDtypeStruct((bsz, seq, FOURIER_W), F32),
                   jax.ShapeDtypeStruct((bsz, seq, Q_W), BF16),
                   jax.ShapeDtypeStruct((bsz, seq, KV_W), F32),
                   jax.ShapeDtypeStruct((bsz, seq, KV_W), F32)],
        compiler_params=_params(("arbitrary", "arbitrary")),
        name="even_pre",
    )(*args)


def _attn_kernel(*refs, seq, has_ctx):
    if has_ctx:
        q_ref, k_ref, v_ref, ck_ref, cv_ref, f_ref, cs_ref, cbd_ref, o_ref, kop, vopt, g_scr, cs_scr = refs
    else:
        q_ref, k_ref, v_ref, f_ref, cs_ref, cbd_ref, o_ref, kop, vopt, g_scr, cs_scr = refs
    tq = q_ref.shape[1]

    @pl.when((pl.program_id(0) == 0) & (pl.program_id(1) == 0))
    def _round_tables():
        for r0 in range(0, seq, tq):
            cs_scr[r0:r0 + tq, :] = cs_ref[r0:r0 + tq, :].astype(BF16)

    @pl.when(pl.program_id(1) == 0)
    def _prepare():
        def put(kx, vx, r0):
            n = kx.shape[0]
            low = lax.broadcasted_iota(jnp.int32, (n, LANES), 1) < HEAD_DIM
            ones = jnp.ones((V_ROWS - HEAD_DIM, n), BF16)
            for jp in range(N_KV_HEADS // 2):
                kp = kx[:, jp * LANES:(jp + 1) * LANES]
                k_lo = jnp.where(low, kp, 0.0)
                k_hi = jnp.where(low, 0.0, kp)
                placed = (k_lo, pltpu.roll(k_lo, HEAD_DIM, 1), pltpu.roll(k_hi, HEAD_DIM, 1), k_hi)
                for t, kk in enumerate(placed):
                    kop[4 * jp + t, r0:r0 + n, :] = kk.astype(BF16)
                vt = vx[:, jp * LANES:(jp + 1) * LANES].T.astype(BF16)
                for t in range(2):
                    vopt[2 * jp + t, 0:HEAD_DIM, r0:r0 + n] = vt[t * HEAD_DIM:(t + 1) * HEAD_DIM]
                    vopt[2 * jp + t, HEAD_DIM:V_ROWS, r0:r0 + n] = ones

        put(k_ref[0], v_ref[0], 0)
        if has_ctx:
            put(ck_ref[0], cv_ref[0], seq)
        f = f_ref[0].astype(BF16)
        g_scr[0:seq, :] = _dot(f, cbd_ref[0].astype(BF16)).astype(BF16)
        g_scr[seq:2 * seq, :] = _dot(f, cbd_ref[1].astype(BF16)).astype(BF16)

    r0 = pl.multiple_of(pl.program_id(1) * tq, tq)
    o_ref[0, :, 0:FOURIER_W] = _dot(cs_scr[pl.ds(r0, tq), :], g_scr[...]).astype(BF16)
    q = q_ref[0]
    n_pairs = N_Q_HEADS // 2
    kv_of = lambda qp: [(2 * qp + half) // (N_Q_HEADS // N_KV_HEADS) for half in range(2)]

    def pair_scores(qp):
        qpair = q[:, qp * LANES:(qp + 1) * LANES]
        return [_dot_nt(qpair, kop[2 * kv_of(qp)[half] + half]) for half in range(2)]

    scores_next = pair_scores(0)
    for qp in range(n_pairs):
        kv = kv_of(qp)
        scores = scores_next
        if qp + 1 < n_pairs:
            scores_next = pair_scores(qp + 1)
        probs = [jnp.exp2(s - jnp.max(s, axis=-1, keepdims=True)).astype(BF16) for s in scores]
        outs = []
        for half in range(2):
            o = _dot_nt(vopt[kv[half]], probs[half])
            outs.append(o[0:HEAD_DIM] / o[HEAD_DIM:HEAD_DIM + 1])
        res = jnp.concatenate(outs, axis=0)
        o_ref[0, :, FOURIER_W + qp * LANES:FOURIER_W + (qp + 1) * LANES] = res.T.astype(BF16)


def _dft_tables(seq):
    s = np.arange(seq)
    ang = 2.0 * np.pi * ((s[:, None] * s[None, :]) % seq) / seq
    pos = np.concatenate([np.cos(ang), -np.sin(ang)], axis=1) / math.sqrt(seq)
    c = np.arange(FOURIER_GROUP_W)
    ang_c = 2.0 * np.pi * ((c[:, None] * c[None, :]) % FOURIER_GROUP_W) / FOURIER_GROUP_W
    eye = np.eye(FOURIER_W // FOURIER_GROUP_W)
    chan = np.stack([np.kron(eye, np.cos(ang_c)), np.kron(eye, np.sin(ang_c))]) / math.sqrt(FOURIER_GROUP_W)
    return jnp.asarray(pos, dtype=F32), jnp.asarray(chan, dtype=F32)


def _even_mixer(q, k, v, f, ctx):
    bsz, seq, _ = q.shape
    tq = ROW_TILE
    has_ctx = ctx is not None
    past = ctx[0].shape[1] if has_ctx else 0
    kt = seq + past
    pos_dft, chan_dft = _dft_tables(seq)
    full = lambda n, w: pl.BlockSpec((1, n, w), lambda b, i: (b, 0, 0))
    in_specs = [pl.BlockSpec((1, tq, Q_W), lambda b, i: (b, i, 0)), full(seq, KV_W), full(seq, KV_W)]
    args = [q, k, v]
    if has_ctx:
        in_specs += [full(past, KV_W), full(past, KV_W)]
        args += list(ctx)
    in_specs += [full(seq, FOURIER_W), _const_spec((seq, 2 * seq)),
                 _const_spec((2, FOURIER_W, FOURIER_W))]
    args += [f, pos_dft, chan_dft]
    return pl.pallas_call(
        functools.partial(_attn_kernel, seq=seq, has_ctx=has_ctx),
        grid=(bsz, seq // tq),
        in_specs=in_specs,
        out_specs=pl.BlockSpec((1, tq, D_MODEL), lambda b, i: (b, i, 0)),
        out_shape=jax.ShapeDtypeStruct((bsz, seq, D_MODEL), BF16),
        scratch_shapes=[pltpu.VMEM((2 * N_KV_HEADS, kt, LANES), BF16),
                        pltpu.VMEM((N_KV_HEADS, V_ROWS, kt), BF16),
                        pltpu.VMEM((2 * seq, FOURIER_W), BF16),
                        pltpu.VMEM((seq, 2 * seq), BF16)],
        compiler_params=_params(("arbitrary", "arbitrary")),
        name="even_mixer",
    )(*args)


def _post_ffn_kernel(*refs, odd):
    if odd:
        (x_ref, ylru_ref, g_ref, yssd_ref, z_ref, sn_ref, mod_ref, wout_ref, gf_ref,
         w1_ref, w3_ref, w2_ref, o_ref) = refs
        y_lru = ylru_ref[0] * _gelu_tanh(g_ref[0])
        y = yssd_ref[0] * _silu(z_ref[0])
        ms = jnp.mean(y * y, axis=-1, keepdims=True)
        y_ssd = y * lax.rsqrt(ms + EPS) * sn_ref[...]
        mix = jnp.concatenate([y_lru, y_ssd], axis=-1).astype(BF16)
    else:
        x_ref, mix_ref, mod_ref, wout_ref, gf_ref, w1_ref, w3_ref, w2_ref, o_ref = refs
        mix = mix_ref[0]
    x = x_ref[0] + mod_ref[0, 2:3, :] * _dot(mix, wout_ref[...])
    h = _modnorm(x, gf_ref[...], mod_ref[0, 3:4, :], mod_ref[0, 4:5, :]).astype(BF16)
    a = (_silu(_dot(h, w1_ref[...])) * _dot(h, w3_ref[...])).astype(BF16)
    o_ref[0] = x + mod_ref[0, 5:6, :] * _dot(a, w2_ref[...])


def _post_ffn(x, mix_args, mod, w_out, g_ffn, w1, w3, w2, layer, ssd_norm=None):
    bsz, seq, d = x.shape
    if mod.shape[0] == 1 and seq < TOKEN_BLOCK:
        fold = TOKEN_BLOCK // seq
        merge = lambda t: t.reshape(bsz // fold, fold * seq, t.shape[-1])
        out = _post_ffn(merge(x), jax.tree.map(merge, mix_args), mod, w_out, g_ffn, w1, w3, w2, layer, ssd_norm)
        return out.reshape(bsz, seq, d)
    tm = min(seq, TOKEN_BLOCK)
    odd = layer % 2 == 1
    tok = lambda w: pl.BlockSpec((1, tm, w), lambda b, i: (b, i, 0))
    slab = lambda w: pl.BlockSpec((None,) + w.shape[1:], lambda b, i: (layer, 0, 0), pipeline_mode=pl.Buffered(1))
    if odd:
        ylru, g, yssd, z = mix_args
        in_specs = [tok(d), tok(LRU_W), tok(LRU_W), tok(SSD_INNER), tok(SSD_INNER), _const_spec((1, SSD_INNER))]
        args = [x, ylru, g, yssd, z, ssd_norm]
    else:
        in_specs = [tok(d), tok(d)]
        args = [x, mix_args]
    in_specs += [_mod_spec(mod), _const_spec(w_out.shape), _const_spec((1, d)),
                 slab(w1), slab(w3), slab(w2)]
    args += [mod, w_out, g_ffn, w1, w3, w2]
    return pl.pallas_call(
        functools.partial(_post_ffn_kernel, odd=odd),
        grid=(bsz, seq // tm),
        in_specs=in_specs,
        out_specs=tok(d),
        out_shape=jax.ShapeDtypeStruct((bsz, seq, d), F32),
        compiler_params=_params(("arbitrary", "arbitrary")),
        name="post_ffn_odd" if odd else "post_ffn_even",
    )(*args)


XL_LO = LRU_W
Z_LO = 2 * LRU_W
XBC_LO = 2 * LRU_W + SSD_INNER
DT_LO = XBC_LO + SSD_CONV_CH


def _conv_block(main, edges, w, b):
    before, after0, after1 = edges
    rows = main.shape[0]
    row = lax.broadcasted_iota(jnp.int32, (SUBLANES, 1), 0)
    last = rows - SUBLANES

    def patch_first(x, fix):
        return jnp.concatenate([fix(x[:SUBLANES]), x[SUBLANES:]], axis=0)

    def patch_last(x, fix):
        return jnp.concatenate([x[:last], fix(x[last:])], axis=0)

    xm1 = patch_first(pltpu.roll(main, 1, 0), lambda t: jnp.where(row == 0, before, t))
    xp1 = patch_last(pltpu.roll(main, rows - 1, 0), lambda t: jnp.where(row == SUBLANES - 1, after0, t))
    xp2 = patch_last(pltpu.roll(main, rows - 2, 0),
                     lambda t: jnp.where(row == SUBLANES - 2, after0, jnp.where(row == SUBLANES - 1, after1, t)))
    return xm1 * w[0:1] + main * w[1:2] + xp1 * w[2:3] + xp2 * w[3:4] + b


def _odd_pre_kernel(x_ref, xprev_ref, xnext_ref, mod_ref, g_ref, w_ref, wdt_ref, cwl_ref, cbl_ref, wg_ref, bg_ref,
                    lam_ref, gate_ref, z_ref, a_ref, b_ref, xbc_ref, dt_ref):
    i = pl.program_id(1)
    g, shift, scale = g_ref[...], mod_ref[0, 0:1, :], mod_ref[0, 1:2, :]
    h = _modnorm(x_ref[0], g, shift, scale).astype(BF16)
    wide = 2 * LANES
    proj = lambda lo: _dot(h, w_ref[:, lo:lo + wide])
    p_xl = [proj(XL_LO + t * wide) for t in range(LRU_W // wide)]
    others = ([(gate_ref, t * wide, t * wide) for t in range(LRU_W // wide)]
              + [(z_ref, Z_LO + t * wide, t * wide) for t in range(SSD_INNER // wide)]
              + [(xbc_ref, XBC_LO + t * wide, t * wide) for t in range(SSD_CONV_CH // wide)])

    halo = jnp.concatenate([xprev_ref[0], xnext_ref[0]], axis=0)
    ph = _dot(_modnorm(halo, g, shift, scale).astype(BF16), w_ref[:, XL_LO:Z_LO])
    has_prev = i > 0
    has_next = i < pl.num_programs(1) - 1
    e_l = (jnp.where(has_prev, ph[SUBLANES - 1:SUBLANES], 0.0),
           jnp.where(has_next, ph[SUBLANES:SUBLANES + 1], 0.0),
           jnp.where(has_next, ph[SUBLANES + 1:SUBLANES + 2], 0.0))

    rate = (0.5 * LRU_C) * _softplus(-lam_ref[...])
    n_lru = LRU_W // LANES
    for lb in range(n_lru):
        sl = slice(lb * LANES, (lb + 1) * LANES)
        for dst, lo, col in others[lb * len(others) // n_lru:(lb + 1) * len(others) // n_lru]:
            dst[0, :, col:col + wide] = proj(lo)
        xb = _conv_block(p_xl[lb // 2][:, (lb % 2) * LANES:(lb % 2 + 1) * LANES], [e[:, sl] for e in e_l],
                         cwl_ref[:, sl], cbl_ref[:, sl])
        xh = 0.5 * xb
        gt = jnp.tanh(_dot(xb.astype(BF16), wg_ref[lb]) + bg_ref[lb])
        for d in range(2):
            neg_log_a = rate[d:d + 1, sl] * (gt[:, 2 * d * LANES:(2 * d + 1) * LANES] + 1.0)
            a = jnp.exp2(neg_log_a * (-math.log2(math.e)))
            g2 = jnp.tanh(neg_log_a) * (a * a + 1.0)
            root = jnp.where(g2 > 0.0, g2 * lax.rsqrt(g2), 0.0)
            a_ref[0, d, :, sl] = a
            b_ref[0, d, :, sl] = root * ((gt[:, (2 * d + 1) * LANES:(2 * d + 2) * LANES] + 1.0) * xh)
    dt_ref[0] = _dot(h, wdt_ref[...])


def _odd_pre(x, mod, g, w_in, w_dt, lru):
    bsz, seq, d = x.shape
    tm = min(seq, TOKEN_BLOCK)
    per8 = tm // SUBLANES
    tok = lambda w: pl.BlockSpec((1, tm, w), lambda b, i: (b, i, 0))
    both = pl.BlockSpec((1, 2, tm, LRU_W), lambda b, i: (b, 0, i, 0))
    consts = list(lru)
    out_specs = [tok(LRU_W), tok(SSD_INNER), both, both, tok(SSD_CONV_CH), tok(LANES)]
    row = lambda w: jax.ShapeDtypeStruct((bsz, seq, w), F32)
    out_shape = ([row(LRU_W), row(SSD_INNER)] + [jax.ShapeDtypeStruct((bsz, 2, seq, LRU_W), F32)] * 2
                 + [row(SSD_CONV_CH), row(LANES)])
    return pl.pallas_call(
        _odd_pre_kernel,
        grid=(bsz, seq // tm),
        in_specs=[tok(d),
                  pl.BlockSpec((1, SUBLANES, d), lambda b, i: (b, jnp.maximum(i * per8 - 1, 0), 0)),
                  pl.BlockSpec((1, SUBLANES, d), lambda b, i: (b, jnp.minimum((i + 1) * per8, seq // SUBLANES - 1), 0)),
                  _mod_spec(mod), _const_spec((1, d)), _const_spec(w_in.shape), _const_spec(w_dt.shape)]
                 + [_const_spec(c.shape) for c in consts],
        out_specs=out_specs,
        out_shape=out_shape,
        compiler_params=_params(("arbitrary", "arbitrary")),
        name="odd_pre",
    )(x, x, x, mod, g, w_in, w_dt, *consts)


def _ssd_kernel(*refs, seq, has_h0):
    xbc_ref, dt_ref, cw_ref, cb_ref, dtb_ref, alog_ref, dexp_ref, ef_ref, eb_ref, la_ref, lb_ref = refs[:11]
    if has_h0:
        h0_ref, lh0_ref, y_ref, ylru_ref = refs[11:15]
    else:
        y_ref, ylru_ref, hout_ref, lhout_ref = refs[11:15]
    xs_ref, bm_ref, cm_ref, cs_ref, dts_ref, cst_ref, dtt_ref, bt_ref, st_scr = refs[15:]
    q = SSD_Q
    nc = seq // q
    lane = lax.broadcasted_iota(jnp.int32, (q, LANES), 1)
    qi = lax.broadcasted_iota(jnp.int32, (q, q), 0)
    ki = lax.broadcasted_iota(jnp.int32, (q, q), 1)
    tri_lower = jnp.where(ki <= qi, 1.0, 0.0).astype(BF16)
    tri_upper = jnp.where(ki >= qi, 1.0, 0.0).astype(BF16)
    a_log2 = jnp.where(lane[0:1, :] < 2 * SSD_HEADS, -math.log2(math.e) * jnp.exp(alog_ref[...]), 0.0)

    def prepare(c):
        r0 = pl.multiple_of(c * q, q)
        rows = pl.ds(r0, q)
        prev8 = pl.ds(pl.multiple_of(jnp.maximum(r0 - SUBLANES, 0), SUBLANES), SUBLANES)
        next8 = pl.ds(pl.multiple_of(jnp.minimum(r0 + q, seq - SUBLANES), SUBLANES), SUBLANES)
        has_prev = r0 > 0
        has_next = r0 + q < seq
        for lb in range(SSD_CONV_CH // LANES):
            sl = slice(lb * LANES, (lb + 1) * LANES)
            before = xbc_ref[0, prev8, sl]
            after = xbc_ref[0, next8, sl]
            edges = (jnp.where(has_prev, before[SUBLANES - 1:SUBLANES], 0.0),
                     jnp.where(has_next, after[0:1], 0.0), jnp.where(has_next, after[1:2], 0.0))
            xc = _conv_block(xbc_ref[0, rows, sl], edges, cw_ref[:, sl], cb_ref[:, sl])
            xc = xc * _sigmoid_tanh(xc)
            if lb < SSD_INNER // LANES:
                xs_ref[rows, sl] = xc
                y_ref[0, rows, sl] = dexp_ref[:, sl] * xc
            elif lb == SSD_INNER // LANES:
                bm_ref[rows, :] = xc
                bt_ref[c] = xc.T
            else:
                cm_ref[rows, :] = xc
        dts = _softplus(dt_ref[0, rows, :] + dtb_ref[...])
        da = dts * a_log2
        cs = jnp.where(lane < SSD_HEADS, _dot3(tri_lower, da), _dot3(tri_upper, da))
        cs_ref[rows, :] = cs
        dts_ref[rows, :] = dts
        cst_ref[c] = cs.T
        dtt_ref[c] = dts.T

    st_scr[...] = h0_ref[0] if has_h0 else jnp.zeros(st_scr.shape, F32)

    low = lane < SSD_STATE
    col = lax.broadcasted_iota(jnp.int32, (SSD_STATE, SSD_INNER), 1)
    first_group = col < SSD_INNER // 2

    def chunk(c, fwd):
        d = 0 if fwd else 1
        r0 = pl.multiple_of(c * q, q)
        rows = pl.ds(r0, q)
        cs = cs_ref[rows, :]
        dts = dts_ref[rows, :]
        cst = cst_ref[c]
        dtt = dtt_ref[c]
        xs = xs_ref[rows, :]
        xsb = xs.astype(BF16)
        bb = bm_ref[rows, :].astype(BF16)
        cb_ = cm_ref[rows, :].astype(BF16)
        zero = jnp.zeros_like(cb_)
        cbg = [_dot_nt(jnp.where(low, cb_, zero), bb), _dot_nt(jnp.where(low, zero, cb_), bb)]
        mask = (ki <= qi) if fwd else (ki >= qi)
        expand = ef_ref[...] if fwd else eb_ref[...]
        st = st_scr[d]
        hh = jnp.concatenate([jnp.where(first_group, st, 0.0), jnp.where(first_group, 0.0, st)], axis=0)
        tot = cs[q - 1:q, :] if fwd else cs[0:1, :]
        mine = (lane >= d * SSD_HEADS) & (lane < (d + 1) * SSD_HEADS)
        w = jnp.exp2(jnp.where(mine, tot - cs, 0.0)) * dts
        dec = jnp.broadcast_to(jnp.exp2(tot), (2 * SUBLANES, LANES))
        dec_hi = dec.astype(BF16)
        dec_lo = (dec - dec_hi.astype(F32)).astype(BF16)
        spread = _dot(jnp.concatenate([jnp.exp2(cs).astype(BF16), w.astype(BF16), dec_hi, dec_lo], axis=0), expand)
        y = _dot(cb_, hh.astype(BF16)) * spread[0:q]
        pairs = []
        for pp in range(SSD_HEADS // 2):
            ms = []
            for h in (2 * pp, 2 * pp + 1):
                l = d * SSD_HEADS + h
                decay = jnp.where(mask, jnp.exp2(cs[:, l:l + 1] - cst[l:l + 1, :]), 0.0)
                ms.append((cbg[pp // (SSD_HEADS // 4)] * decay * dtt[l:l + 1, :]).astype(BF16))
            xp = xsb[:, pp * LANES:(pp + 1) * LANES]
            zx = jnp.zeros_like(xp)
            x2 = jnp.concatenate([jnp.where(low, xp, zx), jnp.where(low, zx, xp)], axis=0)
            pairs.append(_dot(jnp.concatenate(ms, axis=1), x2))
        y_ref[0, rows, :] += y + jnp.concatenate(pairs, axis=1)
        xw = (xs * spread[q:2 * q]).astype(BF16)
        bt = bt_ref[c].astype(BF16)
        half_w = SSD_INNER // 2
        st_new = jnp.concatenate([_dot(bt[:SSD_STATE], xw[:, :half_w]), _dot(bt[SSD_STATE:], xw[:, half_w:])], axis=1)
        chunk_decay = spread[2 * q:2 * q + 1] + spread[2 * q + 2 * SUBLANES:2 * q + 2 * SUBLANES + 1]
        st_scr[d] = chunk_decay * st + st_new

    def prepare_all(c, carry):
        prepare(c)
        return carry

    nb = seq // SUBLANES
    per_chunk = q // SUBLANES
    row = lax.broadcasted_iota(jnp.int32, (SUBLANES, LRU_W), 0)

    def lru_block(blk, hf, hb):
        rf = pl.multiple_of(blk * SUBLANES, SUBLANES)
        rb = pl.multiple_of((nb - 1 - blk) * SUBLANES, SUBLANES)
        a = la_ref[0, 0, pl.ds(rf, SUBLANES), :]
        b = lb_ref[0, 0, pl.ds(rf, SUBLANES), :]
        for k in (1, 2, 4):
            keep = row >= k
            b = a * jnp.where(keep, pltpu.roll(b, k, 0), 0.0) + b
            a = a * jnp.where(keep, pltpu.roll(a, k, 0), 1.0)
        h = a * hf + b
        ylru_ref[0, pl.ds(rf, SUBLANES), :] += h
        hf = h[SUBLANES - 1:SUBLANES, :]
        a = la_ref[0, 1, pl.ds(rb, SUBLANES), :]
        b = lb_ref[0, 1, pl.ds(rb, SUBLANES), :]
        for k in (1, 2, 4):
            keep = row < SUBLANES - k
            b = a * jnp.where(keep, pltpu.roll(b, SUBLANES - k, 0), 0.0) + b
            a = a * jnp.where(keep, pltpu.roll(a, SUBLANES - k, 0), 1.0)
        h = a * hb + b
        ylru_ref[0, pl.ds(rb, SUBLANES), :] += h
        return hf, h[0:1, :]

    def both_directions(i, carry):
        chunk(i, True)
        chunk(nc - 1 - i, False)
        hf, hb = carry
        for j in range(per_chunk):
            hf, hb = lru_block(i * per_chunk + j, hf, hb)
        return hf, hb

    ylru_ref[0] = jnp.zeros((seq, LRU_W), F32)
    lax.fori_loop(0, nc, prepare_all, 0)
    if has_h0:
        lru_start = (lh0_ref[0, 0:1, :], lh0_ref[0, 1:2, :])
    else:
        lru_start = (jnp.zeros((1, LRU_W), F32), jnp.zeros((1, LRU_W), F32))
    hf, hb = lax.fori_loop(0, nc, both_directions, lru_start)
    if not has_h0:
        hout_ref[0] = st_scr[...]
        lhout_ref[0, 0:1, :] = hf
        lhout_ref[0, 1:2, :] = hb


def _ssd(xbc, dt, cw, cb, dtb, alog, dexp, lru_a, lru_b, h0, lru_h0):
    bsz, seq, _ = xbc.shape
    nc = seq // SSD_Q
    expand = np.zeros((2, LANES, SSD_INNER), np.float32)
    for d in range(2):
        for h in range(SSD_HEADS):
            expand[d, d * SSD_HEADS + h, h * 64:(h + 1) * 64] = 1.0
    expand = jnp.asarray(expand, dtype=BF16)
    per_seq = lambda n, w: pl.BlockSpec((1, n, w), lambda b: (b, 0, 0))
    state_spec = pl.BlockSpec((1, 2, SSD_STATE, SSD_INNER), lambda b: (b, 0, 0, 0))
    has_h0 = h0 is not None
    coef = pl.BlockSpec((1, 2, seq, LRU_W), lambda b: (b, 0, 0, 0))
    in_specs = [per_seq(seq, SSD_CONV_CH), per_seq(seq, LANES), _const_spec(cw.shape), _const_spec(cb.shape),
                _const_spec(dtb.shape), _const_spec(alog.shape), _const_spec(dexp.shape),
                _const_spec((LANES, SSD_INNER)), _const_spec((LANES, SSD_INNER)), coef, coef]
    args = [xbc, dt, cw, cb, dtb, alog, dexp, expand[0], expand[1], lru_a, lru_b]
    out_specs = [per_seq(seq, SSD_INNER), per_seq(seq, LRU_W)]
    out_shape = [jax.ShapeDtypeStruct((bsz, seq, SSD_INNER), F32), jax.ShapeDtypeStruct((bsz, seq, LRU_W), F32)]
    if has_h0:
        in_specs += [state_spec, per_seq(2, LRU_W)]
        args += [h0, lru_h0]
    else:
        out_specs += [state_spec, per_seq(2, LRU_W)]
        out_shape += [jax.ShapeDtypeStruct((bsz, 2, SSD_STATE, SSD_INNER), F32),
                      jax.ShapeDtypeStruct((bsz, 2, LRU_W), F32)]
    outs = pl.pallas_call(
        functools.partial(_ssd_kernel, seq=seq, has_h0=has_h0),
        grid=(bsz,),
        in_specs=in_specs,
        out_specs=out_specs,
        out_shape=out_shape,
        scratch_shapes=[pltpu.VMEM((seq, SSD_INNER), F32)] + [pltpu.VMEM((seq, LANES), F32)] * 4
                       + [pltpu.VMEM((nc, LANES, SSD_Q), F32)] * 3 + [pltpu.VMEM((2, SSD_STATE, SSD_INNER), F32)],
        compiler_params=_params(("arbitrary",)),
        name="ssd",
    )(*args)
    return (outs[0], outs[1], None, None) if has_h0 else tuple(outs)


def _lru_gate_weights(wa, ba, wx, bx):
    eye = 0.5 * jnp.eye(2, dtype=F32)

    def pair_blocks(w):
        w4 = w.reshape(4, 2, 64, 64)
        return jnp.einsum("laij,ab->laibj", w4, eye).reshape(4, LANES, LANES)

    mats = [pair_blocks(w) for w in (wa[0], wx[0], wa[1], wx[1])]
    wg = jnp.concatenate(mats, axis=-1).astype(BF16)
    bias = [0.5 * b.reshape(4, 1, LANES) for b in (ba[0], bx[0], ba[1], bx[1])]
    return wg, jnp.concatenate(bias, axis=-1)


def _trunk(x, mods, p, ctx):
    is_ctx = ctx is None
    bsz, seq, _ = x.shape
    f, q, k, v = _even_pre(x, mods[0], p["norm_mix"][0], p["w_in_even"], p["q_gain"], p["k_gain"], rope=not is_ctx)
    mix = _even_mixer(q, k, v, f, None if is_ctx else (ctx["k"], ctx["v"]))
    x = _post_ffn(x, mix, mods[0], p["w_out_even"], p["norm_ffn"][0], p["w1"], p["w3"], p["w2"], layer=0)
    gate, z, lru_a, lru_b, xbc, dt = _odd_pre(
        x, mods[1], p["norm_mix"][1], p["w_in_odd"], p["w_dt_odd"],
        lru=(p["conv_lru_w"], p["conv_lru_b"], p["lru_wg"], p["lru_bg"], p["lru_lambda"]))
    h0_lru, h0_ssd = (None, None) if is_ctx else (ctx["lru"], ctx["ssd"])
    yssd, ylru, ssd_state, lru_state = _ssd(xbc, dt, p["conv_ssd_w"], p["conv_ssd_b"], p["ssd_dtb"], p["ssd_alog"],
                                            p["ssd_dexp"], lru_a, lru_b, h0_ssd, h0_lru)
    x = _post_ffn(x, (ylru, gate, yssd, z), mods[1], p["w_out_odd"], p["norm_ffn"][1], p["w1"], p["w3"], p["w2"],
                  layer=1, ssd_norm=p["ssd_norm"])
    return x, k, v, lru_state, ssd_state


def kernel(x_prompt, x_sample, c, cache_k, cache_v, state_lru, state_ssd, c_ctx, w_ada, b_ada, norm_mix, norm_ffn,
           w_in_even, q_norm, k_norm, w_out_even, w_in_odd, conv_lru_w, conv_lru_b, lru_wa, lru_ba, lru_wx, lru_bx,
           lru_lambda, conv_ssd_w, conv_ssd_b, ssd_dt_bias, ssd_a_log, ssd_d, ssd_norm, w_out_odd, ffn_w1, ffn_w3,
           ffn_w2):
    bsz, seq, d = x_prompt.shape
    dec_b = x_sample.shape[0]
    past = cache_k.shape[2]

    cond_rows = 16
    cond = jnp.concatenate([c, c_ctx[None, :], jnp.zeros((cond_rows - dec_b - 1, d), F32)], axis=0)
    mod = _modulation(cond, w_ada, b_ada).reshape(2, cond_rows, 6, d)
    mods_sample = [mod[l, :dec_b] for l in range(2)]
    mods_prompt = [mod[l, dec_b:dec_b + 1] for l in range(2)]

    lru_wg, lru_bg = _lru_gate_weights(lru_wa[0], lru_ba[0], lru_wx[0], lru_bx[0])
    pad32 = lambda a: jnp.pad(a.reshape(1, 2 * SSD_HEADS), ((0, 0), (0, LANES - 2 * SSD_HEADS)))
    p = dict(
        norm_mix=norm_mix.reshape(2, 1, d), norm_ffn=norm_ffn.reshape(2, 1, d),
        w_in_even=w_in_even[0].astype(BF16),
        q_gain=jnp.tile(q_norm[0], N_Q_HEADS)[None, :], k_gain=jnp.tile(k_norm[0], N_KV_HEADS)[None, :],
        w_out_even=w_out_even[0].astype(BF16),
        w_in_odd=w_in_odd[0].astype(BF16),
        w_dt_odd=jnp.pad(w_in_odd[0, :, DT_LO:], ((0, 0), (0, LANES - 2 * SSD_HEADS))).astype(BF16),
        conv_lru_w=conv_lru_w[0], conv_lru_b=conv_lru_b[0][None, :],
        lru_wg=lru_wg, lru_bg=lru_bg, lru_lambda=lru_lambda[0],
        conv_ssd_w=conv_ssd_w[0], conv_ssd_b=conv_ssd_b[0][None, :],
        ssd_dtb=pad32(ssd_dt_bias[0]), ssd_alog=pad32(ssd_a_log[0]),
        ssd_dexp=jnp.repeat(ssd_d[0], SSD_INNER // SSD_HEADS)[None, :],
        ssd_norm=ssd_norm[0][None, :],
        w_out_odd=w_out_odd[0].astype(BF16),
        w1=ffn_w1.astype(BF16), w3=ffn_w3.astype(BF16), w2=ffn_w2.astype(BF16),
    )

    y_prompt, k_new, v_new, lru_new, ssd_new = _trunk(x_prompt, mods_prompt, p, None)
    ctx = dict(
        k=cache_k[:, 0].reshape(dec_b, past, KV_W), v=cache_v[:, 0].reshape(dec_b, past, KV_W),
        lru=state_lru[:, 0],
        ssd=state_ssd[:, 0].transpose(0, 1, 4, 2, 3).reshape(dec_b, 2, SSD_STATE, SSD_INNER),
    )
    y_sample, _, _, _, _ = _trunk(x_sample, mods_sample, p, ctx)

    new_k = k_new.reshape(bsz, 1, seq, N_KV_HEADS, HEAD_DIM)
    new_v = v_new.reshape(bsz, 1, seq, N_KV_HEADS, HEAD_DIM)
    new_lru = lru_new.reshape(bsz, 1, 2, LRU_W)
    new_ssd = (ssd_new.reshape(bsz, 2, SSD_STATE, SSD_HEADS, SSD_INNER // SSD_HEADS)
               .transpose(0, 1, 3, 4, 2).reshape(bsz, 1, 2, SSD_HEADS, SSD_INNER // SSD_HEADS, SSD_STATE))
    return (y_prompt, y_sample, new_k, new_v, new_lru, new_ssd)
```

```python
import functools
import math

import numpy as np
import jax
import jax.numpy as jnp
from jax import lax
from jax.experimental import pallas as pl
from jax.experimental.pallas import tpu as pltpu

F32 = jnp.float32
BF16 = jnp.bfloat16

D_MODEL = 1024
EPS = 1e-6
GRID_W = 64
HEAD_DIM = 64
N_Q_HEADS = 12
N_KV_HEADS = 4
FOURIER_W = 256
FOURIER_GROUP_W = 64
Q_W = N_Q_HEADS * HEAD_DIM
KV_W = N_KV_HEADS * HEAD_DIM
EVEN_IN = FOURIER_W + Q_W + 2 * KV_W
ROPE_THETA = 10000.0
AX_PAIRS = HEAD_DIM // 4
LRU_W = 512
LRU_C = 8.0
SSD_INNER = 1024
SSD_HEADS = 16
SSD_STATE = 64
SSD_CONV_CH = SSD_INNER + 4 * SSD_STATE
ODD_IN = 2 * LRU_W + SSD_INNER + SSD_CONV_CH + 2 * SSD_HEADS
D_FF = 2816
LANES = 128
SUBLANES = 8
TOKEN_BLOCK = 512
ROW_TILE = 256
SSD_Q = 128
V_ROWS = HEAD_DIM + 16
VMEM_LIMIT = 56 * 1024 * 1024


def _params(sem, vmem=VMEM_LIMIT):
    return pltpu.CompilerParams(dimension_semantics=sem, vmem_limit_bytes=vmem)


def _const_spec(shape):
    nd = len(shape)
    return pl.BlockSpec(shape, lambda *_: (0,) * nd, pipeline_mode=pl.Buffered(1))


def _dot(a, b):
    return jnp.dot(a, b, preferred_element_type=F32)


def _dot_nt(a, b):
    return lax.dot_general(a, b, (((1,), (1,)), ((), ())), preferred_element_type=F32)


def _dot3(m, a):
    hi = a.astype(BF16)
    r1 = a - hi.astype(F32)
    mid = r1.astype(BF16)
    lo = (r1 - mid.astype(F32)).astype(BF16)
    return _dot(m, hi) + _dot(m, mid) + _dot(m, lo)


def _silu(x):
    return x * jax.nn.sigmoid(x)


def _sigmoid_tanh(x):
    return 0.5 * jnp.tanh(0.5 * x) + 0.5


def _softplus(x):
    return jnp.maximum(x, 0.0) + jnp.log1p(jnp.exp(-jnp.abs(x)))


def _gelu_tanh(x):
    c = math.sqrt(2.0 / math.pi)
    return 0.5 * x * (1.0 + jnp.tanh(c * (x + 0.044715 * (x * x * x))))


def _modnorm(x, g, shift, scale):
    ms = jnp.mean(x * x, axis=-1, keepdims=True)
    y = x * lax.rsqrt(ms + EPS) * g
    return y * (1.0 + scale) + shift


def _mod_kernel(cond_ref, w_ref, b_ref, o_ref):
    s = _silu(cond_ref[...])
    o_ref[0] = _dot(s.astype(BF16), w_ref[0].astype(BF16)) + b_ref[0]


def _modulation(cond, w_ada, b_ada):
    depth, d, n = w_ada.shape
    rows = cond.shape[0]
    tn = 1536
    return pl.pallas_call(
        _mod_kernel,
        grid=(depth, n // tn),
        in_specs=[pl.BlockSpec((rows, d), lambda l, j: (0, 0)),
                  pl.BlockSpec((1, d, tn), lambda l, j: (l, 0, j)),
                  pl.BlockSpec((1, 1, tn), lambda l, j: (l, 0, j))],
        out_specs=pl.BlockSpec((1, rows, tn), lambda l, j: (l, 0, j)),
        out_shape=jax.ShapeDtypeStruct((depth, rows, n), F32),
        compiler_params=_params(("arbitrary", "arbitrary")),
        name="modulation",
    )(cond, w_ada, b_ada.reshape(depth, 1, n))


def _mod_spec(mod):
    if mod.shape[0] == 1:
        return pl.BlockSpec((1, 6, D_MODEL), lambda b, i: (0, 0, 0))
    return pl.BlockSpec((1, 6, D_MODEL), lambda b, i: (b, 0, 0))


def _head_rmsnorm(x, gain):
    rows, width = x.shape
    low = lax.broadcasted_iota(jnp.int32, (rows, LANES), 1) < HEAD_DIM
    out = []
    for b in range(width // LANES):
        blk = x[:, b * LANES:(b + 1) * LANES]
        sq = blk * blk
        lo = jnp.sum(jnp.where(low, sq, 0.0), axis=-1, keepdims=True)
        hi = jnp.sum(jnp.where(low, 0.0, sq), axis=-1, keepdims=True)
        ms = jnp.where(low, lo, hi) * (1.0 / HEAD_DIM)
        out.append(blk * lax.rsqrt(ms + EPS) * gain[:, b * LANES:(b + 1) * LANES])
    return jnp.concatenate(out, axis=-1)


def _rope(x, cos, sin_signed):
    rows, width = x.shape
    lane = lax.broadcasted_iota(jnp.int32, (rows, LANES), 1)
    second = (lane % (2 * AX_PAIRS)) >= AX_PAIRS
    out = []
    for b in range(width // LANES):
        blk = x[:, b * LANES:(b + 1) * LANES]
        partner = jnp.where(second, pltpu.roll(blk, AX_PAIRS, 1), pltpu.roll(blk, LANES - AX_PAIRS, 1))
        out.append(blk * cos + partner * sin_signed)
    return jnp.concatenate(out, axis=-1)


def _even_pre_kernel(*refs, rope):
    if rope:
        x_ref, mod_ref, g_ref, w_ref, qg_ref, kg_ref, cos_ref, sin_ref, f_ref, q_ref, k_ref, v_ref = refs
    else:
        x_ref, mod_ref, g_ref, w_ref, qg_ref, kg_ref, f_ref, q_ref, k_ref, v_ref = refs
    h = _modnorm(x_ref[0], g_ref[...], mod_ref[0, 0:1, :], mod_ref[0, 1:2, :])
    p = _dot(h.astype(BF16), w_ref[...])
    f_ref[0] = p[:, :FOURIER_W]
    q = _head_rmsnorm(p[:, FOURIER_W:FOURIER_W + Q_W], qg_ref[...])
    k = _head_rmsnorm(p[:, FOURIER_W + Q_W:FOURIER_W + Q_W + KV_W], kg_ref[...])
    if rope:
        q = _rope(q, cos_ref[...], sin_ref[...])
        k = _rope(k, cos_ref[...], sin_ref[...])
    q_ref[0] = (q * (HEAD_DIM ** -0.5 * math.log2(math.e))).astype(BF16)
    k_ref[0] = k
    v_ref[0] = p[:, FOURIER_W + Q_W + KV_W:]


def _rope_tables(seq):
    pos = np.arange(seq)
    freqs = ROPE_THETA ** (-np.arange(AX_PAIRS, dtype=np.float64) / AX_PAIRS)
    ang = np.zeros((seq, HEAD_DIM))
    sign = np.zeros((HEAD_DIM,))
    for a, p_a in enumerate((pos // GRID_W, pos % GRID_W)):
        for t in range(2):
            lo = a * 2 * AX_PAIRS + t * AX_PAIRS
            ang[:, lo:lo + AX_PAIRS] = p_a[:, None] * freqs[None, :]
            sign[lo:lo + AX_PAIRS] = -1.0 if t == 0 else 1.0
    cos = np.tile(np.cos(ang), (1, 2)).astype(np.float32)
    sin = np.tile(np.sin(ang) * sign[None, :], (1, 2)).astype(np.float32)
    return jnp.asarray(cos), jnp.asarray(sin)


def _even_pre(x, mod, g, w_in, qg, kg, rope):
    bsz, seq, d = x.shape
    tm = ROW_TILE
    tok = lambda w: pl.BlockSpec((1, tm, w), lambda b, i: (b, i, 0))
    in_specs = [tok(d), _mod_spec(mod), _const_spec((1, d)), _const_spec((d, EVEN_IN)),
                _const_spec((1, Q_W)), _const_spec((1, KV_W))]
    args = [x, mod, g, w_in, qg, kg]
    if rope:
        cos, sin = _rope_tables(seq)
        in_specs += [pl.BlockSpec((tm, LANES), lambda b, i: (i, 0))] * 2
        args += [cos, sin]
    return pl.pallas_call(
        functools.partial(_even_pre_kernel, rope=rope),
        grid=(bsz, seq // tm),
        in_specs=in_specs,
        out_specs=[tok(FOURIER_W), tok(Q_W), tok(KV_W), tok(KV_W)],
        out_shape=[jax.ShapeDtypeStruct((bsz, seq, FOURIER_W), F32),
                   jax.ShapeDtypeStruct((bsz, seq, Q_W), BF16),
                   jax.ShapeDtypeStruct((bsz, seq, KV_W), F32),
                   jax.ShapeDtypeStruct((bsz, seq, KV_W), F32)],
        compiler_params=_params(("arbitrary", "arbitrary")),
        name="even_pre",
    )(*args)


def _attn_kernel(*refs, seq, has_ctx):
    if has_ctx:
        q_ref, k_ref, v_ref, ck_ref, cv_ref, f_ref, cs_ref, cbd_ref, o_ref, kop, vopt, g_scr, cs_scr = refs
    else:
        q_ref, k_ref, v_ref, f_ref, cs_ref, cbd_ref, o_ref, kop, vopt, g_scr, cs_scr = refs
    tq = q_ref.shape[1]

    @pl.when((pl.program_id(0) == 0) & (pl.program_id(1) == 0))
    def _round_tables():
        for r0 in range(0, seq, tq):
            cs_scr[r0:r0 + tq, :] = cs_ref[r0:r0 + tq, :].astype(BF16)

    @pl.when(pl.program_id(1) == 0)
    def _prepare():
        def put(kx, vx, r0):
            n = kx.shape[0]
            low = lax.broadcasted_iota(jnp.int32, (n, LANES), 1) < HEAD_DIM
            ones = jnp.ones((V_ROWS - HEAD_DIM, n), BF16)
            for jp in range(N_KV_HEADS // 2):
                kp = kx[:, jp * LANES:(jp + 1) * LANES]
                k_lo = jnp.where(low, kp, 0.0)
                k_hi = jnp.where(low, 0.0, kp)
                placed = (k_lo, pltpu.roll(k_lo, HEAD_DIM, 1), pltpu.roll(k_hi, HEAD_DIM, 1), k_hi)
                for t, kk in enumerate(placed):
                    kop[4 * jp + t, r0:r0 + n, :] = kk.astype(BF16)
                vt = vx[:, jp * LANES:(jp + 1) * LANES].T.astype(BF16)
                for t in range(2):
                    vopt[2 * jp + t, 0:HEAD_DIM, r0:r0 + n] = vt[t * HEAD_DIM:(t + 1) * HEAD_DIM]
                    vopt[2 * jp + t, HEAD_DIM:V_ROWS, r0:r0 + n] = ones

        put(k_ref[0], v_ref[0], 0)
        if has_ctx:
            put(ck_ref[0], cv_ref[0], seq)
        f = f_ref[0].astype(BF16)
        g_scr[0:seq, :] = _dot(f, cbd_ref[0].astype(BF16)).astype(BF16)
        g_scr[seq:2 * seq, :] = _dot(f, cbd_ref[1].astype(BF16)).astype(BF16)

    r0 = pl.multiple_of(pl.program_id(1) * tq, tq)
    o_ref[0, :, 0:FOURIER_W] = _dot(cs_scr[pl.ds(r0, tq), :], g_scr[...]).astype(BF16)
    q = q_ref[0]
    n_pairs = N_Q_HEADS // 2
    kv_of = lambda qp: [(2 * qp + half) // (N_Q_HEADS // N_KV_HEADS) for half in range(2)]

    def pair_scores(qp):
        qpair = q[:, qp * LANES:(qp + 1) * LANES]
        return [_dot_nt(qpair, kop[2 * kv_of(qp)[half] + half]) for half in range(2)]

    scores_next = pair_scores(0)
    for qp in range(n_pairs):
        kv = kv_of(qp)
        scores = scores_next
        if qp + 1 < n_pairs:
            scores_next = pair_scores(qp + 1)
        probs = [jnp.exp2(s - jnp.max(s, axis=-1, keepdims=True)).astype(BF16) for s in scores]
        outs = []
        for half in range(2):
            o = _dot_nt(vopt[kv[half]], probs[half])
            outs.append(o[0:HEAD_DIM] / o[HEAD_DIM:HEAD_DIM + 1])
        res = jnp.concatenate(outs, axis=0)
        o_ref[0, :, FOURIER_W + qp * LANES:FOURIER_W + (qp + 1) * LANES] = res.T.astype(BF16)


def _dft_tables(seq):
    s = np.arange(seq)
    ang = 2.0 * np.pi * ((s[:, None] * s[None, :]) % seq) / seq
    pos = np.concatenate([np.cos(ang), -np.sin(ang)], axis=1) / math.sqrt(seq)
    c = np.arange(FOURIER_GROUP_W)
    ang_c = 2.0 * np.pi * ((c[:, None] * c[None, :]) % FOURIER_GROUP_W) / FOURIER_GROUP_W
    eye = np.eye(FOURIER_W // FOURIER_GROUP_W)
    chan = np.stack([np.kron(eye, np.cos(ang_c)), np.kron(eye, np.sin(ang_c))]) / math.sqrt(FOURIER_GROUP_W)
    return jnp.asarray(pos, dtype=F32), jnp.asarray(chan, dtype=F32)


def _even_mixer(q, k, v, f, ctx):
    bsz, seq, _ = q.shape
    tq = ROW_TILE
    has_ctx = ctx is not None
    past = ctx[0].shape[1] if has_ctx else 0
    kt = seq + past
    pos_dft, chan_dft = _dft_tables(seq)
    full = lambda n, w: pl.BlockSpec((1, n, w), lambda b, i: (b, 0, 0))
    in_specs = [pl.BlockSpec((1, tq, Q_W), lambda b, i: (b, i, 0)), full(seq, KV_W), full(seq, KV_W)]
    args = [q, k, v]
    if has_ctx:
        in_specs += [full(past, KV_W), full(past, KV_W)]
        args += list(ctx)
    in_specs += [full(seq, FOURIER_W), _const_spec((seq, 2 * seq)),
                 _const_spec((2, FOURIER_W, FOURIER_W))]
    args += [f, pos_dft, chan_dft]
    return pl.pallas_call(
        functools.partial(_attn_kernel, seq=seq, has_ctx=has_ctx),
        grid=(bsz, seq // tq),
        in_specs=in_specs,
        out_specs=pl.BlockSpec((1, tq, D_MODEL), lambda b, i: (b, i, 0)),
        out_shape=jax.ShapeDtypeStruct((bsz, seq, D_MODEL), BF16),
        scratch_shapes=[pltpu.VMEM((2 * N_KV_HEADS, kt, LANES), BF16),
                        pltpu.VMEM((N_KV_HEADS, V_ROWS, kt), BF16),
                        pltpu.VMEM((2 * seq, FOURIER_W), BF16),
                        pltpu.VMEM((seq, 2 * seq), BF16)],
        compiler_params=_params(("arbitrary", "arbitrary")),
        name="even_mixer",
    )(*args)


def _post_ffn_kernel(*refs, odd):
    if odd:
        (x_ref, ylru_ref, g_ref, yssd_ref, z_ref, sn_ref, mod_ref, wout_ref, gf_ref,
         w1_ref, w3_ref, w2_ref, o_ref) = refs
        y_lru = ylru_ref[0] * _gelu_tanh(g_ref[0])
        y = yssd_ref[0] * _silu(z_ref[0])
        ms = jnp.mean(y * y, axis=-1, keepdims=True)
        y_ssd = y * lax.rsqrt(ms + EPS) * sn_ref[...]
        mix = jnp.concatenate([y_lru, y_ssd], axis=-1).astype(BF16)
    else:
        x_ref, mix_ref, mod_ref, wout_ref, gf_ref, w1_ref, w3_ref, w2_ref, o_ref = refs
        mix = mix_ref[0]
    x = x_ref[0] + mod_ref[0, 2:3, :] * _dot(mix, wout_ref[...])
    h = _modnorm(x, gf_ref[...], mod_ref[0, 3:4, :], mod_ref[0, 4:5, :]).astype(BF16)
    a = (_silu(_dot(h, w1_ref[...])) * _dot(h, w3_ref[...])).astype(BF16)
    o_ref[0] = x + mod_ref[0, 5:6, :] * _dot(a, w2_ref[...])


def _post_ffn(x, mix_args, mod, w_out, g_ffn, w1, w3, w2, layer, ssd_norm=None):
    bsz, seq, d = x.shape
    if mod.shape[0] == 1 and seq < TOKEN_BLOCK:
        fold = TOKEN_BLOCK // seq
        merge = lambda t: t.reshape(bsz // fold, fold * seq, t.shape[-1])
        out = _post_ffn(merge(x), jax.tree.map(merge, mix_args), mod, w_out, g_ffn, w1, w3, w2, layer, ssd_norm)
        return out.reshape(bsz, seq, d)
    tm = min(seq, TOKEN_BLOCK)
    odd = layer % 2 == 1
    tok = lambda w: pl.BlockSpec((1, tm, w), lambda b, i: (b, i, 0))
    slab = lambda w: pl.BlockSpec((None,) + w.shape[1:], lambda b, i: (layer, 0, 0), pipeline_mode=pl.Buffered(1))
    if odd:
        ylru, g, yssd, z = mix_args
        in_specs = [tok(d), tok(LRU_W), tok(LRU_W), tok(SSD_INNER), tok(SSD_INNER), _const_spec((1, SSD_INNER))]
        args = [x, ylru, g, yssd, z, ssd_norm]
    else:
        in_specs = [tok(d), tok(d)]
        args = [x, mix_args]
    in_specs += [_mod_spec(mod), _const_spec(w_out.shape), _const_spec((1, d)),
                 slab(w1), slab(w3), slab(w2)]
    args += [mod, w_out, g_ffn, w1, w3, w2]
    return pl.pallas_call(
        functools.partial(_post_ffn_kernel, odd=odd),
        grid=(bsz, seq // tm),
        in_specs=in_specs,
        out_specs=tok(d),
        out_shape=jax.ShapeDtypeStruct((bsz, seq, d), F32),
        compiler_params=_params(("arbitrary", "arbitrary")),
        name="post_ffn_odd" if odd else "post_ffn_even",
    )(*args)


XL_LO = LRU_W
Z_LO = 2 * LRU_W
XBC_LO = 2 * LRU_W + SSD_INNER
DT_LO = XBC_LO + SSD_CONV_CH


def _conv_block(main, edges, w, b):
    before, after0, after1 = edges
    rows = main.shape[0]
    row = lax.broadcasted_iota(jnp.int32, (SUBLANES, 1), 0)
    last = rows - SUBLANES

    def patch_first(x, fix):
        return jnp.concatenate([fix(x[:SUBLANES]), x[SUBLANES:]], axis=0)

    def patch_last(x, fix):
        return jnp.concatenate([x[:last], fix(x[last:])], axis=0)

    xm1 = patch_first(pltpu.roll(main, 1, 0), lambda t: jnp.where(row == 0, before, t))
    xp1 = patch_last(pltpu.roll(main, rows - 1, 0), lambda t: jnp.where(row == SUBLANES - 1, after0, t))
    xp2 = patch_last(pltpu.roll(main, rows - 2, 0),
                     lambda t: jnp.where(row == SUBLANES - 2, after0, jnp.where(row == SUBLANES - 1, after1, t)))
    return xm1 * w[0:1] + main * w[1:2] + xp1 * w[2:3] + xp2 * w[3:4] + b


def _odd_pre_kernel(x_ref, xprev_ref, xnext_ref, mod_ref, g_ref, w_ref, wdt_ref, cwl_ref, cbl_ref, wg_ref, bg_ref,
                    lam_ref, gate_ref, z_ref, a_ref, b_ref, xbc_ref, dt_ref):
    i = pl.program_id(1)
    g, shift, scale = g_ref[...], mod_ref[0, 0:1, :], mod_ref[0, 1:2, :]
    h = _modnorm(x_ref[0], g, shift, scale).astype(BF16)
    wide = 2 * LANES
    proj = lambda lo: _dot(h, w_ref[:, lo:lo + wide])
    p_xl = [proj(XL_LO + t * wide) for t in range(LRU_W // wide)]
    others = ([(gate_ref, t * wide, t * wide) for t in range(LRU_W // wide)]
              + [(z_ref, Z_LO + t * wide, t * wide) for t in range(SSD_INNER // wide)]
              + [(xbc_ref, XBC_LO + t * wide, t * wide) for t in range(SSD_CONV_CH // wide)])

    halo = jnp.concatenate([xprev_ref[0], xnext_ref[0]], axis=0)
    ph = _dot(_modnorm(halo, g, shift, scale).astype(BF16), w_ref[:, XL_LO:Z_LO])
    has_prev = i > 0
    has_next = i < pl.num_programs(1) - 1
    e_l = (jnp.where(has_prev, ph[SUBLANES - 1:SUBLANES], 0.0),
           jnp.where(has_next, ph[SUBLANES:SUBLANES + 1], 0.0),
           jnp.where(has_next, ph[SUBLANES + 1:SUBLANES + 2], 0.0))

    rate = (0.5 * LRU_C) * _softplus(-lam_ref[...])
    n_lru = LRU_W // LANES
    for lb in range(n_lru):
        sl = slice(lb * LANES, (lb + 1) * LANES)
        for dst, lo, col in others[lb * len(others) // n_lru:(lb + 1) * len(others) // n_lru]:
            dst[0, :, col:col + wide] = proj(lo)
        xb = _conv_block(p_xl[lb // 2][:, (lb % 2) * LANES:(lb % 2 + 1) * LANES], [e[:, sl] for e in e_l],
                         cwl_ref[:, sl], cbl_ref[:, sl])
        xh = 0.5 * xb
        gt = jnp.tanh(_dot(xb.astype(BF16), wg_ref[lb]) + bg_ref[lb])
        for d in range(2):
            neg_log_a = rate[d:d + 1, sl] * (gt[:, 2 * d * LANES:(2 * d + 1) * LANES] + 1.0)
            a = jnp.exp2(neg_log_a * (-math.log2(math.e)))
            g2 = jnp.tanh(neg_log_a) * (a * a + 1.0)
            root = jnp.where(g2 > 0.0, g2 * lax.rsqrt(g2), 0.0)
            a_ref[0, d, :, sl] = a
            b_ref[0, d, :, sl] = root * ((gt[:, (2 * d + 1) * LANES:(2 * d + 2) * LANES] + 1.0) * xh)
    dt_ref[0] = _dot(h, wdt_ref[...])


def _odd_pre(x, mod, g, w_in, w_dt, lru):
    bsz, seq, d = x.shape
    tm = min(seq, TOKEN_BLOCK)
    per8 = tm // SUBLANES
    tok = lambda w: pl.BlockSpec((1, tm, w), lambda b, i: (b, i, 0))
    both = pl.BlockSpec((1, 2, tm, LRU_W), lambda b, i: (b, 0, i, 0))
    consts = list(lru)
    out_specs = [tok(LRU_W), tok(SSD_INNER), both, both, tok(SSD_CONV_CH), tok(LANES)]
    row = lambda w: jax.ShapeDtypeStruct((bsz, seq, w), F32)
    out_shape = ([row(LRU_W), row(SSD_INNER)] + [jax.ShapeDtypeStruct((bsz, 2, seq, LRU_W), F32)] * 2
                 + [row(SSD_CONV_CH), row(LANES)])
    return pl.pallas_call(
        _odd_pre_kernel,
        grid=(bsz, seq // tm),
        in_specs=[tok(d),
                  pl.BlockSpec((1, SUBLANES, d), lambda b, i: (b, jnp.maximum(i * per8 - 1, 0), 0)),
                  pl.BlockSpec((1, SUBLANES, d), lambda b, i: (b, jnp.minimum((i + 1) * per8, seq // SUBLANES - 1), 0)),
                  _mod_spec(mod), _const_spec((1, d)), _const_spec(w_in.shape), _const_spec(w_dt.shape)]
                 + [_const_spec(c.shape) for c in consts],
        out_specs=out_specs,
        out_shape=out_shape,
        compiler_params=_params(("arbitrary", "arbitrary")),
        name="odd_pre",
    )(x, x, x, mod, g, w_in, w_dt, *consts)


def _ssd_kernel(*refs, seq, has_h0):
    xbc_ref, dt_ref, cw_ref, cb_ref, dtb_ref, alog_ref, dexp_ref, ef_ref, eb_ref, la_ref, lb_ref = refs[:11]
    if has_h0:
        h0_ref, lh0_ref, y_ref, ylru_ref = refs[11:15]
    else:
        y_ref, ylru_ref, hout_ref, lhout_ref = refs[11:15]
    xs_ref, bm_ref, cm_ref, cs_ref, dts_ref, cst_ref, dtt_ref, bt_ref, st_scr = refs[15:]
    q = SSD_Q
    nc = seq // q
    lane = lax.broadcasted_iota(jnp.int32, (q, LANES), 1)
    qi = lax.broadcasted_iota(jnp.int32, (q, q), 0)
    ki = lax.broadcasted_iota(jnp.int32, (q, q), 1)
    tri_lower = jnp.where(ki <= qi, 1.0, 0.0).astype(BF16)
    tri_upper = jnp.where(ki >= qi, 1.0, 0.0).astype(BF16)
    a_log2 = jnp.where(lane[0:1, :] < 2 * SSD_HEADS, -math.log2(math.e) * jnp.exp(alog_ref[...]), 0.0)

    def prepare(c):
        r0 = pl.multiple_of(c * q, q)
        rows = pl.ds(r0, q)
        prev8 = pl.ds(pl.multiple_of(jnp.maximum(r0 - SUBLANES, 0), SUBLANES), SUBLANES)
        next8 = pl.ds(pl.multiple_of(jnp.minimum(r0 + q, seq - SUBLANES), SUBLANES), SUBLANES)
        has_prev = r0 > 0
        has_next = r0 + q < seq
        for lb in range(SSD_CONV_CH // LANES):
            sl = slice(lb * LANES, (lb + 1) * LANES)
            before = xbc_ref[0, prev8, sl]
            after = xbc_ref[0, next8, sl]
            edges = (jnp.where(has_prev, before[SUBLANES - 1:SUBLANES], 0.0),
                     jnp.where(has_next, after[0:1], 0.0), jnp.where(has_next, after[1:2], 0.0))
            xc = _conv_block(xbc_ref[0, rows, sl], edges, cw_ref[:, sl], cb_ref[:, sl])
            xc = xc * _sigmoid_tanh(xc)
            if lb < SSD_INNER // LANES:
                xs_ref[rows, sl] = xc
                y_ref[0, rows, sl] = dexp_ref[:, sl] * xc
            elif lb == SSD_INNER // LANES:
                bm_ref[rows, :] = xc
                bt_ref[c] = xc.T
            else:
                cm_ref[rows, :] = xc
        dts = _softplus(dt_ref[0, rows, :] + dtb_ref[...])
        da = dts * a_log2
        cs = jnp.where(lane < SSD_HEADS, _dot3(tri_lower, da), _dot3(tri_upper, da))
        cs_ref[rows, :] = cs
        dts_ref[rows, :] = dts
        cst_ref[c] = cs.T
        dtt_ref[c] = dts.T

    st_scr[...] = h0_ref[0] if has_h0 else jnp.zeros(st_scr.shape, F32)

    low = lane < SSD_STATE
    col = lax.broadcasted_iota(jnp.int32, (SSD_STATE, SSD_INNER), 1)
    first_group = col < SSD_INNER // 2

    def chunk(c, fwd):
        d = 0 if fwd else 1
        r0 = pl.multiple_of(c * q, q)
        rows = pl.ds(r0, q)
        cs = cs_ref[rows, :]
        dts = dts_ref[rows, :]
        cst = cst_ref[c]
        dtt = dtt_ref[c]
        xs = xs_ref[rows, :]
        xsb = xs.astype(BF16)
        bb = bm_ref[rows, :].astype(BF16)
        cb_ = cm_ref[rows, :].astype(BF16)
        zero = jnp.zeros_like(cb_)
        cbg = [_dot_nt(jnp.where(low, cb_, zero), bb).astype(BF16),
               _dot_nt(jnp.where(low, zero, cb_), bb).astype(BF16)]
        mask = (ki <= qi) if fwd else (ki >= qi)
        expand = ef_ref[...] if fwd else eb_ref[...]
        st = st_scr[d]
        hh = jnp.concatenate([jnp.where(first_group, st, 0.0), jnp.where(first_group, 0.0, st)], axis=0)
        tot = cs[q - 1:q, :] if fwd else cs[0:1, :]
        mine = (lane >= d * SSD_HEADS) & (lane < (d + 1) * SSD_HEADS)
        w = jnp.exp2(jnp.where(mine, tot - cs, 0.0)) * dts
        dec = jnp.broadcast_to(jnp.exp2(tot), (2 * SUBLANES, LANES))
        dec_hi = dec.astype(BF16)
        dec_lo = (dec - dec_hi.astype(F32)).astype(BF16)
        spread = _dot(jnp.concatenate([jnp.exp2(cs).astype(BF16), w.astype(BF16), dec_hi, dec_lo], axis=0), expand)
        y = _dot(cb_, hh.astype(BF16)) * spread[0:q]
        pairs = []
        for pp in range(SSD_HEADS // 2):
            ms = []
            for h in (2 * pp, 2 * pp + 1):
                l = d * SSD_HEADS + h
                decay = jnp.exp2((cs[:, l:l + 1] - cst[l:l + 1, :]).astype(BF16))
                decay = jnp.where(mask, decay, jnp.zeros_like(decay))
                ms.append(cbg[pp // (SSD_HEADS // 4)] * decay * dtt[l:l + 1, :].astype(BF16))
            xp = xsb[:, pp * LANES:(pp + 1) * LANES]
            zx = jnp.zeros_like(xp)
            x2 = jnp.concatenate([jnp.where(low, xp, zx), jnp.where(low, zx, xp)], axis=0)
            pairs.append(_dot(jnp.concatenate(ms, axis=1), x2))
        y_ref[0, rows, :] += y + jnp.concatenate(pairs, axis=1)
        xw = (xs * spread[q:2 * q]).astype(BF16)
        bt = bt_ref[c].astype(BF16)
        half_w = SSD_INNER // 2
        st_new = jnp.concatenate([_dot(bt[:SSD_STATE], xw[:, :half_w]), _dot(bt[SSD_STATE:], xw[:, half_w:])], axis=1)
        chunk_decay = spread[2 * q:2 * q + 1] + spread[2 * q + 2 * SUBLANES:2 * q + 2 * SUBLANES + 1]
        st_scr[d] = chunk_decay * st + st_new

    def prepare_all(c, carry):
        prepare(c)
        return carry

    nb = seq // SUBLANES
    per_chunk = q // SUBLANES
    row = lax.broadcasted_iota(jnp.int32, (SUBLANES, LRU_W), 0)

    def lru_block(blk, hf, hb):
        rf = pl.multiple_of(blk * SUBLANES, SUBLANES)
        rb = pl.multiple_of((nb - 1 - blk) * SUBLANES, SUBLANES)
        a = la_ref[0, 0, pl.ds(rf, SUBLANES), :]
        b = lb_ref[0, 0, pl.ds(rf, SUBLANES), :]
        for k in (1, 2, 4):
            keep = row >= k
            b = a * jnp.where(keep, pltpu.roll(b, k, 0), 0.0) + b
            a = a * jnp.where(keep, pltpu.roll(a, k, 0), 1.0)
        h = a * hf + b
        ylru_ref[0, pl.ds(rf, SUBLANES), :] += h
        hf = h[SUBLANES - 1:SUBLANES, :]
        a = la_ref[0, 1, pl.ds(rb, SUBLANES), :]
        b = lb_ref[0, 1, pl.ds(rb, SUBLANES), :]
        for k in (1, 2, 4):
            keep = row < SUBLANES - k
            b = a * jnp.where(keep, pltpu.roll(b, SUBLANES - k, 0), 0.0) + b
            a = a * jnp.where(keep, pltpu.roll(a, SUBLANES - k, 0), 1.0)
        h = a * hb + b
        ylru_ref[0, pl.ds(rb, SUBLANES), :] += h
        return hf, h[0:1, :]

    def both_directions(i, carry):
        chunk(i, True)
        chunk(nc - 1 - i, False)
        hf, hb = carry
        for j in range(per_chunk):
            hf, hb = lru_block(i * per_chunk + j, hf, hb)
        return hf, hb

    ylru_ref[0] = jnp.zeros((seq, LRU_W), F32)
    lax.fori_loop(0, nc, prepare_all, 0)
    if has_h0:
        lru_start = (lh0_ref[0, 0:1, :], lh0_ref[0, 1:2, :])
    else:
        lru_start = (jnp.zeros((1, LRU_W), F32), jnp.zeros((1, LRU_W), F32))
    hf, hb = lax.fori_loop(0, nc, both_directions, lru_start)
    if not has_h0:
        hout_ref[0] = st_scr[...]
        lhout_ref[0, 0:1, :] = hf
        lhout_ref[0, 1:2, :] = hb


def _ssd(xbc, dt, cw, cb, dtb, alog, dexp, lru_a, lru_b, h0, lru_h0):
    bsz, seq, _ = xbc.shape
    nc = seq // SSD_Q
    expand = np.zeros((2, LANES, SSD_INNER), np.float32)
    for d in range(2):
        for h in range(SSD_HEADS):
            expand[d, d * SSD_HEADS + h, h * 64:(h + 1) * 64] = 1.0
    expand = jnp.asarray(expand, dtype=BF16)
    per_seq = lambda n, w: pl.BlockSpec((1, n, w), lambda b: (b, 0, 0))
    state_spec = pl.BlockSpec((1, 2, SSD_STATE, SSD_INNER), lambda b: (b, 0, 0, 0))
    has_h0 = h0 is not None
    coef = pl.BlockSpec((1, 2, seq, LRU_W), lambda b: (b, 0, 0, 0))
    in_specs = [per_seq(seq, SSD_CONV_CH), per_seq(seq, LANES), _const_spec(cw.shape), _const_spec(cb.shape),
                _const_spec(dtb.shape), _const_spec(alog.shape), _const_spec(dexp.shape),
                _const_spec((LANES, SSD_INNER)), _const_spec((LANES, SSD_INNER)), coef, coef]
    args = [xbc, dt, cw, cb, dtb, alog, dexp, expand[0], expand[1], lru_a, lru_b]
    out_specs = [per_seq(seq, SSD_INNER), per_seq(seq, LRU_W)]
    out_shape = [jax.ShapeDtypeStruct((bsz, seq, SSD_INNER), F32), jax.ShapeDtypeStruct((bsz, seq, LRU_W), F32)]
    if has_h0:
        in_specs += [state_spec, per_seq(2, LRU_W)]
        args += [h0, lru_h0]
    else:
        out_specs += [state_spec, per_seq(2, LRU_W)]
        out_shape += [jax.ShapeDtypeStruct((bsz, 2, SSD_STATE, SSD_INNER), F32),
                      jax.ShapeDtypeStruct((bsz, 2, LRU_W), F32)]
    outs = pl.pallas_call(
        functools.partial(_ssd_kernel, seq=seq, has_h0=has_h0),
        grid=(bsz,),
        in_specs=in_specs,
        out_specs=out_specs,
        out_shape=out_shape,
        scratch_shapes=[pltpu.VMEM((seq, SSD_INNER), F32)] + [pltpu.VMEM((seq, LANES), F32)] * 4
                       + [pltpu.VMEM((nc, LANES, SSD_Q), F32)] * 3 + [pltpu.VMEM((2, SSD_STATE, SSD_INNER), F32)],
        compiler_params=_params(("arbitrary",)),
        name="ssd",
    )(*args)
    return (outs[0], outs[1], None, None) if has_h0 else tuple(outs)


def _lru_gate_weights(wa, ba, wx, bx):
    eye = 0.5 * jnp.eye(2, dtype=F32)

    def pair_blocks(w):
        w4 = w.reshape(4, 2, 64, 64)
        return jnp.einsum("laij,ab->laibj", w4, eye).reshape(4, LANES, LANES)

    mats = [pair_blocks(w) for w in (wa[0], wx[0], wa[1], wx[1])]
    wg = jnp.concatenate(mats, axis=-1).astype(BF16)
    bias = [0.5 * b.reshape(4, 1, LANES) for b in (ba[0], bx[0], ba[1], bx[1])]
    return wg, jnp.concatenate(bias, axis=-1)


def _trunk(x, mods, p, ctx):
    is_ctx = ctx is None
    bsz, seq, _ = x.shape
    f, q, k, v = _even_pre(x, mods[0], p["norm_mix"][0], p["w_in_even"], p["q_gain"], p["k_gain"], rope=not is_ctx)
    mix = _even_mixer(q, k, v, f, None if is_ctx else (ctx["k"], ctx["v"]))
    x = _post_ffn(x, mix, mods[0], p["w_out_even"], p["norm_ffn"][0], p["w1"], p["w3"], p["w2"], layer=0)
    gate, z, lru_a, lru_b, xbc, dt = _odd_pre(
        x, mods[1], p["norm_mix"][1], p["w_in_odd"], p["w_dt_odd"],
        lru=(p["conv_lru_w"], p["conv_lru_b"], p["lru_wg"], p["lru_bg"], p["lru_lambda"]))
    h0_lru, h0_ssd = (None, None) if is_ctx else (ctx["lru"], ctx["ssd"])
    yssd, ylru, ssd_state, lru_state = _ssd(xbc, dt, p["conv_ssd_w"], p["conv_ssd_b"], p["ssd_dtb"], p["ssd_alog"],
                                            p["ssd_dexp"], lru_a, lru_b, h0_ssd, h0_lru)
    x = _post_ffn(x, (ylru, gate, yssd, z), mods[1], p["w_out_odd"], p["norm_ffn"][1], p["w1"], p["w3"], p["w2"],
                  layer=1, ssd_norm=p["ssd_norm"])
    return x, k, v, lru_state, ssd_state


def kernel(x_prompt, x_sample, c, cache_k, cache_v, state_lru, state_ssd, c_ctx, w_ada, b_ada, norm_mix, norm_ffn,
           w_in_even, q_norm, k_norm, w_out_even, w_in_odd, conv_lru_w, conv_lru_b, lru_wa, lru_ba, lru_wx, lru_bx,
           lru_lambda, conv_ssd_w, conv_ssd_b, ssd_dt_bias, ssd_a_log, ssd_d, ssd_norm, w_out_odd, ffn_w1, ffn_w3,
           ffn_w2):
    bsz, seq, d = x_prompt.shape
    dec_b = x_sample.shape[0]
    past = cache_k.shape[2]

    cond_rows = 16
    cond = jnp.concatenate([c, c_ctx[None, :], jnp.zeros((cond_rows - dec_b - 1, d), F32)], axis=0)
    mod = _modulation(cond, w_ada, b_ada).reshape(2, cond_rows, 6, d)
    mods_sample = [mod[l, :dec_b] for l in range(2)]
    mods_prompt = [mod[l, dec_b:dec_b + 1] for l in range(2)]

    lru_wg, lru_bg = _lru_gate_weights(lru_wa[0], lru_ba[0], lru_wx[0], lru_bx[0])
    pad32 = lambda a: jnp.pad(a.reshape(1, 2 * SSD_HEADS), ((0, 0), (0, LANES - 2 * SSD_HEADS)))
    p = dict(
        norm_mix=norm_mix.reshape(2, 1, d), norm_ffn=norm_ffn.reshape(2, 1, d),
        w_in_even=w_in_even[0].astype(BF16),
        q_gain=jnp.tile(q_norm[0], N_Q_HEADS)[None, :], k_gain=jnp.tile(k_norm[0], N_KV_HEADS)[None, :],
        w_out_even=w_out_even[0].astype(BF16),
        w_in_odd=w_in_odd[0].astype(BF16),
        w_dt_odd=jnp.pad(w_in_odd[0, :, DT_LO:], ((0, 0), (0, LANES - 2 * SSD_HEADS))).astype(BF16),
        conv_lru_w=conv_lru_w[0], conv_lru_b=conv_lru_b[0][None, :],
        lru_wg=lru_wg, lru_bg=lru_bg, lru_lambda=lru_lambda[0],
        conv_ssd_w=conv_ssd_w[0], conv_ssd_b=conv_ssd_b[0][None, :],
        ssd_dtb=pad32(ssd_dt_bias[0]), ssd_alog=pad32(ssd_a_log[0]),
        ssd_dexp=jnp.repeat(ssd_d[0], SSD_INNER // SSD_HEADS)[None, :],
        ssd_norm=ssd_norm[0][None, :],
        w_out_odd=w_out_odd[0].astype(BF16),
        w1=ffn_w1.astype(BF16), w3=ffn_w3.astype(BF16), w2=ffn_w2.astype(BF16),
    )

    y_prompt, k_new, v_new, lru_new, ssd_new = _trunk(x_prompt, mods_prompt, p, None)
    ctx = dict(
        k=cache_k[:, 0].reshape(dec_b, past, KV_W), v=cache_v[:, 0].reshape(dec_b, past, KV_W),
        lru=state_lru[:, 0],
        ssd=state_ssd[:, 0].transpose(0, 1, 4, 2, 3).reshape(dec_b, 2, SSD_STATE, SSD_INNER),
    )
    y_sample, _, _, _, _ = _trunk(x_sample, mods_sample, p, ctx)

    new_k = k_new.reshape(bsz, 1, seq, N_KV_HEADS, HEAD_DIM)
    new_v = v_new.reshape(bsz, 1, seq, N_KV_HEADS, HEAD_DIM)
    new_lru = lru_new.reshape(bsz, 1, 2, LRU_W)
    new_ssd = (ssd_new.reshape(bsz, 2, SSD_STATE, SSD_HEADS, SSD_INNER // SSD_HEADS)
               .transpose(0, 1, 3, 4, 2).reshape(bsz, 1, 2, SSD_HEADS, SSD_INNER // SSD_HEADS, SSD_STATE))
    return (y_prompt, y_sample, new_k, new_v, new_lru, new_ssd)
```

```python
import functools
import math

import numpy as np
import jax
import jax.numpy as jnp
from jax import lax
from jax.experimental import pallas as pl
from jax.experimental.pallas import tpu as pltpu

F32 = jnp.float32
BF16 = jnp.bfloat16

D_MODEL = 1024
EPS = 1e-6
GRID_W = 64
HEAD_DIM = 64
N_Q_HEADS = 12
N_KV_HEADS = 4
FOURIER_W = 256
FOURIER_GROUP_W = 64
Q_W = N_Q_HEADS * HEAD_DIM
KV_W = N_KV_HEADS * HEAD_DIM
EVEN_IN = FOURIER_W + Q_W + 2 * KV_W
ROPE_THETA = 10000.0
AX_PAIRS = HEAD_DIM // 4
LRU_W = 512
LRU_C = 8.0
SSD_INNER = 1024
SSD_HEADS = 16
SSD_STATE = 64
SSD_CONV_CH = SSD_INNER + 4 * SSD_STATE
D_FF = 2816
LANES = 128
SUBLANES = 8
TOKEN_BLOCK = 512
ROW_TILE = 256
SSD_Q = 128
V_ROWS = HEAD_DIM + 16
VMEM_LIMIT = 56 * 1024 * 1024


def _params(sem, vmem=VMEM_LIMIT):
    return pltpu.CompilerParams(dimension_semantics=sem, vmem_limit_bytes=vmem)


def _const_spec(shape):
    nd = len(shape)
    return pl.BlockSpec(shape, lambda *_: (0,) * nd, pipeline_mode=pl.Buffered(1))


def _dot(a, b):
    return jnp.dot(a, b, preferred_element_type=F32)


def _dot_nt(a, b):
    return lax.dot_general(a, b, (((1,), (1,)), ((), ())), preferred_element_type=F32)


def _dot3(m, a):
    hi = a.astype(BF16)
    r1 = a - hi.astype(F32)
    mid = r1.astype(BF16)
    lo = (r1 - mid.astype(F32)).astype(BF16)
    return _dot(m, hi) + _dot(m, mid) + _dot(m, lo)


def _silu(x):
    return x * jax.nn.sigmoid(x)


def _softplus(x):
    return jnp.maximum(x, 0.0) + jnp.log1p(jnp.exp(-jnp.abs(x)))


def _gelu_tanh(x):
    c = math.sqrt(2.0 / math.pi)
    return 0.5 * x * (1.0 + jnp.tanh(c * (x + 0.044715 * (x * x * x))))


def _modnorm(x, g, shift, scale):
    ms = jnp.mean(x * x, axis=-1, keepdims=True)
    y = x * lax.rsqrt(ms + EPS) * g
    return y * (1.0 + scale) + shift


def _mod_kernel(cond_ref, w_ref, b_ref, o_ref):
    s = _silu(cond_ref[...])
    o_ref[0] = _dot(s.astype(BF16), w_ref[0].astype(BF16)) + b_ref[0]


def _modulation(cond, w_ada, b_ada):
    depth, d, n = w_ada.shape
    rows = cond.shape[0]
    tn = 3072
    return pl.pallas_call(
        _mod_kernel,
        grid=(depth, n // tn),
        in_specs=[pl.BlockSpec((rows, d), lambda l, j: (0, 0)),
                  pl.BlockSpec((1, d, tn), lambda l, j: (l, 0, j)),
                  pl.BlockSpec((1, 1, tn), lambda l, j: (l, 0, j))],
        out_specs=pl.BlockSpec((1, rows, tn), lambda l, j: (l, 0, j)),
        out_shape=jax.ShapeDtypeStruct((depth, rows, n), F32),
        compiler_params=_params(("arbitrary", "arbitrary")),
        name="modulation",
    )(cond, w_ada, b_ada.reshape(depth, 1, n))


def _mod_spec(mod):
    if mod.shape[0] == 1:
        return pl.BlockSpec((1, 6, D_MODEL), lambda b, i: (0, 0, 0))
    return pl.BlockSpec((1, 6, D_MODEL), lambda b, i: (b, 0, 0))


def _head_rmsnorm(x, gain):
    rows, width = x.shape
    low = lax.broadcasted_iota(jnp.int32, (rows, LANES), 1) < HEAD_DIM
    out = []
    for b in range(width // LANES):
        blk = x[:, b * LANES:(b + 1) * LANES]
        sq = blk * blk
        lo = jnp.sum(jnp.where(low, sq, 0.0), axis=-1, keepdims=True)
        hi = jnp.sum(jnp.where(low, 0.0, sq), axis=-1, keepdims=True)
        ms = jnp.where(low, lo, hi) * (1.0 / HEAD_DIM)
        out.append(blk * lax.rsqrt(ms + EPS) * gain[:, b * LANES:(b + 1) * LANES])
    return jnp.concatenate(out, axis=-1)


def _rope(x, cos, sin_signed):
    rows, width = x.shape
    lane = lax.broadcasted_iota(jnp.int32, (rows, LANES), 1)
    second = (lane % (2 * AX_PAIRS)) >= AX_PAIRS
    out = []
    for b in range(width // LANES):
        blk = x[:, b * LANES:(b + 1) * LANES]
        partner = jnp.where(second, pltpu.roll(blk, AX_PAIRS, 1), pltpu.roll(blk, LANES - AX_PAIRS, 1))
        out.append(blk * cos + partner * sin_signed)
    return jnp.concatenate(out, axis=-1)


def _even_pre_kernel(*refs, rope):
    if rope:
        x_ref, mod_ref, g_ref, w_ref, qg_ref, kg_ref, cos_ref, sin_ref, f_ref, q_ref, k_ref, v_ref = refs
    else:
        x_ref, mod_ref, g_ref, w_ref, qg_ref, kg_ref, f_ref, q_ref, k_ref, v_ref = refs
    h = _modnorm(x_ref[0], g_ref[...], mod_ref[0, 0:1, :], mod_ref[0, 1:2, :])
    p = _dot(h.astype(BF16), w_ref[...])
    f_ref[0] = p[:, :FOURIER_W]
    q = _head_rmsnorm(p[:, FOURIER_W:FOURIER_W + Q_W], qg_ref[...])
    k = _head_rmsnorm(p[:, FOURIER_W + Q_W:FOURIER_W + Q_W + KV_W], kg_ref[...])
    if rope:
        q = _rope(q, cos_ref[...], sin_ref[...])
        k = _rope(k, cos_ref[...], sin_ref[...])
    q_ref[0] = (q * (HEAD_DIM ** -0.5 * math.log2(math.e))).astype(BF16)
    k_ref[0] = k
    v_ref[0] = p[:, FOURIER_W + Q_W + KV_W:]


def _rope_tables(seq):
    pos = np.arange(seq)
    freqs = ROPE_THETA ** (-np.arange(AX_PAIRS, dtype=np.float64) / AX_PAIRS)
    ang = np.zeros((seq, HEAD_DIM))
    sign = np.zeros((HEAD_DIM,))
    for a, p_a in enumerate((pos // GRID_W, pos % GRID_W)):
        for t in range(2):
            lo = a * 2 * AX_PAIRS + t * AX_PAIRS
            ang[:, lo:lo + AX_PAIRS] = p_a[:, None] * freqs[None, :]
            sign[lo:lo + AX_PAIRS] = -1.0 if t == 0 else 1.0
    cos = np.tile(np.cos(ang), (1, 2)).astype(np.float32)
    sin = np.tile(np.sin(ang) * sign[None, :], (1, 2)).astype(np.float32)
    return jnp.asarray(cos), jnp.asarray(sin)


def _even_pre(x, mod, g, w_in, qg, kg, rope):
    bsz, seq, d = x.shape
    tm = ROW_TILE
    tok = lambda w: pl.BlockSpec((1, tm, w), lambda b, i: (b, i, 0))
    in_specs = [tok(d), _mod_spec(mod), _const_spec((1, d)), _const_spec((d, EVEN_IN)),
                _const_spec((1, Q_W)), _const_spec((1, KV_W))]
    args = [x, mod, g, w_in, qg, kg]
    if rope:
        cos, sin = _rope_tables(seq)
        in_specs += [pl.BlockSpec((tm, LANES), lambda b, i: (i, 0))] * 2
        args += [cos, sin]
    return pl.pallas_call(
        functools.partial(_even_pre_kernel, rope=rope),
        grid=(bsz, seq // tm),
        in_specs=in_specs,
        out_specs=[tok(FOURIER_W), tok(Q_W), tok(KV_W), tok(KV_W)],
        out_shape=[jax.ShapeDtypeStruct((bsz, seq, FOURIER_W), F32),
                   jax.ShapeDtypeStruct((bsz, seq, Q_W), BF16),
                   jax.ShapeDtypeStruct((bsz, seq, KV_W), F32),
                   jax.ShapeDtypeStruct((bsz, seq, KV_W), F32)],
        compiler_params=_params(("arbitrary", "arbitrary")),
        name="even_pre",
    )(*args)


def _attn_kernel(*refs, seq, has_ctx):
    if has_ctx:
        q_ref, k_ref, v_ref, ck_ref, cv_ref, f_ref, cs_ref, cbd_ref, o_ref, kop, vopt, g_scr, cs_scr = refs
    else:
        q_ref, k_ref, v_ref, f_ref, cs_ref, cbd_ref, o_ref, kop, vopt, g_scr, cs_scr = refs
    tq = q_ref.shape[1]

    @pl.when((pl.program_id(0) == 0) & (pl.program_id(1) == 0))
    def _round_tables():
        for r0 in range(0, seq, tq):
            cs_scr[r0:r0 + tq, :] = cs_ref[r0:r0 + tq, :].astype(BF16)

    @pl.when(pl.program_id(1) == 0)
    def _prepare():
        def put(kx, vx, r0):
            n = kx.shape[0]
            low = lax.broadcasted_iota(jnp.int32, (n, LANES), 1) < HEAD_DIM
            ones = jnp.ones((V_ROWS - HEAD_DIM, n), BF16)
            for jp in range(N_KV_HEADS // 2):
                kp = kx[:, jp * LANES:(jp + 1) * LANES]
                k_lo = jnp.where(low, kp, 0.0)
                k_hi = jnp.where(low, 0.0, kp)
                placed = (k_lo, pltpu.roll(k_lo, HEAD_DIM, 1), pltpu.roll(k_hi, HEAD_DIM, 1), k_hi)
                for t, kk in enumerate(placed):
                    kop[4 * jp + t, r0:r0 + n, :] = kk.astype(BF16)
                vt = vx[:, jp * LANES:(jp + 1) * LANES].T.astype(BF16)
                for t in range(2):
                    vopt[2 * jp + t, 0:HEAD_DIM, r0:r0 + n] = vt[t * HEAD_DIM:(t + 1) * HEAD_DIM]
                    vopt[2 * jp + t, HEAD_DIM:V_ROWS, r0:r0 + n] = ones

        put(k_ref[0], v_ref[0], 0)
        if has_ctx:
            put(ck_ref[0], cv_ref[0], seq)
        f = f_ref[0].astype(BF16)
        g_scr[0:seq, :] = _dot(f, cbd_ref[0].astype(BF16)).astype(BF16)
        g_scr[seq:2 * seq, :] = _dot(f, cbd_ref[1].astype(BF16)).astype(BF16)

    r0 = pl.multiple_of(pl.program_id(1) * tq, tq)
    o_ref[0, :, 0:FOURIER_W] = _dot(cs_scr[pl.ds(r0, tq), :], g_scr[...]).astype(BF16)
    q = q_ref[0]
    n_pairs = N_Q_HEADS // 2
    kv_of = lambda qp: [(2 * qp + half) // (N_Q_HEADS // N_KV_HEADS) for half in range(2)]

    def pair_scores(qp):
        qpair = q[:, qp * LANES:(qp + 1) * LANES]
        return [_dot_nt(qpair, kop[2 * kv_of(qp)[half] + half]) for half in range(2)]

    scores_next = pair_scores(0)
    for qp in range(n_pairs):
        kv = kv_of(qp)
        scores = scores_next
        if qp + 1 < n_pairs:
            scores_next = pair_scores(qp + 1)
        probs = [jnp.exp2(s - jnp.max(s, axis=-1, keepdims=True)).astype(BF16) for s in scores]
        outs = []
        for half in range(2):
            o = _dot_nt(vopt[kv[half]], probs[half])
            outs.append(o[0:HEAD_DIM] / o[HEAD_DIM:HEAD_DIM + 1])
        res = jnp.concatenate(outs, axis=0)
        o_ref[0, :, FOURIER_W + qp * LANES:FOURIER_W + (qp + 1) * LANES] = res.T.astype(BF16)


def _dft_tables(seq):
    s = np.arange(seq)
    ang = 2.0 * np.pi * ((s[:, None] * s[None, :]) % seq) / seq
    pos = np.concatenate([np.cos(ang), -np.sin(ang)], axis=1) / math.sqrt(seq)
    c = np.arange(FOURIER_GROUP_W)
    ang_c = 2.0 * np.pi * ((c[:, None] * c[None, :]) % FOURIER_GROUP_W) / FOURIER_GROUP_W
    eye = np.eye(FOURIER_W // FOURIER_GROUP_W)
    chan = np.stack([np.kron(eye, np.cos(ang_c)), np.kron(eye, np.sin(ang_c))]) / math.sqrt(FOURIER_GROUP_W)
    return jnp.asarray(pos, dtype=F32), jnp.asarray(chan, dtype=F32)


def _even_mixer(q, k, v, f, ctx):
    bsz, seq, _ = q.shape
    tq = ROW_TILE
    has_ctx = ctx is not None
    past = ctx[0].shape[1] if has_ctx else 0
    kt = seq + past
    pos_dft, chan_dft = _dft_tables(seq)
    full = lambda n, w: pl.BlockSpec((1, n, w), lambda b, i: (b, 0, 0))
    in_specs = [pl.BlockSpec((1, tq, Q_W), lambda b, i: (b, i, 0)), full(seq, KV_W), full(seq, KV_W)]
    args = [q, k, v]
    if has_ctx:
        in_specs += [full(past, KV_W), full(past, KV_W)]
        args += list(ctx)
    in_specs += [full(seq, FOURIER_W), _const_spec((seq, 2 * seq)),
                 _const_spec((2, FOURIER_W, FOURIER_W))]
    args += [f, pos_dft, chan_dft]
    return pl.pallas_call(
        functools.partial(_attn_kernel, seq=seq, has_ctx=has_ctx),
        grid=(bsz, seq // tq),
        in_specs=in_specs,
        out_specs=pl.BlockSpec((1, tq, D_MODEL), lambda b, i: (b, i, 0)),
        out_shape=jax.ShapeDtypeStruct((bsz, seq, D_MODEL), BF16),
        scratch_shapes=[pltpu.VMEM((2 * N_KV_HEADS, kt, LANES), BF16),
                        pltpu.VMEM((N_KV_HEADS, V_ROWS, kt), BF16),
                        pltpu.VMEM((2 * seq, FOURIER_W), BF16),
                        pltpu.VMEM((seq, 2 * seq), BF16)],
        compiler_params=_params(("arbitrary", "arbitrary")),
        name="even_mixer",
    )(*args)


def _post_ffn_kernel(*refs, odd):
    if odd:
        (x_ref, ylru_ref, g_ref, yssd_ref, z_ref, sn_ref, mod_ref, wout_ref, gf_ref,
         w1_ref, w3_ref, w2_ref, o_ref) = refs
        y_lru = ylru_ref[0] * _gelu_tanh(g_ref[0])
        y = yssd_ref[0] * _silu(z_ref[0])
        ms = jnp.mean(y * y, axis=-1, keepdims=True)
        y_ssd = y * lax.rsqrt(ms + EPS) * sn_ref[...]
        mix = jnp.concatenate([y_lru, y_ssd], axis=-1).astype(BF16)
    else:
        x_ref, mix_ref, mod_ref, wout_ref, gf_ref, w1_ref, w3_ref, w2_ref, o_ref = refs
        mix = mix_ref[0]
    x = x_ref[0] + mod_ref[0, 2:3, :] * _dot(mix, wout_ref[...])
    h = _modnorm(x, gf_ref[...], mod_ref[0, 3:4, :], mod_ref[0, 4:5, :]).astype(BF16)
    a = (_silu(_dot(h, w1_ref[...])) * _dot(h, w3_ref[...])).astype(BF16)
    o_ref[0] = x + mod_ref[0, 5:6, :] * _dot(a, w2_ref[...])


def _post_ffn(x, mix_args, mod, w_out, g_ffn, w1, w3, w2, layer, ssd_norm=None):
    bsz, seq, d = x.shape
    if mod.shape[0] == 1 and seq < TOKEN_BLOCK:
        fold = TOKEN_BLOCK // seq
        merge = lambda t: t.reshape(bsz // fold, fold * seq, t.shape[-1])
        out = _post_ffn(merge(x), jax.tree.map(merge, mix_args), mod, w_out, g_ffn, w1, w3, w2, layer, ssd_norm)
        return out.reshape(bsz, seq, d)
    tm = min(seq, TOKEN_BLOCK)
    odd = layer % 2 == 1
    tok = lambda w: pl.BlockSpec((1, tm, w), lambda b, i: (b, i, 0))
    slab = lambda w: pl.BlockSpec((None,) + w.shape[1:], lambda b, i: (layer, 0, 0), pipeline_mode=pl.Buffered(1))
    if odd:
        ylru, g, yssd, z = mix_args
        in_specs = [tok(d), tok(LRU_W), tok(LRU_W), tok(SSD_INNER), tok(SSD_INNER), _const_spec((1, SSD_INNER))]
        args = [x, ylru, g, yssd, z, ssd_norm]
    else:
        in_specs = [tok(d), tok(d)]
        args = [x, mix_args]
    in_specs += [_mod_spec(mod), _const_spec(w_out.shape), _const_spec((1, d)),
                 slab(w1), slab(w3), slab(w2)]
    args += [mod, w_out, g_ffn, w1, w3, w2]
    return pl.pallas_call(
        functools.partial(_post_ffn_kernel, odd=odd),
        grid=(bsz, seq // tm),
        in_specs=in_specs,
        out_specs=tok(d),
        out_shape=jax.ShapeDtypeStruct((bsz, seq, d), F32),
        compiler_params=_params(("arbitrary", "arbitrary")),
        name="post_ffn_odd" if odd else "post_ffn_even",
    )(*args)


XL_LO = LRU_W
Z_LO = 2 * LRU_W
XBC_LO = 2 * LRU_W + SSD_INNER
DT_LO = XBC_LO + SSD_CONV_CH


def _conv_block(main, edges, w, b):
    before, after0, after1 = edges
    rows = main.shape[0]
    row = lax.broadcasted_iota(jnp.int32, (SUBLANES, 1), 0)
    last = rows - SUBLANES

    def patch_first(x, fix):
        return jnp.concatenate([fix(x[:SUBLANES]), x[SUBLANES:]], axis=0)

    def patch_last(x, fix):
        return jnp.concatenate([x[:last], fix(x[last:])], axis=0)

    xm1 = patch_first(pltpu.roll(main, 1, 0), lambda t: jnp.where(row == 0, before, t))
    xp1 = patch_last(pltpu.roll(main, rows - 1, 0), lambda t: jnp.where(row == SUBLANES - 1, after0, t))
    xp2 = patch_last(pltpu.roll(main, rows - 2, 0),
                     lambda t: jnp.where(row == SUBLANES - 2, after0, jnp.where(row == SUBLANES - 1, after1, t)))
    return xm1 * w[0:1] + main * w[1:2] + xp1 * w[2:3] + xp2 * w[3:4] + b


def _odd_pre_kernel(x_ref, xprev_ref, xnext_ref, mod_ref, g_ref, w_ref, wdt_ref, cwl_ref, cbl_ref, wg_ref, bg_ref,
                    lam_ref, gate_ref, z_ref, a_ref, b_ref, xbc_ref, dt_ref):
    i = pl.program_id(1)
    g, shift, scale = g_ref[...], mod_ref[0, 0:1, :], mod_ref[0, 1:2, :]
    h = _modnorm(x_ref[0], g, shift, scale).astype(BF16)
    wide = 2 * LANES
    proj = lambda lo: _dot(h, w_ref[:, lo:lo + wide])
    p_xl = [proj(XL_LO + t * wide) for t in range(LRU_W // wide)]
    others = ([(gate_ref, t * wide, t * wide) for t in range(LRU_W // wide)]
              + [(z_ref, Z_LO + t * wide, t * wide) for t in range(SSD_INNER // wide)]
              + [(xbc_ref, XBC_LO + t * wide, t * wide) for t in range(SSD_CONV_CH // wide)])

    halo = jnp.concatenate([xprev_ref[0], xnext_ref[0]], axis=0)
    ph = _dot(_modnorm(halo, g, shift, scale).astype(BF16), w_ref[:, XL_LO:Z_LO])
    has_prev = i > 0
    has_next = i < pl.num_programs(1) - 1
    e_l = (jnp.where(has_prev, ph[SUBLANES - 1:SUBLANES], 0.0),
           jnp.where(has_next, ph[SUBLANES:SUBLANES + 1], 0.0),
           jnp.where(has_next, ph[SUBLANES + 1:SUBLANES + 2], 0.0))

    rate = (0.5 * LRU_C) * _softplus(-lam_ref[...])
    n_lru = LRU_W // LANES
    for lb in range(n_lru):
        sl = slice(lb * LANES, (lb + 1) * LANES)
        for dst, lo, col in others[lb * len(others) // n_lru:(lb + 1) * len(others) // n_lru]:
            dst[0, :, col:col + wide] = proj(lo)
        xb = _conv_block(p_xl[lb // 2][:, (lb % 2) * LANES:(lb % 2 + 1) * LANES], [e[:, sl] for e in e_l],
                         cwl_ref[:, sl], cbl_ref[:, sl])
        xh = 0.5 * xb
        gt = jnp.tanh(_dot(xb.astype(BF16), wg_ref[lb]) + bg_ref[lb])
        for d in range(2):
            neg_log_a = rate[d:d + 1, sl] * (gt[:, 2 * d * LANES:(2 * d + 1) * LANES] + 1.0)
            a = jnp.exp2(neg_log_a * (-math.log2(math.e)))
            g2 = jnp.tanh(neg_log_a) * (a * a + 1.0)
            root = jnp.where(g2 > 0.0, g2 * lax.rsqrt(g2), 0.0)
            a_ref[0, d, :, sl] = a
            b_ref[0, d, :, sl] = root * ((gt[:, (2 * d + 1) * LANES:(2 * d + 2) * LANES] + 1.0) * xh)
    dt_ref[0] = _dot(h, wdt_ref[...])


def _odd_pre(x, mod, g, w_in, w_dt, lru):
    bsz, seq, d = x.shape
    tm = min(seq, TOKEN_BLOCK)
    per8 = tm // SUBLANES
    tok = lambda w: pl.BlockSpec((1, tm, w), lambda b, i: (b, i, 0))
    both = pl.BlockSpec((1, 2, tm, LRU_W), lambda b, i: (b, 0, i, 0))
    consts = list(lru)
    out_specs = [tok(LRU_W), tok(SSD_INNER), both, both, tok(SSD_CONV_CH), tok(LANES)]
    row = lambda w: jax.ShapeDtypeStruct((bsz, seq, w), F32)
    out_shape = ([row(LRU_W), row(SSD_INNER)] + [jax.ShapeDtypeStruct((bsz, 2, seq, LRU_W), F32)] * 2
                 + [row(SSD_CONV_CH), row(LANES)])
    return pl.pallas_call(
        _odd_pre_kernel,
        grid=(bsz, seq // tm),
        in_specs=[tok(d),
                  pl.BlockSpec((1, SUBLANES, d), lambda b, i: (b, jnp.maximum(i * per8 - 1, 0), 0)),
                  pl.BlockSpec((1, SUBLANES, d), lambda b, i: (b, jnp.minimum((i + 1) * per8, seq // SUBLANES - 1), 0)),
                  _mod_spec(mod), _const_spec((1, d)), _const_spec(w_in.shape), _const_spec(w_dt.shape)]
                 + [_const_spec(c.shape) for c in consts],
        out_specs=out_specs,
        out_shape=out_shape,
        compiler_params=_params(("arbitrary", "arbitrary")),
        name="odd_pre",
    )(x, x, x, mod, g, w_in, w_dt, *consts)


def _ssd_kernel(*refs, seq, has_h0):
    xbc_ref, dt_ref, cw_ref, cb_ref, dtb_ref, alog_ref, dexp_ref, ef_ref, eb_ref, la_ref, lb_ref = refs[:11]
    if has_h0:
        h0_ref, lh0_ref, y_ref, ylru_ref = refs[11:15]
    else:
        y_ref, ylru_ref, hout_ref, lhout_ref = refs[11:15]
    xs_ref, bm_ref, cm_ref, cs_ref, dts_ref, cst_ref, dtt_ref, bt_ref, st_scr = refs[15:]
    q = SSD_Q
    nc = seq // q
    lane = lax.broadcasted_iota(jnp.int32, (q, LANES), 1)
    qi = lax.broadcasted_iota(jnp.int32, (q, q), 0)
    ki = lax.broadcasted_iota(jnp.int32, (q, q), 1)
    tri_lower = jnp.where(ki <= qi, 1.0, 0.0).astype(BF16)
    tri_upper = jnp.where(ki >= qi, 1.0, 0.0).astype(BF16)
    a_log2 = jnp.where(lane[0:1, :] < 2 * SSD_HEADS, -math.log2(math.e) * jnp.exp(alog_ref[...]), 0.0)

    def prepare(c):
        r0 = pl.multiple_of(c * q, q)
        rows = pl.ds(r0, q)
        prev8 = pl.ds(pl.multiple_of(jnp.maximum(r0 - SUBLANES, 0), SUBLANES), SUBLANES)
        next8 = pl.ds(pl.multiple_of(jnp.minimum(r0 + q, seq - SUBLANES), SUBLANES), SUBLANES)
        has_prev = r0 > 0
        has_next = r0 + q < seq
        for lb in range(SSD_CONV_CH // LANES):
            sl = slice(lb * LANES, (lb + 1) * LANES)
            before = xbc_ref[0, prev8, sl]
            after = xbc_ref[0, next8, sl]
            edges = (jnp.where(has_prev, before[SUBLANES - 1:SUBLANES], 0.0),
                     jnp.where(has_next, after[0:1], 0.0), jnp.where(has_next, after[1:2], 0.0))
            xc = _conv_block(xbc_ref[0, rows, sl], edges, cw_ref[:, sl], cb_ref[:, sl])
            half_xc = 0.5 * xc
            xc = half_xc * (jnp.tanh(half_xc) + 1.0)
            if lb < SSD_INNER // LANES:
                xs_ref[rows, sl] = xc
                y_ref[0, rows, sl] = dexp_ref[:, sl] * xc
            elif lb == SSD_INNER // LANES:
                bm_ref[rows, :] = xc
                bt_ref[c] = xc.T
            else:
                cm_ref[rows, :] = xc
        dts = _softplus(dt_ref[0, rows, :] + dtb_ref[...])
        da = dts * a_log2
        cs = jnp.where(lane < SSD_HEADS, _dot3(tri_lower, da), _dot3(tri_upper, da))
        cs_ref[rows, :] = cs
        dts_ref[rows, :] = dts
        cst_ref[c] = cs.T
        dtt_ref[c] = dts.T

    st_scr[...] = h0_ref[0] if has_h0 else jnp.zeros(st_scr.shape, F32)

    low = lane < SSD_STATE
    col = lax.broadcasted_iota(jnp.int32, (SSD_STATE, SSD_INNER), 1)
    first_group = col < SSD_INNER // 2

    def chunk(c, fwd):
        d = 0 if fwd else 1
        r0 = pl.multiple_of(c * q, q)
        rows = pl.ds(r0, q)
        cs = cs_ref[rows, :]
        dts = dts_ref[rows, :]
        cst = cst_ref[c]
        dtt = dtt_ref[c]
        xs = xs_ref[rows, :]
        xsb = xs.astype(BF16)
        bb = bm_ref[rows, :].astype(BF16)
        cb_ = cm_ref[rows, :].astype(BF16)
        zero = jnp.zeros_like(cb_)
        cbg = [_dot_nt(jnp.where(low, cb_, zero), bb).astype(BF16),
               _dot_nt(jnp.where(low, zero, cb_), bb).astype(BF16)]
        mask = (ki <= qi) if fwd else (ki >= qi)
        expand = ef_ref[...] if fwd else eb_ref[...]
        st = st_scr[d]
        hh = jnp.concatenate([jnp.where(first_group, st, 0.0), jnp.where(first_group, 0.0, st)], axis=0)
        tot = cs[q - 1:q, :] if fwd else cs[0:1, :]
        mine = (lane >= d * SSD_HEADS) & (lane < (d + 1) * SSD_HEADS)
        w = jnp.exp2(jnp.where(mine, tot - cs, 0.0)) * dts
        dec = jnp.broadcast_to(jnp.exp2(tot), (2 * SUBLANES, LANES))
        dec_hi = dec.astype(BF16)
        dec_lo = (dec - dec_hi.astype(F32)).astype(BF16)
        spread = _dot(jnp.concatenate([jnp.exp2(cs).astype(BF16), w.astype(BF16), dec_hi, dec_lo], axis=0), expand)
        y = _dot(cb_, hh.astype(BF16)) * spread[0:q]
        pairs = []
        for pp in range(SSD_HEADS // 2):
            ms = []
            for h in (2 * pp, 2 * pp + 1):
                l = d * SSD_HEADS + h
                decay = jnp.exp2((cs[:, l:l + 1] - cst[l:l + 1, :]).astype(BF16))
                decay = jnp.where(mask, decay, jnp.zeros_like(decay))
                ms.append(cbg[pp // (SSD_HEADS // 4)] * decay * dtt[l:l + 1, :].astype(BF16))
            xp = xsb[:, pp * LANES:(pp + 1) * LANES]
            zx = jnp.zeros_like(xp)
            x2 = jnp.concatenate([jnp.where(low, xp, zx), jnp.where(low, zx, xp)], axis=0)
            pairs.append(_dot(jnp.concatenate(ms, axis=1), x2))
        y_ref[0, rows, :] += y + jnp.concatenate(pairs, axis=1)
        xw = (xs * spread[q:2 * q]).astype(BF16)
        bt = bt_ref[c].astype(BF16)
        half_w = SSD_INNER // 2
        st_new = jnp.concatenate([_dot(bt[:SSD_STATE], xw[:, :half_w]), _dot(bt[SSD_STATE:], xw[:, half_w:])], axis=1)
        chunk_decay = spread[2 * q:2 * q + 1] + spread[2 * q + 2 * SUBLANES:2 * q + 2 * SUBLANES + 1]
        st_scr[d] = chunk_decay * st + st_new

    def prepare_all(c, carry):
        prepare(c)
        return carry

    nb = seq // SUBLANES
    per_chunk = q // SUBLANES
    row = lax.broadcasted_iota(jnp.int32, (SUBLANES, LRU_W), 0)

    def lru_block(blk, hf, hb):
        rf = pl.multiple_of(blk * SUBLANES, SUBLANES)
        rb = pl.multiple_of((nb - 1 - blk) * SUBLANES, SUBLANES)
        a = la_ref[0, 0, pl.ds(rf, SUBLANES), :]
        b = lb_ref[0, 0, pl.ds(rf, SUBLANES), :]
        for k in (1, 2, 4):
            keep = row >= k
            b = a * jnp.where(keep, pltpu.roll(b, k, 0), 0.0) + b
            a = a * jnp.where(keep, pltpu.roll(a, k, 0), 1.0)
        h = a * hf + b
        ylru_ref[0, pl.ds(rf, SUBLANES), :] += h
        hf = h[SUBLANES - 1:SUBLANES, :]
        a = la_ref[0, 1, pl.ds(rb, SUBLANES), :]
        b = lb_ref[0, 1, pl.ds(rb, SUBLANES), :]
        for k in (1, 2, 4):
            keep = row < SUBLANES - k
            b = a * jnp.where(keep, pltpu.roll(b, SUBLANES - k, 0), 0.0) + b
            a = a * jnp.where(keep, pltpu.roll(a, SUBLANES - k, 0), 1.0)
        h = a * hb + b
        ylru_ref[0, pl.ds(rb, SUBLANES), :] += h
        return hf, h[0:1, :]

    def both_directions(i, carry):
        chunk(i, True)
        chunk(nc - 1 - i, False)
        hf, hb = carry
        for j in range(per_chunk):
            hf, hb = lru_block(i * per_chunk + j, hf, hb)
        return hf, hb

    ylru_ref[0] = jnp.zeros((seq, LRU_W), F32)
    lax.fori_loop(0, nc, prepare_all, 0)
    if has_h0:
        lru_start = (lh0_ref[0, 0:1, :], lh0_ref[0, 1:2, :])
    else:
        lru_start = (jnp.zeros((1, LRU_W), F32), jnp.zeros((1, LRU_W), F32))
    hf, hb = lax.fori_loop(0, nc, both_directions, lru_start)
    if not has_h0:
        hout_ref[0] = st_scr[...]
        lhout_ref[0, 0:1, :] = hf
        lhout_ref[0, 1:2, :] = hb


def _ssd(xbc, dt, cw, cb, dtb, alog, dexp, lru_a, lru_b, h0, lru_h0):
    bsz, seq, _ = xbc.shape
    nc = seq // SSD_Q
    expand = np.zeros((2, LANES, SSD_INNER), np.float32)
    for d in range(2):
        for h in range(SSD_HEADS):
            expand[d, d * SSD_HEADS + h, h * 64:(h + 1) * 64] = 1.0
    expand = jnp.asarray(expand, dtype=BF16)
    per_seq = lambda n, w: pl.BlockSpec((1, n, w), lambda b: (b, 0, 0))
    state_spec = pl.BlockSpec((1, 2, SSD_STATE, SSD_INNER), lambda b: (b, 0, 0, 0))
    has_h0 = h0 is not None
    coef = pl.BlockSpec((1, 2, seq, LRU_W), lambda b: (b, 0, 0, 0))
    in_specs = [per_seq(seq, SSD_CONV_CH), per_seq(seq, LANES), _const_spec(cw.shape), _const_spec(cb.shape),
                _const_spec(dtb.shape), _const_spec(alog.shape), _const_spec(dexp.shape),
                _const_spec((LANES, SSD_INNER)), _const_spec((LANES, SSD_INNER)), coef, coef]
    args = [xbc, dt, cw, cb, dtb, alog, dexp, expand[0], expand[1], lru_a, lru_b]
    out_specs = [per_seq(seq, SSD_INNER), per_seq(seq, LRU_W)]
    out_shape = [jax.ShapeDtypeStruct((bsz, seq, SSD_INNER), F32), jax.ShapeDtypeStruct((bsz, seq, LRU_W), F32)]
    if has_h0:
        in_specs += [state_spec, per_seq(2, LRU_W)]
        args += [h0, lru_h0]
    else:
        out_specs += [state_spec, per_seq(2, LRU_W)]
        out_shape += [jax.ShapeDtypeStruct((bsz, 2, SSD_STATE, SSD_INNER), F32),
                      jax.ShapeDtypeStruct((bsz, 2, LRU_W), F32)]
    outs = pl.pallas_call(
        functools.partial(_ssd_kernel, seq=seq, has_h0=has_h0),
        grid=(bsz,),
        in_specs=in_specs,
        out_specs=out_specs,
        out_shape=out_shape,
        scratch_shapes=[pltpu.VMEM((seq, SSD_INNER), F32)] + [pltpu.VMEM((seq, LANES), F32)] * 4
                       + [pltpu.VMEM((nc, LANES, SSD_Q), F32)] * 3 + [pltpu.VMEM((2, SSD_STATE, SSD_INNER), F32)],
        compiler_params=_params(("arbitrary",)),
        name="ssd",
    )(*args)
    return (outs[0], outs[1], None, None) if has_h0 else tuple(outs)


def _lru_gate_weights(wa, ba, wx, bx):
    eye = 0.5 * jnp.eye(2, dtype=F32)

    def pair_blocks(w):
        w4 = w.reshape(4, 2, 64, 64)
        return jnp.einsum("laij,ab->laibj", w4, eye).reshape(4, LANES, LANES)

    mats = [pair_blocks(w) for w in (wa[0], wx[0], wa[1], wx[1])]
    wg = jnp.concatenate(mats, axis=-1).astype(BF16)
    bias = [0.5 * b.reshape(4, 1, LANES) for b in (ba[0], bx[0], ba[1], bx[1])]
    return wg, jnp.concatenate(bias, axis=-1)


def _trunk(x, mods, p, ctx):
    is_ctx = ctx is None
    bsz, seq, _ = x.shape
    f, q, k, v = _even_pre(x, mods[0], p["norm_mix"][0], p["w_in_even"], p["q_gain"], p["k_gain"], rope=not is_ctx)
    mix = _even_mixer(q, k, v, f, None if is_ctx else (ctx["k"], ctx["v"]))
    x = _post_ffn(x, mix, mods[0], p["w_out_even"], p["norm_ffn"][0], p["w1"], p["w3"], p["w2"], layer=0)
    gate, z, lru_a, lru_b, xbc, dt = _odd_pre(
        x, mods[1], p["norm_mix"][1], p["w_in_odd"], p["w_dt_odd"],
        lru=(p["conv_lru_w"], p["conv_lru_b"], p["lru_wg"], p["lru_bg"], p["lru_lambda"]))
    h0_lru, h0_ssd = (None, None) if is_ctx else (ctx["lru"], ctx["ssd"])
    yssd, ylru, ssd_state, lru_state = _ssd(xbc, dt, p["conv_ssd_w"], p["conv_ssd_b"], p["ssd_dtb"], p["ssd_alog"],
                                            p["ssd_dexp"], lru_a, lru_b, h0_ssd, h0_lru)
    x = _post_ffn(x, (ylru, gate, yssd, z), mods[1], p["w_out_odd"], p["norm_ffn"][1], p["w1"], p["w3"], p["w2"],
                  layer=1, ssd_norm=p["ssd_norm"])
    return x, k, v, lru_state, ssd_state


def kernel(x_prompt, x_sample, c, cache_k, cache_v, state_lru, state_ssd, c_ctx, w_ada, b_ada, norm_mix, norm_ffn,
           w_in_even, q_norm, k_norm, w_out_even, w_in_odd, conv_lru_w, conv_lru_b, lru_wa, lru_ba, lru_wx, lru_bx,
           lru_lambda, conv_ssd_w, conv_ssd_b, ssd_dt_bias, ssd_a_log, ssd_d, ssd_norm, w_out_odd, ffn_w1, ffn_w3,
           ffn_w2):
    bsz, seq, d = x_prompt.shape
    dec_b = x_sample.shape[0]
    past = cache_k.shape[2]

    cond_rows = 16
    cond = jnp.concatenate([c, c_ctx[None, :], jnp.zeros((cond_rows - dec_b - 1, d), F32)], axis=0)
    mod = _modulation(cond, w_ada, b_ada).reshape(2, cond_rows, 6, d)
    mods_sample = [mod[l, :dec_b] for l in range(2)]
    mods_prompt = [mod[l, dec_b:dec_b + 1] for l in range(2)]

    lru_wg, lru_bg = _lru_gate_weights(lru_wa[0], lru_ba[0], lru_wx[0], lru_bx[0])
    pad32 = lambda a: jnp.pad(a.reshape(1, 2 * SSD_HEADS), ((0, 0), (0, LANES - 2 * SSD_HEADS)))
    p = dict(
        norm_mix=norm_mix.reshape(2, 1, d), norm_ffn=norm_ffn.reshape(2, 1, d),
        w_in_even=w_in_even[0].astype(BF16),
        q_gain=jnp.tile(q_norm[0], N_Q_HEADS)[None, :], k_gain=jnp.tile(k_norm[0], N_KV_HEADS)[None, :],
        w_out_even=w_out_even[0].astype(BF16),
        w_in_odd=w_in_odd[0].astype(BF16),
        w_dt_odd=jnp.pad(w_in_odd[0, :, DT_LO:], ((0, 0), (0, LANES - 2 * SSD_HEADS))).astype(BF16),
        conv_lru_w=conv_lru_w[0], conv_lru_b=conv_lru_b[0][None, :],
        lru_wg=lru_wg, lru_bg=lru_bg, lru_lambda=lru_lambda[0],
        conv_ssd_w=conv_ssd_w[0], conv_ssd_b=conv_ssd_b[0][None, :],
        ssd_dtb=pad32(ssd_dt_bias[0]), ssd_alog=pad32(ssd_a_log[0]),
        ssd_dexp=jnp.repeat(ssd_d[0], SSD_INNER // SSD_HEADS)[None, :],
        ssd_norm=ssd_norm[0][None, :],
        w_out_odd=w_out_odd[0].astype(BF16),
        w1=ffn_w1.astype(BF16), w3=ffn_w3.astype(BF16), w2=ffn_w2.astype(BF16),
    )

    y_prompt, k_new, v_new, lru_new, ssd_new = _trunk(x_prompt, mods_prompt, p, None)
    ctx = dict(
        k=cache_k[:, 0].reshape(dec_b, past, KV_W), v=cache_v[:, 0].reshape(dec_b, past, KV_W),
        lru=state_lru[:, 0],
        ssd=state_ssd[:, 0].transpose(0, 1, 4, 2, 3).reshape(dec_b, 2, SSD_STATE, SSD_INNER),
    )
    y_sample, _, _, _, _ = _trunk(x_sample, mods_sample, p, ctx)

    new_k = k_new.reshape(bsz, 1, seq, N_KV_HEADS, HEAD_DIM)
    new_v = v_new.reshape(bsz, 1, seq, N_KV_HEADS, HEAD_DIM)
    new_lru = lru_new.reshape(bsz, 1, 2, LRU_W)
    new_ssd = (ssd_new.reshape(bsz, 2, SSD_STATE, SSD_HEADS, SSD_INNER // SSD_HEADS)
               .transpose(0, 1, 3, 4, 2).reshape(bsz, 1, 2, SSD_HEADS, SSD_INNER // SSD_HEADS, SSD_STATE))
    return (y_prompt, y_sample, new_k, new_v, new_lru, new_ssd)
```

```python
import functools
import math

import numpy as np
import jax
import jax.numpy as jnp
from jax import lax
from jax.experimental import pallas as pl
from jax.experimental.pallas import tpu as pltpu

F32 = jnp.float32
BF16 = jnp.bfloat16

D_MODEL = 1024
EPS = 1e-6
GRID_W = 64
HEAD_DIM = 64
N_Q_HEADS = 12
N_KV_HEADS = 4
FOURIER_W = 256
FOURIER_GROUP_W = 64
Q_W = N_Q_HEADS * HEAD_DIM
KV_W = N_KV_HEADS * HEAD_DIM
EVEN_IN = FOURIER_W + Q_W + 2 * KV_W
ROPE_THETA = 10000.0
AX_PAIRS = HEAD_DIM // 4
LRU_W = 512
LRU_C = 8.0
SSD_INNER = 1024
SSD_HEADS = 16
SSD_STATE = 64
SSD_CONV_CH = SSD_INNER + 4 * SSD_STATE
D_FF = 2816
LANES = 128
SUBLANES = 8
TOKEN_BLOCK = 512
ROW_TILE = 256
SSD_Q = 128
V_ROWS = HEAD_DIM + 16
VMEM_LIMIT = 56 * 1024 * 1024


def _params(sem, vmem=VMEM_LIMIT):
    return pltpu.CompilerParams(dimension_semantics=sem, vmem_limit_bytes=vmem)


def _const_spec(shape):
    nd = len(shape)
    return pl.BlockSpec(shape, lambda *_: (0,) * nd, pipeline_mode=pl.Buffered(1))


def _dot(a, b):
    return jnp.dot(a, b, preferred_element_type=F32)


def _dot_nt(a, b):
    return lax.dot_general(a, b, (((1,), (1,)), ((), ())), preferred_element_type=F32)


def _dot3(m, a):
    hi = a.astype(BF16)
    r1 = a - hi.astype(F32)
    mid = r1.astype(BF16)
    lo = (r1 - mid.astype(F32)).astype(BF16)
    return _dot(m, hi) + _dot(m, mid) + _dot(m, lo)


def _silu(x):
    return x * jax.nn.sigmoid(x)


def _softplus(x):
    return jnp.maximum(x, 0.0) + jnp.log1p(jnp.exp(-jnp.abs(x)))


def _gelu_tanh(x):
    c = math.sqrt(2.0 / math.pi)
    return 0.5 * x * (1.0 + jnp.tanh(c * (x + 0.044715 * (x * x * x))))


def _modnorm(x, g, shift, scale):
    ms = jnp.mean(x * x, axis=-1, keepdims=True)
    y = x * lax.rsqrt(ms + EPS) * g
    return y * (1.0 + scale) + shift


def _mod_kernel(cond_ref, w_ref, b_ref, o_ref):
    s = _silu(cond_ref[...])
    o_ref[0] = _dot(s.astype(BF16), w_ref[0].astype(BF16)) + b_ref[0]


def _modulation(cond, w_ada, b_ada):
    depth, d, n = w_ada.shape
    rows = cond.shape[0]
    tn = 3072
    return pl.pallas_call(
        _mod_kernel,
        grid=(depth, n // tn),
        in_specs=[pl.BlockSpec((rows, d), lambda l, j: (0, 0)),
                  pl.BlockSpec((1, d, tn), lambda l, j: (l, 0, j)),
                  pl.BlockSpec((1, 1, tn), lambda l, j: (l, 0, j))],
        out_specs=pl.BlockSpec((1, rows, tn), lambda l, j: (l, 0, j)),
        out_shape=jax.ShapeDtypeStruct((depth, rows, n), F32),
        compiler_params=_params(("arbitrary", "arbitrary")),
        name="modulation",
    )(cond, w_ada, b_ada.reshape(depth, 1, n))


def _mod_spec(mod):
    if mod.shape[0] == 1:
        return pl.BlockSpec((1, 6, D_MODEL), lambda b, i: (0, 0, 0))
    return pl.BlockSpec((1, 6, D_MODEL), lambda b, i: (b, 0, 0))


def _head_rmsnorm(x, gain):
    rows, width = x.shape
    low = lax.broadcasted_iota(jnp.int32, (rows, LANES), 1) < HEAD_DIM
    out = []
    for b in range(width // LANES):
        blk = x[:, b * LANES:(b + 1) * LANES]
        sq = blk * blk
        lo = jnp.sum(jnp.where(low, sq, 0.0), axis=-1, keepdims=True)
        hi = jnp.sum(jnp.where(low, 0.0, sq), axis=-1, keepdims=True)
        ms = jnp.where(low, lo, hi) * (1.0 / HEAD_DIM)
        out.append(blk * lax.rsqrt(ms + EPS) * gain[:, b * LANES:(b + 1) * LANES])
    return jnp.concatenate(out, axis=-1)


def _rope(x, cos, sin_signed):
    rows, width = x.shape
    lane = lax.broadcasted_iota(jnp.int32, (rows, LANES), 1)
    second = (lane % (2 * AX_PAIRS)) >= AX_PAIRS
    out = []
    for b in range(width // LANES):
        blk = x[:, b * LANES:(b + 1) * LANES]
        partner = jnp.where(second, pltpu.roll(blk, AX_PAIRS, 1), pltpu.roll(blk, LANES - AX_PAIRS, 1))
        out.append(blk * cos + partner * sin_signed)
    return jnp.concatenate(out, axis=-1)


def _even_pre_kernel(*refs, rope):
    if rope:
        x_ref, mod_ref, g_ref, w_ref, qg_ref, kg_ref, cos_ref, sin_ref, f_ref, q_ref, k_ref, v_ref = refs
    else:
        x_ref, mod_ref, g_ref, w_ref, qg_ref, kg_ref, f_ref, q_ref, k_ref, v_ref = refs
    h = _modnorm(x_ref[0], g_ref[...], mod_ref[0, 0:1, :], mod_ref[0, 1:2, :])
    p = _dot(h.astype(BF16), w_ref[...])
    f_ref[0] = p[:, :FOURIER_W]
    q = _head_rmsnorm(p[:, FOURIER_W:FOURIER_W + Q_W], qg_ref[...])
    k = _head_rmsnorm(p[:, FOURIER_W + Q_W:FOURIER_W + Q_W + KV_W], kg_ref[...])
    if rope:
        q = _rope(q, cos_ref[...], sin_ref[...])
        k = _rope(k, cos_ref[...], sin_ref[...])
    q_ref[0] = (q * (HEAD_DIM ** -0.5 * math.log2(math.e))).astype(BF16)
    k_ref[0] = k
    v_ref[0] = p[:, FOURIER_W + Q_W + KV_W:]


def _rope_tables(seq):
    pos = np.arange(seq)
    freqs = ROPE_THETA ** (-np.arange(AX_PAIRS, dtype=np.float64) / AX_PAIRS)
    ang = np.zeros((seq, HEAD_DIM))
    sign = np.zeros((HEAD_DIM,))
    for a, p_a in enumerate((pos // GRID_W, pos % GRID_W)):
        for t in range(2):
            lo = a * 2 * AX_PAIRS + t * AX_PAIRS
            ang[:, lo:lo + AX_PAIRS] = p_a[:, None] * freqs[None, :]
            sign[lo:lo + AX_PAIRS] = -1.0 if t == 0 else 1.0
    cos = np.tile(np.cos(ang), (1, 2)).astype(np.float32)
    sin = np.tile(np.sin(ang) * sign[None, :], (1, 2)).astype(np.float32)
    return jnp.asarray(cos), jnp.asarray(sin)


def _even_pre(x, mod, g, w_in, qg, kg, rope):
    bsz, seq, d = x.shape
    if mod.shape[0] == 1 and not rope and seq < TOKEN_BLOCK:
        fold = TOKEN_BLOCK // seq
        outs = _even_pre(x.reshape(bsz // fold, fold * seq, d), mod, g, w_in, qg, kg, rope)
        return [o.reshape(bsz, seq, o.shape[-1]) for o in outs]
    tm = ROW_TILE if rope else min(seq, TOKEN_BLOCK)
    tok = lambda w: pl.BlockSpec((1, tm, w), lambda b, i: (b, i, 0))
    in_specs = [tok(d), _mod_spec(mod), _const_spec((1, d)), _const_spec((d, EVEN_IN)),
                _const_spec((1, Q_W)), _const_spec((1, KV_W))]
    args = [x, mod, g, w_in, qg, kg]
    if rope:
        cos, sin = _rope_tables(seq)
        in_specs += [pl.BlockSpec((tm, LANES), lambda b, i: (i, 0))] * 2
        args += [cos, sin]
    return pl.pallas_call(
        functools.partial(_even_pre_kernel, rope=rope),
        grid=(bsz, seq // tm),
        in_specs=in_specs,
        out_specs=[tok(FOURIER_W), tok(Q_W), tok(KV_W), tok(KV_W)],
        out_shape=[jax.ShapeDtypeStruct((bsz, seq, FOURIER_W), F32),
                   jax.ShapeDtypeStruct((bsz, seq, Q_W), BF16),
                   jax.ShapeDtypeStruct((bsz, seq, KV_W), F32),
                   jax.ShapeDtypeStruct((bsz, seq, KV_W), F32)],
        compiler_params=_params(("arbitrary", "arbitrary")),
        name="even_pre",
    )(*args)


def _attn_kernel(*refs, seq, has_ctx):
    if has_ctx:
        q_ref, k_ref, v_ref, ck_ref, cv_ref, f_ref, cs_ref, cbd_ref, o_ref, kop, vopt, g_scr, cs_scr = refs
    else:
        q_ref, k_ref, v_ref, f_ref, cs_ref, cbd_ref, o_ref, kop, vopt, g_scr, cs_scr = refs
    tq = q_ref.shape[1]

    @pl.when((pl.program_id(0) == 0) & (pl.program_id(1) == 0))
    def _round_tables():
        for r0 in range(0, seq, tq):
            cs_scr[r0:r0 + tq, :] = cs_ref[r0:r0 + tq, :].astype(BF16)

    @pl.when(pl.program_id(1) == 0)
    def _prepare():
        def put(kx, vx, r0):
            n = kx.shape[0]
            low = lax.broadcasted_iota(jnp.int32, (n, LANES), 1) < HEAD_DIM
            ones = jnp.ones((V_ROWS - HEAD_DIM, n), BF16)
            for jp in range(N_KV_HEADS // 2):
                kp = kx[:, jp * LANES:(jp + 1) * LANES]
                k_lo = jnp.where(low, kp, 0.0)
                k_hi = jnp.where(low, 0.0, kp)
                placed = (k_lo, pltpu.roll(k_lo, HEAD_DIM, 1), pltpu.roll(k_hi, HEAD_DIM, 1), k_hi)
                for t, kk in enumerate(placed):
                    kop[4 * jp + t, r0:r0 + n, :] = kk.astype(BF16)
                vt = vx[:, jp * LANES:(jp + 1) * LANES].T.astype(BF16)
                for t in range(2):
                    vopt[2 * jp + t, 0:HEAD_DIM, r0:r0 + n] = vt[t * HEAD_DIM:(t + 1) * HEAD_DIM]
                    vopt[2 * jp + t, HEAD_DIM:V_ROWS, r0:r0 + n] = ones

        put(k_ref[0], v_ref[0], 0)
        if has_ctx:
            put(ck_ref[0], cv_ref[0], seq)
        f = f_ref[0].astype(BF16)
        g_scr[0:seq, :] = _dot(f, cbd_ref[0].astype(BF16)).astype(BF16)
        g_scr[seq:2 * seq, :] = _dot(f, cbd_ref[1].astype(BF16)).astype(BF16)

    r0 = pl.multiple_of(pl.program_id(1) * tq, tq)
    o_ref[0, :, 0:FOURIER_W] = _dot(cs_scr[pl.ds(r0, tq), :], g_scr[...]).astype(BF16)
    q = q_ref[0]
    n_pairs = N_Q_HEADS // 2
    kv_of = lambda qp: [(2 * qp + half) // (N_Q_HEADS // N_KV_HEADS) for half in range(2)]

    def pair_scores(qp):
        qpair = q[:, qp * LANES:(qp + 1) * LANES]
        return [_dot_nt(qpair, kop[2 * kv_of(qp)[half] + half]) for half in range(2)]

    scores_next = pair_scores(0)
    for qp in range(n_pairs):
        kv = kv_of(qp)
        scores = scores_next
        if qp + 1 < n_pairs:
            scores_next = pair_scores(qp + 1)
        probs = [jnp.exp2(s - jnp.max(s, axis=-1, keepdims=True)).astype(BF16) for s in scores]
        outs = []
        for half in range(2):
            o = _dot_nt(vopt[kv[half]], probs[half])
            outs.append(o[0:HEAD_DIM] / o[HEAD_DIM:HEAD_DIM + 1])
        res = jnp.concatenate(outs, axis=0)
        o_ref[0, :, FOURIER_W + qp * LANES:FOURIER_W + (qp + 1) * LANES] = res.T.astype(BF16)


def _dft_tables(seq):
    s = np.arange(seq)
    ang = 2.0 * np.pi * ((s[:, None] * s[None, :]) % seq) / seq
    pos = np.concatenate([np.cos(ang), -np.sin(ang)], axis=1) / math.sqrt(seq)
    c = np.arange(FOURIER_GROUP_W)
    ang_c = 2.0 * np.pi * ((c[:, None] * c[None, :]) % FOURIER_GROUP_W) / FOURIER_GROUP_W
    eye = np.eye(FOURIER_W // FOURIER_GROUP_W)
    chan = np.stack([np.kron(eye, np.cos(ang_c)), np.kron(eye, np.sin(ang_c))]) / math.sqrt(FOURIER_GROUP_W)
    return jnp.asarray(pos, dtype=F32), jnp.asarray(chan, dtype=F32)


def _even_mixer(q, k, v, f, ctx):
    bsz, seq, _ = q.shape
    tq = ROW_TILE
    has_ctx = ctx is not None
    past = ctx[0].shape[1] if has_ctx else 0
    kt = seq + past
    pos_dft, chan_dft = _dft_tables(seq)
    full = lambda n, w: pl.BlockSpec((1, n, w), lambda b, i: (b, 0, 0))
    in_specs = [pl.BlockSpec((1, tq, Q_W), lambda b, i: (b, i, 0)), full(seq, KV_W), full(seq, KV_W)]
    args = [q, k, v]
    if has_ctx:
        in_specs += [full(past, KV_W), full(past, KV_W)]
        args += list(ctx)
    in_specs += [full(seq, FOURIER_W), _const_spec((seq, 2 * seq)),
                 _const_spec((2, FOURIER_W, FOURIER_W))]
    args += [f, pos_dft, chan_dft]
    return pl.pallas_call(
        functools.partial(_attn_kernel, seq=seq, has_ctx=has_ctx),
        grid=(bsz, seq // tq),
        in_specs=in_specs,
        out_specs=pl.BlockSpec((1, tq, D_MODEL), lambda b, i: (b, i, 0)),
        out_shape=jax.ShapeDtypeStruct((bsz, seq, D_MODEL), BF16),
        scratch_shapes=[pltpu.VMEM((2 * N_KV_HEADS, kt, LANES), BF16),
                        pltpu.VMEM((N_KV_HEADS, V_ROWS, kt), BF16),
                        pltpu.VMEM((2 * seq, FOURIER_W), BF16),
                        pltpu.VMEM((seq, 2 * seq), BF16)],
        compiler_params=_params(("arbitrary", "arbitrary")),
        name="even_mixer",
    )(*args)


def _post_ffn_kernel(*refs, odd):
    if odd:
        (x_ref, ylru_ref, g_ref, yssd_ref, z_ref, sn_ref, mod_ref, wout_ref, gf_ref,
         w1_ref, w3_ref, w2_ref, o_ref) = refs
        y_lru = ylru_ref[0] * _gelu_tanh(g_ref[0])
        y = yssd_ref[0] * _silu(z_ref[0])
        ms = jnp.mean(y * y, axis=-1, keepdims=True)
        y_ssd = y * lax.rsqrt(ms + EPS) * sn_ref[...]
        mix = jnp.concatenate([y_lru, y_ssd], axis=-1).astype(BF16)
    else:
        x_ref, mix_ref, mod_ref, wout_ref, gf_ref, w1_ref, w3_ref, w2_ref, o_ref = refs
        mix = mix_ref[0]
    x = x_ref[0] + mod_ref[0, 2:3, :] * _dot(mix, wout_ref[...])
    h = _modnorm(x, gf_ref[...], mod_ref[0, 3:4, :], mod_ref[0, 4:5, :]).astype(BF16)
    a = (_silu(_dot(h, w1_ref[...])) * _dot(h, w3_ref[...])).astype(BF16)
    o_ref[0] = x + mod_ref[0, 5:6, :] * _dot(a, w2_ref[...])


def _post_ffn(x, mix_args, mod, w_out, g_ffn, w1, w3, w2, layer, ssd_norm=None):
    bsz, seq, d = x.shape
    if mod.shape[0] == 1 and seq < TOKEN_BLOCK:
        fold = TOKEN_BLOCK // seq
        merge = lambda t: t.reshape(bsz // fold, fold * seq, t.shape[-1])
        out = _post_ffn(merge(x), jax.tree.map(merge, mix_args), mod, w_out, g_ffn, w1, w3, w2, layer, ssd_norm)
        return out.reshape(bsz, seq, d)
    tm = min(seq, TOKEN_BLOCK)
    odd = layer % 2 == 1
    tok = lambda w: pl.BlockSpec((1, tm, w), lambda b, i: (b, i, 0))
    slab = lambda w: pl.BlockSpec((None,) + w.shape[1:], lambda b, i: (layer, 0, 0), pipeline_mode=pl.Buffered(1))
    if odd:
        ylru, g, yssd, z = mix_args
        in_specs = [tok(d), tok(LRU_W), tok(LRU_W), tok(SSD_INNER), tok(SSD_INNER), _const_spec((1, SSD_INNER))]
        args = [x, ylru, g, yssd, z, ssd_norm]
    else:
        in_specs = [tok(d), tok(d)]
        args = [x, mix_args]
    in_specs += [_mod_spec(mod), _const_spec(w_out.shape), _const_spec((1, d)),
                 slab(w1), slab(w3), slab(w2)]
    args += [mod, w_out, g_ffn, w1, w3, w2]
    return pl.pallas_call(
        functools.partial(_post_ffn_kernel, odd=odd),
        grid=(bsz, seq // tm),
        in_specs=in_specs,
        out_specs=tok(d),
        out_shape=jax.ShapeDtypeStruct((bsz, seq, d), F32),
        compiler_params=_params(("arbitrary", "arbitrary")),
        name="post_ffn_odd" if odd else "post_ffn_even",
    )(*args)


XL_LO = LRU_W
Z_LO = 2 * LRU_W
XBC_LO = 2 * LRU_W + SSD_INNER
DT_LO = XBC_LO + SSD_CONV_CH


def _conv_block(main, edges, w, b):
    before, after0, after1 = edges
    rows = main.shape[0]
    row = lax.broadcasted_iota(jnp.int32, (SUBLANES, 1), 0)
    last = rows - SUBLANES

    def patch_first(x, fix):
        return jnp.concatenate([fix(x[:SUBLANES]), x[SUBLANES:]], axis=0)

    def patch_last(x, fix):
        return jnp.concatenate([x[:last], fix(x[last:])], axis=0)

    xm1 = patch_first(pltpu.roll(main, 1, 0), lambda t: jnp.where(row == 0, before, t))
    xp1 = patch_last(pltpu.roll(main, rows - 1, 0), lambda t: jnp.where(row == SUBLANES - 1, after0, t))
    xp2 = patch_last(pltpu.roll(main, rows - 2, 0),
                     lambda t: jnp.where(row == SUBLANES - 2, after0, jnp.where(row == SUBLANES - 1, after1, t)))
    return xm1 * w[0:1] + main * w[1:2] + xp1 * w[2:3] + xp2 * w[3:4] + b


def _odd_pre_kernel(x_ref, xprev_ref, xnext_ref, mod_ref, g_ref, w_ref, wdt_ref, cwl_ref, cbl_ref, wg_ref, bg_ref,
                    lam_ref, gate_ref, z_ref, a_ref, b_ref, xbc_ref, dt_ref):
    i = pl.program_id(1)
    g, shift, scale = g_ref[...], mod_ref[0, 0:1, :], mod_ref[0, 1:2, :]
    h = _modnorm(x_ref[0], g, shift, scale).astype(BF16)
    wide = 2 * LANES
    proj = lambda lo: _dot(h, w_ref[:, lo:lo + wide])
    p_xl = [proj(XL_LO + t * wide) for t in range(LRU_W // wide)]
    others = ([(gate_ref, t * wide, t * wide) for t in range(LRU_W // wide)]
              + [(z_ref, Z_LO + t * wide, t * wide) for t in range(SSD_INNER // wide)]
              + [(xbc_ref, XBC_LO + t * wide, t * wide) for t in range(SSD_CONV_CH // wide)])

    halo = jnp.concatenate([xprev_ref[0], xnext_ref[0]], axis=0)
    ph = _dot(_modnorm(halo, g, shift, scale).astype(BF16), w_ref[:, XL_LO:Z_LO])
    has_prev = i > 0
    has_next = i < pl.num_programs(1) - 1
    e_l = (jnp.where(has_prev, ph[SUBLANES - 1:SUBLANES], 0.0),
           jnp.where(has_next, ph[SUBLANES:SUBLANES + 1], 0.0),
           jnp.where(has_next, ph[SUBLANES + 1:SUBLANES + 2], 0.0))

    rate = (0.5 * LRU_C) * _softplus(-lam_ref[...])
    n_lru = LRU_W // LANES
    for lb in range(n_lru):
        sl = slice(lb * LANES, (lb + 1) * LANES)
        for dst, lo, col in others[lb * len(others) // n_lru:(lb + 1) * len(others) // n_lru]:
            dst[0, :, col:col + wide] = proj(lo)
        xb = _conv_block(p_xl[lb // 2][:, (lb % 2) * LANES:(lb % 2 + 1) * LANES], [e[:, sl] for e in e_l],
                         cwl_ref[:, sl], cbl_ref[:, sl])
        xh = 0.5 * xb
        gt = jnp.tanh(_dot(xb.astype(BF16), wg_ref[lb]) + bg_ref[lb])
        for d in range(2):
            neg_log_a = rate[d:d + 1, sl] * (gt[:, 2 * d * LANES:(2 * d + 1) * LANES] + 1.0)
            a = jnp.exp2(neg_log_a * (-math.log2(math.e)))
            g2 = jnp.tanh(neg_log_a) * (a * a + 1.0)
            root = jnp.where(g2 > 0.0, g2 * lax.rsqrt(g2), 0.0)
            a_ref[0, d, :, sl] = a
            b_ref[0, d, :, sl] = root * ((gt[:, (2 * d + 1) * LANES:(2 * d + 2) * LANES] + 1.0) * xh)
    dt_ref[0] = _dot(h, wdt_ref[...])


def _odd_pre(x, mod, g, w_in, w_dt, lru):
    bsz, seq, d = x.shape
    tm = min(seq, TOKEN_BLOCK)
    per8 = tm // SUBLANES
    tok = lambda w: pl.BlockSpec((1, tm, w), lambda b, i: (b, i, 0))
    both = pl.BlockSpec((1, 2, tm, LRU_W), lambda b, i: (b, 0, i, 0))
    consts = list(lru)
    out_specs = [tok(LRU_W), tok(SSD_INNER), both, both, tok(SSD_CONV_CH), tok(LANES)]
    row = lambda w: jax.ShapeDtypeStruct((bsz, seq, w), F32)
    out_shape = ([row(LRU_W), row(SSD_INNER)] + [jax.ShapeDtypeStruct((bsz, 2, seq, LRU_W), F32)] * 2
                 + [row(SSD_CONV_CH), row(LANES)])
    return pl.pallas_call(
        _odd_pre_kernel,
        grid=(bsz, seq // tm),
        in_specs=[tok(d),
                  pl.BlockSpec((1, SUBLANES, d), lambda b, i: (b, jnp.maximum(i * per8 - 1, 0), 0)),
                  pl.BlockSpec((1, SUBLANES, d), lambda b, i: (b, jnp.minimum((i + 1) * per8, seq // SUBLANES - 1), 0)),
                  _mod_spec(mod), _const_spec((1, d)), _const_spec(w_in.shape), _const_spec(w_dt.shape)]
                 + [_const_spec(c.shape) for c in consts],
        out_specs=out_specs,
        out_shape=out_shape,
        compiler_params=_params(("arbitrary", "arbitrary")),
        name="odd_pre",
    )(x, x, x, mod, g, w_in, w_dt, *consts)


def _ssd_kernel(*refs, seq, has_h0):
    xbc_ref, dt_ref, cw_ref, cb_ref, dtb_ref, alog_ref, dexp_ref, ef_ref, eb_ref, la_ref, lb_ref = refs[:11]
    if has_h0:
        h0_ref, lh0_ref, y_ref, ylru_ref = refs[11:15]
    else:
        y_ref, ylru_ref, hout_ref, lhout_ref = refs[11:15]
    xs_ref, bm_ref, cm_ref, cs_ref, dts_ref, cst_ref, dtt_ref, bt_ref, st_scr = refs[15:]
    q = SSD_Q
    nc = seq // q
    lane = lax.broadcasted_iota(jnp.int32, (q, LANES), 1)
    qi = lax.broadcasted_iota(jnp.int32, (q, q), 0)
    ki = lax.broadcasted_iota(jnp.int32, (q, q), 1)
    tri_lower = jnp.where(ki <= qi, 1.0, 0.0).astype(BF16)
    tri_upper = jnp.where(ki >= qi, 1.0, 0.0).astype(BF16)
    a_log2 = jnp.where(lane[0:1, :] < 2 * SSD_HEADS, -math.log2(math.e) * jnp.exp(alog_ref[...]), 0.0)

    def prepare(c):
        r0 = pl.multiple_of(c * q, q)
        rows = pl.ds(r0, q)
        prev8 = pl.ds(pl.multiple_of(jnp.maximum(r0 - SUBLANES, 0), SUBLANES), SUBLANES)
        next8 = pl.ds(pl.multiple_of(jnp.minimum(r0 + q, seq - SUBLANES), SUBLANES), SUBLANES)
        has_prev = r0 > 0
        has_next = r0 + q < seq
        for lb in range(SSD_CONV_CH // LANES):
            sl = slice(lb * LANES, (lb + 1) * LANES)
            before = xbc_ref[0, prev8, sl]
            after = xbc_ref[0, next8, sl]
            edges = (jnp.where(has_prev, before[SUBLANES - 1:SUBLANES], 0.0),
                     jnp.where(has_next, after[0:1], 0.0), jnp.where(has_next, after[1:2], 0.0))
            xc = _conv_block(xbc_ref[0, rows, sl], edges, cw_ref[:, sl], cb_ref[:, sl])
            half_xc = 0.5 * xc
            xc = half_xc * (jnp.tanh(half_xc) + 1.0)
            if lb < SSD_INNER // LANES:
                xs_ref[rows, sl] = xc
                y_ref[0, rows, sl] = dexp_ref[:, sl] * xc
            elif lb == SSD_INNER // LANES:
                bm_ref[rows, :] = xc
                bt_ref[c] = xc.T
            else:
                cm_ref[rows, :] = xc
        dts = _softplus(dt_ref[0, rows, :] + dtb_ref[...])
        da = dts * a_log2
        cs = jnp.where(lane < SSD_HEADS, _dot3(tri_lower, da), _dot3(tri_upper, da))
        cs_ref[rows, :] = cs
        dts_ref[rows, :] = dts
        cst_ref[c] = cs.T
        dtt_ref[c] = dts.T

    st_scr[...] = h0_ref[0] if has_h0 else jnp.zeros(st_scr.shape, F32)

    low = lane < SSD_STATE
    col = lax.broadcasted_iota(jnp.int32, (SSD_STATE, SSD_INNER), 1)
    first_group = col < SSD_INNER // 2

    def chunk(c, fwd):
        d = 0 if fwd else 1
        r0 = pl.multiple_of(c * q, q)
        rows = pl.ds(r0, q)
        cs = cs_ref[rows, :]
        dts = dts_ref[rows, :]
        cst = cst_ref[c]
        dtt = dtt_ref[c]
        xs = xs_ref[rows, :]
        xsb = xs.astype(BF16)
        bb = bm_ref[rows, :].astype(BF16)
        cb_ = cm_ref[rows, :].astype(BF16)
        zero = jnp.zeros_like(cb_)
        cbg = [_dot_nt(jnp.where(low, cb_, zero), bb).astype(BF16),
               _dot_nt(jnp.where(low, zero, cb_), bb).astype(BF16)]
        mask = (ki <= qi) if fwd else (ki >= qi)
        expand = ef_ref[...] if fwd else eb_ref[...]
        st = st_scr[d]
        hh = jnp.concatenate([jnp.where(first_group, st, 0.0), jnp.where(first_group, 0.0, st)], axis=0)
        tot = cs[q - 1:q, :] if fwd else cs[0:1, :]
        mine = (lane >= d * SSD_HEADS) & (lane < (d + 1) * SSD_HEADS)
        w = jnp.exp2(jnp.where(mine, tot - cs, 0.0)) * dts
        dec = jnp.broadcast_to(jnp.exp2(tot), (2 * SUBLANES, LANES))
        dec_hi = dec.astype(BF16)
        dec_lo = (dec - dec_hi.astype(F32)).astype(BF16)
        spread = _dot(jnp.concatenate([jnp.exp2(cs).astype(BF16), w.astype(BF16), dec_hi, dec_lo], axis=0), expand)
        y = _dot(cb_, hh.astype(BF16)) * spread[0:q]
        pairs = []
        for pp in range(SSD_HEADS // 2):
            ms = []
            for h in (2 * pp, 2 * pp + 1):
                l = d * SSD_HEADS + h
                decay = jnp.exp2((cs[:, l:l + 1] - cst[l:l + 1, :]).astype(BF16))
                decay = jnp.where(mask, decay, jnp.zeros_like(decay))
                ms.append(cbg[pp // (SSD_HEADS // 4)] * decay * dtt[l:l + 1, :].astype(BF16))
            xp = xsb[:, pp * LANES:(pp + 1) * LANES]
            zx = jnp.zeros_like(xp)
            x2 = jnp.concatenate([jnp.where(low, xp, zx), jnp.where(low, zx, xp)], axis=0)
            pairs.append(_dot(jnp.concatenate(ms, axis=1), x2))
        y_ref[0, rows, :] += y + jnp.concatenate(pairs, axis=1)
        xw = (xs * spread[q:2 * q]).astype(BF16)
        bt = bt_ref[c].astype(BF16)
        half_w = SSD_INNER // 2
        st_new = jnp.concatenate([_dot(bt[:SSD_STATE], xw[:, :half_w]), _dot(bt[SSD_STATE:], xw[:, half_w:])], axis=1)
        chunk_decay = spread[2 * q:2 * q + 1] + spread[2 * q + 2 * SUBLANES:2 * q + 2 * SUBLANES + 1]
        st_scr[d] = chunk_decay * st + st_new

    def prepare_all(c, carry):
        prepare(c)
        return carry

    nb = seq // SUBLANES
    per_chunk = q // SUBLANES
    row = lax.broadcasted_iota(jnp.int32, (SUBLANES, LRU_W), 0)

    def lru_block(blk, hf, hb):
        rf = pl.multiple_of(blk * SUBLANES, SUBLANES)
        rb = pl.multiple_of((nb - 1 - blk) * SUBLANES, SUBLANES)
        a = la_ref[0, 0, pl.ds(rf, SUBLANES), :]
        b = lb_ref[0, 0, pl.ds(rf, SUBLANES), :]
        for k in (1, 2, 4):
            keep = row >= k
            b = a * jnp.where(keep, pltpu.roll(b, k, 0), 0.0) + b
            a = a * jnp.where(keep, pltpu.roll(a, k, 0), 1.0)
        h = a * hf + b
        ylru_ref[0, pl.ds(rf, SUBLANES), :] += h
        hf = h[SUBLANES - 1:SUBLANES, :]
        a = la_ref[0, 1, pl.ds(rb, SUBLANES), :]
        b = lb_ref[0, 1, pl.ds(rb, SUBLANES), :]
        for k in (1, 2, 4):
            keep = row < SUBLANES - k
            b = a * jnp.where(keep, pltpu.roll(b, SUBLANES - k, 0), 0.0) + b
            a = a * jnp.where(keep, pltpu.roll(a, SUBLANES - k, 0), 1.0)
        h = a * hb + b
        ylru_ref[0, pl.ds(rb, SUBLANES), :] += h
        return hf, h[0:1, :]

    def both_directions(i, carry):
        chunk(i, True)
        chunk(nc - 1 - i, False)
        hf, hb = carry
        for j in range(per_chunk):
            hf, hb = lru_block(i * per_chunk + j, hf, hb)
        return hf, hb

    ylru_ref[0] = jnp.zeros((seq, LRU_W), F32)
    lax.fori_loop(0, nc, prepare_all, 0)
    if has_h0:
        lru_start = (lh0_ref[0, 0:1, :], lh0_ref[0, 1:2, :])
    else:
        lru_start = (jnp.zeros((1, LRU_W), F32), jnp.zeros((1, LRU_W), F32))
    hf, hb = lax.fori_loop(0, nc, both_directions, lru_start)
    if not has_h0:
        hout_ref[0] = st_scr[...]
        lhout_ref[0, 0:1, :] = hf
        lhout_ref[0, 1:2, :] = hb


def _ssd(xbc, dt, cw, cb, dtb, alog, dexp, lru_a, lru_b, h0, lru_h0):
    bsz, seq, _ = xbc.shape
    nc = seq // SSD_Q
    expand = np.zeros((2, LANES, SSD_INNER), np.float32)
    for d in range(2):
        for h in range(SSD_HEADS):
            expand[d, d * SSD_HEADS + h, h * 64:(h + 1) * 64] = 1.0
    expand = jnp.asarray(expand, dtype=BF16)
    per_seq = lambda n, w: pl.BlockSpec((1, n, w), lambda b: (b, 0, 0))
    state_spec = pl.BlockSpec((1, 2, SSD_STATE, SSD_INNER), lambda b: (b, 0, 0, 0))
    has_h0 = h0 is not None
    coef = pl.BlockSpec((1, 2, seq, LRU_W), lambda b: (b, 0, 0, 0))
    in_specs = [per_seq(seq, SSD_CONV_CH), per_seq(seq, LANES), _const_spec(cw.shape), _const_spec(cb.shape),
                _const_spec(dtb.shape), _const_spec(alog.shape), _const_spec(dexp.shape),
                _const_spec((LANES, SSD_INNER)), _const_spec((LANES, SSD_INNER)), coef, coef]
    args = [xbc, dt, cw, cb, dtb, alog, dexp, expand[0], expand[1], lru_a, lru_b]
    out_specs = [per_seq(seq, SSD_INNER), per_seq(seq, LRU_W)]
    out_shape = [jax.ShapeDtypeStruct((bsz, seq, SSD_INNER), F32), jax.ShapeDtypeStruct((bsz, seq, LRU_W), F32)]
    if has_h0:
        in_specs += [state_spec, per_seq(2, LRU_W)]
        args += [h0, lru_h0]
    else:
        out_specs += [state_spec, per_seq(2, LRU_W)]
        out_shape += [jax.ShapeDtypeStruct((bsz, 2, SSD_STATE, SSD_INNER), F32),
                      jax.ShapeDtypeStruct((bsz, 2, LRU_W), F32)]
    outs = pl.pallas_call(
        functools.partial(_ssd_kernel, seq=seq, has_h0=has_h0),
        grid=(bsz,),
        in_specs=in_specs,
        out_specs=out_specs,
        out_shape=out_shape,
        scratch_shapes=[pltpu.VMEM((seq, SSD_INNER), F32)] + [pltpu.VMEM((seq, LANES), F32)] * 4
                       + [pltpu.VMEM((nc, LANES, SSD_Q), F32)] * 3 + [pltpu.VMEM((2, SSD_STATE, SSD_INNER), F32)],
        compiler_params=_params(("arbitrary",)),
        name="ssd",
    )(*args)
    return (outs[0], outs[1], None, None) if has_h0 else tuple(outs)


def _lru_gate_weights(wa, ba, wx, bx):
    eye = 0.5 * jnp.eye(2, dtype=F32)

    def pair_blocks(w):
        w4 = w.reshape(4, 2, 64, 64)
        return jnp.einsum("laij,ab->laibj", w4, eye).reshape(4, LANES, LANES)

    mats = [pair_blocks(w) for w in (wa[0], wx[0], wa[1], wx[1])]
    wg = jnp.concatenate(mats, axis=-1).astype(BF16)
    bias = [0.5 * b.reshape(4, 1, LANES) for b in (ba[0], bx[0], ba[1], bx[1])]
    return wg, jnp.concatenate(bias, axis=-1)


def _trunk(x, mods, p, ctx):
    is_ctx = ctx is None
    bsz, seq, _ = x.shape
    f, q, k, v = _even_pre(x, mods[0], p["norm_mix"][0], p["w_in_even"], p["q_gain"], p["k_gain"], rope=not is_ctx)
    mix = _even_mixer(q, k, v, f, None if is_ctx else (ctx["k"], ctx["v"]))
    x = _post_ffn(x, mix, mods[0], p["w_out_even"], p["norm_ffn"][0], p["w1"], p["w3"], p["w2"], layer=0)
    gate, z, lru_a, lru_b, xbc, dt = _odd_pre(
        x, mods[1], p["norm_mix"][1], p["w_in_odd"], p["w_dt_odd"],
        lru=(p["conv_lru_w"], p["conv_lru_b"], p["lru_wg"], p["lru_bg"], p["lru_lambda"]))
    h0_lru, h0_ssd = (None, None) if is_ctx else (ctx["lru"], ctx["ssd"])
    yssd, ylru, ssd_state, lru_state = _ssd(xbc, dt, p["conv_ssd_w"], p["conv_ssd_b"], p["ssd_dtb"], p["ssd_alog"],
                                            p["ssd_dexp"], lru_a, lru_b, h0_ssd, h0_lru)
    x = _post_ffn(x, (ylru, gate, yssd, z), mods[1], p["w_out_odd"], p["norm_ffn"][1], p["w1"], p["w3"], p["w2"],
                  layer=1, ssd_norm=p["ssd_norm"])
    return x, k, v, lru_state, ssd_state


def kernel(x_prompt, x_sample, c, cache_k, cache_v, state_lru, state_ssd, c_ctx, w_ada, b_ada, norm_mix, norm_ffn,
           w_in_even, q_norm, k_norm, w_out_even, w_in_odd, conv_lru_w, conv_lru_b, lru_wa, lru_ba, lru_wx, lru_bx,
           lru_lambda, conv_ssd_w, conv_ssd_b, ssd_dt_bias, ssd_a_log, ssd_d, ssd_norm, w_out_odd, ffn_w1, ffn_w3,
           ffn_w2):
    bsz, seq, d = x_prompt.shape
    dec_b = x_sample.shape[0]
    past = cache_k.shape[2]

    cond_rows = 16
    cond = jnp.concatenate([c, c_ctx[None, :], jnp.zeros((cond_rows - dec_b - 1, d), F32)], axis=0)
    mod = _modulation(cond, w_ada, b_ada).reshape(2, cond_rows, 6, d)
    mods_sample = [mod[l, :dec_b] for l in range(2)]
    mods_prompt = [mod[l, dec_b:dec_b + 1] for l in range(2)]

    lru_wg, lru_bg = _lru_gate_weights(lru_wa[0], lru_ba[0], lru_wx[0], lru_bx[0])
    pad32 = lambda a: jnp.pad(a.reshape(1, 2 * SSD_HEADS), ((0, 0), (0, LANES - 2 * SSD_HEADS)))
    p = dict(
        norm_mix=norm_mix.reshape(2, 1, d), norm_ffn=norm_ffn.reshape(2, 1, d),
        w_in_even=w_in_even[0].astype(BF16),
        q_gain=jnp.tile(q_norm[0], N_Q_HEADS)[None, :], k_gain=jnp.tile(k_norm[0], N_KV_HEADS)[None, :],
        w_out_even=w_out_even[0].astype(BF16),
        w_in_odd=w_in_odd[0].astype(BF16),
        w_dt_odd=jnp.pad(w_in_odd[0, :, DT_LO:], ((0, 0), (0, LANES - 2 * SSD_HEADS))).astype(BF16),
        conv_lru_w=conv_lru_w[0], conv_lru_b=conv_lru_b[0][None, :],
        lru_wg=lru_wg, lru_bg=lru_bg, lru_lambda=lru_lambda[0],
        conv_ssd_w=conv_ssd_w[0], conv_ssd_b=conv_ssd_b[0][None, :],
        ssd_dtb=pad32(ssd_dt_bias[0]), ssd_alog=pad32(ssd_a_log[0]),
        ssd_dexp=jnp.repeat(ssd_d[0], SSD_INNER // SSD_HEADS)[None, :],
        ssd_norm=ssd_norm[0][None, :],
        w_out_odd=w_out_odd[0].astype(BF16),
        w1=ffn_w1.astype(BF16), w3=ffn_w3.astype(BF16), w2=ffn_w2.astype(BF16),
    )

    y_prompt, k_new, v_new, lru_new, ssd_new = _trunk(x_prompt, mods_prompt, p, None)
    ctx = dict(
        k=cache_k[:, 0].reshape(dec_b, past, KV_W), v=cache_v[:, 0].reshape(dec_b, past, KV_W),
        lru=state_lru[:, 0],
        ssd=state_ssd[:, 0].transpose(0, 1, 4, 2, 3).reshape(dec_b, 2, SSD_STATE, SSD_INNER),
    )
    y_sample, _, _, _, _ = _trunk(x_sample, mods_sample, p, ctx)

    new_k = k_new.reshape(bsz, 1, seq, N_KV_HEADS, HEAD_DIM)
    new_v = v_new.reshape(bsz, 1, seq, N_KV_HEADS, HEAD_DIM)
    new_lru = lru_new.reshape(bsz, 1, 2, LRU_W)
    new_ssd = (ssd_new.reshape(bsz, 2, SSD_STATE, SSD_HEADS, SSD_INNER // SSD_HEADS)
               .transpose(0, 1, 3, 4, 2).reshape(bsz, 1, 2, SSD_HEADS, SSD_INNER // SSD_HEADS, SSD_STATE))
    return (y_prompt, y_sample, new_k, new_v, new_lru, new_ssd)
```

```python
import functools
import math

import numpy as np
import jax
import jax.numpy as jnp
from jax import lax
from jax.experimental import pallas as pl
from jax.experimental.pallas import tpu as pltpu

F32 = jnp.float32
BF16 = jnp.bfloat16

D_MODEL = 1024
EPS = 1e-6
GRID_W = 64
HEAD_DIM = 64
N_Q_HEADS = 12
N_KV_HEADS = 4
FOURIER_W = 256
FOURIER_GROUP_W = 64
Q_W = N_Q_HEADS * HEAD_DIM
KV_W = N_KV_HEADS * HEAD_DIM
EVEN_IN = FOURIER_W + Q_W + 2 * KV_W
ROPE_THETA = 10000.0
AX_PAIRS = HEAD_DIM // 4
LRU_W = 512
LRU_C = 8.0
SSD_INNER = 1024
SSD_HEADS = 16
SSD_STATE = 64
SSD_CONV_CH = SSD_INNER + 4 * SSD_STATE
D_FF = 2816
LANES = 128
SUBLANES = 8
TOKEN_BLOCK = 512
ROW_TILE = 256
SSD_Q = 128
V_ROWS = HEAD_DIM + 16
VMEM_LIMIT = 56 * 1024 * 1024


def _params(sem, vmem=VMEM_LIMIT):
    return pltpu.CompilerParams(dimension_semantics=sem, vmem_limit_bytes=vmem)


def _const_spec(shape):
    nd = len(shape)
    return pl.BlockSpec(shape, lambda *_: (0,) * nd, pipeline_mode=pl.Buffered(1))


def _dot(a, b):
    return jnp.dot(a, b, preferred_element_type=F32)


def _dot_nt(a, b):
    return lax.dot_general(a, b, (((1,), (1,)), ((), ())), preferred_element_type=F32)


def _dot3(m, a):
    hi = a.astype(BF16)
    r1 = a - hi.astype(F32)
    mid = r1.astype(BF16)
    lo = (r1 - mid.astype(F32)).astype(BF16)
    return _dot(m, hi) + _dot(m, mid) + _dot(m, lo)


def _silu(x):
    return x * jax.nn.sigmoid(x)


def _softplus(x):
    return jnp.maximum(x, 0.0) + jnp.log1p(jnp.exp(-jnp.abs(x)))


def _gelu_tanh(x):
    c = math.sqrt(2.0 / math.pi)
    return 0.5 * x * (1.0 + jnp.tanh(c * (x + 0.044715 * (x * x * x))))


def _modnorm(x, g, shift, scale):
    ms = jnp.mean(x * x, axis=-1, keepdims=True)
    y = x * lax.rsqrt(ms + EPS) * g
    return y * (1.0 + scale) + shift


def _mod_kernel(cond_ref, w_ref, b_ref, o_ref):
    s = _silu(cond_ref[...])
    o_ref[0] = _dot(s.astype(BF16), w_ref[0].astype(BF16)) + b_ref[0]


def _modulation(cond, w_ada, b_ada):
    depth, d, n = w_ada.shape
    rows = cond.shape[0]
    tn = 3072
    return pl.pallas_call(
        _mod_kernel,
        grid=(depth, n // tn),
        in_specs=[pl.BlockSpec((rows, d), lambda l, j: (0, 0)),
                  pl.BlockSpec((1, d, tn), lambda l, j: (l, 0, j)),
                  pl.BlockSpec((1, 1, tn), lambda l, j: (l, 0, j))],
        out_specs=pl.BlockSpec((1, rows, tn), lambda l, j: (l, 0, j)),
        out_shape=jax.ShapeDtypeStruct((depth, rows, n), F32),
        compiler_params=_params(("arbitrary", "arbitrary")),
        name="modulation",
    )(cond, w_ada, b_ada.reshape(depth, 1, n))


def _mod_spec(mod):
    if mod.shape[0] == 1:
        return pl.BlockSpec((1, 6, D_MODEL), lambda b, i: (0, 0, 0))
    return pl.BlockSpec((1, 6, D_MODEL), lambda b, i: (b, 0, 0))


def _head_rmsnorm(x, gain):
    rows, width = x.shape
    low = lax.broadcasted_iota(jnp.int32, (rows, LANES), 1) < HEAD_DIM
    out = []
    for b in range(width // LANES):
        blk = x[:, b * LANES:(b + 1) * LANES]
        sq = blk * blk
        lo = jnp.sum(jnp.where(low, sq, 0.0), axis=-1, keepdims=True)
        hi = jnp.sum(jnp.where(low, 0.0, sq), axis=-1, keepdims=True)
        ms = jnp.where(low, lo, hi) * (1.0 / HEAD_DIM)
        out.append(blk * lax.rsqrt(ms + EPS) * gain[:, b * LANES:(b + 1) * LANES])
    return jnp.concatenate(out, axis=-1)


def _rope(x, cos, sin_signed):
    rows, width = x.shape
    lane = lax.broadcasted_iota(jnp.int32, (rows, LANES), 1)
    second = (lane % (2 * AX_PAIRS)) >= AX_PAIRS
    out = []
    for b in range(width // LANES):
        blk = x[:, b * LANES:(b + 1) * LANES]
        partner = jnp.where(second, pltpu.roll(blk, AX_PAIRS, 1), pltpu.roll(blk, LANES - AX_PAIRS, 1))
        out.append(blk * cos + partner * sin_signed)
    return jnp.concatenate(out, axis=-1)


def _even_pre_kernel(*refs, rope):
    if rope:
        x_ref, mod_ref, g_ref, w_ref, qg_ref, kg_ref, cos_ref, sin_ref, f_ref, q_ref, k_ref, v_ref = refs
    else:
        x_ref, mod_ref, g_ref, w_ref, qg_ref, kg_ref, f_ref, q_ref, k_ref, v_ref, kt_ref, vt_ref = refs
    h = _modnorm(x_ref[0], g_ref[...], mod_ref[0, 0:1, :], mod_ref[0, 1:2, :])
    p = _dot(h.astype(BF16), w_ref[...])
    f_ref[0] = p[:, :FOURIER_W]
    q = _head_rmsnorm(p[:, FOURIER_W:FOURIER_W + Q_W], qg_ref[...])
    k = _head_rmsnorm(p[:, FOURIER_W + Q_W:FOURIER_W + Q_W + KV_W], kg_ref[...])
    if rope:
        q = _rope(q, cos_ref[...], sin_ref[...])
        k = _rope(k, cos_ref[...], sin_ref[...])
    q_ref[0] = (q * (HEAD_DIM ** -0.5 * math.log2(math.e))).astype(BF16)
    v = p[:, FOURIER_W + Q_W + KV_W:]
    k_ref[0] = k
    v_ref[0] = v
    if not rope:
        kt_ref[0] = k.T
        vt_ref[0] = v.T


def _rope_tables(seq):
    pos = np.arange(seq)
    freqs = ROPE_THETA ** (-np.arange(AX_PAIRS, dtype=np.float64) / AX_PAIRS)
    ang = np.zeros((seq, HEAD_DIM))
    sign = np.zeros((HEAD_DIM,))
    for a, p_a in enumerate((pos // GRID_W, pos % GRID_W)):
        for t in range(2):
            lo = a * 2 * AX_PAIRS + t * AX_PAIRS
            ang[:, lo:lo + AX_PAIRS] = p_a[:, None] * freqs[None, :]
            sign[lo:lo + AX_PAIRS] = -1.0 if t == 0 else 1.0
    cos = np.tile(np.cos(ang), (1, 2)).astype(np.float32)
    sin = np.tile(np.sin(ang) * sign[None, :], (1, 2)).astype(np.float32)
    return jnp.asarray(cos), jnp.asarray(sin)


def _even_pre(x, mod, g, w_in, qg, kg, rope):
    bsz, seq, d = x.shape
    tm = ROW_TILE
    tok = lambda w: pl.BlockSpec((1, tm, w), lambda b, i: (b, i, 0))
    in_specs = [tok(d), _mod_spec(mod), _const_spec((1, d)), _const_spec((d, EVEN_IN)),
                _const_spec((1, Q_W)), _const_spec((1, KV_W))]
    args = [x, mod, g, w_in, qg, kg]
    if rope:
        cos, sin = _rope_tables(seq)
        in_specs += [pl.BlockSpec((tm, LANES), lambda b, i: (i, 0))] * 2
        args += [cos, sin]
    out_specs = [tok(FOURIER_W), tok(Q_W), tok(KV_W), tok(KV_W)]
    out_shape = [jax.ShapeDtypeStruct((bsz, seq, FOURIER_W), F32),
                 jax.ShapeDtypeStruct((bsz, seq, Q_W), BF16),
                 jax.ShapeDtypeStruct((bsz, seq, KV_W), F32),
                 jax.ShapeDtypeStruct((bsz, seq, KV_W), F32)]
    if not rope:
        out_specs += [pl.BlockSpec((1, KV_W, tm), lambda b, i: (b, 0, i))] * 2
        out_shape += [jax.ShapeDtypeStruct((bsz, KV_W, seq), F32)] * 2
    return pl.pallas_call(
        functools.partial(_even_pre_kernel, rope=rope),
        grid=(bsz, seq // tm),
        in_specs=in_specs,
        out_specs=out_specs,
        out_shape=out_shape,
        compiler_params=_params(("arbitrary", "arbitrary")),
        name="even_pre",
    )(*args)


def _attn_kernel(*refs, seq, has_ctx):
    if has_ctx:
        q_ref, k_ref, v_ref, ck_ref, cv_ref, f_ref, cs_ref, cbd_ref, o_ref, kop, vopt, g_scr, cs_scr = refs
    else:
        q_ref, k_ref, v_ref, f_ref, cs_ref, cbd_ref, o_ref, kop, vopt, g_scr, cs_scr = refs
    tq = q_ref.shape[1]

    @pl.when((pl.program_id(0) == 0) & (pl.program_id(1) == 0))
    def _round_tables():
        for r0 in range(0, seq, tq):
            cs_scr[r0:r0 + tq, :] = cs_ref[r0:r0 + tq, :].astype(BF16)

    @pl.when(pl.program_id(1) == 0)
    def _prepare():
        def put(kx, vx, r0):
            n = kx.shape[0]
            low = lax.broadcasted_iota(jnp.int32, (n, LANES), 1) < HEAD_DIM
            ones = jnp.ones((V_ROWS - HEAD_DIM, n), BF16)
            for jp in range(N_KV_HEADS // 2):
                kp = kx[:, jp * LANES:(jp + 1) * LANES]
                k_lo = jnp.where(low, kp, 0.0)
                k_hi = jnp.where(low, 0.0, kp)
                placed = (k_lo, pltpu.roll(k_lo, HEAD_DIM, 1), pltpu.roll(k_hi, HEAD_DIM, 1), k_hi)
                for t, kk in enumerate(placed):
                    kop[4 * jp + t, r0:r0 + n, :] = kk.astype(BF16)
                vt = vx[:, jp * LANES:(jp + 1) * LANES].T.astype(BF16)
                for t in range(2):
                    vopt[2 * jp + t, 0:HEAD_DIM, r0:r0 + n] = vt[t * HEAD_DIM:(t + 1) * HEAD_DIM]
                    vopt[2 * jp + t, HEAD_DIM:V_ROWS, r0:r0 + n] = ones

        put(k_ref[0], v_ref[0], 0)
        if has_ctx:
            put(ck_ref[0], cv_ref[0], seq)
        f = f_ref[0].astype(BF16)
        g_scr[0:seq, :] = _dot(f, cbd_ref[0].astype(BF16)).astype(BF16)
        g_scr[seq:2 * seq, :] = _dot(f, cbd_ref[1].astype(BF16)).astype(BF16)

    r0 = pl.multiple_of(pl.program_id(1) * tq, tq)
    o_ref[0, :, 0:FOURIER_W] = _dot(cs_scr[pl.ds(r0, tq), :], g_scr[...]).astype(BF16)
    q = q_ref[0]
    n_pairs = N_Q_HEADS // 2
    kv_of = lambda qp: [(2 * qp + half) // (N_Q_HEADS // N_KV_HEADS) for half in range(2)]

    def pair_scores(qp):
        qpair = q[:, qp * LANES:(qp + 1) * LANES]
        return [_dot_nt(qpair, kop[2 * kv_of(qp)[half] + half]) for half in range(2)]

    scores_next = pair_scores(0)
    for qp in range(n_pairs):
        kv = kv_of(qp)
        scores = scores_next
        if qp + 1 < n_pairs:
            scores_next = pair_scores(qp + 1)
        probs = [jnp.exp2(s - jnp.max(s, axis=-1, keepdims=True)).astype(BF16) for s in scores]
        outs = []
        for half in range(2):
            o = _dot_nt(vopt[kv[half]], probs[half])
            outs.append(o[0:HEAD_DIM] / o[HEAD_DIM:HEAD_DIM + 1])
        res = jnp.concatenate(outs, axis=0)
        o_ref[0, :, FOURIER_W + qp * LANES:FOURIER_W + (qp + 1) * LANES] = res.T.astype(BF16)


def _dft_tables(seq):
    s = np.arange(seq)
    ang = 2.0 * np.pi * ((s[:, None] * s[None, :]) % seq) / seq
    pos = np.concatenate([np.cos(ang), -np.sin(ang)], axis=1) / math.sqrt(seq)
    c = np.arange(FOURIER_GROUP_W)
    ang_c = 2.0 * np.pi * ((c[:, None] * c[None, :]) % FOURIER_GROUP_W) / FOURIER_GROUP_W
    eye = np.eye(FOURIER_W // FOURIER_GROUP_W)
    chan = np.stack([np.kron(eye, np.cos(ang_c)), np.kron(eye, np.sin(ang_c))]) / math.sqrt(FOURIER_GROUP_W)
    return jnp.asarray(pos, dtype=F32), jnp.asarray(chan, dtype=F32)


def _even_mixer(q, k, v, f, ctx):
    bsz, seq, _ = q.shape
    tq = ROW_TILE
    has_ctx = ctx is not None
    past = ctx[0].shape[1] if has_ctx else 0
    kt = seq + past
    pos_dft, chan_dft = _dft_tables(seq)
    full = lambda n, w: pl.BlockSpec((1, n, w), lambda b, i: (b, 0, 0))
    in_specs = [pl.BlockSpec((1, tq, Q_W), lambda b, i: (b, i, 0)), full(seq, KV_W), full(seq, KV_W)]
    args = [q, k, v]
    if has_ctx:
        in_specs += [full(past, KV_W), full(past, KV_W)]
        args += list(ctx)
    in_specs += [full(seq, FOURIER_W), _const_spec((seq, 2 * seq)),
                 _const_spec((2, FOURIER_W, FOURIER_W))]
    args += [f, pos_dft, chan_dft]
    return pl.pallas_call(
        functools.partial(_attn_kernel, seq=seq, has_ctx=has_ctx),
        grid=(bsz, seq // tq),
        in_specs=in_specs,
        out_specs=pl.BlockSpec((1, tq, D_MODEL), lambda b, i: (b, i, 0)),
        out_shape=jax.ShapeDtypeStruct((bsz, seq, D_MODEL), BF16),
        scratch_shapes=[pltpu.VMEM((2 * N_KV_HEADS, kt, LANES), BF16),
                        pltpu.VMEM((N_KV_HEADS, V_ROWS, kt), BF16),
                        pltpu.VMEM((2 * seq, FOURIER_W), BF16),
                        pltpu.VMEM((seq, 2 * seq), BF16)],
        compiler_params=_params(("arbitrary", "arbitrary")),
        name="even_mixer",
    )(*args)


def _post_ffn_kernel(*refs, odd):
    if odd:
        (x_ref, ylru_ref, g_ref, yssd_ref, z_ref, sn_ref, mod_ref, wout_ref, gf_ref,
         w1_ref, w3_ref, w2_ref, o_ref) = refs
        y_lru = ylru_ref[0] * _gelu_tanh(g_ref[0])
        y = yssd_ref[0] * _silu(z_ref[0])
        ms = jnp.mean(y * y, axis=-1, keepdims=True)
        y_ssd = y * lax.rsqrt(ms + EPS) * sn_ref[...]
        mix = jnp.concatenate([y_lru, y_ssd], axis=-1).astype(BF16)
    else:
        x_ref, mix_ref, mod_ref, wout_ref, gf_ref, w1_ref, w3_ref, w2_ref, o_ref = refs
        mix = mix_ref[0]
    x = x_ref[0] + mod_ref[0, 2:3, :] * _dot(mix, wout_ref[...])
    h = _modnorm(x, gf_ref[...], mod_ref[0, 3:4, :], mod_ref[0, 4:5, :]).astype(BF16)
    a = (_silu(_dot(h, w1_ref[...])) * _dot(h, w3_ref[...])).astype(BF16)
    o_ref[0] = x + mod_ref[0, 5:6, :] * _dot(a, w2_ref[...])


def _post_ffn(x, mix_args, mod, w_out, g_ffn, w1, w3, w2, layer, ssd_norm=None):
    bsz, seq, d = x.shape
    if mod.shape[0] == 1 and seq < TOKEN_BLOCK:
        fold = TOKEN_BLOCK // seq
        merge = lambda t: t.reshape(bsz // fold, fold * seq, t.shape[-1])
        out = _post_ffn(merge(x), jax.tree.map(merge, mix_args), mod, w_out, g_ffn, w1, w3, w2, layer, ssd_norm)
        return out.reshape(bsz, seq, d)
    tm = min(seq, TOKEN_BLOCK)
    odd = layer % 2 == 1
    tok = lambda w: pl.BlockSpec((1, tm, w), lambda b, i: (b, i, 0))
    slab = lambda w: pl.BlockSpec((None,) + w.shape[1:], lambda b, i: (layer, 0, 0), pipeline_mode=pl.Buffered(1))
    if odd:
        ylru, g, yssd, z = mix_args
        in_specs = [tok(d), tok(LRU_W), tok(LRU_W), tok(SSD_INNER), tok(SSD_INNER), _const_spec((1, SSD_INNER))]
        args = [x, ylru, g, yssd, z, ssd_norm]
    else:
        in_specs = [tok(d), tok(d)]
        args = [x, mix_args]
    in_specs += [_mod_spec(mod), _const_spec(w_out.shape), _const_spec((1, d)),
                 slab(w1), slab(w3), slab(w2)]
    args += [mod, w_out, g_ffn, w1, w3, w2]
    return pl.pallas_call(
        functools.partial(_post_ffn_kernel, odd=odd),
        grid=(bsz, seq // tm),
        in_specs=in_specs,
        out_specs=tok(d),
        out_shape=jax.ShapeDtypeStruct((bsz, seq, d), F32),
        compiler_params=_params(("arbitrary", "arbitrary")),
        name="post_ffn_odd" if odd else "post_ffn_even",
    )(*args)


XL_LO = LRU_W
Z_LO = 2 * LRU_W
XBC_LO = 2 * LRU_W + SSD_INNER
DT_LO = XBC_LO + SSD_CONV_CH


def _conv_block(main, edges, w, b):
    before, after0, after1 = edges
    rows = main.shape[0]
    row = lax.broadcasted_iota(jnp.int32, (SUBLANES, 1), 0)
    last = rows - SUBLANES

    def patch_first(x, fix):
        return jnp.concatenate([fix(x[:SUBLANES]), x[SUBLANES:]], axis=0)

    def patch_last(x, fix):
        return jnp.concatenate([x[:last], fix(x[last:])], axis=0)

    xm1 = patch_first(pltpu.roll(main, 1, 0), lambda t: jnp.where(row == 0, before, t))
    xp1 = patch_last(pltpu.roll(main, rows - 1, 0), lambda t: jnp.where(row == SUBLANES - 1, after0, t))
    xp2 = patch_last(pltpu.roll(main, rows - 2, 0),
                     lambda t: jnp.where(row == SUBLANES - 2, after0, jnp.where(row == SUBLANES - 1, after1, t)))
    return xm1 * w[0:1] + main * w[1:2] + xp1 * w[2:3] + xp2 * w[3:4] + b


def _odd_pre_kernel(x_ref, xprev_ref, xnext_ref, mod_ref, g_ref, w_ref, wdt_ref, cwl_ref, cbl_ref, wg_ref, bg_ref,
                    lam_ref, gate_ref, z_ref, a_ref, b_ref, xbc_ref, dt_ref):
    i = pl.program_id(1)
    g, shift, scale = g_ref[...], mod_ref[0, 0:1, :], mod_ref[0, 1:2, :]
    h = _modnorm(x_ref[0], g, shift, scale).astype(BF16)
    wide = 2 * LANES
    proj = lambda lo: _dot(h, w_ref[:, lo:lo + wide])
    p_xl = [proj(XL_LO + t * wide) for t in range(LRU_W // wide)]
    others = ([(gate_ref, t * wide, t * wide) for t in range(LRU_W // wide)]
              + [(z_ref, Z_LO + t * wide, t * wide) for t in range(SSD_INNER // wide)]
              + [(xbc_ref, XBC_LO + t * wide, t * wide) for t in range(SSD_CONV_CH // wide)])

    halo = jnp.concatenate([xprev_ref[0], xnext_ref[0]], axis=0)
    ph = _dot(_modnorm(halo, g, shift, scale).astype(BF16), w_ref[:, XL_LO:Z_LO])
    has_prev = i > 0
    has_next = i < pl.num_programs(1) - 1
    e_l = (jnp.where(has_prev, ph[SUBLANES - 1:SUBLANES], 0.0),
           jnp.where(has_next, ph[SUBLANES:SUBLANES + 1], 0.0),
           jnp.where(has_next, ph[SUBLANES + 1:SUBLANES + 2], 0.0))

    rate = (0.5 * LRU_C) * _softplus(-lam_ref[...])
    n_lru = LRU_W // LANES
    for lb in range(n_lru):
        sl = slice(lb * LANES, (lb + 1) * LANES)
        for dst, lo, col in others[lb * len(others) // n_lru:(lb + 1) * len(others) // n_lru]:
            dst[0, :, col:col + wide] = proj(lo)
        xb = _conv_block(p_xl[lb // 2][:, (lb % 2) * LANES:(lb % 2 + 1) * LANES], [e[:, sl] for e in e_l],
                         cwl_ref[:, sl], cbl_ref[:, sl])
        xh = 0.5 * xb
        gt = jnp.tanh(_dot(xb.astype(BF16), wg_ref[lb]) + bg_ref[lb])
        for d in range(2):
            neg_log_a = rate[d:d + 1, sl] * (gt[:, 2 * d * LANES:(2 * d + 1) * LANES] + 1.0)
            a = jnp.exp2(neg_log_a * (-math.log2(math.e)))
            g2 = jnp.tanh(neg_log_a) * (a * a + 1.0)
            root = jnp.where(g2 > 0.0, g2 * lax.rsqrt(g2), 0.0)
            a_ref[0, d, :, sl] = a
            b_ref[0, d, :, sl] = root * ((gt[:, (2 * d + 1) * LANES:(2 * d + 2) * LANES] + 1.0) * xh)
    dt_ref[0] = _dot(h, wdt_ref[...])


def _odd_pre(x, mod, g, w_in, w_dt, lru):
    bsz, seq, d = x.shape
    tm = min(seq, TOKEN_BLOCK)
    per8 = tm // SUBLANES
    tok = lambda w: pl.BlockSpec((1, tm, w), lambda b, i: (b, i, 0))
    both = pl.BlockSpec((1, 2, tm, LRU_W), lambda b, i: (b, 0, i, 0))
    consts = list(lru)
    out_specs = [tok(LRU_W), tok(SSD_INNER), both, both, tok(SSD_CONV_CH), tok(LANES)]
    row = lambda w: jax.ShapeDtypeStruct((bsz, seq, w), F32)
    out_shape = ([row(LRU_W), row(SSD_INNER)] + [jax.ShapeDtypeStruct((bsz, 2, seq, LRU_W), F32)] * 2
                 + [row(SSD_CONV_CH), row(LANES)])
    return pl.pallas_call(
        _odd_pre_kernel,
        grid=(bsz, seq // tm),
        in_specs=[tok(d),
                  pl.BlockSpec((1, SUBLANES, d), lambda b, i: (b, jnp.maximum(i * per8 - 1, 0), 0)),
                  pl.BlockSpec((1, SUBLANES, d), lambda b, i: (b, jnp.minimum((i + 1) * per8, seq // SUBLANES - 1), 0)),
                  _mod_spec(mod), _const_spec((1, d)), _const_spec(w_in.shape), _const_spec(w_dt.shape)]
                 + [_const_spec(c.shape) for c in consts],
        out_specs=out_specs,
        out_shape=out_shape,
        compiler_params=_params(("arbitrary", "arbitrary")),
        name="odd_pre",
    )(x, x, x, mod, g, w_in, w_dt, *consts)


def _ssd_kernel(*refs, seq, has_h0):
    xbc_ref, dt_ref, cw_ref, cb_ref, dtb_ref, alog_ref, dexp_ref, ef_ref, eb_ref, la_ref, lb_ref = refs[:11]
    if has_h0:
        h0_ref, lh0_ref, y_ref, ylru_ref = refs[11:15]
    else:
        y_ref, ylru_ref, hout_ref, lhout_ref = refs[11:15]
    xs_ref, bm_ref, cm_ref, cs_ref, dts_ref, cst_ref, dtt_ref, bt_ref, st_scr = refs[15:]
    q = SSD_Q
    nc = seq // q
    lane = lax.broadcasted_iota(jnp.int32, (q, LANES), 1)
    qi = lax.broadcasted_iota(jnp.int32, (q, q), 0)
    ki = lax.broadcasted_iota(jnp.int32, (q, q), 1)
    tri_lower = jnp.where(ki <= qi, 1.0, 0.0).astype(BF16)
    tri_upper = jnp.where(ki >= qi, 1.0, 0.0).astype(BF16)
    a_log2 = jnp.where(lane[0:1, :] < 2 * SSD_HEADS, -math.log2(math.e) * jnp.exp(alog_ref[...]), 0.0)

    def prepare(c):
        r0 = pl.multiple_of(c * q, q)
        rows = pl.ds(r0, q)
        prev8 = pl.ds(pl.multiple_of(jnp.maximum(r0 - SUBLANES, 0), SUBLANES), SUBLANES)
        next8 = pl.ds(pl.multiple_of(jnp.minimum(r0 + q, seq - SUBLANES), SUBLANES), SUBLANES)
        has_prev = r0 > 0
        has_next = r0 + q < seq
        for lb in range(SSD_CONV_CH // LANES):
            sl = slice(lb * LANES, (lb + 1) * LANES)
            before = xbc_ref[0, prev8, sl]
            after = xbc_ref[0, next8, sl]
            edges = (jnp.where(has_prev, before[SUBLANES - 1:SUBLANES], 0.0),
                     jnp.where(has_next, after[0:1], 0.0), jnp.where(has_next, after[1:2], 0.0))
            xc = _conv_block(xbc_ref[0, rows, sl], edges, cw_ref[:, sl], cb_ref[:, sl])
            half_xc = 0.5 * xc
            xc = half_xc * (jnp.tanh(half_xc) + 1.0)
            if lb < SSD_INNER // LANES:
                xs_ref[rows, sl] = xc
                y_ref[0, rows, sl] = dexp_ref[:, sl] * xc
            elif lb == SSD_INNER // LANES:
                bm_ref[rows, :] = xc
                bt_ref[c] = xc.T
            else:
                cm_ref[rows, :] = xc
        dts = _softplus(dt_ref[0, rows, :] + dtb_ref[...])
        da = dts * a_log2
        cs = jnp.where(lane < SSD_HEADS, _dot3(tri_lower, da), _dot3(tri_upper, da))
        cs_ref[rows, :] = cs
        dts_ref[rows, :] = dts
        cst_ref[c] = cs.T
        dtt_ref[c] = dts.T

    st_scr[...] = h0_ref[0] if has_h0 else jnp.zeros(st_scr.shape, F32)

    low = lane < SSD_STATE
    col = lax.broadcasted_iota(jnp.int32, (SSD_STATE, SSD_INNER), 1)
    first_group = col < SSD_INNER // 2

    def chunk(c, fwd):
        d = 0 if fwd else 1
        r0 = pl.multiple_of(c * q, q)
        rows = pl.ds(r0, q)
        cs = cs_ref[rows, :]
        dts = dts_ref[rows, :]
        cst = cst_ref[c]
        dtt = dtt_ref[c]
        xs = xs_ref[rows, :]
        xsb = xs.astype(BF16)
        bb = bm_ref[rows, :].astype(BF16)
        cb_ = cm_ref[rows, :].astype(BF16)
        zero = jnp.zeros_like(cb_)
        cbg = [_dot_nt(jnp.where(low, cb_, zero), bb).astype(BF16),
               _dot_nt(jnp.where(low, zero, cb_), bb).astype(BF16)]
        mask = (ki <= qi) if fwd else (ki >= qi)
        expand = ef_ref[...] if fwd else eb_ref[...]
        st = st_scr[d]
        hh = jnp.concatenate([jnp.where(first_group, st, 0.0), jnp.where(first_group, 0.0, st)], axis=0)
        tot = cs[q - 1:q, :] if fwd else cs[0:1, :]
        mine = (lane >= d * SSD_HEADS) & (lane < (d + 1) * SSD_HEADS)
        w = jnp.exp2(jnp.where(mine, tot - cs, 0.0)) * dts
        dec = jnp.broadcast_to(jnp.exp2(tot), (2 * SUBLANES, LANES))
        dec_hi = dec.astype(BF16)
        dec_lo = (dec - dec_hi.astype(F32)).astype(BF16)
        spread = _dot(jnp.concatenate([jnp.exp2(cs).astype(BF16), w.astype(BF16), dec_hi, dec_lo], axis=0), expand)
        y = _dot(cb_, hh.astype(BF16)) * spread[0:q]
        pairs = []
        for pp in range(SSD_HEADS // 2):
            ms = []
            for h in (2 * pp, 2 * pp + 1):
                l = d * SSD_HEADS + h
                decay = jnp.exp2((cs[:, l:l + 1] - cst[l:l + 1, :]).astype(BF16))
                decay = jnp.where(mask, decay, jnp.zeros_like(decay))
                ms.append(cbg[pp // (SSD_HEADS // 4)] * decay * dtt[l:l + 1, :].astype(BF16))
            xp = xsb[:, pp * LANES:(pp + 1) * LANES]
            zx = jnp.zeros_like(xp)
            x2 = jnp.concatenate([jnp.where(low, xp, zx), jnp.where(low, zx, xp)], axis=0)
            pairs.append(_dot(jnp.concatenate(ms, axis=1), x2))
        y_ref[0, rows, :] += y + jnp.concatenate(pairs, axis=1)
        xw = (xs * spread[q:2 * q]).astype(BF16)
        bt = bt_ref[c].astype(BF16)
        half_w = SSD_INNER // 2
        st_new = jnp.concatenate([_dot(bt[:SSD_STATE], xw[:, :half_w]), _dot(bt[SSD_STATE:], xw[:, half_w:])], axis=1)
        chunk_decay = spread[2 * q:2 * q + 1] + spread[2 * q + 2 * SUBLANES:2 * q + 2 * SUBLANES + 1]
        st_scr[d] = chunk_decay * st + st_new

    def prepare_all(c, carry):
        prepare(c)
        return carry

    nb = seq // SUBLANES
    per_chunk = q // SUBLANES
    row = lax.broadcasted_iota(jnp.int32, (SUBLANES, LRU_W), 0)

    def lru_block(blk, hf, hb):
        rf = pl.multiple_of(blk * SUBLANES, SUBLANES)
        rb = pl.multiple_of((nb - 1 - blk) * SUBLANES, SUBLANES)
        a = la_ref[0, 0, pl.ds(rf, SUBLANES), :]
        b = lb_ref[0, 0, pl.ds(rf, SUBLANES), :]
        for k in (1, 2, 4):
            keep = row >= k
            b = a * jnp.where(keep, pltpu.roll(b, k, 0), 0.0) + b
            a = a * jnp.where(keep, pltpu.roll(a, k, 0), 1.0)
        h = a * hf + b
        ylru_ref[0, pl.ds(rf, SUBLANES), :] += h
        hf = h[SUBLANES - 1:SUBLANES, :]
        a = la_ref[0, 1, pl.ds(rb, SUBLANES), :]
        b = lb_ref[0, 1, pl.ds(rb, SUBLANES), :]
        for k in (1, 2, 4):
            keep = row < SUBLANES - k
            b = a * jnp.where(keep, pltpu.roll(b, SUBLANES - k, 0), 0.0) + b
            a = a * jnp.where(keep, pltpu.roll(a, SUBLANES - k, 0), 1.0)
        h = a * hb + b
        ylru_ref[0, pl.ds(rb, SUBLANES), :] += h
        return hf, h[0:1, :]

    def both_directions(i, carry):
        chunk(i, True)
        chunk(nc - 1 - i, False)
        hf, hb = carry
        for j in range(per_chunk):
            hf, hb = lru_block(i * per_chunk + j, hf, hb)
        return hf, hb

    ylru_ref[0] = jnp.zeros((seq, LRU_W), F32)
    lax.fori_loop(0, nc, prepare_all, 0)
    if has_h0:
        lru_start = (lh0_ref[0, 0:1, :], lh0_ref[0, 1:2, :])
    else:
        lru_start = (jnp.zeros((1, LRU_W), F32), jnp.zeros((1, LRU_W), F32))
    hf, hb = lax.fori_loop(0, nc, both_directions, lru_start)
    if not has_h0:
        hout_ref[0] = st_scr[...]
        lhout_ref[0, 0:1, :] = hf
        lhout_ref[0, 1:2, :] = hb


def _ssd(xbc, dt, cw, cb, dtb, alog, dexp, lru_a, lru_b, h0, lru_h0):
    bsz, seq, _ = xbc.shape
    nc = seq // SSD_Q
    expand = np.zeros((2, LANES, SSD_INNER), np.float32)
    for d in range(2):
        for h in range(SSD_HEADS):
            expand[d, d * SSD_HEADS + h, h * 64:(h + 1) * 64] = 1.0
    expand = jnp.asarray(expand, dtype=BF16)
    per_seq = lambda n, w: pl.BlockSpec((1, n, w), lambda b: (b, 0, 0))
    state_spec = pl.BlockSpec((1, 2, SSD_STATE, SSD_INNER), lambda b: (b, 0, 0, 0))
    has_h0 = h0 is not None
    coef = pl.BlockSpec((1, 2, seq, LRU_W), lambda b: (b, 0, 0, 0))
    in_specs = [per_seq(seq, SSD_CONV_CH), per_seq(seq, LANES), _const_spec(cw.shape), _const_spec(cb.shape),
                _const_spec(dtb.shape), _const_spec(alog.shape), _const_spec(dexp.shape),
                _const_spec((LANES, SSD_INNER)), _const_spec((LANES, SSD_INNER)), coef, coef]
    args = [xbc, dt, cw, cb, dtb, alog, dexp, expand[0], expand[1], lru_a, lru_b]
    out_specs = [per_seq(seq, SSD_INNER), per_seq(seq, LRU_W)]
    out_shape = [jax.ShapeDtypeStruct((bsz, seq, SSD_INNER), F32), jax.ShapeDtypeStruct((bsz, seq, LRU_W), F32)]
    if has_h0:
        in_specs += [state_spec, per_seq(2, LRU_W)]
        args += [h0, lru_h0]
    else:
        out_specs += [state_spec, per_seq(2, LRU_W)]
        out_shape += [jax.ShapeDtypeStruct((bsz, 2, SSD_STATE, SSD_INNER), F32),
                      jax.ShapeDtypeStruct((bsz, 2, LRU_W), F32)]
    outs = pl.pallas_call(
        functools.partial(_ssd_kernel, seq=seq, has_h0=has_h0),
        grid=(bsz,),
        in_specs=in_specs,
        out_specs=out_specs,
        out_shape=out_shape,
        scratch_shapes=[pltpu.VMEM((seq, SSD_INNER), F32)] + [pltpu.VMEM((seq, LANES), F32)] * 4
                       + [pltpu.VMEM((nc, LANES, SSD_Q), F32)] * 3 + [pltpu.VMEM((2, SSD_STATE, SSD_INNER), F32)],
        compiler_params=_params(("arbitrary",)),
        name="ssd",
    )(*args)
    return (outs[0], outs[1], None, None) if has_h0 else tuple(outs)


def _lru_gate_weights(wa, ba, wx, bx):
    eye = 0.5 * jnp.eye(2, dtype=F32)

    def pair_blocks(w):
        w4 = w.reshape(4, 2, 64, 64)
        return jnp.einsum("laij,ab->laibj", w4, eye).reshape(4, LANES, LANES)

    mats = [pair_blocks(w) for w in (wa[0], wx[0], wa[1], wx[1])]
    wg = jnp.concatenate(mats, axis=-1).astype(BF16)
    bias = [0.5 * b.reshape(4, 1, LANES) for b in (ba[0], bx[0], ba[1], bx[1])]
    return wg, jnp.concatenate(bias, axis=-1)


def _trunk(x, mods, p, ctx):
    is_ctx = ctx is None
    bsz, seq, _ = x.shape
    f, q, k, v, *kv_t = _even_pre(x, mods[0], p["norm_mix"][0], p["w_in_even"], p["q_gain"], p["k_gain"],
                                  rope=not is_ctx)
    mix = _even_mixer(q, k, v, f, None if is_ctx else (ctx["k"], ctx["v"]))
    x = _post_ffn(x, mix, mods[0], p["w_out_even"], p["norm_ffn"][0], p["w1"], p["w3"], p["w2"], layer=0)
    gate, z, lru_a, lru_b, xbc, dt = _odd_pre(
        x, mods[1], p["norm_mix"][1], p["w_in_odd"], p["w_dt_odd"],
        lru=(p["conv_lru_w"], p["conv_lru_b"], p["lru_wg"], p["lru_bg"], p["lru_lambda"]))
    h0_lru, h0_ssd = (None, None) if is_ctx else (ctx["lru"], ctx["ssd"])
    yssd, ylru, ssd_state, lru_state = _ssd(xbc, dt, p["conv_ssd_w"], p["conv_ssd_b"], p["ssd_dtb"], p["ssd_alog"],
                                            p["ssd_dexp"], lru_a, lru_b, h0_ssd, h0_lru)
    x = _post_ffn(x, (ylru, gate, yssd, z), mods[1], p["w_out_odd"], p["norm_ffn"][1], p["w1"], p["w3"], p["w2"],
                  layer=1, ssd_norm=p["ssd_norm"])
    kt, vt = kv_t if is_ctx else (None, None)
    return x, kt, vt, lru_state, ssd_state


def kernel(x_prompt, x_sample, c, cache_k, cache_v, state_lru, state_ssd, c_ctx, w_ada, b_ada, norm_mix, norm_ffn,
           w_in_even, q_norm, k_norm, w_out_even, w_in_odd, conv_lru_w, conv_lru_b, lru_wa, lru_ba, lru_wx, lru_bx,
           lru_lambda, conv_ssd_w, conv_ssd_b, ssd_dt_bias, ssd_a_log, ssd_d, ssd_norm, w_out_odd, ffn_w1, ffn_w3,
           ffn_w2):
    bsz, seq, d = x_prompt.shape
    dec_b = x_sample.shape[0]
    past = cache_k.shape[2]

    cond_rows = 16
    cond = jnp.concatenate([c, c_ctx[None, :], jnp.zeros((cond_rows - dec_b - 1, d), F32)], axis=0)
    mod = _modulation(cond, w_ada, b_ada).reshape(2, cond_rows, 6, d)
    mods_sample = [mod[l, :dec_b] for l in range(2)]
    mods_prompt = [mod[l, dec_b:dec_b + 1] for l in range(2)]

    lru_wg, lru_bg = _lru_gate_weights(lru_wa[0], lru_ba[0], lru_wx[0], lru_bx[0])
    pad32 = lambda a: jnp.pad(a.reshape(1, 2 * SSD_HEADS), ((0, 0), (0, LANES - 2 * SSD_HEADS)))
    p = dict(
        norm_mix=norm_mix.reshape(2, 1, d), norm_ffn=norm_ffn.reshape(2, 1, d),
        w_in_even=w_in_even[0].astype(BF16),
        q_gain=jnp.tile(q_norm[0], N_Q_HEADS)[None, :], k_gain=jnp.tile(k_norm[0], N_KV_HEADS)[None, :],
        w_out_even=w_out_even[0].astype(BF16),
        w_in_odd=w_in_odd[0].astype(BF16),
        w_dt_odd=jnp.pad(w_in_odd[0, :, DT_LO:], ((0, 0), (0, LANES - 2 * SSD_HEADS))).astype(BF16),
        conv_lru_w=conv_lru_w[0], conv_lru_b=conv_lru_b[0][None, :],
        lru_wg=lru_wg, lru_bg=lru_bg, lru_lambda=lru_lambda[0],
        conv_ssd_w=conv_ssd_w[0], conv_ssd_b=conv_ssd_b[0][None, :],
        ssd_dtb=pad32(ssd_dt_bias[0]), ssd_alog=pad32(ssd_a_log[0]),
        ssd_dexp=jnp.repeat(ssd_d[0], SSD_INNER // SSD_HEADS)[None, :],
        ssd_norm=ssd_norm[0][None, :],
        w_out_odd=w_out_odd[0].astype(BF16),
        w1=ffn_w1.astype(BF16), w3=ffn_w3.astype(BF16), w2=ffn_w2.astype(BF16),
    )

    y_prompt, k_new, v_new, lru_new, ssd_new = _trunk(x_prompt, mods_prompt, p, None)
    ctx = dict(
        k=cache_k[:, 0].reshape(dec_b, past, KV_W), v=cache_v[:, 0].reshape(dec_b, past, KV_W),
        lru=state_lru[:, 0],
        ssd=state_ssd[:, 0].transpose(0, 1, 4, 2, 3).reshape(dec_b, 2, SSD_STATE, SSD_INNER),
    )
    y_sample, _, _, _, _ = _trunk(x_sample, mods_sample, p, ctx)

    new_k = k_new.transpose(0, 2, 1).reshape(bsz, 1, seq, N_KV_HEADS, HEAD_DIM)
    new_v = v_new.transpose(0, 2, 1).reshape(bsz, 1, seq, N_KV_HEADS, HEAD_DIM)
    new_lru = lru_new.reshape(bsz, 1, 2, LRU_W)
    new_ssd = (ssd_new.reshape(bsz, 2, SSD_STATE, SSD_HEADS, SSD_INNER // SSD_HEADS)
               .transpose(0, 1, 3, 4, 2).reshape(bsz, 1, 2, SSD_HEADS, SSD_INNER // SSD_HEADS, SSD_STATE))
    return (y_prompt, y_sample, new_k, new_v, new_lru, new_ssd)
```

```python
import functools
import math

import numpy as np
import jax
import jax.numpy as jnp
from jax import lax
from jax.experimental import pallas as pl
from jax.experimental.pallas import tpu as pltpu

F32 = jnp.float32
BF16 = jnp.bfloat16

D_MODEL = 1024
EPS = 1e-6
GRID_W = 64
HEAD_DIM = 64
N_Q_HEADS = 12
N_KV_HEADS = 4
FOURIER_W = 256
FOURIER_GROUP_W = 64
Q_W = N_Q_HEADS * HEAD_DIM
KV_W = N_KV_HEADS * HEAD_DIM
EVEN_IN = FOURIER_W + Q_W + 2 * KV_W
ROPE_THETA = 10000.0
AX_PAIRS = HEAD_DIM // 4
LRU_W = 512
LRU_C = 8.0
SSD_INNER = 1024
SSD_HEADS = 16
SSD_STATE = 64
SSD_CONV_CH = SSD_INNER + 4 * SSD_STATE
D_FF = 2816
LANES = 128
SUBLANES = 8
TOKEN_BLOCK = 512
ROW_TILE = 256
SSD_Q = 128
V_ROWS = HEAD_DIM + 16
VMEM_LIMIT = 56 * 1024 * 1024


def _params(sem, vmem=VMEM_LIMIT):
    return pltpu.CompilerParams(dimension_semantics=sem, vmem_limit_bytes=vmem)


def _const_spec(shape):
    nd = len(shape)
    return pl.BlockSpec(shape, lambda *_: (0,) * nd, pipeline_mode=pl.Buffered(1))


def _dot(a, b):
    return jnp.dot(a, b, preferred_element_type=F32)


def _dot_nt(a, b):
    return lax.dot_general(a, b, (((1,), (1,)), ((), ())), preferred_element_type=F32)


def _dot3(m, a):
    hi = a.astype(BF16)
    r1 = a - hi.astype(F32)
    mid = r1.astype(BF16)
    lo = (r1 - mid.astype(F32)).astype(BF16)
    return _dot(m, hi) + _dot(m, mid) + _dot(m, lo)


def _silu(x):
    return x * jax.nn.sigmoid(x)


def _softplus(x):
    return jnp.maximum(x, 0.0) + jnp.log1p(jnp.exp(-jnp.abs(x)))


def _gelu_tanh(x):
    c = math.sqrt(2.0 / math.pi)
    return 0.5 * x * (1.0 + jnp.tanh(c * (x + 0.044715 * (x * x * x))))


def _modnorm(x, g, shift, scale):
    ms = jnp.mean(x * x, axis=-1, keepdims=True)
    y = x * lax.rsqrt(ms + EPS) * g
    return y * (1.0 + scale) + shift


def _mod_kernel(cond_ref, w_ref, b_ref, o_ref):
    s = _silu(cond_ref[...])
    o_ref[0] = _dot(s.astype(BF16), w_ref[0].astype(BF16)) + b_ref[0]


def _modulation(cond, w_ada, b_ada):
    depth, d, n = w_ada.shape
    rows = cond.shape[0]
    tn = 3072
    return pl.pallas_call(
        _mod_kernel,
        grid=(depth, n // tn),
        in_specs=[pl.BlockSpec((rows, d), lambda l, j: (0, 0)),
                  pl.BlockSpec((1, d, tn), lambda l, j: (l, 0, j)),
                  pl.BlockSpec((1, 1, tn), lambda l, j: (l, 0, j))],
        out_specs=pl.BlockSpec((1, rows, tn), lambda l, j: (l, 0, j)),
        out_shape=jax.ShapeDtypeStruct((depth, rows, n), F32),
        compiler_params=_params(("arbitrary", "arbitrary")),
        name="modulation",
    )(cond, w_ada, b_ada.reshape(depth, 1, n))


def _mod_spec(mod):
    if mod.shape[0] == 1:
        return pl.BlockSpec((1, 6, D_MODEL), lambda b, i: (0, 0, 0))
    return pl.BlockSpec((1, 6, D_MODEL), lambda b, i: (b, 0, 0))


def _head_rmsnorm(x, gain):
    rows, width = x.shape
    low = lax.broadcasted_iota(jnp.int32, (rows, LANES), 1) < HEAD_DIM
    out = []
    for b in range(width // LANES):
        blk = x[:, b * LANES:(b + 1) * LANES]
        sq = blk * blk
        lo = jnp.sum(jnp.where(low, sq, 0.0), axis=-1, keepdims=True)
        hi = jnp.sum(jnp.where(low, 0.0, sq), axis=-1, keepdims=True)
        ms = jnp.where(low, lo, hi) * (1.0 / HEAD_DIM)
        out.append(blk * lax.rsqrt(ms + EPS) * gain[:, b * LANES:(b + 1) * LANES])
    return jnp.concatenate(out, axis=-1)


def _rope(x, cos, sin_signed):
    rows, width = x.shape
    lane = lax.broadcasted_iota(jnp.int32, (rows, LANES), 1)
    second = (lane % (2 * AX_PAIRS)) >= AX_PAIRS
    out = []
    for b in range(width // LANES):
        blk = x[:, b * LANES:(b + 1) * LANES]
        partner = jnp.where(second, pltpu.roll(blk, AX_PAIRS, 1), pltpu.roll(blk, LANES - AX_PAIRS, 1))
        out.append(blk * cos + partner * sin_signed)
    return jnp.concatenate(out, axis=-1)


def _even_pre_kernel(*refs, rope):
    if rope:
        x_ref, mod_ref, g_ref, w_ref, qg_ref, kg_ref, cos_ref, sin_ref, f_ref, q_ref, k_ref, v_ref = refs
    else:
        x_ref, mod_ref, g_ref, w_ref, qg_ref, kg_ref, f_ref, q_ref, k_ref, v_ref, kt_ref, vt_ref = refs
    h = _modnorm(x_ref[0], g_ref[...], mod_ref[0, 0:1, :], mod_ref[0, 1:2, :])
    p = _dot(h.astype(BF16), w_ref[...])
    f_ref[0] = p[:, :FOURIER_W]
    q = _head_rmsnorm(p[:, FOURIER_W:FOURIER_W + Q_W], qg_ref[...])
    k = _head_rmsnorm(p[:, FOURIER_W + Q_W:FOURIER_W + Q_W + KV_W], kg_ref[...])
    if rope:
        q = _rope(q, cos_ref[...], sin_ref[...])
        k = _rope(k, cos_ref[...], sin_ref[...])
    q_ref[0] = (q * (HEAD_DIM ** -0.5 * math.log2(math.e))).astype(BF16)
    v = p[:, FOURIER_W + Q_W + KV_W:]
    k_ref[0] = k
    v_ref[0] = v
    if not rope:
        kt_ref[0] = k.T
        vt_ref[0] = v.T


def _rope_tables(seq):
    pos = np.arange(seq)
    freqs = ROPE_THETA ** (-np.arange(AX_PAIRS, dtype=np.float64) / AX_PAIRS)
    ang = np.zeros((seq, HEAD_DIM))
    sign = np.zeros((HEAD_DIM,))
    for a, p_a in enumerate((pos // GRID_W, pos % GRID_W)):
        for t in range(2):
            lo = a * 2 * AX_PAIRS + t * AX_PAIRS
            ang[:, lo:lo + AX_PAIRS] = p_a[:, None] * freqs[None, :]
            sign[lo:lo + AX_PAIRS] = -1.0 if t == 0 else 1.0
    cos = np.tile(np.cos(ang), (1, 2)).astype(np.float32)
    sin = np.tile(np.sin(ang) * sign[None, :], (1, 2)).astype(np.float32)
    return jnp.asarray(cos), jnp.asarray(sin)


def _even_pre(x, mod, g, w_in, qg, kg, rope):
    bsz, seq, d = x.shape
    tm = ROW_TILE
    tok = lambda w: pl.BlockSpec((1, tm, w), lambda b, i: (b, i, 0))
    in_specs = [tok(d), _mod_spec(mod), _const_spec((1, d)), _const_spec((d, EVEN_IN)),
                _const_spec((1, Q_W)), _const_spec((1, KV_W))]
    args = [x, mod, g, w_in, qg, kg]
    if rope:
        cos, sin = _rope_tables(seq)
        in_specs += [pl.BlockSpec((tm, LANES), lambda b, i: (i, 0))] * 2
        args += [cos, sin]
    out_specs = [tok(FOURIER_W), tok(Q_W), tok(KV_W), tok(KV_W)]
    out_shape = [jax.ShapeDtypeStruct((bsz, seq, FOURIER_W), F32),
                 jax.ShapeDtypeStruct((bsz, seq, Q_W), BF16),
                 jax.ShapeDtypeStruct((bsz, seq, KV_W), F32),
                 jax.ShapeDtypeStruct((bsz, seq, KV_W), F32)]
    if not rope:
        out_specs += [pl.BlockSpec((1, KV_W, tm), lambda b, i: (b, 0, i))] * 2
        out_shape += [jax.ShapeDtypeStruct((bsz, KV_W, seq), F32)] * 2
    return pl.pallas_call(
        functools.partial(_even_pre_kernel, rope=rope),
        grid=(bsz, seq // tm),
        in_specs=in_specs,
        out_specs=out_specs,
        out_shape=out_shape,
        compiler_params=_params(("arbitrary", "arbitrary")),
        name="even_pre",
    )(*args)


def _attn_kernel(*refs, seq, has_ctx):
    if has_ctx:
        q_ref, k_ref, v_ref, ck_ref, cv_ref, f_ref, cs_ref, cbd_ref, o_ref, kop, vopt, g_scr, cs_scr = refs
    else:
        q_ref, k_ref, v_ref, f_ref, cs_ref, cbd_ref, o_ref, kop, vopt, g_scr, cs_scr = refs
    tq = q_ref.shape[1]

    @pl.when((pl.program_id(0) == 0) & (pl.program_id(1) == 0))
    def _round_tables():
        for r0 in range(0, seq, tq):
            cs_scr[r0:r0 + tq, :] = cs_ref[r0:r0 + tq, :].astype(BF16)

    @pl.when(pl.program_id(1) == 0)
    def _prepare():
        def put(kx, vx, r0, transposed):
            n = kx.shape[1] if transposed else kx.shape[0]
            low = lax.broadcasted_iota(jnp.int32, (n, LANES), 1) < HEAD_DIM
            ones = jnp.ones((V_ROWS - HEAD_DIM, n), BF16)
            for jp in range(N_KV_HEADS // 2):
                kp = kx[jp * LANES:(jp + 1) * LANES, :].T if transposed else kx[:, jp * LANES:(jp + 1) * LANES]
                k_lo = jnp.where(low, kp, 0.0)
                k_hi = jnp.where(low, 0.0, kp)
                placed = (k_lo, pltpu.roll(k_lo, HEAD_DIM, 1), pltpu.roll(k_hi, HEAD_DIM, 1), k_hi)
                for t, kk in enumerate(placed):
                    kop[4 * jp + t, r0:r0 + n, :] = kk.astype(BF16)
                vt = vx[jp * LANES:(jp + 1) * LANES, :] if transposed else vx[:, jp * LANES:(jp + 1) * LANES].T
                vt = vt.astype(BF16)
                for t in range(2):
                    vopt[2 * jp + t, 0:HEAD_DIM, r0:r0 + n] = vt[t * HEAD_DIM:(t + 1) * HEAD_DIM]
                    vopt[2 * jp + t, HEAD_DIM:V_ROWS, r0:r0 + n] = ones

        put(k_ref[0], v_ref[0], 0, False)
        if has_ctx:
            put(ck_ref[0], cv_ref[0], seq, True)
        f = f_ref[0].astype(BF16)
        g_scr[0:seq, :] = _dot(f, cbd_ref[0].astype(BF16)).astype(BF16)
        g_scr[seq:2 * seq, :] = _dot(f, cbd_ref[1].astype(BF16)).astype(BF16)

    r0 = pl.multiple_of(pl.program_id(1) * tq, tq)
    o_ref[0, :, 0:FOURIER_W] = _dot(cs_scr[pl.ds(r0, tq), :], g_scr[...]).astype(BF16)
    q = q_ref[0]
    n_pairs = N_Q_HEADS // 2
    kv_of = lambda qp: [(2 * qp + half) // (N_Q_HEADS // N_KV_HEADS) for half in range(2)]

    def pair_scores(qp):
        qpair = q[:, qp * LANES:(qp + 1) * LANES]
        return [_dot_nt(qpair, kop[2 * kv_of(qp)[half] + half]) for half in range(2)]

    scores_next = pair_scores(0)
    for qp in range(n_pairs):
        kv = kv_of(qp)
        scores = scores_next
        if qp + 1 < n_pairs:
            scores_next = pair_scores(qp + 1)
        probs = [jnp.exp2(s - jnp.max(s, axis=-1, keepdims=True)).astype(BF16) for s in scores]
        outs = []
        for half in range(2):
            o = _dot_nt(vopt[kv[half]], probs[half])
            outs.append(o[0:HEAD_DIM] / o[HEAD_DIM:HEAD_DIM + 1])
        res = jnp.concatenate(outs, axis=0)
        o_ref[0, :, FOURIER_W + qp * LANES:FOURIER_W + (qp + 1) * LANES] = res.T.astype(BF16)


def _dft_tables(seq):
    s = np.arange(seq)
    ang = 2.0 * np.pi * ((s[:, None] * s[None, :]) % seq) / seq
    pos = np.concatenate([np.cos(ang), -np.sin(ang)], axis=1) / math.sqrt(seq)
    c = np.arange(FOURIER_GROUP_W)
    ang_c = 2.0 * np.pi * ((c[:, None] * c[None, :]) % FOURIER_GROUP_W) / FOURIER_GROUP_W
    eye = np.eye(FOURIER_W // FOURIER_GROUP_W)
    chan = np.stack([np.kron(eye, np.cos(ang_c)), np.kron(eye, np.sin(ang_c))]) / math.sqrt(FOURIER_GROUP_W)
    return jnp.asarray(pos, dtype=F32), jnp.asarray(chan, dtype=F32)


def _even_mixer(q, k, v, f, ctx):
    bsz, seq, _ = q.shape
    tq = ROW_TILE
    has_ctx = ctx is not None
    past = ctx[0].shape[2] if has_ctx else 0
    kt = seq + past
    pos_dft, chan_dft = _dft_tables(seq)
    full = lambda n, w: pl.BlockSpec((1, n, w), lambda b, i: (b, 0, 0))
    in_specs = [pl.BlockSpec((1, tq, Q_W), lambda b, i: (b, i, 0)), full(seq, KV_W), full(seq, KV_W)]
    args = [q, k, v]
    if has_ctx:
        in_specs += [full(KV_W, past), full(KV_W, past)]
        args += list(ctx)
    in_specs += [full(seq, FOURIER_W), _const_spec((seq, 2 * seq)),
                 _const_spec((2, FOURIER_W, FOURIER_W))]
    args += [f, pos_dft, chan_dft]
    return pl.pallas_call(
        functools.partial(_attn_kernel, seq=seq, has_ctx=has_ctx),
        grid=(bsz, seq // tq),
        in_specs=in_specs,
        out_specs=pl.BlockSpec((1, tq, D_MODEL), lambda b, i: (b, i, 0)),
        out_shape=jax.ShapeDtypeStruct((bsz, seq, D_MODEL), BF16),
        scratch_shapes=[pltpu.VMEM((2 * N_KV_HEADS, kt, LANES), BF16),
                        pltpu.VMEM((N_KV_HEADS, V_ROWS, kt), BF16),
                        pltpu.VMEM((2 * seq, FOURIER_W), BF16),
                        pltpu.VMEM((seq, 2 * seq), BF16)],
        compiler_params=_params(("arbitrary", "arbitrary")),
        name="even_mixer",
    )(*args)


def _post_ffn_kernel(*refs, odd):
    if odd:
        (x_ref, ylru_ref, g_ref, yssd_ref, z_ref, sn_ref, mod_ref, wout_ref, gf_ref,
         w1_ref, w3_ref, w2_ref, o_ref) = refs
        y_lru = ylru_ref[0] * _gelu_tanh(g_ref[0])
        y = yssd_ref[0] * _silu(z_ref[0])
        ms = jnp.mean(y * y, axis=-1, keepdims=True)
        y_ssd = y * lax.rsqrt(ms + EPS) * sn_ref[...]
        mix = jnp.concatenate([y_lru, y_ssd], axis=-1).astype(BF16)
    else:
        x_ref, mix_ref, mod_ref, wout_ref, gf_ref, w1_ref, w3_ref, w2_ref, o_ref = refs
        mix = mix_ref[0]
    x = x_ref[0] + mod_ref[0, 2:3, :] * _dot(mix, wout_ref[...])
    h = _modnorm(x, gf_ref[...], mod_ref[0, 3:4, :], mod_ref[0, 4:5, :]).astype(BF16)
    a = (_silu(_dot(h, w1_ref[...])) * _dot(h, w3_ref[...])).astype(BF16)
    o_ref[0] = x + mod_ref[0, 5:6, :] * _dot(a, w2_ref[...])


def _post_ffn(x, mix_args, mod, w_out, g_ffn, w1, w3, w2, layer, ssd_norm=None):
    bsz, seq, d = x.shape
    if mod.shape[0] == 1 and seq < TOKEN_BLOCK:
        fold = TOKEN_BLOCK // seq
        merge = lambda t: t.reshape(bsz // fold, fold * seq, t.shape[-1])
        out = _post_ffn(merge(x), jax.tree.map(merge, mix_args), mod, w_out, g_ffn, w1, w3, w2, layer, ssd_norm)
        return out.reshape(bsz, seq, d)
    tm = min(seq, TOKEN_BLOCK)
    odd = layer % 2 == 1
    tok = lambda w: pl.BlockSpec((1, tm, w), lambda b, i: (b, i, 0))
    slab = lambda w: pl.BlockSpec((None,) + w.shape[1:], lambda b, i: (layer, 0, 0), pipeline_mode=pl.Buffered(1))
    if odd:
        ylru, g, yssd, z = mix_args
        in_specs = [tok(d), tok(LRU_W), tok(LRU_W), tok(SSD_INNER), tok(SSD_INNER), _const_spec((1, SSD_INNER))]
        args = [x, ylru, g, yssd, z, ssd_norm]
    else:
        in_specs = [tok(d), tok(d)]
        args = [x, mix_args]
    in_specs += [_mod_spec(mod), _const_spec(w_out.shape), _const_spec((1, d)),
                 slab(w1), slab(w3), slab(w2)]
    args += [mod, w_out, g_ffn, w1, w3, w2]
    return pl.pallas_call(
        functools.partial(_post_ffn_kernel, odd=odd),
        grid=(bsz, seq // tm),
        in_specs=in_specs,
        out_specs=tok(d),
        out_shape=jax.ShapeDtypeStruct((bsz, seq, d), F32),
        compiler_params=_params(("arbitrary", "arbitrary")),
        name="post_ffn_odd" if odd else "post_ffn_even",
    )(*args)


XL_LO = LRU_W
Z_LO = 2 * LRU_W
XBC_LO = 2 * LRU_W + SSD_INNER
DT_LO = XBC_LO + SSD_CONV_CH


def _conv_block(main, edges, w, b):
    before, after0, after1 = edges
    rows = main.shape[0]
    row = lax.broadcasted_iota(jnp.int32, (SUBLANES, 1), 0)
    last = rows - SUBLANES

    def patch_first(x, fix):
        return jnp.concatenate([fix(x[:SUBLANES]), x[SUBLANES:]], axis=0)

    def patch_last(x, fix):
        return jnp.concatenate([x[:last], fix(x[last:])], axis=0)

    xm1 = patch_first(pltpu.roll(main, 1, 0), lambda t: jnp.where(row == 0, before, t))
    xp1 = patch_last(pltpu.roll(main, rows - 1, 0), lambda t: jnp.where(row == SUBLANES - 1, after0, t))
    xp2 = patch_last(pltpu.roll(main, rows - 2, 0),
                     lambda t: jnp.where(row == SUBLANES - 2, after0, jnp.where(row == SUBLANES - 1, after1, t)))
    return xm1 * w[0:1] + main * w[1:2] + xp1 * w[2:3] + xp2 * w[3:4] + b


def _odd_pre_kernel(x_ref, xprev_ref, xnext_ref, mod_ref, g_ref, w_ref, wdt_ref, cwl_ref, cbl_ref, wg_ref, bg_ref,
                    lam_ref, gate_ref, z_ref, a_ref, b_ref, xbc_ref, dt_ref):
    i = pl.program_id(1)
    g, shift, scale = g_ref[...], mod_ref[0, 0:1, :], mod_ref[0, 1:2, :]
    h = _modnorm(x_ref[0], g, shift, scale).astype(BF16)
    wide = 2 * LANES
    proj = lambda lo: _dot(h, w_ref[:, lo:lo + wide])
    p_xl = [proj(XL_LO + t * wide) for t in range(LRU_W // wide)]
    others = ([(gate_ref, t * wide, t * wide) for t in range(LRU_W // wide)]
              + [(z_ref, Z_LO + t * wide, t * wide) for t in range(SSD_INNER // wide)]
              + [(xbc_ref, XBC_LO + t * wide, t * wide) for t in range(SSD_CONV_CH // wide)])

    halo = jnp.concatenate([xprev_ref[0], xnext_ref[0]], axis=0)
    ph = _dot(_modnorm(halo, g, shift, scale).astype(BF16), w_ref[:, XL_LO:Z_LO])
    has_prev = i > 0
    has_next = i < pl.num_programs(1) - 1
    e_l = (jnp.where(has_prev, ph[SUBLANES - 1:SUBLANES], 0.0),
           jnp.where(has_next, ph[SUBLANES:SUBLANES + 1], 0.0),
           jnp.where(has_next, ph[SUBLANES + 1:SUBLANES + 2], 0.0))

    rate = (0.5 * LRU_C) * _softplus(-lam_ref[...])
    n_lru = LRU_W // LANES
    for lb in range(n_lru):
        sl = slice(lb * LANES, (lb + 1) * LANES)
        for dst, lo, col in others[lb * len(others) // n_lru:(lb + 1) * len(others) // n_lru]:
            dst[0, :, col:col + wide] = proj(lo)
        xb = _conv_block(p_xl[lb // 2][:, (lb % 2) * LANES:(lb % 2 + 1) * LANES], [e[:, sl] for e in e_l],
                         cwl_ref[:, sl], cbl_ref[:, sl])
        xh = 0.5 * xb
        gt = jnp.tanh(_dot(xb.astype(BF16), wg_ref[lb]) + bg_ref[lb])
        for d in range(2):
            neg_log_a = rate[d:d + 1, sl] * (gt[:, 2 * d * LANES:(2 * d + 1) * LANES] + 1.0)
            a = jnp.exp2(neg_log_a * (-math.log2(math.e)))
            g2 = jnp.tanh(neg_log_a) * (a * a + 1.0)
            root = jnp.where(g2 > 0.0, g2 * lax.rsqrt(g2), 0.0)
            a_ref[0, d, :, sl] = a
            b_ref[0, d, :, sl] = root * ((gt[:, (2 * d + 1) * LANES:(2 * d + 2) * LANES] + 1.0) * xh)
    dt_ref[0] = _dot(h, wdt_ref[...])


def _odd_pre(x, mod, g, w_in, w_dt, lru):
    bsz, seq, d = x.shape
    tm = min(seq, TOKEN_BLOCK)
    per8 = tm // SUBLANES
    tok = lambda w: pl.BlockSpec((1, tm, w), lambda b, i: (b, i, 0))
    both = pl.BlockSpec((1, 2, tm, LRU_W), lambda b, i: (b, 0, i, 0))
    consts = list(lru)
    out_specs = [tok(LRU_W), tok(SSD_INNER), both, both, tok(SSD_CONV_CH), tok(LANES)]
    row = lambda w: jax.ShapeDtypeStruct((bsz, seq, w), F32)
    out_shape = ([row(LRU_W), row(SSD_INNER)] + [jax.ShapeDtypeStruct((bsz, 2, seq, LRU_W), F32)] * 2
                 + [row(SSD_CONV_CH), row(LANES)])
    return pl.pallas_call(
        _odd_pre_kernel,
        grid=(bsz, seq // tm),
        in_specs=[tok(d),
                  pl.BlockSpec((1, SUBLANES, d), lambda b, i: (b, jnp.maximum(i * per8 - 1, 0), 0)),
                  pl.BlockSpec((1, SUBLANES, d), lambda b, i: (b, jnp.minimum((i + 1) * per8, seq // SUBLANES - 1), 0)),
                  _mod_spec(mod), _const_spec((1, d)), _const_spec(w_in.shape), _const_spec(w_dt.shape)]
                 + [_const_spec(c.shape) for c in consts],
        out_specs=out_specs,
        out_shape=out_shape,
        compiler_params=_params(("arbitrary", "arbitrary")),
        name="odd_pre",
    )(x, x, x, mod, g, w_in, w_dt, *consts)


def _ssd_kernel(*refs, seq, has_h0):
    xbc_ref, dt_ref, cw_ref, cb_ref, dtb_ref, alog_ref, dexp_ref, ef_ref, eb_ref, la_ref, lb_ref = refs[:11]
    if has_h0:
        h0_ref, lh0_ref, y_ref, ylru_ref = refs[11:15]
    else:
        y_ref, ylru_ref, hout_ref, lhout_ref = refs[11:15]
    xs_ref, bm_ref, cm_ref, cs_ref, dts_ref, cst_ref, dtt_ref, bt_ref, st_scr = refs[15:]
    q = SSD_Q
    nc = seq // q
    lane = lax.broadcasted_iota(jnp.int32, (q, LANES), 1)
    qi = lax.broadcasted_iota(jnp.int32, (q, q), 0)
    ki = lax.broadcasted_iota(jnp.int32, (q, q), 1)
    tri_lower = jnp.where(ki <= qi, 1.0, 0.0).astype(BF16)
    tri_upper = jnp.where(ki >= qi, 1.0, 0.0).astype(BF16)
    a_log2 = jnp.where(lane[0:1, :] < 2 * SSD_HEADS, -math.log2(math.e) * jnp.exp(alog_ref[...]), 0.0)

    def prepare(c):
        r0 = pl.multiple_of(c * q, q)
        rows = pl.ds(r0, q)
        prev8 = pl.ds(pl.multiple_of(jnp.maximum(r0 - SUBLANES, 0), SUBLANES), SUBLANES)
        next8 = pl.ds(pl.multiple_of(jnp.minimum(r0 + q, seq - SUBLANES), SUBLANES), SUBLANES)
        has_prev = r0 > 0
        has_next = r0 + q < seq
        for lb in range(SSD_CONV_CH // LANES):
            sl = slice(lb * LANES, (lb + 1) * LANES)
            before = xbc_ref[0, prev8, sl]
            after = xbc_ref[0, next8, sl]
            edges = (jnp.where(has_prev, before[SUBLANES - 1:SUBLANES], 0.0),
                     jnp.where(has_next, after[0:1], 0.0), jnp.where(has_next, after[1:2], 0.0))
            xc = _conv_block(xbc_ref[0, rows, sl], edges, cw_ref[:, sl], cb_ref[:, sl])
            half_xc = 0.5 * xc
            xc = half_xc * (jnp.tanh(half_xc) + 1.0)
            if lb < SSD_INNER // LANES:
                xs_ref[rows, sl] = xc
                y_ref[0, rows, sl] = dexp_ref[:, sl] * xc
            elif lb == SSD_INNER // LANES:
                bm_ref[rows, :] = xc
                bt_ref[c] = xc.T
            else:
                cm_ref[rows, :] = xc
        dts = _softplus(dt_ref[0, rows, :] + dtb_ref[...])
        da = dts * a_log2
        cs = jnp.where(lane < SSD_HEADS, _dot3(tri_lower, da), _dot3(tri_upper, da))
        cs_ref[rows, :] = cs
        dts_ref[rows, :] = dts
        cst_ref[c] = cs.T
        dtt_ref[c] = dts.T

    st_scr[...] = h0_ref[0] if has_h0 else jnp.zeros(st_scr.shape, F32)

    low = lane < SSD_STATE
    col = lax.broadcasted_iota(jnp.int32, (SSD_STATE, SSD_INNER), 1)
    first_group = col < SSD_INNER // 2

    def chunk(c, fwd):
        d = 0 if fwd else 1
        r0 = pl.multiple_of(c * q, q)
        rows = pl.ds(r0, q)
        cs = cs_ref[rows, :]
        dts = dts_ref[rows, :]
        cst = cst_ref[c]
        dtt = dtt_ref[c]
        xs = xs_ref[rows, :]
        xsb = xs.astype(BF16)
        bb = bm_ref[rows, :].astype(BF16)
        cb_ = cm_ref[rows, :].astype(BF16)
        zero = jnp.zeros_like(cb_)
        cbg = [_dot_nt(jnp.where(low, cb_, zero), bb).astype(BF16),
               _dot_nt(jnp.where(low, zero, cb_), bb).astype(BF16)]
        mask = (ki <= qi) if fwd else (ki >= qi)
        expand = ef_ref[...] if fwd else eb_ref[...]
        st = st_scr[d]
        hh = jnp.concatenate([jnp.where(first_group, st, 0.0), jnp.where(first_group, 0.0, st)], axis=0)
        tot = cs[q - 1:q, :] if fwd else cs[0:1, :]
        mine = (lane >= d * SSD_HEADS) & (lane < (d + 1) * SSD_HEADS)
        w = jnp.exp2(jnp.where(mine, tot - cs, 0.0)) * dts
        dec = jnp.broadcast_to(jnp.exp2(tot), (2 * SUBLANES, LANES))
        dec_hi = dec.astype(BF16)
        dec_lo = (dec - dec_hi.astype(F32)).astype(BF16)
        spread = _dot(jnp.concatenate([jnp.exp2(cs).astype(BF16), w.astype(BF16), dec_hi, dec_lo], axis=0), expand)
        y = _dot(cb_, hh.astype(BF16)) * spread[0:q]
        pairs = []
        for pp in range(SSD_HEADS // 2):
            ms = []
            for h in (2 * pp, 2 * pp + 1):
                l = d * SSD_HEADS + h
                decay = jnp.exp2((cs[:, l:l + 1] - cst[l:l + 1, :]).astype(BF16))
                decay = jnp.where(mask, decay, jnp.zeros_like(decay))
                ms.append(cbg[pp // (SSD_HEADS // 4)] * decay * dtt[l:l + 1, :].astype(BF16))
            xp = xsb[:, pp * LANES:(pp + 1) * LANES]
            zx = jnp.zeros_like(xp)
            x2 = jnp.concatenate([jnp.where(low, xp, zx), jnp.where(low, zx, xp)], axis=0)
            pairs.append(_dot(jnp.concatenate(ms, axis=1), x2))
        y_ref[0, rows, :] += y + jnp.concatenate(pairs, axis=1)
        xw = (xs * spread[q:2 * q]).astype(BF16)
        bt = bt_ref[c].astype(BF16)
        half_w = SSD_INNER // 2
        st_new = jnp.concatenate([_dot(bt[:SSD_STATE], xw[:, :half_w]), _dot(bt[SSD_STATE:], xw[:, half_w:])], axis=1)
        chunk_decay = spread[2 * q:2 * q + 1] + spread[2 * q + 2 * SUBLANES:2 * q + 2 * SUBLANES + 1]
        st_scr[d] = chunk_decay * st + st_new

    def prepare_all(c, carry):
        prepare(c)
        return carry

    nb = seq // SUBLANES
    per_chunk = q // SUBLANES
    row = lax.broadcasted_iota(jnp.int32, (SUBLANES, LRU_W), 0)

    def lru_block(blk, hf, hb):
        rf = pl.multiple_of(blk * SUBLANES, SUBLANES)
        rb = pl.multiple_of((nb - 1 - blk) * SUBLANES, SUBLANES)
        a = la_ref[0, 0, pl.ds(rf, SUBLANES), :]
        b = lb_ref[0, 0, pl.ds(rf, SUBLANES), :]
        for k in (1, 2, 4):
            keep = row >= k
            b = a * jnp.where(keep, pltpu.roll(b, k, 0), 0.0) + b
            a = a * jnp.where(keep, pltpu.roll(a, k, 0), 1.0)
        h = a * hf + b
        ylru_ref[0, pl.ds(rf, SUBLANES), :] += h
        hf = h[SUBLANES - 1:SUBLANES, :]
        a = la_ref[0, 1, pl.ds(rb, SUBLANES), :]
        b = lb_ref[0, 1, pl.ds(rb, SUBLANES), :]
        for k in (1, 2, 4):
            keep = row < SUBLANES - k
            b = a * jnp.where(keep, pltpu.roll(b, SUBLANES - k, 0), 0.0) + b
            a = a * jnp.where(keep, pltpu.roll(a, SUBLANES - k, 0), 1.0)
        h = a * hb + b
        ylru_ref[0, pl.ds(rb, SUBLANES), :] += h
        return hf, h[0:1, :]

    def both_directions(i, carry):
        chunk(i, True)
        chunk(nc - 1 - i, False)
        hf, hb = carry
        for j in range(per_chunk):
            hf, hb = lru_block(i * per_chunk + j, hf, hb)
        return hf, hb

    ylru_ref[0] = jnp.zeros((seq, LRU_W), F32)
    lax.fori_loop(0, nc, prepare_all, 0)
    if has_h0:
        lru_start = (lh0_ref[0, 0:1, :], lh0_ref[0, 1:2, :])
    else:
        lru_start = (jnp.zeros((1, LRU_W), F32), jnp.zeros((1, LRU_W), F32))
    hf, hb = lax.fori_loop(0, nc, both_directions, lru_start)
    if not has_h0:
        hout_ref[0] = st_scr[...]
        lhout_ref[0, 0:1, :] = hf
        lhout_ref[0, 1:2, :] = hb


def _ssd(xbc, dt, cw, cb, dtb, alog, dexp, lru_a, lru_b, h0, lru_h0):
    bsz, seq, _ = xbc.shape
    nc = seq // SSD_Q
    expand = np.zeros((2, LANES, SSD_INNER), np.float32)
    for d in range(2):
        for h in range(SSD_HEADS):
            expand[d, d * SSD_HEADS + h, h * 64:(h + 1) * 64] = 1.0
    expand = jnp.asarray(expand, dtype=BF16)
    per_seq = lambda n, w: pl.BlockSpec((1, n, w), lambda b: (b, 0, 0))
    state_spec = pl.BlockSpec((1, 2, SSD_STATE, SSD_INNER), lambda b: (b, 0, 0, 0))
    has_h0 = h0 is not None
    coef = pl.BlockSpec((1, 2, seq, LRU_W), lambda b: (b, 0, 0, 0))
    in_specs = [per_seq(seq, SSD_CONV_CH), per_seq(seq, LANES), _const_spec(cw.shape), _const_spec(cb.shape),
                _const_spec(dtb.shape), _const_spec(alog.shape), _const_spec(dexp.shape),
                _const_spec((LANES, SSD_INNER)), _const_spec((LANES, SSD_INNER)), coef, coef]
    args = [xbc, dt, cw, cb, dtb, alog, dexp, expand[0], expand[1], lru_a, lru_b]
    out_specs = [per_seq(seq, SSD_INNER), per_seq(seq, LRU_W)]
    out_shape = [jax.ShapeDtypeStruct((bsz, seq, SSD_INNER), F32), jax.ShapeDtypeStruct((bsz, seq, LRU_W), F32)]
    if has_h0:
        in_specs += [state_spec, per_seq(2, LRU_W)]
        args += [h0, lru_h0]
    else:
        out_specs += [state_spec, per_seq(2, LRU_W)]
        out_shape += [jax.ShapeDtypeStruct((bsz, 2, SSD_STATE, SSD_INNER), F32),
                      jax.ShapeDtypeStruct((bsz, 2, LRU_W), F32)]
    outs = pl.pallas_call(
        functools.partial(_ssd_kernel, seq=seq, has_h0=has_h0),
        grid=(bsz,),
        in_specs=in_specs,
        out_specs=out_specs,
        out_shape=out_shape,
        scratch_shapes=[pltpu.VMEM((seq, SSD_INNER), F32)] + [pltpu.VMEM((seq, LANES), F32)] * 4
                       + [pltpu.VMEM((nc, LANES, SSD_Q), F32)] * 3 + [pltpu.VMEM((2, SSD_STATE, SSD_INNER), F32)],
        compiler_params=_params(("arbitrary",)),
        name="ssd",
    )(*args)
    return (outs[0], outs[1], None, None) if has_h0 else tuple(outs)


def _lru_gate_weights(wa, ba, wx, bx):
    eye = 0.5 * jnp.eye(2, dtype=F32)

    def pair_blocks(w):
        w4 = w.reshape(4, 2, 64, 64)
        return jnp.einsum("laij,ab->laibj", w4, eye).reshape(4, LANES, LANES)

    mats = [pair_blocks(w) for w in (wa[0], wx[0], wa[1], wx[1])]
    wg = jnp.concatenate(mats, axis=-1).astype(BF16)
    bias = [0.5 * b.reshape(4, 1, LANES) for b in (ba[0], bx[0], ba[1], bx[1])]
    return wg, jnp.concatenate(bias, axis=-1)


def _trunk(x, mods, p, ctx):
    is_ctx = ctx is None
    bsz, seq, _ = x.shape
    f, q, k, v, *kv_t = _even_pre(x, mods[0], p["norm_mix"][0], p["w_in_even"], p["q_gain"], p["k_gain"],
                                  rope=not is_ctx)
    mix = _even_mixer(q, k, v, f, None if is_ctx else (ctx["k"], ctx["v"]))
    x = _post_ffn(x, mix, mods[0], p["w_out_even"], p["norm_ffn"][0], p["w1"], p["w3"], p["w2"], layer=0)
    gate, z, lru_a, lru_b, xbc, dt = _odd_pre(
        x, mods[1], p["norm_mix"][1], p["w_in_odd"], p["w_dt_odd"],
        lru=(p["conv_lru_w"], p["conv_lru_b"], p["lru_wg"], p["lru_bg"], p["lru_lambda"]))
    h0_lru, h0_ssd = (None, None) if is_ctx else (ctx["lru"], ctx["ssd"])
    yssd, ylru, ssd_state, lru_state = _ssd(xbc, dt, p["conv_ssd_w"], p["conv_ssd_b"], p["ssd_dtb"], p["ssd_alog"],
                                            p["ssd_dexp"], lru_a, lru_b, h0_ssd, h0_lru)
    x = _post_ffn(x, (ylru, gate, yssd, z), mods[1], p["w_out_odd"], p["norm_ffn"][1], p["w1"], p["w3"], p["w2"],
                  layer=1, ssd_norm=p["ssd_norm"])
    kt, vt = kv_t if is_ctx else (None, None)
    return x, kt, vt, lru_state, ssd_state


def kernel(x_prompt, x_sample, c, cache_k, cache_v, state_lru, state_ssd, c_ctx, w_ada, b_ada, norm_mix, norm_ffn,
           w_in_even, q_norm, k_norm, w_out_even, w_in_odd, conv_lru_w, conv_lru_b, lru_wa, lru_ba, lru_wx, lru_bx,
           lru_lambda, conv_ssd_w, conv_ssd_b, ssd_dt_bias, ssd_a_log, ssd_d, ssd_norm, w_out_odd, ffn_w1, ffn_w3,
           ffn_w2):
    bsz, seq, d = x_prompt.shape
    dec_b = x_sample.shape[0]
    past = cache_k.shape[2]

    cond_rows = 16
    cond = jnp.concatenate([c, c_ctx[None, :], jnp.zeros((cond_rows - dec_b - 1, d), F32)], axis=0)
    mod = _modulation(cond, w_ada, b_ada).reshape(2, cond_rows, 6, d)
    mods_sample = [mod[l, :dec_b] for l in range(2)]
    mods_prompt = [mod[l, dec_b:dec_b + 1] for l in range(2)]

    lru_wg, lru_bg = _lru_gate_weights(lru_wa[0], lru_ba[0], lru_wx[0], lru_bx[0])
    pad32 = lambda a: jnp.pad(a.reshape(1, 2 * SSD_HEADS), ((0, 0), (0, LANES - 2 * SSD_HEADS)))
    p = dict(
        norm_mix=norm_mix.reshape(2, 1, d), norm_ffn=norm_ffn.reshape(2, 1, d),
        w_in_even=w_in_even[0].astype(BF16),
        q_gain=jnp.tile(q_norm[0], N_Q_HEADS)[None, :], k_gain=jnp.tile(k_norm[0], N_KV_HEADS)[None, :],
        w_out_even=w_out_even[0].astype(BF16),
        w_in_odd=w_in_odd[0].astype(BF16),
        w_dt_odd=jnp.pad(w_in_odd[0, :, DT_LO:], ((0, 0), (0, LANES - 2 * SSD_HEADS))).astype(BF16),
        conv_lru_w=conv_lru_w[0], conv_lru_b=conv_lru_b[0][None, :],
        lru_wg=lru_wg, lru_bg=lru_bg, lru_lambda=lru_lambda[0],
        conv_ssd_w=conv_ssd_w[0], conv_ssd_b=conv_ssd_b[0][None, :],
        ssd_dtb=pad32(ssd_dt_bias[0]), ssd_alog=pad32(ssd_a_log[0]),
        ssd_dexp=jnp.repeat(ssd_d[0], SSD_INNER // SSD_HEADS)[None, :],
        ssd_norm=ssd_norm[0][None, :],
        w_out_odd=w_out_odd[0].astype(BF16),
        w1=ffn_w1.astype(BF16), w3=ffn_w3.astype(BF16), w2=ffn_w2.astype(BF16),
    )

    y_prompt, k_new, v_new, lru_new, ssd_new = _trunk(x_prompt, mods_prompt, p, None)
    ctx = dict(
        k=cache_k[:, 0].transpose(0, 2, 3, 1).reshape(dec_b, KV_W, past),
        v=cache_v[:, 0].transpose(0, 2, 3, 1).reshape(dec_b, KV_W, past),
        lru=state_lru[:, 0],
        ssd=state_ssd[:, 0].transpose(0, 1, 4, 2, 3).reshape(dec_b, 2, SSD_STATE, SSD_INNER),
    )
    y_sample, _, _, _, _ = _trunk(x_sample, mods_sample, p, ctx)

    new_k = k_new.transpose(0, 2, 1).reshape(bsz, 1, seq, N_KV_HEADS, HEAD_DIM)
    new_v = v_new.transpose(0, 2, 1).reshape(bsz, 1, seq, N_KV_HEADS, HEAD_DIM)
    new_lru = lru_new.reshape(bsz, 1, 2, LRU_W)
    new_ssd = (ssd_new.reshape(bsz, 2, SSD_STATE, SSD_HEADS, SSD_INNER // SSD_HEADS)
               .transpose(0, 1, 3, 4, 2).reshape(bsz, 1, 2, SSD_HEADS, SSD_INNER // SSD_HEADS, SSD_STATE))
    return (y_prompt, y_sample, new_k, new_v, new_lru, new_ssd)
```

```python
import functools
import math

import numpy as np
import jax
import jax.numpy as jnp
from jax import lax
from jax.experimental import pallas as pl
from jax.experimental.pallas import tpu as pltpu

F32 = jnp.float32
BF16 = jnp.bfloat16

D_MODEL = 1024
EPS = 1e-6
GRID_W = 64
HEAD_DIM = 64
N_Q_HEADS = 12
N_KV_HEADS = 4
FOURIER_W = 256
FOURIER_GROUP_W = 64
Q_W = N_Q_HEADS * HEAD_DIM
KV_W = N_KV_HEADS * HEAD_DIM
EVEN_IN = FOURIER_W + Q_W + 2 * KV_W
ROPE_THETA = 10000.0
AX_PAIRS = HEAD_DIM // 4
LRU_W = 512
LRU_C = 8.0
SSD_INNER = 1024
SSD_HEADS = 16
SSD_STATE = 64
SSD_CONV_CH = SSD_INNER + 4 * SSD_STATE
D_FF = 2816
LANES = 128
SUBLANES = 8
TOKEN_BLOCK = 512
ROW_TILE = 256
SSD_Q = 128
V_ROWS = HEAD_DIM + 16
VMEM_LIMIT = 56 * 1024 * 1024


def _params(sem, vmem=VMEM_LIMIT):
    return pltpu.CompilerParams(dimension_semantics=sem, vmem_limit_bytes=vmem)


def _const_spec(shape):
    nd = len(shape)
    return pl.BlockSpec(shape, lambda *_: (0,) * nd, pipeline_mode=pl.Buffered(1))


def _dot(a, b):
    return jnp.dot(a, b, preferred_element_type=F32)


def _dot_nt(a, b):
    return lax.dot_general(a, b, (((1,), (1,)), ((), ())), preferred_element_type=F32)


def _dot3(m, a):
    hi = a.astype(BF16)
    r1 = a - hi.astype(F32)
    mid = r1.astype(BF16)
    lo = (r1 - mid.astype(F32)).astype(BF16)
    return _dot(m, hi) + _dot(m, mid) + _dot(m, lo)


def _silu(x):
    return x * jax.nn.sigmoid(x)


def _softplus(x):
    return jnp.maximum(x, 0.0) + jnp.log1p(jnp.exp(-jnp.abs(x)))


def _gelu_tanh(x):
    c = math.sqrt(2.0 / math.pi)
    return 0.5 * x * (1.0 + jnp.tanh(c * (x + 0.044715 * (x * x * x))))


def _modnorm(x, g, shift, scale):
    ms = jnp.mean(x * x, axis=-1, keepdims=True)
    y = x * lax.rsqrt(ms + EPS) * g
    return y * (1.0 + scale) + shift


def _mod_kernel(cond_ref, w_ref, b_ref, o_ref):
    s = _silu(cond_ref[...])
    o_ref[0] = _dot(s.astype(BF16), w_ref[0].astype(BF16)) + b_ref[0]


def _modulation(cond, w_ada, b_ada):
    depth, d, n = w_ada.shape
    rows = cond.shape[0]
    tn = 3072
    return pl.pallas_call(
        _mod_kernel,
        grid=(depth, n // tn),
        in_specs=[pl.BlockSpec((rows, d), lambda l, j: (0, 0)),
                  pl.BlockSpec((1, d, tn), lambda l, j: (l, 0, j)),
                  pl.BlockSpec((1, 1, tn), lambda l, j: (l, 0, j))],
        out_specs=pl.BlockSpec((1, rows, tn), lambda l, j: (l, 0, j)),
        out_shape=jax.ShapeDtypeStruct((depth, rows, n), F32),
        compiler_params=_params(("arbitrary", "arbitrary")),
        name="modulation",
    )(cond, w_ada, b_ada.reshape(depth, 1, n))


def _mod_spec(mod):
    if mod.shape[0] == 1:
        return pl.BlockSpec((1, 6, D_MODEL), lambda b, i: (0, 0, 0))
    return pl.BlockSpec((1, 6, D_MODEL), lambda b, i: (b, 0, 0))


def _head_rmsnorm(x, gain):
    rows, width = x.shape
    low = lax.broadcasted_iota(jnp.int32, (rows, LANES), 1) < HEAD_DIM
    out = []
    for b in range(width // LANES):
        blk = x[:, b * LANES:(b + 1) * LANES]
        sq = blk * blk
        lo = jnp.sum(jnp.where(low, sq, 0.0), axis=-1, keepdims=True)
        hi = jnp.sum(jnp.where(low, 0.0, sq), axis=-1, keepdims=True)
        ms = jnp.where(low, lo, hi) * (1.0 / HEAD_DIM)
        out.append(blk * lax.rsqrt(ms + EPS) * gain[:, b * LANES:(b + 1) * LANES])
    return jnp.concatenate(out, axis=-1)


def _rope(x, cos, sin_signed):
    rows, width = x.shape
    lane = lax.broadcasted_iota(jnp.int32, (rows, LANES), 1)
    second = (lane % (2 * AX_PAIRS)) >= AX_PAIRS
    out = []
    for b in range(width // LANES):
        blk = x[:, b * LANES:(b + 1) * LANES]
        partner = jnp.where(second, pltpu.roll(blk, AX_PAIRS, 1), pltpu.roll(blk, LANES - AX_PAIRS, 1))
        out.append(blk * cos + partner * sin_signed)
    return jnp.concatenate(out, axis=-1)


def _even_pre_kernel(*refs, rope):
    if rope:
        x_ref, mod_ref, g_ref, w_ref, qg_ref, kg_ref, cos_ref, sin_ref, f_ref, q_ref, k_ref, v_ref = refs
    else:
        x_ref, mod_ref, g_ref, w_ref, qg_ref, kg_ref, f_ref, q_ref, k_ref, v_ref, kt_ref, vt_ref = refs
    h = _modnorm(x_ref[0], g_ref[...], mod_ref[0, 0:1, :], mod_ref[0, 1:2, :])
    p = _dot(h.astype(BF16), w_ref[...])
    f_ref[0] = p[:, :FOURIER_W]
    q = _head_rmsnorm(p[:, FOURIER_W:FOURIER_W + Q_W], qg_ref[...])
    k = _head_rmsnorm(p[:, FOURIER_W + Q_W:FOURIER_W + Q_W + KV_W], kg_ref[...])
    if rope:
        q = _rope(q, cos_ref[...], sin_ref[...])
        k = _rope(k, cos_ref[...], sin_ref[...])
    q_ref[0] = (q * (HEAD_DIM ** -0.5 * math.log2(math.e))).astype(BF16)
    v = p[:, FOURIER_W + Q_W + KV_W:]
    k_ref[0] = k
    v_ref[0] = v
    if not rope:
        kt_ref[0] = k.T
        vt_ref[0] = v.T


def _rope_tables(seq):
    pos = np.arange(seq)
    freqs = ROPE_THETA ** (-np.arange(AX_PAIRS, dtype=np.float64) / AX_PAIRS)
    ang = np.zeros((seq, HEAD_DIM))
    sign = np.zeros((HEAD_DIM,))
    for a, p_a in enumerate((pos // GRID_W, pos % GRID_W)):
        for t in range(2):
            lo = a * 2 * AX_PAIRS + t * AX_PAIRS
            ang[:, lo:lo + AX_PAIRS] = p_a[:, None] * freqs[None, :]
            sign[lo:lo + AX_PAIRS] = -1.0 if t == 0 else 1.0
    cos = np.tile(np.cos(ang), (1, 2)).astype(np.float32)
    sin = np.tile(np.sin(ang) * sign[None, :], (1, 2)).astype(np.float32)
    return jnp.asarray(cos), jnp.asarray(sin)


def _even_pre(x, mod, g, w_in, qg, kg, rope):
    bsz, seq, d = x.shape
    tm = ROW_TILE
    tok = lambda w: pl.BlockSpec((1, tm, w), lambda b, i: (b, i, 0))
    in_specs = [tok(d), _mod_spec(mod), _const_spec((1, d)), _const_spec((d, EVEN_IN)),
                _const_spec((1, Q_W)), _const_spec((1, KV_W))]
    args = [x, mod, g, w_in, qg, kg]
    if rope:
        cos, sin = _rope_tables(seq)
        in_specs += [pl.BlockSpec((tm, LANES), lambda b, i: (i, 0))] * 2
        args += [cos, sin]
    out_specs = [tok(FOURIER_W), tok(Q_W), tok(KV_W), tok(KV_W)]
    out_shape = [jax.ShapeDtypeStruct((bsz, seq, FOURIER_W), F32),
                 jax.ShapeDtypeStruct((bsz, seq, Q_W), BF16),
                 jax.ShapeDtypeStruct((bsz, seq, KV_W), F32),
                 jax.ShapeDtypeStruct((bsz, seq, KV_W), F32)]
    if not rope:
        out_specs += [pl.BlockSpec((1, KV_W, tm), lambda b, i: (b, 0, i))] * 2
        out_shape += [jax.ShapeDtypeStruct((bsz, KV_W, seq), F32)] * 2
    return pl.pallas_call(
        functools.partial(_even_pre_kernel, rope=rope),
        grid=(bsz, seq // tm),
        in_specs=in_specs,
        out_specs=out_specs,
        out_shape=out_shape,
        compiler_params=_params(("arbitrary", "arbitrary")),
        name="even_pre",
    )(*args)


def _attn_kernel(*refs, seq, has_ctx):
    if has_ctx:
        q_ref, k_ref, v_ref, ck_ref, cv_ref, f_ref, cs_ref, cbd_ref, o_ref, kop, vopt, g_scr, cs_scr = refs
    else:
        q_ref, k_ref, v_ref, f_ref, cs_ref, cbd_ref, o_ref, kop, vopt, g_scr, cs_scr = refs
    tq = q_ref.shape[1]

    @pl.when((pl.program_id(0) == 0) & (pl.program_id(1) == 0))
    def _round_tables():
        for r0 in range(0, seq, tq):
            cs_scr[r0:r0 + tq, :] = cs_ref[r0:r0 + tq, :].astype(BF16)

    @pl.when(pl.program_id(1) == 0)
    def _prepare():
        def put(kx, vx, r0, transposed):
            n = kx.shape[1] if transposed else kx.shape[0]
            low = lax.broadcasted_iota(jnp.int32, (n, LANES), 1) < HEAD_DIM
            ones = jnp.ones((V_ROWS - HEAD_DIM, n), BF16)
            for jp in range(N_KV_HEADS // 2):
                kp = kx[jp * LANES:(jp + 1) * LANES, :].T if transposed else kx[:, jp * LANES:(jp + 1) * LANES]
                k_lo = jnp.where(low, kp, 0.0)
                k_hi = jnp.where(low, 0.0, kp)
                placed = (k_lo, pltpu.roll(k_lo, HEAD_DIM, 1), pltpu.roll(k_hi, HEAD_DIM, 1), k_hi)
                for t, kk in enumerate(placed):
                    kop[4 * jp + t, r0:r0 + n, :] = kk.astype(BF16)
                vt = vx[jp * LANES:(jp + 1) * LANES, :] if transposed else vx[:, jp * LANES:(jp + 1) * LANES].T
                vt = vt.astype(BF16)
                for t in range(2):
                    vopt[2 * jp + t, 0:HEAD_DIM, r0:r0 + n] = vt[t * HEAD_DIM:(t + 1) * HEAD_DIM]
                    vopt[2 * jp + t, HEAD_DIM:V_ROWS, r0:r0 + n] = ones

        put(k_ref[0], v_ref[0], 0, False)
        if has_ctx:
            put(ck_ref[0], cv_ref[0], seq, True)
        f = f_ref[0].astype(BF16)
        g_scr[0:seq, :] = _dot(f, cbd_ref[0].astype(BF16)).astype(BF16)
        g_scr[seq:2 * seq, :] = _dot(f, cbd_ref[1].astype(BF16)).astype(BF16)

    r0 = pl.multiple_of(pl.program_id(1) * tq, tq)
    o_ref[0, :, 0:FOURIER_W] = _dot(cs_scr[pl.ds(r0, tq), :], g_scr[...]).astype(BF16)
    q = q_ref[0]
    n_pairs = N_Q_HEADS // 2
    kv_of = lambda qp: [(2 * qp + half) // (N_Q_HEADS // N_KV_HEADS) for half in range(2)]

    def pair_scores(qp):
        qpair = q[:, qp * LANES:(qp + 1) * LANES]
        return [_dot_nt(qpair, kop[2 * kv_of(qp)[half] + half]) for half in range(2)]

    scores_next = pair_scores(0)
    for qp in range(n_pairs):
        kv = kv_of(qp)
        scores = scores_next
        if qp + 1 < n_pairs:
            scores_next = pair_scores(qp + 1)
        probs = [jnp.exp2(s - jnp.max(s, axis=-1, keepdims=True)).astype(BF16) for s in scores]
        outs = []
        for half in range(2):
            o = _dot_nt(vopt[kv[half]], probs[half])
            outs.append(o[0:HEAD_DIM] / o[HEAD_DIM:HEAD_DIM + 1])
        res = jnp.concatenate(outs, axis=0)
        o_ref[0, :, FOURIER_W + qp * LANES:FOURIER_W + (qp + 1) * LANES] = res.T.astype(BF16)


def _dft_tables(seq):
    s = np.arange(seq)
    ang = 2.0 * np.pi * ((s[:, None] * s[None, :]) % seq) / seq
    pos = np.concatenate([np.cos(ang), -np.sin(ang)], axis=1) / math.sqrt(seq)
    c = np.arange(FOURIER_GROUP_W)
    ang_c = 2.0 * np.pi * ((c[:, None] * c[None, :]) % FOURIER_GROUP_W) / FOURIER_GROUP_W
    eye = np.eye(FOURIER_W // FOURIER_GROUP_W)
    chan = np.stack([np.kron(eye, np.cos(ang_c)), np.kron(eye, np.sin(ang_c))]) / math.sqrt(FOURIER_GROUP_W)
    return jnp.asarray(pos, dtype=F32), jnp.asarray(chan, dtype=F32)


def _even_mixer(q, k, v, f, ctx):
    bsz, seq, _ = q.shape
    tq = ROW_TILE
    has_ctx = ctx is not None
    past = ctx[0].shape[2] if has_ctx else 0
    kt = seq + past
    pos_dft, chan_dft = _dft_tables(seq)
    full = lambda n, w: pl.BlockSpec((1, n, w), lambda b, i: (b, 0, 0))
    in_specs = [pl.BlockSpec((1, tq, Q_W), lambda b, i: (b, i, 0)), full(seq, KV_W), full(seq, KV_W)]
    args = [q, k, v]
    if has_ctx:
        in_specs += [full(KV_W, past), full(KV_W, past)]
        args += list(ctx)
    in_specs += [full(seq, FOURIER_W), _const_spec((seq, 2 * seq)),
                 _const_spec((2, FOURIER_W, FOURIER_W))]
    args += [f, pos_dft, chan_dft]
    return pl.pallas_call(
        functools.partial(_attn_kernel, seq=seq, has_ctx=has_ctx),
        grid=(bsz, seq // tq),
        in_specs=in_specs,
        out_specs=pl.BlockSpec((1, tq, D_MODEL), lambda b, i: (b, i, 0)),
        out_shape=jax.ShapeDtypeStruct((bsz, seq, D_MODEL), BF16),
        scratch_shapes=[pltpu.VMEM((2 * N_KV_HEADS, kt, LANES), BF16),
                        pltpu.VMEM((N_KV_HEADS, V_ROWS, kt), BF16),
                        pltpu.VMEM((2 * seq, FOURIER_W), BF16),
                        pltpu.VMEM((seq, 2 * seq), BF16)],
        compiler_params=_params(("arbitrary", "arbitrary")),
        name="even_mixer",
    )(*args)


def _post_ffn_kernel(*refs, odd):
    if odd:
        (x_ref, ylru_ref, g_ref, yssd_ref, z_ref, sn_ref, mod_ref, wout_ref, gf_ref,
         w1_ref, w3_ref, w2_ref, o_ref) = refs
        y_lru = ylru_ref[0] * _gelu_tanh(g_ref[0])
        y = yssd_ref[0] * _silu(z_ref[0])
        ms = jnp.mean(y * y, axis=-1, keepdims=True)
        y_ssd = y * lax.rsqrt(ms + EPS) * sn_ref[...]
        mix = jnp.concatenate([y_lru, y_ssd], axis=-1).astype(BF16)
    else:
        x_ref, mix_ref, mod_ref, wout_ref, gf_ref, w1_ref, w3_ref, w2_ref, o_ref = refs
        mix = mix_ref[0]
    x = x_ref[0] + mod_ref[0, 2:3, :] * _dot(mix, wout_ref[...])
    h = _modnorm(x, gf_ref[...], mod_ref[0, 3:4, :], mod_ref[0, 4:5, :]).astype(BF16)
    a = (_silu(_dot(h, w1_ref[...])) * _dot(h, w3_ref[...])).astype(BF16)
    o_ref[0] = x + mod_ref[0, 5:6, :] * _dot(a, w2_ref[...])


def _post_ffn(x, mix_args, mod, w_out, g_ffn, w1, w3, w2, layer, ssd_norm=None):
    bsz, seq, d = x.shape
    if mod.shape[0] == 1 and seq < TOKEN_BLOCK:
        fold = TOKEN_BLOCK // seq
        merge = lambda t: t.reshape(bsz // fold, fold * seq, t.shape[-1])
        out = _post_ffn(merge(x), jax.tree.map(merge, mix_args), mod, w_out, g_ffn, w1, w3, w2, layer, ssd_norm)
        return out.reshape(bsz, seq, d)
    tm = min(seq, TOKEN_BLOCK)
    odd = layer % 2 == 1
    tok = lambda w: pl.BlockSpec((1, tm, w), lambda b, i: (b, i, 0))
    slab = lambda w: pl.BlockSpec((None,) + w.shape[1:], lambda b, i: (layer, 0, 0), pipeline_mode=pl.Buffered(1))
    if odd:
        ylru, g, yssd, z = mix_args
        in_specs = [tok(d), tok(LRU_W), tok(LRU_W), tok(SSD_INNER), tok(SSD_INNER), _const_spec((1, SSD_INNER))]
        args = [x, ylru, g, yssd, z, ssd_norm]
    else:
        in_specs = [tok(d), tok(d)]
        args = [x, mix_args]
    in_specs += [_mod_spec(mod), _const_spec(w_out.shape), _const_spec((1, d)),
                 slab(w1), slab(w3), slab(w2)]
    args += [mod, w_out, g_ffn, w1, w3, w2]
    return pl.pallas_call(
        functools.partial(_post_ffn_kernel, odd=odd),
        grid=(bsz, seq // tm),
        in_specs=in_specs,
        out_specs=tok(d),
        out_shape=jax.ShapeDtypeStruct((bsz, seq, d), F32),
        compiler_params=_params(("arbitrary", "arbitrary")),
        name="post_ffn_odd" if odd else "post_ffn_even",
    )(*args)


XL_LO = LRU_W
Z_LO = 2 * LRU_W
XBC_LO = 2 * LRU_W + SSD_INNER
DT_LO = XBC_LO + SSD_CONV_CH


def _conv_block(main, edges, w, b):
    before, after0, after1 = edges
    rows = main.shape[0]
    row = lax.broadcasted_iota(jnp.int32, (SUBLANES, 1), 0)
    last = rows - SUBLANES

    def patch_first(x, fix):
        return jnp.concatenate([fix(x[:SUBLANES]), x[SUBLANES:]], axis=0)

    def patch_last(x, fix):
        return jnp.concatenate([x[:last], fix(x[last:])], axis=0)

    xm1 = patch_first(pltpu.roll(main, 1, 0), lambda t: jnp.where(row == 0, before, t))
    xp1 = patch_last(pltpu.roll(main, rows - 1, 0), lambda t: jnp.where(row == SUBLANES - 1, after0, t))
    xp2 = patch_last(pltpu.roll(main, rows - 2, 0),
                     lambda t: jnp.where(row == SUBLANES - 2, after0, jnp.where(row == SUBLANES - 1, after1, t)))
    return xm1 * w[0:1] + main * w[1:2] + xp1 * w[2:3] + xp2 * w[3:4] + b


def _odd_pre_kernel(x_ref, xprev_ref, xnext_ref, mod_ref, g_ref, w_ref, wdt_ref, cwl_ref, cbl_ref, wg_ref, bg_ref,
                    lam_ref, gate_ref, z_ref, a_ref, b_ref, xbc_ref, dt_ref):
    i = pl.program_id(1)
    g, shift, scale = g_ref[...], mod_ref[0, 0:1, :], mod_ref[0, 1:2, :]
    h = _modnorm(x_ref[0], g, shift, scale).astype(BF16)
    wide = 2 * LANES
    proj = lambda lo: _dot(h, w_ref[:, lo:lo + wide])
    p_xl = [proj(XL_LO + t * wide) for t in range(LRU_W // wide)]
    others = ([(gate_ref, t * wide, t * wide) for t in range(LRU_W // wide)]
              + [(z_ref, Z_LO + t * wide, t * wide) for t in range(SSD_INNER // wide)]
              + [(xbc_ref, XBC_LO + t * wide, t * wide) for t in range(SSD_CONV_CH // wide)])

    halo = jnp.concatenate([xprev_ref[0], xnext_ref[0]], axis=0)
    ph = _dot(_modnorm(halo, g, shift, scale).astype(BF16), w_ref[:, XL_LO:Z_LO])
    has_prev = i > 0
    has_next = i < pl.num_programs(1) - 1
    e_l = (jnp.where(has_prev, ph[SUBLANES - 1:SUBLANES], 0.0),
           jnp.where(has_next, ph[SUBLANES:SUBLANES + 1], 0.0),
           jnp.where(has_next, ph[SUBLANES + 1:SUBLANES + 2], 0.0))

    rate = (0.5 * LRU_C) * _softplus(-lam_ref[...])
    n_lru = LRU_W // LANES
    for lb in range(n_lru):
        sl = slice(lb * LANES, (lb + 1) * LANES)
        for dst, lo, col in others[lb * len(others) // n_lru:(lb + 1) * len(others) // n_lru]:
            dst[0, :, col:col + wide] = proj(lo)
        xb = _conv_block(p_xl[lb // 2][:, (lb % 2) * LANES:(lb % 2 + 1) * LANES], [e[:, sl] for e in e_l],
                         cwl_ref[:, sl], cbl_ref[:, sl])
        xh = 0.5 * xb
        gt = jnp.tanh(_dot(xb.astype(BF16), wg_ref[lb]) + bg_ref[lb])
        for d in range(2):
            neg_log_a = rate[d:d + 1, sl] * (gt[:, 2 * d * LANES:(2 * d + 1) * LANES] + 1.0)
            a = jnp.exp2(neg_log_a * (-math.log2(math.e)))
            g2 = jnp.tanh(neg_log_a) * (a * a + 1.0)
            root = jnp.where(g2 > 0.0, g2 * lax.rsqrt(g2), 0.0)
            a_ref[0, d, :, sl] = a
            b_ref[0, d, :, sl] = root * ((gt[:, (2 * d + 1) * LANES:(2 * d + 2) * LANES] + 1.0) * xh)
    dt_ref[0] = _dot(h, wdt_ref[...])


def _odd_pre(x, mod, g, w_in, w_dt, lru):
    bsz, seq, d = x.shape
    tm = min(seq, TOKEN_BLOCK)
    per8 = tm // SUBLANES
    tok = lambda w: pl.BlockSpec((1, tm, w), lambda b, i: (b, i, 0))
    both = pl.BlockSpec((1, 2, tm, LRU_W), lambda b, i: (b, 0, i, 0))
    consts = list(lru)
    out_specs = [tok(LRU_W), tok(SSD_INNER), both, both, tok(SSD_CONV_CH), tok(LANES)]
    row = lambda w: jax.ShapeDtypeStruct((bsz, seq, w), F32)
    out_shape = ([row(LRU_W), row(SSD_INNER)] + [jax.ShapeDtypeStruct((bsz, 2, seq, LRU_W), F32)] * 2
                 + [row(SSD_CONV_CH), row(LANES)])
    return pl.pallas_call(
        _odd_pre_kernel,
        grid=(bsz, seq // tm),
        in_specs=[tok(d),
                  pl.BlockSpec((1, SUBLANES, d), lambda b, i: (b, jnp.maximum(i * per8 - 1, 0), 0)),
                  pl.BlockSpec((1, SUBLANES, d), lambda b, i: (b, jnp.minimum((i + 1) * per8, seq // SUBLANES - 1), 0)),
                  _mod_spec(mod), _const_spec((1, d)), _const_spec(w_in.shape), _const_spec(w_dt.shape)]
                 + [_const_spec(c.shape) for c in consts],
        out_specs=out_specs,
        out_shape=out_shape,
        compiler_params=_params(("arbitrary", "arbitrary")),
        name="odd_pre",
    )(x, x, x, mod, g, w_in, w_dt, *consts)


def _ssd_kernel(*refs, seq, has_h0):
    xbc_ref, dt_ref, cw_ref, cb_ref, dtb_ref, alog_ref, dexp_ref, ef_ref, eb_ref, la_ref, lb_ref = refs[:11]
    if has_h0:
        h0_ref, lh0_ref, y_ref, ylru_ref = refs[11:15]
    else:
        y_ref, ylru_ref, hout_ref, lhout_ref = refs[11:15]
    xs_ref, bm_ref, cm_ref, cs_ref, dts_ref, cst_ref, dtt_ref, bt_ref, st_scr = refs[15:]
    q = SSD_Q
    nc = seq // q
    lane = lax.broadcasted_iota(jnp.int32, (q, LANES), 1)
    qi = lax.broadcasted_iota(jnp.int32, (q, q), 0)
    ki = lax.broadcasted_iota(jnp.int32, (q, q), 1)
    tri_lower = jnp.where(ki <= qi, 1.0, 0.0).astype(BF16)
    tri_upper = jnp.where(ki >= qi, 1.0, 0.0).astype(BF16)
    a_log2 = jnp.where(lane[0:1, :] < 2 * SSD_HEADS, -math.log2(math.e) * jnp.exp(alog_ref[...]), 0.0)

    def prepare(c):
        r0 = pl.multiple_of(c * q, q)
        rows = pl.ds(r0, q)
        prev8 = pl.ds(pl.multiple_of(jnp.maximum(r0 - SUBLANES, 0), SUBLANES), SUBLANES)
        next8 = pl.ds(pl.multiple_of(jnp.minimum(r0 + q, seq - SUBLANES), SUBLANES), SUBLANES)
        has_prev = r0 > 0
        has_next = r0 + q < seq
        for lb in range(SSD_CONV_CH // LANES):
            sl = slice(lb * LANES, (lb + 1) * LANES)
            before = xbc_ref[0, prev8, sl]
            after = xbc_ref[0, next8, sl]
            edges = (jnp.where(has_prev, before[SUBLANES - 1:SUBLANES], 0.0),
                     jnp.where(has_next, after[0:1], 0.0), jnp.where(has_next, after[1:2], 0.0))
            xc = _conv_block(xbc_ref[0, rows, sl], edges, cw_ref[:, sl], cb_ref[:, sl])
            half_xc = 0.5 * xc
            xc = half_xc * (jnp.tanh(half_xc) + 1.0)
            if lb < SSD_INNER // LANES:
                xs_ref[rows, sl] = xc
                y_ref[0, rows, sl] = dexp_ref[:, sl] * xc
            elif lb == SSD_INNER // LANES:
                bm_ref[rows, :] = xc
                bt_ref[c] = xc.T
            else:
                cm_ref[rows, :] = xc
        dts = _softplus(dt_ref[0, rows, :] + dtb_ref[...])
        da = dts * a_log2
        cs = jnp.where(lane < SSD_HEADS, _dot3(tri_lower, da), _dot3(tri_upper, da))
        cs_ref[rows, :] = cs
        dts_ref[rows, :] = dts
        cst_ref[c] = cs.T
        dtt_ref[c] = dts.T

    for d in range(2):
        st_scr[d] = h0_ref[0, d].T if has_h0 else jnp.zeros(st_scr.shape[1:], F32)

    low = lane < SSD_STATE
    col = lax.broadcasted_iota(jnp.int32, (SSD_STATE, SSD_INNER), 1)
    first_group = col < SSD_INNER // 2

    def chunk(c, fwd):
        d = 0 if fwd else 1
        r0 = pl.multiple_of(c * q, q)
        rows = pl.ds(r0, q)
        cs = cs_ref[rows, :]
        dts = dts_ref[rows, :]
        cst = cst_ref[c]
        dtt = dtt_ref[c]
        xs = xs_ref[rows, :]
        xsb = xs.astype(BF16)
        bb = bm_ref[rows, :].astype(BF16)
        cb_ = cm_ref[rows, :].astype(BF16)
        zero = jnp.zeros_like(cb_)
        cbg = [_dot_nt(jnp.where(low, cb_, zero), bb).astype(BF16),
               _dot_nt(jnp.where(low, zero, cb_), bb).astype(BF16)]
        mask = (ki <= qi) if fwd else (ki >= qi)
        expand = ef_ref[...] if fwd else eb_ref[...]
        st = st_scr[d]
        hh = jnp.concatenate([jnp.where(first_group, st, 0.0), jnp.where(first_group, 0.0, st)], axis=0)
        tot = cs[q - 1:q, :] if fwd else cs[0:1, :]
        mine = (lane >= d * SSD_HEADS) & (lane < (d + 1) * SSD_HEADS)
        w = jnp.exp2(jnp.where(mine, tot - cs, 0.0)) * dts
        dec = jnp.broadcast_to(jnp.exp2(tot), (2 * SUBLANES, LANES))
        dec_hi = dec.astype(BF16)
        dec_lo = (dec - dec_hi.astype(F32)).astype(BF16)
        spread = _dot(jnp.concatenate([jnp.exp2(cs).astype(BF16), w.astype(BF16), dec_hi, dec_lo], axis=0), expand)
        y = _dot(cb_, hh.astype(BF16)) * spread[0:q]
        pairs = []
        for pp in range(SSD_HEADS // 2):
            ms = []
            for h in (2 * pp, 2 * pp + 1):
                l = d * SSD_HEADS + h
                decay = jnp.exp2((cs[:, l:l + 1] - cst[l:l + 1, :]).astype(BF16))
                decay = jnp.where(mask, decay, jnp.zeros_like(decay))
                ms.append(cbg[pp // (SSD_HEADS // 4)] * decay * dtt[l:l + 1, :].astype(BF16))
            xp = xsb[:, pp * LANES:(pp + 1) * LANES]
            zx = jnp.zeros_like(xp)
            x2 = jnp.concatenate([jnp.where(low, xp, zx), jnp.where(low, zx, xp)], axis=0)
            pairs.append(_dot(jnp.concatenate(ms, axis=1), x2))
        y_ref[0, rows, :] += y + jnp.concatenate(pairs, axis=1)
        xw = (xs * spread[q:2 * q]).astype(BF16)
        bt = bt_ref[c].astype(BF16)
        half_w = SSD_INNER // 2
        st_new = jnp.concatenate([_dot(bt[:SSD_STATE], xw[:, :half_w]), _dot(bt[SSD_STATE:], xw[:, half_w:])], axis=1)
        chunk_decay = spread[2 * q:2 * q + 1] + spread[2 * q + 2 * SUBLANES:2 * q + 2 * SUBLANES + 1]
        st_scr[d] = chunk_decay * st + st_new

    def prepare_all(c, carry):
        prepare(c)
        return carry

    nb = seq // SUBLANES
    per_chunk = q // SUBLANES
    row = lax.broadcasted_iota(jnp.int32, (SUBLANES, LRU_W), 0)

    def lru_block(blk, hf, hb):
        rf = pl.multiple_of(blk * SUBLANES, SUBLANES)
        rb = pl.multiple_of((nb - 1 - blk) * SUBLANES, SUBLANES)
        a = la_ref[0, 0, pl.ds(rf, SUBLANES), :]
        b = lb_ref[0, 0, pl.ds(rf, SUBLANES), :]
        for k in (1, 2, 4):
            keep = row >= k
            b = a * jnp.where(keep, pltpu.roll(b, k, 0), 0.0) + b
            a = a * jnp.where(keep, pltpu.roll(a, k, 0), 1.0)
        h = a * hf + b
        ylru_ref[0, pl.ds(rf, SUBLANES), :] += h
        hf = h[SUBLANES - 1:SUBLANES, :]
        a = la_ref[0, 1, pl.ds(rb, SUBLANES), :]
        b = lb_ref[0, 1, pl.ds(rb, SUBLANES), :]
        for k in (1, 2, 4):
            keep = row < SUBLANES - k
            b = a * jnp.where(keep, pltpu.roll(b, SUBLANES - k, 0), 0.0) + b
            a = a * jnp.where(keep, pltpu.roll(a, SUBLANES - k, 0), 1.0)
        h = a * hb + b
        ylru_ref[0, pl.ds(rb, SUBLANES), :] += h
        return hf, h[0:1, :]

    def both_directions(i, carry):
        chunk(i, True)
        chunk(nc - 1 - i, False)
        hf, hb = carry
        for j in range(per_chunk):
            hf, hb = lru_block(i * per_chunk + j, hf, hb)
        return hf, hb

    ylru_ref[0] = jnp.zeros((seq, LRU_W), F32)
    lax.fori_loop(0, nc, prepare_all, 0)
    if has_h0:
        lru_start = (lh0_ref[0, 0:1, :], lh0_ref[0, 1:2, :])
    else:
        lru_start = (jnp.zeros((1, LRU_W), F32), jnp.zeros((1, LRU_W), F32))
    hf, hb = lax.fori_loop(0, nc, both_directions, lru_start)
    if not has_h0:
        for d in range(2):
            hout_ref[0, d] = st_scr[d].T
        lhout_ref[0, 0:1, :] = hf
        lhout_ref[0, 1:2, :] = hb


def _ssd(xbc, dt, cw, cb, dtb, alog, dexp, lru_a, lru_b, h0, lru_h0):
    bsz, seq, _ = xbc.shape
    nc = seq // SSD_Q
    expand = np.zeros((2, LANES, SSD_INNER), np.float32)
    for d in range(2):
        for h in range(SSD_HEADS):
            expand[d, d * SSD_HEADS + h, h * 64:(h + 1) * 64] = 1.0
    expand = jnp.asarray(expand, dtype=BF16)
    per_seq = lambda n, w: pl.BlockSpec((1, n, w), lambda b: (b, 0, 0))
    state_spec = pl.BlockSpec((1, 2, SSD_INNER, SSD_STATE), lambda b: (b, 0, 0, 0))
    has_h0 = h0 is not None
    coef = pl.BlockSpec((1, 2, seq, LRU_W), lambda b: (b, 0, 0, 0))
    in_specs = [per_seq(seq, SSD_CONV_CH), per_seq(seq, LANES), _const_spec(cw.shape), _const_spec(cb.shape),
                _const_spec(dtb.shape), _const_spec(alog.shape), _const_spec(dexp.shape),
                _const_spec((LANES, SSD_INNER)), _const_spec((LANES, SSD_INNER)), coef, coef]
    args = [xbc, dt, cw, cb, dtb, alog, dexp, expand[0], expand[1], lru_a, lru_b]
    out_specs = [per_seq(seq, SSD_INNER), per_seq(seq, LRU_W)]
    out_shape = [jax.ShapeDtypeStruct((bsz, seq, SSD_INNER), F32), jax.ShapeDtypeStruct((bsz, seq, LRU_W), F32)]
    if has_h0:
        in_specs += [state_spec, per_seq(2, LRU_W)]
        args += [h0, lru_h0]
    else:
        out_specs += [state_spec, per_seq(2, LRU_W)]
        out_shape += [jax.ShapeDtypeStruct((bsz, 2, SSD_INNER, SSD_STATE), F32),
                      jax.ShapeDtypeStruct((bsz, 2, LRU_W), F32)]
    outs = pl.pallas_call(
        functools.partial(_ssd_kernel, seq=seq, has_h0=has_h0),
        grid=(bsz,),
        in_specs=in_specs,
        out_specs=out_specs,
        out_shape=out_shape,
        scratch_shapes=[pltpu.VMEM((seq, SSD_INNER), F32)] + [pltpu.VMEM((seq, LANES), F32)] * 4
                       + [pltpu.VMEM((nc, LANES, SSD_Q), F32)] * 3 + [pltpu.VMEM((2, SSD_STATE, SSD_INNER), F32)],
        compiler_params=_params(("arbitrary",)),
        name="ssd",
    )(*args)
    return (outs[0], outs[1], None, None) if has_h0 else tuple(outs)


def _lru_gate_weights(wa, ba, wx, bx):
    eye = 0.5 * jnp.eye(2, dtype=F32)

    def pair_blocks(w):
        w4 = w.reshape(4, 2, 64, 64)
        return jnp.einsum("laij,ab->laibj", w4, eye).reshape(4, LANES, LANES)

    mats = [pair_blocks(w) for w in (wa[0], wx[0], wa[1], wx[1])]
    wg = jnp.concatenate(mats, axis=-1).astype(BF16)
    bias = [0.5 * b.reshape(4, 1, LANES) for b in (ba[0], bx[0], ba[1], bx[1])]
    return wg, jnp.concatenate(bias, axis=-1)


def _trunk(x, mods, p, ctx):
    is_ctx = ctx is None
    bsz, seq, _ = x.shape
    f, q, k, v, *kv_t = _even_pre(x, mods[0], p["norm_mix"][0], p["w_in_even"], p["q_gain"], p["k_gain"],
                                  rope=not is_ctx)
    mix = _even_mixer(q, k, v, f, None if is_ctx else (ctx["k"], ctx["v"]))
    x = _post_ffn(x, mix, mods[0], p["w_out_even"], p["norm_ffn"][0], p["w1"], p["w3"], p["w2"], layer=0)
    gate, z, lru_a, lru_b, xbc, dt = _odd_pre(
        x, mods[1], p["norm_mix"][1], p["w_in_odd"], p["w_dt_odd"],
        lru=(p["conv_lru_w"], p["conv_lru_b"], p["lru_wg"], p["lru_bg"], p["lru_lambda"]))
    h0_lru, h0_ssd = (None, None) if is_ctx else (ctx["lru"], ctx["ssd"])
    yssd, ylru, ssd_state, lru_state = _ssd(xbc, dt, p["conv_ssd_w"], p["conv_ssd_b"], p["ssd_dtb"], p["ssd_alog"],
                                            p["ssd_dexp"], lru_a, lru_b, h0_ssd, h0_lru)
    x = _post_ffn(x, (ylru, gate, yssd, z), mods[1], p["w_out_odd"], p["norm_ffn"][1], p["w1"], p["w3"], p["w2"],
                  layer=1, ssd_norm=p["ssd_norm"])
    kt, vt = kv_t if is_ctx else (None, None)
    return x, kt, vt, lru_state, ssd_state


def kernel(x_prompt, x_sample, c, cache_k, cache_v, state_lru, state_ssd, c_ctx, w_ada, b_ada, norm_mix, norm_ffn,
           w_in_even, q_norm, k_norm, w_out_even, w_in_odd, conv_lru_w, conv_lru_b, lru_wa, lru_ba, lru_wx, lru_bx,
           lru_lambda, conv_ssd_w, conv_ssd_b, ssd_dt_bias, ssd_a_log, ssd_d, ssd_norm, w_out_odd, ffn_w1, ffn_w3,
           ffn_w2):
    bsz, seq, d = x_prompt.shape
    dec_b = x_sample.shape[0]
    past = cache_k.shape[2]

    cond_rows = 16
    cond = jnp.concatenate([c, c_ctx[None, :], jnp.zeros((cond_rows - dec_b - 1, d), F32)], axis=0)
    mod = _modulation(cond, w_ada, b_ada).reshape(2, cond_rows, 6, d)
    mods_sample = [mod[l, :dec_b] for l in range(2)]
    mods_prompt = [mod[l, dec_b:dec_b + 1] for l in range(2)]

    lru_wg, lru_bg = _lru_gate_weights(lru_wa[0], lru_ba[0], lru_wx[0], lru_bx[0])
    pad32 = lambda a: jnp.pad(a.reshape(1, 2 * SSD_HEADS), ((0, 0), (0, LANES - 2 * SSD_HEADS)))
    p = dict(
        norm_mix=norm_mix.reshape(2, 1, d), norm_ffn=norm_ffn.reshape(2, 1, d),
        w_in_even=w_in_even[0].astype(BF16),
        q_gain=jnp.tile(q_norm[0], N_Q_HEADS)[None, :], k_gain=jnp.tile(k_norm[0], N_KV_HEADS)[None, :],
        w_out_even=w_out_even[0].astype(BF16),
        w_in_odd=w_in_odd[0].astype(BF16),
        w_dt_odd=jnp.pad(w_in_odd[0, :, DT_LO:], ((0, 0), (0, LANES - 2 * SSD_HEADS))).astype(BF16),
        conv_lru_w=conv_lru_w[0], conv_lru_b=conv_lru_b[0][None, :],
        lru_wg=lru_wg, lru_bg=lru_bg, lru_lambda=lru_lambda[0],
        conv_ssd_w=conv_ssd_w[0], conv_ssd_b=conv_ssd_b[0][None, :],
        ssd_dtb=pad32(ssd_dt_bias[0]), ssd_alog=pad32(ssd_a_log[0]),
        ssd_dexp=jnp.repeat(ssd_d[0], SSD_INNER // SSD_HEADS)[None, :],
        ssd_norm=ssd_norm[0][None, :],
        w_out_odd=w_out_odd[0].astype(BF16),
        w1=ffn_w1.astype(BF16), w3=ffn_w3.astype(BF16), w2=ffn_w2.astype(BF16),
    )

    y_prompt, k_new, v_new, lru_new, ssd_new = _trunk(x_prompt, mods_prompt, p, None)
    ctx = dict(
        k=cache_k[:, 0].transpose(0, 2, 3, 1).reshape(dec_b, KV_W, past),
        v=cache_v[:, 0].transpose(0, 2, 3, 1).reshape(dec_b, KV_W, past),
        lru=state_lru[:, 0],
        ssd=state_ssd[:, 0].reshape(dec_b, 2, SSD_INNER, SSD_STATE),
    )
    y_sample, _, _, _, _ = _trunk(x_sample, mods_sample, p, ctx)

    new_k = k_new.transpose(0, 2, 1).reshape(bsz, 1, seq, N_KV_HEADS, HEAD_DIM)
    new_v = v_new.transpose(0, 2, 1).reshape(bsz, 1, seq, N_KV_HEADS, HEAD_DIM)
    new_lru = lru_new.reshape(bsz, 1, 2, LRU_W)
    new_ssd = ssd_new.reshape(bsz, 1, 2, SSD_HEADS, SSD_INNER // SSD_HEADS, SSD_STATE)
    return (y_prompt, y_sample, new_k, new_v, new_lru, new_ssd)
```

```python
import functools
import math

import numpy as np
import jax
import jax.numpy as jnp
from jax import lax
from jax.experimental import pallas as pl
from jax.experimental.pallas import tpu as pltpu

F32 = jnp.float32
BF16 = jnp.bfloat16

D_MODEL = 1024
EPS = 1e-6
GRID_W = 64
HEAD_DIM = 64
N_Q_HEADS = 12
N_KV_HEADS = 4
FOURIER_W = 256
FOURIER_GROUP_W = 64
Q_W = N_Q_HEADS * HEAD_DIM
KV_W = N_KV_HEADS * HEAD_DIM
EVEN_IN = FOURIER_W + Q_W + 2 * KV_W
ROPE_THETA = 10000.0
AX_PAIRS = HEAD_DIM // 4
LRU_W = 512
LRU_C = 8.0
SSD_INNER = 1024
SSD_HEADS = 16
SSD_STATE = 64
SSD_CONV_CH = SSD_INNER + 4 * SSD_STATE
D_FF = 2816
LANES = 128
SUBLANES = 8
TOKEN_BLOCK = 512
ROW_TILE = 256
SSD_Q = 128
V_ROWS = HEAD_DIM + 16
VMEM_LIMIT = 56 * 1024 * 1024


def _params(sem, vmem=VMEM_LIMIT):
    return pltpu.CompilerParams(dimension_semantics=sem, vmem_limit_bytes=vmem)


def _const_spec(shape):
    nd = len(shape)
    return pl.BlockSpec(shape, lambda *_: (0,) * nd, pipeline_mode=pl.Buffered(1))


def _dot(a, b):
    return jnp.dot(a, b, preferred_element_type=F32)


def _dot_nt(a, b):
    return lax.dot_general(a, b, (((1,), (1,)), ((), ())), preferred_element_type=F32)


def _dot3(m, a):
    hi = a.astype(BF16)
    r1 = a - hi.astype(F32)
    mid = r1.astype(BF16)
    lo = (r1 - mid.astype(F32)).astype(BF16)
    return _dot(m, hi) + _dot(m, mid) + _dot(m, lo)


def _silu(x):
    return x * jax.nn.sigmoid(x)


def _softplus(x):
    return jnp.maximum(x, 0.0) + jnp.log1p(jnp.exp(-jnp.abs(x)))


def _gelu_tanh(x):
    c = math.sqrt(2.0 / math.pi)
    return 0.5 * x * (1.0 + jnp.tanh(c * (x + 0.044715 * (x * x * x))))


def _modnorm(x, g, shift, scale):
    ms = jnp.mean(x * x, axis=-1, keepdims=True)
    y = x * lax.rsqrt(ms + EPS) * g
    return y * (1.0 + scale) + shift


def _mod_kernel(cond_ref, w_ref, b_ref, o_ref):
    s = _silu(cond_ref[...])
    o_ref[0] = _dot(s.astype(BF16), w_ref[0].astype(BF16)) + b_ref[0]


def _modulation(cond, w_ada, b_ada):
    depth, d, n = w_ada.shape
    rows = cond.shape[0]
    tn = 3072
    return pl.pallas_call(
        _mod_kernel,
        grid=(depth, n // tn),
        in_specs=[pl.BlockSpec((rows, d), lambda l, j: (0, 0)),
                  pl.BlockSpec((1, d, tn), lambda l, j: (l, 0, j)),
                  pl.BlockSpec((1, 1, tn), lambda l, j: (l, 0, j))],
        out_specs=pl.BlockSpec((1, rows, tn), lambda l, j: (l, 0, j)),
        out_shape=jax.ShapeDtypeStruct((depth, rows, n), F32),
        compiler_params=_params(("arbitrary", "arbitrary")),
        name="modulation",
    )(cond, w_ada, b_ada.reshape(depth, 1, n))


def _mod_spec(mod):
    if mod.shape[0] == 1:
        return pl.BlockSpec((1, 6, D_MODEL), lambda b, i: (0, 0, 0))
    return pl.BlockSpec((1, 6, D_MODEL), lambda b, i: (b, 0, 0))


def _head_rmsnorm(x, gain):
    rows, width = x.shape
    low = lax.broadcasted_iota(jnp.int32, (rows, LANES), 1) < HEAD_DIM
    out = []
    for b in range(width // LANES):
        blk = x[:, b * LANES:(b + 1) * LANES]
        sq = blk * blk
        lo = jnp.sum(jnp.where(low, sq, 0.0), axis=-1, keepdims=True)
        hi = jnp.sum(jnp.where(low, 0.0, sq), axis=-1, keepdims=True)
        ms = jnp.where(low, lo, hi) * (1.0 / HEAD_DIM)
        out.append(blk * lax.rsqrt(ms + EPS) * gain[:, b * LANES:(b + 1) * LANES])
    return jnp.concatenate(out, axis=-1)


def _rope(x, cos, sin_signed):
    rows, width = x.shape
    lane = lax.broadcasted_iota(jnp.int32, (rows, LANES), 1)
    second = (lane % (2 * AX_PAIRS)) >= AX_PAIRS
    out = []
    for b in range(width // LANES):
        blk = x[:, b * LANES:(b + 1) * LANES]
        partner = jnp.where(second, pltpu.roll(blk, AX_PAIRS, 1), pltpu.roll(blk, LANES - AX_PAIRS, 1))
        out.append(blk * cos + partner * sin_signed)
    return jnp.concatenate(out, axis=-1)


def _even_pre_kernel(*refs, rope):
    if rope:
        x_ref, mod_ref, g_ref, w_ref, qg_ref, kg_ref, cos_ref, sin_ref, f_ref, q_ref, k_ref, v_ref = refs
    else:
        x_ref, mod_ref, g_ref, w_ref, qg_ref, kg_ref, f_ref, q_ref, k_ref, v_ref, kt_ref, vt_ref = refs
    h = _modnorm(x_ref[0], g_ref[...], mod_ref[0, 0:1, :], mod_ref[0, 1:2, :])
    p = _dot(h.astype(BF16), w_ref[...])
    f_ref[0] = p[:, :FOURIER_W]
    q = _head_rmsnorm(p[:, FOURIER_W:FOURIER_W + Q_W], qg_ref[...])
    k = _head_rmsnorm(p[:, FOURIER_W + Q_W:FOURIER_W + Q_W + KV_W], kg_ref[...])
    if rope:
        q = _rope(q, cos_ref[...], sin_ref[...])
        k = _rope(k, cos_ref[...], sin_ref[...])
    q_ref[0] = (q * (HEAD_DIM ** -0.5 * math.log2(math.e))).astype(BF16)
    v = p[:, FOURIER_W + Q_W + KV_W:]
    k_ref[0] = k
    v_ref[0] = v
    if not rope:
        kt_ref[0] = k.T
        vt_ref[0] = v.T


def _rope_tables(seq):
    pos = np.arange(seq)
    freqs = ROPE_THETA ** (-np.arange(AX_PAIRS, dtype=np.float64) / AX_PAIRS)
    ang = np.zeros((seq, HEAD_DIM))
    sign = np.zeros((HEAD_DIM,))
    for a, p_a in enumerate((pos // GRID_W, pos % GRID_W)):
        for t in range(2):
            lo = a * 2 * AX_PAIRS + t * AX_PAIRS
            ang[:, lo:lo + AX_PAIRS] = p_a[:, None] * freqs[None, :]
            sign[lo:lo + AX_PAIRS] = -1.0 if t == 0 else 1.0
    cos = np.tile(np.cos(ang), (1, 2)).astype(np.float32)
    sin = np.tile(np.sin(ang) * sign[None, :], (1, 2)).astype(np.float32)
    return jnp.asarray(cos), jnp.asarray(sin)


def _even_pre(x, mod, g, w_in, qg, kg, rope):
    bsz, seq, d = x.shape
    tm = ROW_TILE
    tok = lambda w: pl.BlockSpec((1, tm, w), lambda b, i: (b, i, 0))
    in_specs = [tok(d), _mod_spec(mod), _const_spec((1, d)), _const_spec((d, EVEN_IN)),
                _const_spec((1, Q_W)), _const_spec((1, KV_W))]
    args = [x, mod, g, w_in, qg, kg]
    if rope:
        cos, sin = _rope_tables(seq)
        in_specs += [pl.BlockSpec((tm, LANES), lambda b, i: (i, 0))] * 2
        args += [cos, sin]
    out_specs = [tok(FOURIER_W), tok(Q_W), tok(KV_W), tok(KV_W)]
    out_shape = [jax.ShapeDtypeStruct((bsz, seq, FOURIER_W), F32),
                 jax.ShapeDtypeStruct((bsz, seq, Q_W), BF16),
                 jax.ShapeDtypeStruct((bsz, seq, KV_W), F32),
                 jax.ShapeDtypeStruct((bsz, seq, KV_W), F32)]
    if not rope:
        out_specs += [pl.BlockSpec((1, KV_W, tm), lambda b, i: (b, 0, i))] * 2
        out_shape += [jax.ShapeDtypeStruct((bsz, KV_W, seq), F32)] * 2
    return pl.pallas_call(
        functools.partial(_even_pre_kernel, rope=rope),
        grid=(bsz, seq // tm),
        in_specs=in_specs,
        out_specs=out_specs,
        out_shape=out_shape,
        compiler_params=_params(("arbitrary", "arbitrary")),
        name="even_pre",
    )(*args)


def _attn_kernel(*refs, seq, has_ctx):
    if has_ctx:
        q_ref, k_ref, v_ref, ck_ref, cv_ref, f_ref, cs_ref, cbd_ref, o_ref, kop, vopt, g_scr, cs_scr = refs
    else:
        q_ref, k_ref, v_ref, f_ref, cs_ref, cbd_ref, o_ref, kop, vopt, g_scr, cs_scr = refs
    tq = q_ref.shape[1]

    @pl.when((pl.program_id(0) == 0) & (pl.program_id(1) == 0))
    def _round_tables():
        for r0 in range(0, seq, tq):
            cs_scr[r0:r0 + tq, :] = cs_ref[r0:r0 + tq, :].astype(BF16)

    @pl.when(pl.program_id(1) == 0)
    def _prepare():
        def put(kx, vx, r0, transposed):
            n = kx.shape[1] if transposed else kx.shape[0]
            low = lax.broadcasted_iota(jnp.int32, (n, LANES), 1) < HEAD_DIM
            ones = jnp.ones((V_ROWS - HEAD_DIM, n), BF16)
            for jp in range(N_KV_HEADS // 2):
                kp = kx[jp * LANES:(jp + 1) * LANES, :].T if transposed else kx[:, jp * LANES:(jp + 1) * LANES]
                k_lo = jnp.where(low, kp, 0.0)
                k_hi = jnp.where(low, 0.0, kp)
                placed = (k_lo, pltpu.roll(k_lo, HEAD_DIM, 1), pltpu.roll(k_hi, HEAD_DIM, 1), k_hi)
                for t, kk in enumerate(placed):
                    kop[4 * jp + t, r0:r0 + n, :] = kk.astype(BF16)
                vt = vx[jp * LANES:(jp + 1) * LANES, :] if transposed else vx[:, jp * LANES:(jp + 1) * LANES].T
                vt = vt.astype(BF16)
                for t in range(2):
                    vopt[2 * jp + t, 0:HEAD_DIM, r0:r0 + n] = vt[t * HEAD_DIM:(t + 1) * HEAD_DIM]
                    vopt[2 * jp + t, HEAD_DIM:V_ROWS, r0:r0 + n] = ones

        put(k_ref[0], v_ref[0], 0, False)
        if has_ctx:
            put(ck_ref[0], cv_ref[0], seq, True)
        f = f_ref[0].astype(BF16)
        g_scr[0:seq, :] = _dot(f, cbd_ref[0].astype(BF16)).astype(BF16)
        g_scr[seq:2 * seq, :] = _dot(f, cbd_ref[1].astype(BF16)).astype(BF16)

    r0 = pl.multiple_of(pl.program_id(1) * tq, tq)
    o_ref[0, :, 0:FOURIER_W] = _dot(cs_scr[pl.ds(r0, tq), :], g_scr[...]).astype(BF16)
    q = q_ref[0]
    n_pairs = N_Q_HEADS // 2
    kv_of = lambda qp: [(2 * qp + half) // (N_Q_HEADS // N_KV_HEADS) for half in range(2)]

    def pair_scores(qp):
        qpair = q[:, qp * LANES:(qp + 1) * LANES]
        return [_dot_nt(qpair, kop[2 * kv_of(qp)[half] + half]) for half in range(2)]

    scores_next = pair_scores(0)
    for qp in range(n_pairs):
        kv = kv_of(qp)
        scores = scores_next
        if qp + 1 < n_pairs:
            scores_next = pair_scores(qp + 1)
        probs = [jnp.exp2(s - jnp.max(s, axis=-1, keepdims=True)).astype(BF16) for s in scores]
        outs = []
        for half in range(2):
            o = _dot_nt(vopt[kv[half]], probs[half])
            outs.append(o[0:HEAD_DIM] / o[HEAD_DIM:HEAD_DIM + 1])
        res = jnp.concatenate(outs, axis=0)
        o_ref[0, :, FOURIER_W + qp * LANES:FOURIER_W + (qp + 1) * LANES] = res.T.astype(BF16)


def _dft_tables(seq):
    s = np.arange(seq)
    ang = 2.0 * np.pi * ((s[:, None] * s[None, :]) % seq) / seq
    pos = np.concatenate([np.cos(ang), -np.sin(ang)], axis=1) / math.sqrt(seq)
    c = np.arange(FOURIER_GROUP_W)
    ang_c = 2.0 * np.pi * ((c[:, None] * c[None, :]) % FOURIER_GROUP_W) / FOURIER_GROUP_W
    eye = np.eye(FOURIER_W // FOURIER_GROUP_W)
    chan = np.stack([np.kron(eye, np.cos(ang_c)), np.kron(eye, np.sin(ang_c))]) / math.sqrt(FOURIER_GROUP_W)
    return jnp.asarray(pos, dtype=F32), jnp.asarray(chan, dtype=F32)


def _even_mixer(q, k, v, f, ctx):
    bsz, seq, _ = q.shape
    tq = ROW_TILE
    has_ctx = ctx is not None
    past = ctx[0].shape[2] if has_ctx else 0
    kt = seq + past
    pos_dft, chan_dft = _dft_tables(seq)
    full = lambda n, w: pl.BlockSpec((1, n, w), lambda b, i: (b, 0, 0))
    in_specs = [pl.BlockSpec((1, tq, Q_W), lambda b, i: (b, i, 0)), full(seq, KV_W), full(seq, KV_W)]
    args = [q, k, v]
    if has_ctx:
        in_specs += [full(KV_W, past), full(KV_W, past)]
        args += list(ctx)
    in_specs += [full(seq, FOURIER_W), _const_spec((seq, 2 * seq)),
                 _const_spec((2, FOURIER_W, FOURIER_W))]
    args += [f, pos_dft, chan_dft]
    return pl.pallas_call(
        functools.partial(_attn_kernel, seq=seq, has_ctx=has_ctx),
        grid=(bsz, seq // tq),
        in_specs=in_specs,
        out_specs=pl.BlockSpec((1, tq, D_MODEL), lambda b, i: (b, i, 0)),
        out_shape=jax.ShapeDtypeStruct((bsz, seq, D_MODEL), BF16),
        scratch_shapes=[pltpu.VMEM((2 * N_KV_HEADS, kt, LANES), BF16),
                        pltpu.VMEM((N_KV_HEADS, V_ROWS, kt), BF16),
                        pltpu.VMEM((2 * seq, FOURIER_W), BF16),
                        pltpu.VMEM((seq, 2 * seq), BF16)],
        compiler_params=_params(("arbitrary", "arbitrary")),
        name="even_mixer",
    )(*args)


def _post_ffn_kernel(*refs, odd):
    if odd:
        (x_ref, ylru_ref, g_ref, yssd_ref, z_ref, sn_ref, mod_ref, wout_ref, gf_ref,
         w1_ref, w3_ref, w2_ref, o_ref) = refs
        y_lru = ylru_ref[0] * _gelu_tanh(g_ref[0])
        y = yssd_ref[0] * _silu(z_ref[0])
        ms = jnp.mean(y * y, axis=-1, keepdims=True)
        y_ssd = y * lax.rsqrt(ms + EPS) * sn_ref[...]
        mix = jnp.concatenate([y_lru, y_ssd], axis=-1).astype(BF16)
    else:
        x_ref, mix_ref, mod_ref, wout_ref, gf_ref, w1_ref, w3_ref, w2_ref, o_ref = refs
        mix = mix_ref[0]
    x = x_ref[0] + mod_ref[0, 2:3, :] * _dot(mix, wout_ref[...])
    h = _modnorm(x, gf_ref[...], mod_ref[0, 3:4, :], mod_ref[0, 4:5, :]).astype(BF16)
    a = (_silu(_dot(h, w1_ref[...])) * _dot(h, w3_ref[...])).astype(BF16)
    o_ref[0] = x + mod_ref[0, 5:6, :] * _dot(a, w2_ref[...])


def _post_ffn(x, mix_args, mod, w_out, g_ffn, w1, w3, w2, layer, ssd_norm=None):
    bsz, seq, d = x.shape
    if mod.shape[0] == 1 and seq < TOKEN_BLOCK:
        fold = TOKEN_BLOCK // seq
        merge = lambda t: t.reshape(bsz // fold, fold * seq, t.shape[-1])
        out = _post_ffn(merge(x), jax.tree.map(merge, mix_args), mod, w_out, g_ffn, w1, w3, w2, layer, ssd_norm)
        return out.reshape(bsz, seq, d)
    tm = min(seq, TOKEN_BLOCK)
    odd = layer % 2 == 1
    tok = lambda w: pl.BlockSpec((1, tm, w), lambda b, i: (b, i, 0))
    slab = lambda w: pl.BlockSpec((None,) + w.shape[1:], lambda b, i: (layer, 0, 0), pipeline_mode=pl.Buffered(1))
    if odd:
        ylru, g, yssd, z = mix_args
        in_specs = [tok(d), tok(LRU_W), tok(LRU_W), tok(SSD_INNER), tok(SSD_INNER), _const_spec((1, SSD_INNER))]
        args = [x, ylru, g, yssd, z, ssd_norm]
    else:
        in_specs = [tok(d), tok(d)]
        args = [x, mix_args]
    in_specs += [_mod_spec(mod), _const_spec(w_out.shape), _const_spec((1, d)),
                 slab(w1), slab(w3), slab(w2)]
    args += [mod, w_out, g_ffn, w1, w3, w2]
    return pl.pallas_call(
        functools.partial(_post_ffn_kernel, odd=odd),
        grid=(bsz, seq // tm),
        in_specs=in_specs,
        out_specs=tok(d),
        out_shape=jax.ShapeDtypeStruct((bsz, seq, d), F32),
        compiler_params=_params(("arbitrary", "arbitrary")),
        name="post_ffn_odd" if odd else "post_ffn_even",
    )(*args)


XL_LO = LRU_W
Z_LO = 2 * LRU_W
XBC_LO = 2 * LRU_W + SSD_INNER
DT_LO = XBC_LO + SSD_CONV_CH


def _conv_block(main, edges, w, b):
    before, after0, after1 = edges
    rows = main.shape[0]
    row = lax.broadcasted_iota(jnp.int32, (SUBLANES, 1), 0)
    last = rows - SUBLANES

    def patch_first(x, fix):
        return jnp.concatenate([fix(x[:SUBLANES]), x[SUBLANES:]], axis=0)

    def patch_last(x, fix):
        return jnp.concatenate([x[:last], fix(x[last:])], axis=0)

    xm1 = patch_first(pltpu.roll(main, 1, 0), lambda t: jnp.where(row == 0, before, t))
    xp1 = patch_last(pltpu.roll(main, rows - 1, 0), lambda t: jnp.where(row == SUBLANES - 1, after0, t))
    xp2 = patch_last(pltpu.roll(main, rows - 2, 0),
                     lambda t: jnp.where(row == SUBLANES - 2, after0, jnp.where(row == SUBLANES - 1, after1, t)))
    return xm1 * w[0:1] + main * w[1:2] + xp1 * w[2:3] + xp2 * w[3:4] + b


def _odd_pre_kernel(x_ref, xprev_ref, xnext_ref, mod_ref, g_ref, w_ref, wdt_ref, cwl_ref, cbl_ref, wg_ref, bg_ref,
                    lam_ref, gate_ref, z_ref, a_ref, b_ref, xbc_ref, dt_ref, w_scr):
    i = pl.program_id(1)
    wide = 2 * LANES

    @pl.when((pl.program_id(0) == 0) & (i == 0))
    def _():
        for lo in range(0, DT_LO, wide):
            w_scr[:, lo:lo + wide] = w_ref[:, lo:lo + wide].astype(BF16)

    g, shift, scale = g_ref[...], mod_ref[0, 0:1, :], mod_ref[0, 1:2, :]
    h = _modnorm(x_ref[0], g, shift, scale).astype(BF16)
    proj = lambda lo: _dot(h, w_scr[:, lo:lo + wide])
    p_xl = [proj(XL_LO + t * wide) for t in range(LRU_W // wide)]
    others = ([(gate_ref, t * wide, t * wide) for t in range(LRU_W // wide)]
              + [(z_ref, Z_LO + t * wide, t * wide) for t in range(SSD_INNER // wide)]
              + [(xbc_ref, XBC_LO + t * wide, t * wide) for t in range(SSD_CONV_CH // wide)])

    halo = jnp.concatenate([xprev_ref[0], xnext_ref[0]], axis=0)
    ph = _dot(_modnorm(halo, g, shift, scale).astype(BF16), w_scr[:, XL_LO:Z_LO])
    has_prev = i > 0
    has_next = i < pl.num_programs(1) - 1
    e_l = (jnp.where(has_prev, ph[SUBLANES - 1:SUBLANES], 0.0),
           jnp.where(has_next, ph[SUBLANES:SUBLANES + 1], 0.0),
           jnp.where(has_next, ph[SUBLANES + 1:SUBLANES + 2], 0.0))

    rate = (0.5 * LRU_C) * _softplus(-lam_ref[...])
    n_lru = LRU_W // LANES
    for lb in range(n_lru):
        sl = slice(lb * LANES, (lb + 1) * LANES)
        for dst, lo, col in others[lb * len(others) // n_lru:(lb + 1) * len(others) // n_lru]:
            dst[0, :, col:col + wide] = proj(lo)
        xb = _conv_block(p_xl[lb // 2][:, (lb % 2) * LANES:(lb % 2 + 1) * LANES], [e[:, sl] for e in e_l],
                         cwl_ref[:, sl], cbl_ref[:, sl])
        xh = 0.5 * xb
        gt = jnp.tanh(_dot(xb.astype(BF16), wg_ref[lb]) + bg_ref[lb])
        for d in range(2):
            neg_log_a = rate[d:d + 1, sl] * (gt[:, 2 * d * LANES:(2 * d + 1) * LANES] + 1.0)
            a = jnp.exp2(neg_log_a * (-math.log2(math.e)))
            g2 = jnp.tanh(neg_log_a) * (a * a + 1.0)
            root = jnp.where(g2 > 0.0, g2 * lax.rsqrt(g2), 0.0)
            a_ref[0, d, :, sl] = a
            b_ref[0, d, :, sl] = root * ((gt[:, (2 * d + 1) * LANES:(2 * d + 2) * LANES] + 1.0) * xh)
    dt_ref[0] = _dot(h, wdt_ref[...])


def _odd_pre(x, mod, g, w_in, w_dt, lru):
    bsz, seq, d = x.shape
    tm = min(seq, TOKEN_BLOCK)
    per8 = tm // SUBLANES
    tok = lambda w: pl.BlockSpec((1, tm, w), lambda b, i: (b, i, 0))
    both = pl.BlockSpec((1, 2, tm, LRU_W), lambda b, i: (b, 0, i, 0))
    consts = list(lru)
    out_specs = [tok(LRU_W), tok(SSD_INNER), both, both, tok(SSD_CONV_CH), tok(LANES)]
    row = lambda w: jax.ShapeDtypeStruct((bsz, seq, w), F32)
    out_shape = ([row(LRU_W), row(SSD_INNER)] + [jax.ShapeDtypeStruct((bsz, 2, seq, LRU_W), F32)] * 2
                 + [row(SSD_CONV_CH), row(LANES)])
    return pl.pallas_call(
        _odd_pre_kernel,
        grid=(bsz, seq // tm),
        in_specs=[tok(d),
                  pl.BlockSpec((1, SUBLANES, d), lambda b, i: (b, jnp.maximum(i * per8 - 1, 0), 0)),
                  pl.BlockSpec((1, SUBLANES, d), lambda b, i: (b, jnp.minimum((i + 1) * per8, seq // SUBLANES - 1), 0)),
                  _mod_spec(mod), _const_spec((1, d)), _const_spec(w_in.shape), _const_spec(w_dt.shape)]
                 + [_const_spec(c.shape) for c in consts],
        out_specs=out_specs,
        out_shape=out_shape,
        scratch_shapes=[pltpu.VMEM((d, DT_LO), BF16)],
        compiler_params=_params(("arbitrary", "arbitrary")),
        name="odd_pre",
    )(x, x, x, mod, g, w_in, w_dt, *consts)


def _ssd_kernel(*refs, seq, has_h0):
    xbc_ref, dt_ref, cw_ref, cb_ref, dtb_ref, alog_ref, dexp_ref, ef_ref, eb_ref, la_ref, lb_ref = refs[:11]
    if has_h0:
        h0_ref, lh0_ref, y_ref, ylru_ref = refs[11:15]
    else:
        y_ref, ylru_ref, hout_ref, lhout_ref = refs[11:15]
    xs_ref, bm_ref, cm_ref, cs_ref, dts_ref, cst_ref, dtt_ref, bt_ref, st_scr = refs[15:]
    q = SSD_Q
    nc = seq // q
    lane = lax.broadcasted_iota(jnp.int32, (q, LANES), 1)
    qi = lax.broadcasted_iota(jnp.int32, (q, q), 0)
    ki = lax.broadcasted_iota(jnp.int32, (q, q), 1)
    tri_lower = jnp.where(ki <= qi, 1.0, 0.0).astype(BF16)
    tri_upper = jnp.where(ki >= qi, 1.0, 0.0).astype(BF16)
    a_log2 = jnp.where(lane[0:1, :] < 2 * SSD_HEADS, -math.log2(math.e) * jnp.exp(alog_ref[...]), 0.0)

    def prepare(c):
        r0 = pl.multiple_of(c * q, q)
        rows = pl.ds(r0, q)
        prev8 = pl.ds(pl.multiple_of(jnp.maximum(r0 - SUBLANES, 0), SUBLANES), SUBLANES)
        next8 = pl.ds(pl.multiple_of(jnp.minimum(r0 + q, seq - SUBLANES), SUBLANES), SUBLANES)
        has_prev = r0 > 0
        has_next = r0 + q < seq
        for lb in range(SSD_CONV_CH // LANES):
            sl = slice(lb * LANES, (lb + 1) * LANES)
            before = xbc_ref[0, prev8, sl]
            after = xbc_ref[0, next8, sl]
            edges = (jnp.where(has_prev, before[SUBLANES - 1:SUBLANES], 0.0),
                     jnp.where(has_next, after[0:1], 0.0), jnp.where(has_next, after[1:2], 0.0))
            xc = _conv_block(xbc_ref[0, rows, sl], edges, cw_ref[:, sl], cb_ref[:, sl])
            half_xc = 0.5 * xc
            xc = half_xc * (jnp.tanh(half_xc) + 1.0)
            if lb < SSD_INNER // LANES:
                xs_ref[rows, sl] = xc
                y_ref[0, rows, sl] = dexp_ref[:, sl] * xc
            elif lb == SSD_INNER // LANES:
                bm_ref[rows, :] = xc
                bt_ref[c] = xc.T
            else:
                cm_ref[rows, :] = xc
        dts = _softplus(dt_ref[0, rows, :] + dtb_ref[...])
        da = dts * a_log2
        cs = jnp.where(lane < SSD_HEADS, _dot3(tri_lower, da), _dot3(tri_upper, da))
        cs_ref[rows, :] = cs
        dts_ref[rows, :] = dts
        cst_ref[c] = cs.T
        dtt_ref[c] = dts.T

    for d in range(2):
        st_scr[d] = h0_ref[0, d].T if has_h0 else jnp.zeros(st_scr.shape[1:], F32)

    low = lane < SSD_STATE
    col = lax.broadcasted_iota(jnp.int32, (SSD_STATE, SSD_INNER), 1)
    first_group = col < SSD_INNER // 2

    def chunk(c, fwd):
        d = 0 if fwd else 1
        r0 = pl.multiple_of(c * q, q)
        rows = pl.ds(r0, q)
        cs = cs_ref[rows, :]
        dts = dts_ref[rows, :]
        cst = cst_ref[c]
        dtt = dtt_ref[c]
        xs = xs_ref[rows, :]
        xsb = xs.astype(BF16)
        bb = bm_ref[rows, :].astype(BF16)
        cb_ = cm_ref[rows, :].astype(BF16)
        zero = jnp.zeros_like(cb_)
        cbg = [_dot_nt(jnp.where(low, cb_, zero), bb).astype(BF16),
               _dot_nt(jnp.where(low, zero, cb_), bb).astype(BF16)]
        mask = (ki <= qi) if fwd else (ki >= qi)
        expand = ef_ref[...] if fwd else eb_ref[...]
        st = st_scr[d]
        hh = jnp.concatenate([jnp.where(first_group, st, 0.0), jnp.where(first_group, 0.0, st)], axis=0)
        tot = cs[q - 1:q, :] if fwd else cs[0:1, :]
        mine = (lane >= d * SSD_HEADS) & (lane < (d + 1) * SSD_HEADS)
        w = jnp.exp2(jnp.where(mine, tot - cs, 0.0)) * dts
        dec = jnp.broadcast_to(jnp.exp2(tot), (2 * SUBLANES, LANES))
        dec_hi = dec.astype(BF16)
        dec_lo = (dec - dec_hi.astype(F32)).astype(BF16)
        spread = _dot(jnp.concatenate([jnp.exp2(cs).astype(BF16), w.astype(BF16), dec_hi, dec_lo], axis=0), expand)
        y = _dot(cb_, hh.astype(BF16)) * spread[0:q]
        pairs = []
        for pp in range(SSD_HEADS // 2):
            ms = []
            for h in (2 * pp, 2 * pp + 1):
                l = d * SSD_HEADS + h
                decay = jnp.exp2((cs[:, l:l + 1] - cst[l:l + 1, :]).astype(BF16))
                decay = jnp.where(mask, decay, jnp.zeros_like(decay))
                ms.append(cbg[pp // (SSD_HEADS // 4)] * decay * dtt[l:l + 1, :].astype(BF16))
            xp = xsb[:, pp * LANES:(pp + 1) * LANES]
            zx = jnp.zeros_like(xp)
            x2 = jnp.concatenate([jnp.where(low, xp, zx), jnp.where(low, zx, xp)], axis=0)
            pairs.append(_dot(jnp.concatenate(ms, axis=1), x2))
        y_ref[0, rows, :] += y + jnp.concatenate(pairs, axis=1)
        xw = (xs * spread[q:2 * q]).astype(BF16)
        bt = bt_ref[c].astype(BF16)
        half_w = SSD_INNER // 2
        st_new = jnp.concatenate([_dot(bt[:SSD_STATE], xw[:, :half_w]), _dot(bt[SSD_STATE:], xw[:, half_w:])], axis=1)
        chunk_decay = spread[2 * q:2 * q + 1] + spread[2 * q + 2 * SUBLANES:2 * q + 2 * SUBLANES + 1]
        st_scr[d] = chunk_decay * st + st_new

    def prepare_all(c, carry):
        prepare(c)
        return carry

    nb = seq // SUBLANES
    per_chunk = q // SUBLANES
    row = lax.broadcasted_iota(jnp.int32, (SUBLANES, LRU_W), 0)

    def lru_block(blk, hf, hb):
        rf = pl.multiple_of(blk * SUBLANES, SUBLANES)
        rb = pl.multiple_of((nb - 1 - blk) * SUBLANES, SUBLANES)
        a = la_ref[0, 0, pl.ds(rf, SUBLANES), :]
        b = lb_ref[0, 0, pl.ds(rf, SUBLANES), :]
        for k in (1, 2, 4):
            keep = row >= k
            b = a * jnp.where(keep, pltpu.roll(b, k, 0), 0.0) + b
            a = a * jnp.where(keep, pltpu.roll(a, k, 0), 1.0)
        h = a * hf + b
        ylru_ref[0, pl.ds(rf, SUBLANES), :] += h
        hf = h[SUBLANES - 1:SUBLANES, :]
        a = la_ref[0, 1, pl.ds(rb, SUBLANES), :]
        b = lb_ref[0, 1, pl.ds(rb, SUBLANES), :]
        for k in (1, 2, 4):
            keep = row < SUBLANES - k
            b = a * jnp.where(keep, pltpu.roll(b, SUBLANES - k, 0), 0.0) + b
            a = a * jnp.where(keep, pltpu.roll(a, SUBLANES - k, 0), 1.0)
        h = a * hb + b
        ylru_ref[0, pl.ds(rb, SUBLANES), :] += h
        return hf, h[0:1, :]

    def both_directions(i, carry):
        chunk(i, True)
        chunk(nc - 1 - i, False)
        hf, hb = carry
        for j in range(per_chunk):
            hf, hb = lru_block(i * per_chunk + j, hf, hb)
        return hf, hb

    ylru_ref[0] = jnp.zeros((seq, LRU_W), F32)
    lax.fori_loop(0, nc, prepare_all, 0)
    if has_h0:
        lru_start = (lh0_ref[0, 0:1, :], lh0_ref[0, 1:2, :])
    else:
        lru_start = (jnp.zeros((1, LRU_W), F32), jnp.zeros((1, LRU_W), F32))
    hf, hb = lax.fori_loop(0, nc, both_directions, lru_start)
    if not has_h0:
        for d in range(2):
            hout_ref[0, d] = st_scr[d].T
        lhout_ref[0, 0:1, :] = hf
        lhout_ref[0, 1:2, :] = hb


def _ssd(xbc, dt, cw, cb, dtb, alog, dexp, lru_a, lru_b, h0, lru_h0):
    bsz, seq, _ = xbc.shape
    nc = seq // SSD_Q
    expand = np.zeros((2, LANES, SSD_INNER), np.float32)
    for d in range(2):
        for h in range(SSD_HEADS):
            expand[d, d * SSD_HEADS + h, h * 64:(h + 1) * 64] = 1.0
    expand = jnp.asarray(expand, dtype=BF16)
    per_seq = lambda n, w: pl.BlockSpec((1, n, w), lambda b: (b, 0, 0))
    state_spec = pl.BlockSpec((1, 2, SSD_INNER, SSD_STATE), lambda b: (b, 0, 0, 0))
    has_h0 = h0 is not None
    coef = pl.BlockSpec((1, 2, seq, LRU_W), lambda b: (b, 0, 0, 0))
    in_specs = [per_seq(seq, SSD_CONV_CH), per_seq(seq, LANES), _const_spec(cw.shape), _const_spec(cb.shape),
                _const_spec(dtb.shape), _const_spec(alog.shape), _const_spec(dexp.shape),
                _const_spec((LANES, SSD_INNER)), _const_spec((LANES, SSD_INNER)), coef, coef]
    args = [xbc, dt, cw, cb, dtb, alog, dexp, expand[0], expand[1], lru_a, lru_b]
    out_specs = [per_seq(seq, SSD_INNER), per_seq(seq, LRU_W)]
    out_shape = [jax.ShapeDtypeStruct((bsz, seq, SSD_INNER), F32), jax.ShapeDtypeStruct((bsz, seq, LRU_W), F32)]
    if has_h0:
        in_specs += [state_spec, per_seq(2, LRU_W)]
        args += [h0, lru_h0]
    else:
        out_specs += [state_spec, per_seq(2, LRU_W)]
        out_shape += [jax.ShapeDtypeStruct((bsz, 2, SSD_INNER, SSD_STATE), F32),
                      jax.ShapeDtypeStruct((bsz, 2, LRU_W), F32)]
    outs = pl.pallas_call(
        functools.partial(_ssd_kernel, seq=seq, has_h0=has_h0),
        grid=(bsz,),
        in_specs=in_specs,
        out_specs=out_specs,
        out_shape=out_shape,
        scratch_shapes=[pltpu.VMEM((seq, SSD_INNER), F32)] + [pltpu.VMEM((seq, LANES), F32)] * 4
                       + [pltpu.VMEM((nc, LANES, SSD_Q), F32)] * 3 + [pltpu.VMEM((2, SSD_STATE, SSD_INNER), F32)],
        compiler_params=_params(("arbitrary",)),
        name="ssd",
    )(*args)
    return (outs[0], outs[1], None, None) if has_h0 else tuple(outs)


def _lru_gate_weights(wa, ba, wx, bx):
    eye = 0.5 * jnp.eye(2, dtype=F32)

    def pair_blocks(w):
        w4 = w.reshape(4, 2, 64, 64)
        return jnp.einsum("laij,ab->laibj", w4, eye).reshape(4, LANES, LANES)

    mats = [pair_blocks(w) for w in (wa[0], wx[0], wa[1], wx[1])]
    wg = jnp.concatenate(mats, axis=-1).astype(BF16)
    bias = [0.5 * b.reshape(4, 1, LANES) for b in (ba[0], bx[0], ba[1], bx[1])]
    return wg, jnp.concatenate(bias, axis=-1)


def _trunk(x, mods, p, ctx):
    is_ctx = ctx is None
    bsz, seq, _ = x.shape
    f, q, k, v, *kv_t = _even_pre(x, mods[0], p["norm_mix"][0], p["w_in_even"], p["q_gain"], p["k_gain"],
                                  rope=not is_ctx)
    mix = _even_mixer(q, k, v, f, None if is_ctx else (ctx["k"], ctx["v"]))
    x = _post_ffn(x, mix, mods[0], p["w_out_even"], p["norm_ffn"][0], p["w1"], p["w3"], p["w2"], layer=0)
    gate, z, lru_a, lru_b, xbc, dt = _odd_pre(
        x, mods[1], p["norm_mix"][1], p["w_in_odd"], p["w_dt_odd"],
        lru=(p["conv_lru_w"], p["conv_lru_b"], p["lru_wg"], p["lru_bg"], p["lru_lambda"]))
    h0_lru, h0_ssd = (None, None) if is_ctx else (ctx["lru"], ctx["ssd"])
    yssd, ylru, ssd_state, lru_state = _ssd(xbc, dt, p["conv_ssd_w"], p["conv_ssd_b"], p["ssd_dtb"], p["ssd_alog"],
                                            p["ssd_dexp"], lru_a, lru_b, h0_ssd, h0_lru)
    x = _post_ffn(x, (ylru, gate, yssd, z), mods[1], p["w_out_odd"], p["norm_ffn"][1], p["w1"], p["w3"], p["w2"],
                  layer=1, ssd_norm=p["ssd_norm"])
    kt, vt = kv_t if is_ctx else (None, None)
    return x, kt, vt, lru_state, ssd_state


def kernel(x_prompt, x_sample, c, cache_k, cache_v, state_lru, state_ssd, c_ctx, w_ada, b_ada, norm_mix, norm_ffn,
           w_in_even, q_norm, k_norm, w_out_even, w_in_odd, conv_lru_w, conv_lru_b, lru_wa, lru_ba, lru_wx, lru_bx,
           lru_lambda, conv_ssd_w, conv_ssd_b, ssd_dt_bias, ssd_a_log, ssd_d, ssd_norm, w_out_odd, ffn_w1, ffn_w3,
           ffn_w2):
    bsz, seq, d = x_prompt.shape
    dec_b = x_sample.shape[0]
    past = cache_k.shape[2]

    cond_rows = 16
    cond = jnp.concatenate([c, c_ctx[None, :], jnp.zeros((cond_rows - dec_b - 1, d), F32)], axis=0)
    mod = _modulation(cond, w_ada, b_ada).reshape(2, cond_rows, 6, d)
    mods_sample = [mod[l, :dec_b] for l in range(2)]
    mods_prompt = [mod[l, dec_b:dec_b + 1] for l in range(2)]

    lru_wg, lru_bg = _lru_gate_weights(lru_wa[0], lru_ba[0], lru_wx[0], lru_bx[0])
    pad32 = lambda a: jnp.pad(a.reshape(1, 2 * SSD_HEADS), ((0, 0), (0, LANES - 2 * SSD_HEADS)))
    p = dict(
        norm_mix=norm_mix.reshape(2, 1, d), norm_ffn=norm_ffn.reshape(2, 1, d),
        w_in_even=w_in_even[0].astype(BF16),
        q_gain=jnp.tile(q_norm[0], N_Q_HEADS)[None, :], k_gain=jnp.tile(k_norm[0], N_KV_HEADS)[None, :],
        w_out_even=w_out_even[0].astype(BF16),
        w_in_odd=w_in_odd[0],
        w_dt_odd=jnp.pad(w_in_odd[0, :, DT_LO:], ((0, 0), (0, LANES - 2 * SSD_HEADS))).astype(BF16),
        conv_lru_w=conv_lru_w[0], conv_lru_b=conv_lru_b[0][None, :],
        lru_wg=lru_wg, lru_bg=lru_bg, lru_lambda=lru_lambda[0],
        conv_ssd_w=conv_ssd_w[0], conv_ssd_b=conv_ssd_b[0][None, :],
        ssd_dtb=pad32(ssd_dt_bias[0]), ssd_alog=pad32(ssd_a_log[0]),
        ssd_dexp=jnp.repeat(ssd_d[0], SSD_INNER // SSD_HEADS)[None, :],
        ssd_norm=ssd_norm[0][None, :],
        w_out_odd=w_out_odd[0].astype(BF16),
        w1=ffn_w1.astype(BF16), w3=ffn_w3.astype(BF16), w2=ffn_w2.astype(BF16),
    )

    y_prompt, k_new, v_new, lru_new, ssd_new = _trunk(x_prompt, mods_prompt, p, None)
    ctx = dict(
        k=cache_k[:, 0].transpose(0, 2, 3, 1).reshape(dec_b, KV_W, past),
        v=cache_v[:, 0].transpose(0, 2, 3, 1).reshape(dec_b, KV_W, past),
        lru=state_lru[:, 0],
        ssd=state_ssd[:, 0].reshape(dec_b, 2, SSD_INNER, SSD_STATE),
    )
    y_sample, _, _, _, _ = _trunk(x_sample, mods_sample, p, ctx)

    new_k = k_new.transpose(0, 2, 1).reshape(bsz, 1, seq, N_KV_HEADS, HEAD_DIM)
    new_v = v_new.transpose(0, 2, 1).reshape(bsz, 1, seq, N_KV_HEADS, HEAD_DIM)
    new_lru = lru_new.reshape(bsz, 1, 2, LRU_W)
    new_ssd = ssd_new.reshape(bsz, 1, 2, SSD_HEADS, SSD_INNER // SSD_HEADS, SSD_STATE)
    return (y_prompt, y_sample, new_k, new_v, new_lru, new_ssd)
```

```python
import functools
import math

import numpy as np
import jax
import jax.numpy as jnp
from jax import lax
from jax.experimental import pallas as pl
from jax.experimental.pallas import tpu as pltpu

F32 = jnp.float32
BF16 = jnp.bfloat16

D_MODEL = 1024
EPS = 1e-6
GRID_W = 64
HEAD_DIM = 64
N_Q_HEADS = 12
N_KV_HEADS = 4
FOURIER_W = 256
FOURIER_GROUP_W = 64
Q_W = N_Q_HEADS * HEAD_DIM
KV_W = N_KV_HEADS * HEAD_DIM
EVEN_IN = FOURIER_W + Q_W + 2 * KV_W
ROPE_THETA = 10000.0
AX_PAIRS = HEAD_DIM // 4
LRU_W = 512
LRU_C = 8.0
SSD_INNER = 1024
SSD_HEADS = 16
SSD_STATE = 64
SSD_CONV_CH = SSD_INNER + 4 * SSD_STATE
D_FF = 2816
LANES = 128
SUBLANES = 8
TOKEN_BLOCK = 512
ROW_TILE = 256
SSD_Q = 128
V_ROWS = HEAD_DIM + 16
VMEM_LIMIT = 56 * 1024 * 1024


def _params(sem, vmem=VMEM_LIMIT):
    return pltpu.CompilerParams(dimension_semantics=sem, vmem_limit_bytes=vmem)


def _const_spec(shape):
    nd = len(shape)
    return pl.BlockSpec(shape, lambda *_: (0,) * nd, pipeline_mode=pl.Buffered(1))


def _dot(a, b):
    return jnp.dot(a, b, preferred_element_type=F32)


def _dot_nt(a, b):
    return lax.dot_general(a, b, (((1,), (1,)), ((), ())), preferred_element_type=F32)


def _dot3(m, a):
    hi = a.astype(BF16)
    r1 = a - hi.astype(F32)
    mid = r1.astype(BF16)
    lo = (r1 - mid.astype(F32)).astype(BF16)
    return _dot(m, hi) + _dot(m, mid) + _dot(m, lo)


def _silu(x):
    return x * jax.nn.sigmoid(x)


def _softplus(x):
    return jnp.maximum(x, 0.0) + jnp.log1p(jnp.exp(-jnp.abs(x)))


def _gelu_tanh(x):
    c = math.sqrt(2.0 / math.pi)
    return 0.5 * x * (1.0 + jnp.tanh(c * (x + 0.044715 * (x * x * x))))


def _modnorm(x, g, shift, scale):
    ms = jnp.mean(x * x, axis=-1, keepdims=True)
    y = x * lax.rsqrt(ms + EPS) * g
    return y * (1.0 + scale) + shift


def _mod_kernel(cond_ref, w_ref, b_ref, o_ref):
    s = _silu(cond_ref[...])
    o_ref[0] = _dot(s.astype(BF16), w_ref[0].astype(BF16)) + b_ref[0]


def _modulation(cond, w_ada, b_ada):
    depth, d, n = w_ada.shape
    rows = cond.shape[0]
    tn = 3072
    return pl.pallas_call(
        _mod_kernel,
        grid=(depth, n // tn),
        in_specs=[pl.BlockSpec((rows, d), lambda l, j: (0, 0)),
                  pl.BlockSpec((1, d, tn), lambda l, j: (l, 0, j)),
                  pl.BlockSpec((1, 1, tn), lambda l, j: (l, 0, j))],
        out_specs=pl.BlockSpec((1, rows, tn), lambda l, j: (l, 0, j)),
        out_shape=jax.ShapeDtypeStruct((depth, rows, n), F32),
        compiler_params=_params(("arbitrary", "arbitrary")),
        name="modulation",
    )(cond, w_ada, b_ada.reshape(depth, 1, n))


def _mod_spec(mod):
    if mod.shape[0] == 1:
        return pl.BlockSpec((1, 6, D_MODEL), lambda b, i: (0, 0, 0))
    return pl.BlockSpec((1, 6, D_MODEL), lambda b, i: (b, 0, 0))


def _head_rmsnorm(x, gain):
    rows, width = x.shape
    low = lax.broadcasted_iota(jnp.int32, (rows, LANES), 1) < HEAD_DIM
    out = []
    for b in range(width // LANES):
        blk = x[:, b * LANES:(b + 1) * LANES]
        sq = blk * blk
        lo = jnp.sum(jnp.where(low, sq, 0.0), axis=-1, keepdims=True)
        hi = jnp.sum(jnp.where(low, 0.0, sq), axis=-1, keepdims=True)
        ms = jnp.where(low, lo, hi) * (1.0 / HEAD_DIM)
        out.append(blk * lax.rsqrt(ms + EPS) * gain[:, b * LANES:(b + 1) * LANES])
    return jnp.concatenate(out, axis=-1)


def _rope(x, cos, sin_signed):
    rows, width = x.shape
    lane = lax.broadcasted_iota(jnp.int32, (rows, LANES), 1)
    second = (lane % (2 * AX_PAIRS)) >= AX_PAIRS
    out = []
    for b in range(width // LANES):
        blk = x[:, b * LANES:(b + 1) * LANES]
        partner = jnp.where(second, pltpu.roll(blk, AX_PAIRS, 1), pltpu.roll(blk, LANES - AX_PAIRS, 1))
        out.append(blk * cos + partner * sin_signed)
    return jnp.concatenate(out, axis=-1)


def _even_pre_kernel(*refs, rope):
    if rope:
        x_ref, mod_ref, g_ref, w_ref, qg_ref, kg_ref, cos_ref, sin_ref, f_ref, q_ref, k_ref, v_ref = refs
    else:
        x_ref, mod_ref, g_ref, w_ref, qg_ref, kg_ref, f_ref, q_ref, k_ref, v_ref, kt_ref, vt_ref = refs
    h = _modnorm(x_ref[0], g_ref[...], mod_ref[0, 0:1, :], mod_ref[0, 1:2, :])
    p = _dot(h.astype(BF16), w_ref[...])
    f_ref[0] = p[:, :FOURIER_W]
    q = _head_rmsnorm(p[:, FOURIER_W:FOURIER_W + Q_W], qg_ref[...])
    k = _head_rmsnorm(p[:, FOURIER_W + Q_W:FOURIER_W + Q_W + KV_W], kg_ref[...])
    if rope:
        q = _rope(q, cos_ref[...], sin_ref[...])
        k = _rope(k, cos_ref[...], sin_ref[...])
    q_ref[0] = (q * (HEAD_DIM ** -0.5 * math.log2(math.e))).astype(BF16)
    v = p[:, FOURIER_W + Q_W + KV_W:]
    k_ref[0] = k
    v_ref[0] = v
    if not rope:
        kt_ref[0] = k.T
        vt_ref[0] = v.T


def _rope_tables(seq):
    pos = np.arange(seq)
    freqs = ROPE_THETA ** (-np.arange(AX_PAIRS, dtype=np.float64) / AX_PAIRS)
    ang = np.zeros((seq, HEAD_DIM))
    sign = np.zeros((HEAD_DIM,))
    for a, p_a in enumerate((pos // GRID_W, pos % GRID_W)):
        for t in range(2):
            lo = a * 2 * AX_PAIRS + t * AX_PAIRS
            ang[:, lo:lo + AX_PAIRS] = p_a[:, None] * freqs[None, :]
            sign[lo:lo + AX_PAIRS] = -1.0 if t == 0 else 1.0
    cos = np.tile(np.cos(ang), (1, 2)).astype(np.float32)
    sin = np.tile(np.sin(ang) * sign[None, :], (1, 2)).astype(np.float32)
    return jnp.asarray(cos), jnp.asarray(sin)


def _even_pre(x, mod, g, w_in, qg, kg, rope):
    bsz, seq, d = x.shape
    tm = ROW_TILE
    tok = lambda w: pl.BlockSpec((1, tm, w), lambda b, i: (b, i, 0))
    in_specs = [tok(d), _mod_spec(mod), _const_spec((1, d)), _const_spec((d, EVEN_IN)),
                _const_spec((1, Q_W)), _const_spec((1, KV_W))]
    args = [x, mod, g, w_in, qg, kg]
    if rope:
        cos, sin = _rope_tables(seq)
        in_specs += [pl.BlockSpec((tm, LANES), lambda b, i: (i, 0))] * 2
        args += [cos, sin]
    out_specs = [tok(FOURIER_W), tok(Q_W), tok(KV_W), tok(KV_W)]
    out_shape = [jax.ShapeDtypeStruct((bsz, seq, FOURIER_W), F32),
                 jax.ShapeDtypeStruct((bsz, seq, Q_W), BF16),
                 jax.ShapeDtypeStruct((bsz, seq, KV_W), F32),
                 jax.ShapeDtypeStruct((bsz, seq, KV_W), F32)]
    if not rope:
        out_specs += [pl.BlockSpec((1, KV_W, tm), lambda b, i: (b, 0, i))] * 2
        out_shape += [jax.ShapeDtypeStruct((bsz, KV_W, seq), F32)] * 2
    return pl.pallas_call(
        functools.partial(_even_pre_kernel, rope=rope),
        grid=(bsz, seq // tm),
        in_specs=in_specs,
        out_specs=out_specs,
        out_shape=out_shape,
        compiler_params=_params(("arbitrary", "arbitrary")),
        name="even_pre",
    )(*args)


def _attn_kernel(*refs, seq, has_ctx):
    if has_ctx:
        q_ref, k_ref, v_ref, ck_ref, cv_ref, f_ref, cs_ref, cbd_ref, o_ref, kop, vopt, g_scr, cs_scr = refs
    else:
        q_ref, k_ref, v_ref, f_ref, cs_ref, cbd_ref, o_ref, kop, vopt, g_scr, cs_scr = refs
    tq = q_ref.shape[1]

    @pl.when((pl.program_id(0) == 0) & (pl.program_id(1) == 0))
    def _round_tables():
        for r0 in range(0, seq, tq):
            cs_scr[r0:r0 + tq, :] = cs_ref[r0:r0 + tq, :].astype(BF16)

    @pl.when(pl.program_id(1) == 0)
    def _prepare():
        def put(kx, vx, r0, transposed):
            n = kx.shape[1] if transposed else kx.shape[0]
            low = lax.broadcasted_iota(jnp.int32, (n, LANES), 1) < HEAD_DIM
            ones = jnp.ones((V_ROWS - HEAD_DIM, n), BF16)
            for jp in range(N_KV_HEADS // 2):
                kp = kx[jp * LANES:(jp + 1) * LANES, :].T if transposed else kx[:, jp * LANES:(jp + 1) * LANES]
                k_lo = jnp.where(low, kp, 0.0)
                k_hi = jnp.where(low, 0.0, kp)
                placed = (k_lo, pltpu.roll(k_lo, HEAD_DIM, 1), pltpu.roll(k_hi, HEAD_DIM, 1), k_hi)
                for t, kk in enumerate(placed):
                    kop[4 * jp + t, r0:r0 + n, :] = kk.astype(BF16)
                vt = vx[jp * LANES:(jp + 1) * LANES, :] if transposed else vx[:, jp * LANES:(jp + 1) * LANES].T
                vt = vt.astype(BF16)
                for t in range(2):
                    vopt[2 * jp + t, 0:HEAD_DIM, r0:r0 + n] = vt[t * HEAD_DIM:(t + 1) * HEAD_DIM]
                    vopt[2 * jp + t, HEAD_DIM:V_ROWS, r0:r0 + n] = ones

        put(k_ref[0], v_ref[0], 0, False)
        if has_ctx:
            put(ck_ref[0], cv_ref[0], seq, True)
        f = f_ref[0].astype(BF16)
        g_scr[0:seq, :] = _dot(f, cbd_ref[0].astype(BF16)).astype(BF16)
        g_scr[seq:2 * seq, :] = _dot(f, cbd_ref[1].astype(BF16)).astype(BF16)

    r0 = pl.multiple_of(pl.program_id(1) * tq, tq)
    o_ref[0, :, 0:FOURIER_W] = _dot(cs_scr[pl.ds(r0, tq), :], g_scr[...]).astype(BF16)
    q = q_ref[0]
    n_pairs = N_Q_HEADS // 2
    kv_of = lambda qp: [(2 * qp + half) // (N_Q_HEADS // N_KV_HEADS) for half in range(2)]

    def pair_scores(qp):
        qpair = q[:, qp * LANES:(qp + 1) * LANES]
        return [_dot_nt(qpair, kop[2 * kv_of(qp)[half] + half]) for half in range(2)]

    scores_next = pair_scores(0)
    for qp in range(n_pairs):
        kv = kv_of(qp)
        scores = scores_next
        if qp + 1 < n_pairs:
            scores_next = pair_scores(qp + 1)
        probs = [jnp.exp2(s - jnp.max(s, axis=-1, keepdims=True)).astype(BF16) for s in scores]
        outs = []
        for half in range(2):
            o = _dot_nt(vopt[kv[half]], probs[half])
            outs.append(o[0:HEAD_DIM] / o[HEAD_DIM:HEAD_DIM + 1])
        res = jnp.concatenate(outs, axis=0)
        o_ref[0, :, FOURIER_W + qp * LANES:FOURIER_W + (qp + 1) * LANES] = res.T.astype(BF16)


def _dft_tables(seq):
    s = np.arange(seq)
    ang = 2.0 * np.pi * ((s[:, None] * s[None, :]) % seq) / seq
    pos = np.concatenate([np.cos(ang), -np.sin(ang)], axis=1) / math.sqrt(seq)
    c = np.arange(FOURIER_GROUP_W)
    ang_c = 2.0 * np.pi * ((c[:, None] * c[None, :]) % FOURIER_GROUP_W) / FOURIER_GROUP_W
    eye = np.eye(FOURIER_W // FOURIER_GROUP_W)
    chan = np.stack([np.kron(eye, np.cos(ang_c)), np.kron(eye, np.sin(ang_c))]) / math.sqrt(FOURIER_GROUP_W)
    return jnp.asarray(pos, dtype=F32), jnp.asarray(chan, dtype=F32)


def _even_mixer(q, k, v, f, ctx):
    bsz, seq, _ = q.shape
    tq = ROW_TILE
    has_ctx = ctx is not None
    past = ctx[0].shape[2] if has_ctx else 0
    kt = seq + past
    pos_dft, chan_dft = _dft_tables(seq)
    full = lambda n, w: pl.BlockSpec((1, n, w), lambda b, i: (b, 0, 0))
    in_specs = [pl.BlockSpec((1, tq, Q_W), lambda b, i: (b, i, 0)), full(seq, KV_W), full(seq, KV_W)]
    args = [q, k, v]
    if has_ctx:
        in_specs += [full(KV_W, past), full(KV_W, past)]
        args += list(ctx)
    in_specs += [full(seq, FOURIER_W), _const_spec((seq, 2 * seq)),
                 _const_spec((2, FOURIER_W, FOURIER_W))]
    args += [f, pos_dft, chan_dft]
    return pl.pallas_call(
        functools.partial(_attn_kernel, seq=seq, has_ctx=has_ctx),
        grid=(bsz, seq // tq),
        in_specs=in_specs,
        out_specs=pl.BlockSpec((1, tq, D_MODEL), lambda b, i: (b, i, 0)),
        out_shape=jax.ShapeDtypeStruct((bsz, seq, D_MODEL), BF16),
        scratch_shapes=[pltpu.VMEM((2 * N_KV_HEADS, kt, LANES), BF16),
                        pltpu.VMEM((N_KV_HEADS, V_ROWS, kt), BF16),
                        pltpu.VMEM((2 * seq, FOURIER_W), BF16),
                        pltpu.VMEM((seq, 2 * seq), BF16)],
        compiler_params=_params(("arbitrary", "arbitrary")),
        name="even_mixer",
    )(*args)


def _post_ffn_kernel(*refs, odd):
    if odd:
        (x_ref, ylru_ref, g_ref, yssd_ref, z_ref, sn_ref, mod_ref, wout_ref, gf_ref,
         w1_ref, w3_ref, w2_ref, o_ref) = refs
        y_lru = ylru_ref[0] * _gelu_tanh(g_ref[0])
        y = yssd_ref[0] * _silu(z_ref[0])
        ms = jnp.mean(y * y, axis=-1, keepdims=True)
        y_ssd = y * lax.rsqrt(ms + EPS) * sn_ref[...]
        mix = jnp.concatenate([y_lru, y_ssd], axis=-1).astype(BF16)
    else:
        x_ref, mix_ref, mod_ref, wout_ref, gf_ref, w1_ref, w3_ref, w2_ref, o_ref = refs
        mix = mix_ref[0]
    x = x_ref[0] + mod_ref[0, 2:3, :] * _dot(mix, wout_ref[...])
    h = _modnorm(x, gf_ref[...], mod_ref[0, 3:4, :], mod_ref[0, 4:5, :]).astype(BF16)
    a = (_silu(_dot(h, w1_ref[...])) * _dot(h, w3_ref[...])).astype(BF16)
    o_ref[0] = x + mod_ref[0, 5:6, :] * _dot(a, w2_ref[...])


def _post_ffn(x, mix_args, mod, w_out, g_ffn, w1, w3, w2, layer, ssd_norm=None):
    bsz, seq, d = x.shape
    if mod.shape[0] == 1 and seq < TOKEN_BLOCK:
        fold = TOKEN_BLOCK // seq
        merge = lambda t: t.reshape(bsz // fold, fold * seq, t.shape[-1])
        out = _post_ffn(merge(x), jax.tree.map(merge, mix_args), mod, w_out, g_ffn, w1, w3, w2, layer, ssd_norm)
        return out.reshape(bsz, seq, d)
    tm = min(seq, TOKEN_BLOCK)
    odd = layer % 2 == 1
    tok = lambda w: pl.BlockSpec((1, tm, w), lambda b, i: (b, i, 0))
    slab = lambda w: pl.BlockSpec((None,) + w.shape[1:], lambda b, i: (layer, 0, 0), pipeline_mode=pl.Buffered(1))
    if odd:
        ylru, g, yssd, z = mix_args
        in_specs = [tok(d), tok(LRU_W), tok(LRU_W), tok(SSD_INNER), tok(SSD_INNER), _const_spec((1, SSD_INNER))]
        args = [x, ylru, g, yssd, z, ssd_norm]
    else:
        in_specs = [tok(d), tok(d)]
        args = [x, mix_args]
    in_specs += [_mod_spec(mod), _const_spec(w_out.shape), _const_spec((1, d)),
                 slab(w1), slab(w3), slab(w2)]
    args += [mod, w_out, g_ffn, w1, w3, w2]
    return pl.pallas_call(
        functools.partial(_post_ffn_kernel, odd=odd),
        grid=(bsz, seq // tm),
        in_specs=in_specs,
        out_specs=tok(d),
        out_shape=jax.ShapeDtypeStruct((bsz, seq, d), F32),
        compiler_params=_params(("arbitrary", "arbitrary")),
        name="post_ffn_odd" if odd else "post_ffn_even",
    )(*args)


XL_LO = LRU_W
Z_LO = 2 * LRU_W
XBC_LO = 2 * LRU_W + SSD_INNER
DT_LO = XBC_LO + SSD_CONV_CH


def _conv_block(main, edges, w, b):
    before, after0, after1 = edges
    rows = main.shape[0]
    row = lax.broadcasted_iota(jnp.int32, (SUBLANES, 1), 0)
    last = rows - SUBLANES

    def patch_first(x, fix):
        return jnp.concatenate([fix(x[:SUBLANES]), x[SUBLANES:]], axis=0)

    def patch_last(x, fix):
        return jnp.concatenate([x[:last], fix(x[last:])], axis=0)

    xm1 = patch_first(pltpu.roll(main, 1, 0), lambda t: jnp.where(row == 0, before, t))
    xp1 = patch_last(pltpu.roll(main, rows - 1, 0), lambda t: jnp.where(row == SUBLANES - 1, after0, t))
    xp2 = patch_last(pltpu.roll(main, rows - 2, 0),
                     lambda t: jnp.where(row == SUBLANES - 2, after0, jnp.where(row == SUBLANES - 1, after1, t)))
    return xm1 * w[0:1] + main * w[1:2] + xp1 * w[2:3] + xp2 * w[3:4] + b


def _odd_pre_kernel(x_ref, xprev_ref, xnext_ref, mod_ref, g_ref, w_ref, wdt_ref, cwl_ref, cbl_ref, wg_ref, bg_ref,
                    lam_ref, gate_ref, z_ref, a_ref, b_ref, xbc_ref, dt_ref, w_scr):
    i = pl.program_id(1)
    wide = 2 * LANES

    @pl.when((pl.program_id(0) == 0) & (i == 0))
    def _():
        for lo in range(0, DT_LO, wide):
            w_scr[:, lo:lo + wide] = w_ref[:, lo:lo + wide].astype(BF16)

    g, shift, scale = g_ref[...], mod_ref[0, 0:1, :], mod_ref[0, 1:2, :]
    h = _modnorm(x_ref[0], g, shift, scale).astype(BF16)
    proj = lambda lo: _dot(h, w_scr[:, lo:lo + wide])
    p_xl = [proj(XL_LO + t * wide) for t in range(LRU_W // wide)]
    others = ([(gate_ref, t * wide, t * wide) for t in range(LRU_W // wide)]
              + [(z_ref, Z_LO + t * wide, t * wide) for t in range(SSD_INNER // wide)]
              + [(xbc_ref, XBC_LO + t * wide, t * wide) for t in range(SSD_CONV_CH // wide)])

    halo = jnp.concatenate([xprev_ref[0], xnext_ref[0]], axis=0)
    ph = _dot(_modnorm(halo, g, shift, scale).astype(BF16), w_scr[:, XL_LO:Z_LO])
    has_prev = i > 0
    has_next = i < pl.num_programs(1) - 1
    e_l = (jnp.where(has_prev, ph[SUBLANES - 1:SUBLANES], 0.0),
           jnp.where(has_next, ph[SUBLANES:SUBLANES + 1], 0.0),
           jnp.where(has_next, ph[SUBLANES + 1:SUBLANES + 2], 0.0))

    rate = (0.5 * LRU_C) * _softplus(-lam_ref[...])
    n_lru = LRU_W // LANES
    for lb in range(n_lru):
        sl = slice(lb * LANES, (lb + 1) * LANES)
        for dst, lo, col in others[lb * len(others) // n_lru:(lb + 1) * len(others) // n_lru]:
            dst[0, :, col:col + wide] = proj(lo)
        xb = _conv_block(p_xl[lb // 2][:, (lb % 2) * LANES:(lb % 2 + 1) * LANES], [e[:, sl] for e in e_l],
                         cwl_ref[:, sl], cbl_ref[:, sl])
        xh = 0.5 * xb
        gt = jnp.tanh(_dot(xb.astype(BF16), wg_ref[lb]) + bg_ref[lb])
        for d in range(2):
            neg_log_a = rate[d:d + 1, sl] * (gt[:, 2 * d * LANES:(2 * d + 1) * LANES] + 1.0)
            a = jnp.exp2(neg_log_a * (-math.log2(math.e)))
            g2 = jnp.tanh(neg_log_a) * (a * a + 1.0)
            root = jnp.where(g2 > 0.0, g2 * lax.rsqrt(g2), 0.0)
            a_ref[0, d, :, sl] = a
            b_ref[0, d, :, sl] = root * ((gt[:, (2 * d + 1) * LANES:(2 * d + 2) * LANES] + 1.0) * xh)
    dt_ref[0] = _dot(h, wdt_ref[...])


def _odd_pre(x, mod, g, w_in, w_dt, lru):
    bsz, seq, d = x.shape
    tm = min(seq, TOKEN_BLOCK)
    per8 = tm // SUBLANES
    tok = lambda w: pl.BlockSpec((1, tm, w), lambda b, i: (b, i, 0))
    both = pl.BlockSpec((1, 2, tm, LRU_W), lambda b, i: (b, 0, i, 0))
    consts = list(lru)
    out_specs = [tok(LRU_W), tok(SSD_INNER), both, both, tok(SSD_CONV_CH), tok(LANES)]
    row = lambda w: jax.ShapeDtypeStruct((bsz, seq, w), F32)
    out_shape = ([row(LRU_W), row(SSD_INNER)] + [jax.ShapeDtypeStruct((bsz, 2, seq, LRU_W), F32)] * 2
                 + [row(SSD_CONV_CH), row(LANES)])
    return pl.pallas_call(
        _odd_pre_kernel,
        grid=(bsz, seq // tm),
        in_specs=[tok(d),
                  pl.BlockSpec((1, SUBLANES, d), lambda b, i: (b, jnp.maximum(i * per8 - 1, 0), 0)),
                  pl.BlockSpec((1, SUBLANES, d), lambda b, i: (b, jnp.minimum((i + 1) * per8, seq // SUBLANES - 1), 0)),
                  _mod_spec(mod), _const_spec((1, d)),
                  pl.BlockSpec((None,) + w_in.shape[1:], lambda b, i: (0, 0, 0), pipeline_mode=pl.Buffered(1)),
                  _const_spec(w_dt.shape)]
                 + [_const_spec(c.shape) for c in consts],
        out_specs=out_specs,
        out_shape=out_shape,
        scratch_shapes=[pltpu.VMEM((d, DT_LO), BF16)],
        compiler_params=_params(("arbitrary", "arbitrary")),
        name="odd_pre",
    )(x, x, x, mod, g, w_in, w_dt, *consts)


def _ssd_kernel(*refs, seq, has_h0):
    xbc_ref, dt_ref, cw_ref, cb_ref, dtb_ref, alog_ref, dexp_ref, ef_ref, eb_ref, la_ref, lb_ref = refs[:11]
    if has_h0:
        h0_ref, lh0_ref, y_ref, ylru_ref = refs[11:15]
    else:
        y_ref, ylru_ref, hout_ref, lhout_ref = refs[11:15]
    xs_ref, bm_ref, cm_ref, cs_ref, dts_ref, cst_ref, dtt_ref, bt_ref, st_scr = refs[15:]
    q = SSD_Q
    nc = seq // q
    lane = lax.broadcasted_iota(jnp.int32, (q, LANES), 1)
    qi = lax.broadcasted_iota(jnp.int32, (q, q), 0)
    ki = lax.broadcasted_iota(jnp.int32, (q, q), 1)
    tri_lower = jnp.where(ki <= qi, 1.0, 0.0).astype(BF16)
    tri_upper = jnp.where(ki >= qi, 1.0, 0.0).astype(BF16)
    a_log2 = jnp.where(lane[0:1, :] < 2 * SSD_HEADS, -math.log2(math.e) * jnp.exp(alog_ref[...]), 0.0)

    def prepare(c):
        r0 = pl.multiple_of(c * q, q)
        rows = pl.ds(r0, q)
        prev8 = pl.ds(pl.multiple_of(jnp.maximum(r0 - SUBLANES, 0), SUBLANES), SUBLANES)
        next8 = pl.ds(pl.multiple_of(jnp.minimum(r0 + q, seq - SUBLANES), SUBLANES), SUBLANES)
        has_prev = r0 > 0
        has_next = r0 + q < seq
        for lb in range(SSD_CONV_CH // LANES):
            sl = slice(lb * LANES, (lb + 1) * LANES)
            before = xbc_ref[0, prev8, sl]
            after = xbc_ref[0, next8, sl]
            edges = (jnp.where(has_prev, before[SUBLANES - 1:SUBLANES], 0.0),
                     jnp.where(has_next, after[0:1], 0.0), jnp.where(has_next, after[1:2], 0.0))
            xc = _conv_block(xbc_ref[0, rows, sl], edges, cw_ref[:, sl], cb_ref[:, sl])
            half_xc = 0.5 * xc
            xc = half_xc * (jnp.tanh(half_xc) + 1.0)
            if lb < SSD_INNER // LANES:
                xs_ref[rows, sl] = xc
                y_ref[0, rows, sl] = dexp_ref[:, sl] * xc
            elif lb == SSD_INNER // LANES:
                bm_ref[rows, :] = xc
                bt_ref[c] = xc.T
            else:
                cm_ref[rows, :] = xc
        dts = _softplus(dt_ref[0, rows, :] + dtb_ref[...])
        da = dts * a_log2
        cs = jnp.where(lane < SSD_HEADS, _dot3(tri_lower, da), _dot3(tri_upper, da))
        cs_ref[rows, :] = cs
        dts_ref[rows, :] = dts
        cst_ref[c] = cs.T
        dtt_ref[c] = dts.T

    for d in range(2):
        st_scr[d] = h0_ref[0, d].T if has_h0 else jnp.zeros(st_scr.shape[1:], F32)

    low = lane < SSD_STATE
    col = lax.broadcasted_iota(jnp.int32, (SSD_STATE, SSD_INNER), 1)
    first_group = col < SSD_INNER // 2

    def chunk(c, fwd):
        d = 0 if fwd else 1
        r0 = pl.multiple_of(c * q, q)
        rows = pl.ds(r0, q)
        cs = cs_ref[rows, :]
        dts = dts_ref[rows, :]
        cst = cst_ref[c]
        dtt = dtt_ref[c]
        xs = xs_ref[rows, :]
        xsb = xs.astype(BF16)
        bb = bm_ref[rows, :].astype(BF16)
        cb_ = cm_ref[rows, :].astype(BF16)
        zero = jnp.zeros_like(cb_)
        cbg = [_dot_nt(jnp.where(low, cb_, zero), bb).astype(BF16),
               _dot_nt(jnp.where(low, zero, cb_), bb).astype(BF16)]
        mask = (ki <= qi) if fwd else (ki >= qi)
        expand = ef_ref[...] if fwd else eb_ref[...]
        st = st_scr[d]
        hh = jnp.concatenate([jnp.where(first_group, st, 0.0), jnp.where(first_group, 0.0, st)], axis=0)
        tot = cs[q - 1:q, :] if fwd else cs[0:1, :]
        mine = (lane >= d * SSD_HEADS) & (lane < (d + 1) * SSD_HEADS)
        w = jnp.exp2(jnp.where(mine, tot - cs, 0.0)) * dts
        dec = jnp.broadcast_to(jnp.exp2(tot), (2 * SUBLANES, LANES))
        dec_hi = dec.astype(BF16)
        dec_lo = (dec - dec_hi.astype(F32)).astype(BF16)
        spread = _dot(jnp.concatenate([jnp.exp2(cs).astype(BF16), w.astype(BF16), dec_hi, dec_lo], axis=0), expand)
        y = _dot(cb_, hh.astype(BF16)) * spread[0:q]
        pairs = []
        for pp in range(SSD_HEADS // 2):
            ms = []
            for h in (2 * pp, 2 * pp + 1):
                l = d * SSD_HEADS + h
                decay = jnp.exp2((cs[:, l:l + 1] - cst[l:l + 1, :]).astype(BF16))
                decay = jnp.where(mask, decay, jnp.zeros_like(decay))
                ms.append(cbg[pp // (SSD_HEADS // 4)] * decay * dtt[l:l + 1, :].astype(BF16))
            xp = xsb[:, pp * LANES:(pp + 1) * LANES]
            zx = jnp.zeros_like(xp)
            x2 = jnp.concatenate([jnp.where(low, xp, zx), jnp.where(low, zx, xp)], axis=0)
            pairs.append(_dot(jnp.concatenate(ms, axis=1), x2))
        y_ref[0, rows, :] += y + jnp.concatenate(pairs, axis=1)
        xw = (xs * spread[q:2 * q]).astype(BF16)
        bt = bt_ref[c].astype(BF16)
        half_w = SSD_INNER // 2
        st_new = jnp.concatenate([_dot(bt[:SSD_STATE], xw[:, :half_w]), _dot(bt[SSD_STATE:], xw[:, half_w:])], axis=1)
        chunk_decay = spread[2 * q:2 * q + 1] + spread[2 * q + 2 * SUBLANES:2 * q + 2 * SUBLANES + 1]
        st_scr[d] = chunk_decay * st + st_new

    def prepare_all(c, carry):
        prepare(c)
        return carry

    nb = seq // SUBLANES
    per_chunk = q // SUBLANES
    row = lax.broadcasted_iota(jnp.int32, (SUBLANES, LRU_W), 0)

    def lru_block(blk, hf, hb):
        rf = pl.multiple_of(blk * SUBLANES, SUBLANES)
        rb = pl.multiple_of((nb - 1 - blk) * SUBLANES, SUBLANES)
        a = la_ref[0, 0, pl.ds(rf, SUBLANES), :]
        b = lb_ref[0, 0, pl.ds(rf, SUBLANES), :]
        for k in (1, 2, 4):
            keep = row >= k
            b = a * jnp.where(keep, pltpu.roll(b, k, 0), 0.0) + b
            a = a * jnp.where(keep, pltpu.roll(a, k, 0), 1.0)
        h = a * hf + b
        ylru_ref[0, pl.ds(rf, SUBLANES), :] += h
        hf = h[SUBLANES - 1:SUBLANES, :]
        a = la_ref[0, 1, pl.ds(rb, SUBLANES), :]
        b = lb_ref[0, 1, pl.ds(rb, SUBLANES), :]
        for k in (1, 2, 4):
            keep = row < SUBLANES - k
            b = a * jnp.where(keep, pltpu.roll(b, SUBLANES - k, 0), 0.0) + b
            a = a * jnp.where(keep, pltpu.roll(a, SUBLANES - k, 0), 1.0)
        h = a * hb + b
        ylru_ref[0, pl.ds(rb, SUBLANES), :] += h
        return hf, h[0:1, :]

    def both_directions(i, carry):
        chunk(i, True)
        chunk(nc - 1 - i, False)
        hf, hb = carry
        for j in range(per_chunk):
            hf, hb = lru_block(i * per_chunk + j, hf, hb)
        return hf, hb

    ylru_ref[0] = jnp.zeros((seq, LRU_W), F32)
    lax.fori_loop(0, nc, prepare_all, 0)
    if has_h0:
        lru_start = (lh0_ref[0, 0:1, :], lh0_ref[0, 1:2, :])
    else:
        lru_start = (jnp.zeros((1, LRU_W), F32), jnp.zeros((1, LRU_W), F32))
    hf, hb = lax.fori_loop(0, nc, both_directions, lru_start)
    if not has_h0:
        for d in range(2):
            hout_ref[0, d] = st_scr[d].T
        lhout_ref[0, 0:1, :] = hf
        lhout_ref[0, 1:2, :] = hb


def _ssd(xbc, dt, cw, cb, dtb, alog, dexp, lru_a, lru_b, h0, lru_h0):
    bsz, seq, _ = xbc.shape
    nc = seq // SSD_Q
    expand = np.zeros((2, LANES, SSD_INNER), np.float32)
    for d in range(2):
        for h in range(SSD_HEADS):
            expand[d, d * SSD_HEADS + h, h * 64:(h + 1) * 64] = 1.0
    expand = jnp.asarray(expand, dtype=BF16)
    per_seq = lambda n, w: pl.BlockSpec((1, n, w), lambda b: (b, 0, 0))
    state_spec = pl.BlockSpec((1, 2, SSD_INNER, SSD_STATE), lambda b: (b, 0, 0, 0))
    has_h0 = h0 is not None
    coef = pl.BlockSpec((1, 2, seq, LRU_W), lambda b: (b, 0, 0, 0))
    in_specs = [per_seq(seq, SSD_CONV_CH), per_seq(seq, LANES), _const_spec(cw.shape), _const_spec(cb.shape),
                _const_spec(dtb.shape), _const_spec(alog.shape), _const_spec(dexp.shape),
                _const_spec((LANES, SSD_INNER)), _const_spec((LANES, SSD_INNER)), coef, coef]
    args = [xbc, dt, cw, cb, dtb, alog, dexp, expand[0], expand[1], lru_a, lru_b]
    out_specs = [per_seq(seq, SSD_INNER), per_seq(seq, LRU_W)]
    out_shape = [jax.ShapeDtypeStruct((bsz, seq, SSD_INNER), F32), jax.ShapeDtypeStruct((bsz, seq, LRU_W), F32)]
    if has_h0:
        in_specs += [state_spec, per_seq(2, LRU_W)]
        args += [h0, lru_h0]
    else:
        out_specs += [state_spec, per_seq(2, LRU_W)]
        out_shape += [jax.ShapeDtypeStruct((bsz, 2, SSD_INNER, SSD_STATE), F32),
                      jax.ShapeDtypeStruct((bsz, 2, LRU_W), F32)]
    outs = pl.pallas_call(
        functools.partial(_ssd_kernel, seq=seq, has_h0=has_h0),
        grid=(bsz,),
        in_specs=in_specs,
        out_specs=out_specs,
        out_shape=out_shape,
        scratch_shapes=[pltpu.VMEM((seq, SSD_INNER), F32)] + [pltpu.VMEM((seq, LANES), F32)] * 4
                       + [pltpu.VMEM((nc, LANES, SSD_Q), F32)] * 3 + [pltpu.VMEM((2, SSD_STATE, SSD_INNER), F32)],
        compiler_params=_params(("arbitrary",)),
        name="ssd",
    )(*args)
    return (outs[0], outs[1], None, None) if has_h0 else tuple(outs)


def _lru_gate_weights(wa, ba, wx, bx):
    eye = 0.5 * jnp.eye(2, dtype=F32)

    def pair_blocks(w):
        w4 = w.reshape(4, 2, 64, 64)
        return jnp.einsum("laij,ab->laibj", w4, eye).reshape(4, LANES, LANES)

    mats = [pair_blocks(w) for w in (wa[0], wx[0], wa[1], wx[1])]
    wg = jnp.concatenate(mats, axis=-1).astype(BF16)
    bias = [0.5 * b.reshape(4, 1, LANES) for b in (ba[0], bx[0], ba[1], bx[1])]
    return wg, jnp.concatenate(bias, axis=-1)


def _trunk(x, mods, p, ctx):
    is_ctx = ctx is None
    bsz, seq, _ = x.shape
    f, q, k, v, *kv_t = _even_pre(x, mods[0], p["norm_mix"][0], p["w_in_even"], p["q_gain"], p["k_gain"],
                                  rope=not is_ctx)
    mix = _even_mixer(q, k, v, f, None if is_ctx else (ctx["k"], ctx["v"]))
    x = _post_ffn(x, mix, mods[0], p["w_out_even"], p["norm_ffn"][0], p["w1"], p["w3"], p["w2"], layer=0)
    gate, z, lru_a, lru_b, xbc, dt = _odd_pre(
        x, mods[1], p["norm_mix"][1], p["w_in_odd"], p["w_dt_odd"],
        lru=(p["conv_lru_w"], p["conv_lru_b"], p["lru_wg"], p["lru_bg"], p["lru_lambda"]))
    h0_lru, h0_ssd = (None, None) if is_ctx else (ctx["lru"], ctx["ssd"])
    yssd, ylru, ssd_state, lru_state = _ssd(xbc, dt, p["conv_ssd_w"], p["conv_ssd_b"], p["ssd_dtb"], p["ssd_alog"],
                                            p["ssd_dexp"], lru_a, lru_b, h0_ssd, h0_lru)
    x = _post_ffn(x, (ylru, gate, yssd, z), mods[1], p["w_out_odd"], p["norm_ffn"][1], p["w1"], p["w3"], p["w2"],
                  layer=1, ssd_norm=p["ssd_norm"])
    kt, vt = kv_t if is_ctx else (None, None)
    return x, kt, vt, lru_state, ssd_state


def kernel(x_prompt, x_sample, c, cache_k, cache_v, state_lru, state_ssd, c_ctx, w_ada, b_ada, norm_mix, norm_ffn,
           w_in_even, q_norm, k_norm, w_out_even, w_in_odd, conv_lru_w, conv_lru_b, lru_wa, lru_ba, lru_wx, lru_bx,
           lru_lambda, conv_ssd_w, conv_ssd_b, ssd_dt_bias, ssd_a_log, ssd_d, ssd_norm, w_out_odd, ffn_w1, ffn_w3,
           ffn_w2):
    bsz, seq, d = x_prompt.shape
    dec_b = x_sample.shape[0]
    past = cache_k.shape[2]

    cond_rows = 16
    cond = jnp.concatenate([c, c_ctx[None, :], jnp.zeros((cond_rows - dec_b - 1, d), F32)], axis=0)
    mod = _modulation(cond, w_ada, b_ada).reshape(2, cond_rows, 6, d)
    mods_sample = [mod[l, :dec_b] for l in range(2)]
    mods_prompt = [mod[l, dec_b:dec_b + 1] for l in range(2)]

    lru_wg, lru_bg = _lru_gate_weights(lru_wa[0], lru_ba[0], lru_wx[0], lru_bx[0])
    pad32 = lambda a: jnp.pad(a.reshape(1, 2 * SSD_HEADS), ((0, 0), (0, LANES - 2 * SSD_HEADS)))
    p = dict(
        norm_mix=norm_mix.reshape(2, 1, d), norm_ffn=norm_ffn.reshape(2, 1, d),
        w_in_even=w_in_even[0].astype(BF16),
        q_gain=jnp.tile(q_norm[0], N_Q_HEADS)[None, :], k_gain=jnp.tile(k_norm[0], N_KV_HEADS)[None, :],
        w_out_even=w_out_even[0].astype(BF16),
        w_in_odd=w_in_odd,
        w_dt_odd=jnp.pad(w_in_odd[0, :, DT_LO:], ((0, 0), (0, LANES - 2 * SSD_HEADS))).astype(BF16),
        conv_lru_w=conv_lru_w[0], conv_lru_b=conv_lru_b[0][None, :],
        lru_wg=lru_wg, lru_bg=lru_bg, lru_lambda=lru_lambda[0],
        conv_ssd_w=conv_ssd_w[0], conv_ssd_b=conv_ssd_b[0][None, :],
        ssd_dtb=pad32(ssd_dt_bias[0]), ssd_alog=pad32(ssd_a_log[0]),
        ssd_dexp=jnp.repeat(ssd_d[0], SSD_INNER // SSD_HEADS)[None, :],
        ssd_norm=ssd_norm[0][None, :],
        w_out_odd=w_out_odd[0].astype(BF16),
        w1=ffn_w1.astype(BF16), w3=ffn_w3.astype(BF16), w2=ffn_w2.astype(BF16),
    )

    y_prompt, k_new, v_new, lru_new, ssd_new = _trunk(x_prompt, mods_prompt, p, None)
    ctx = dict(
        k=cache_k[:, 0].transpose(0, 2, 3, 1).reshape(dec_b, KV_W, past),
        v=cache_v[:, 0].transpose(0, 2, 3, 1).reshape(dec_b, KV_W, past),
        lru=state_lru[:, 0],
        ssd=state_ssd[:, 0].reshape(dec_b, 2, SSD_INNER, SSD_STATE),
    )
    y_sample, _, _, _, _ = _trunk(x_sample, mods_sample, p, ctx)

    new_k = k_new.transpose(0, 2, 1).reshape(bsz, 1, seq, N_KV_HEADS, HEAD_DIM)
    new_v = v_new.transpose(0, 2, 1).reshape(bsz, 1, seq, N_KV_HEADS, HEAD_DIM)
    new_lru = lru_new.reshape(bsz, 1, 2, LRU_W)
    new_ssd = ssd_new.reshape(bsz, 1, 2, SSD_HEADS, SSD_INNER // SSD_HEADS, SSD_STATE)
    return (y_prompt, y_sample, new_k, new_v, new_lru, new_ssd)
```
